```python
import jax
import jax.numpy as jnp
from jax import lax
import numpy as np

D_MODEL = 1024
BATCH = 2
SEQ = 16384
DEPTH = 1
DEC_BATCH = 128
DEC_SEQ = 1
PAST_LEN = 8192
PAGE_SIZE = 128

NSA_HEADS = 8
NSA_KV_GROUPS = 2
NSA_GROUP_HEADS = NSA_HEADS // NSA_KV_GROUPS
HEAD_DIM = 64
CMP_BLOCK = 32
CMP_STRIDE = 16
CMP_HIDDEN = 64
SEL_BLOCK = 64
N_SEL = 16
WINDOW = 512
Q_BLOCK = 128
FORCE_SCORE = 1.0e3
NEG_INF = -1.0e30
RET_HEADS = 4
RET_DK = 128
RET_DV = 128
RET_CHUNK = 128
ROPE_BASE = 10000.0
N_GROUPS = 4
EXPERTS_PER_GROUP = 8
N_EXPERTS = N_GROUPS * EXPERTS_PER_GROUP
TOP_K = 2
EXPERT_FF = 512
MOE_BLOCK = 128
LN_EPS = 1e-5
DEEPNORM_ALPHA = (2.0 * DEPTH) ** 0.25
DEEPNORM_BETA = (8.0 * DEPTH) ** -0.25
NSA_Q_COLS = NSA_HEADS * HEAD_DIM
NSA_KV_COLS = 3 * 2 * NSA_KV_GROUPS * HEAD_DIM
NSA_GATE_COLS = 3 * NSA_HEADS
RET_QK_COLS = RET_HEADS * RET_DK
RET_V_COLS = RET_HEADS * RET_DV
MERGE_COLS = 2 * D_MODEL
IN_COLS = NSA_Q_COLS + NSA_KV_COLS + NSA_GATE_COLS + 2 * RET_QK_COLS + 2 * RET_V_COLS + MERGE_COLS

kernel_name = 'nsa_retention_hmoe_deepnorm_step'


def layer_norm(x, g, b):
    xf = x.astype(jnp.float32)
    mu = jnp.mean(xf, -1, keepdims=True)
    var = jnp.mean(jnp.square(xf - mu), -1, keepdims=True)
    y = (xf - mu) * lax.rsqrt(var + LN_EPS) * g.astype(jnp.float32) + b.astype(jnp.float32)
    return y.astype(x.dtype)


def masked_softmax(s, mask):
    p = jax.nn.softmax(jnp.where(mask, s.astype(jnp.float32), NEG_INF), axis=-1)
    return p * mask


def rotary(x, pos):
    half = x.shape[-1] // 2
    inv = ROPE_BASE ** (-jnp.arange(half, dtype=jnp.float32) / half)
    ang = pos.astype(jnp.float32)[:, None] * inv[None, :]
    cos = jnp.cos(ang)[None, :, None, :]
    sin = jnp.sin(ang)[None, :, None, :]
    xf = x.astype(jnp.float32)
    x1, x2 = xf[..., :half], xf[..., half:]
    return jnp.concatenate([x1 * cos - x2 * sin, x1 * sin + x2 * cos], axis=-1).astype(x.dtype)


def project_inputs(x, pos, w_in, b_in):
    B, T, _ = x.shape
    h = jnp.einsum('btd,dc->btc', x, w_in) + b_in
    sizes = [NSA_Q_COLS, NSA_KV_COLS, NSA_GATE_COLS, RET_QK_COLS, RET_QK_COLS, RET_V_COLS, RET_V_COLS]
    points = [int(p) for p in np.cumsum(sizes)]
    q_a, kv_a, g_a, q_b, k_b, v_b, g_b, g_m = jnp.split(h, points, axis=-1)
    q_a = q_a.reshape(B, T, NSA_KV_GROUPS, NSA_GROUP_HEADS, HEAD_DIM).transpose(0, 2, 3, 1, 4)
    kv_a = kv_a.reshape(B, T, 3, 2, NSA_KV_GROUPS, HEAD_DIM).transpose(2, 0, 4, 1, 3, 5)
    g_a = jax.nn.sigmoid(g_a.reshape(B, T, NSA_KV_GROUPS, NSA_GROUP_HEADS, 3).transpose(0, 2, 3, 1, 4))
    q_b = rotary(q_b.reshape(B, T, RET_HEADS, RET_DK), pos)
    k_b = rotary(k_b.reshape(B, T, RET_HEADS, RET_DK), pos) * (RET_DK ** -0.5)
    v_b = v_b.reshape(B, T, RET_HEADS, RET_DV)
    return q_a, g_a, kv_a[0], kv_a[1], kv_a[2], q_b, k_b, v_b, g_b, g_m


def chunk_proj(chunks, w_cmp1):
    w = w_cmp1.reshape(2, CMP_BLOCK // CMP_STRIDE, CMP_STRIDE, HEAD_DIM, CMP_HIDDEN)
    return jnp.einsum('...cvd,vhcde->...hve', chunks, w)


def compress(cp, cmp_pos, w_cmp1, w_cmp2, b_cmp2):
    pe_bias = jnp.einsum('vcd,vcde->ve', cmp_pos, w_cmp1)
    h = jax.nn.gelu(cp[:, :, :-1, 0] + cp[:, :, 1:, 1] + pe_bias)
    return jnp.einsum('bgnve,ved->bgnvd', h, w_cmp2) + b_cmp2


def nsa_core(q, gates, pos_q, kv_cmp_c, n_key_blocks, gather_sel, kv_win, pos_win):
    B, G, R, Tq, _ = q.shape
    scale = HEAD_DIM ** -0.5
    n_cmp = kv_cmp_c.shape[2]
    cmp_end = jnp.arange(n_cmp, dtype=jnp.int32) * CMP_STRIDE + (CMP_BLOCK - 1)
    s = jnp.einsum('bgrqd,bgnd->bgrqn', q, kv_cmp_c[..., 0, :]) * scale
    p_cmp = masked_softmax(s, cmp_end[None, :] <= pos_q[:, None])
    o_cmp = jnp.einsum('bgrqn,bgnd->bgrqd', p_cmp.astype(q.dtype), kv_cmp_c[..., 1, :])
    n_blk = max(n_key_blocks, N_SEL)
    ratio = SEL_BLOCK // CMP_STRIDE
    imp = jnp.pad(jnp.sum(p_cmp, axis=2), ((0, 0), (0, 0), (0, 0), (0, n_blk * ratio - n_cmp)))
    imp = imp.reshape(B, G, Tq, n_blk, ratio)
    prev = jnp.pad(imp[..., :-1, ratio - 1], ((0, 0), (0, 0), (0, 0), (1, 0)))
    imp = jnp.sum(imp, axis=-1) + prev
    cur = pos_q // SEL_BLOCK
    blk = jnp.arange(n_blk, dtype=jnp.int32)
    forced = (blk[None, :] == 0) | (blk[None, :] == cur[:, None]) | (blk[None, :] == cur[:, None] - 1)
    valid = blk[None, :] <= cur[:, None]
    imp = jnp.where(forced, FORCE_SCORE, jnp.where(valid, imp, -1.0))
    _, idx = lax.top_k(imp, N_SEL)
    kv_sel = gather_sel(idx).reshape(B, G, Tq, N_SEL * SEL_BLOCK, 2, HEAD_DIM)
    sel_pos = (idx[..., None] * SEL_BLOCK + jnp.arange(SEL_BLOCK, dtype=jnp.int32)).reshape(B, G, Tq, N_SEL * SEL_BLOCK)
    s = jnp.einsum('bgrqd,bgqkd->bgrqk', q, kv_sel[..., 0, :]) * scale
    p = masked_softmax(s, (sel_pos <= pos_q[:, None])[:, :, None])
    o_slc = jnp.einsum('bgrqk,bgqkd->bgrqd', p.astype(q.dtype), kv_sel[..., 1, :])
    dpos = pos_q[:, None] - pos_win[None, :]
    wmask = (dpos >= 0) & (dpos < WINDOW) & (pos_win[None, :] >= 0)
    s = jnp.einsum('bgrqd,bgkd->bgrqk', q, kv_win[..., 0, :]) * scale
    p = masked_softmax(s, wmask)
    o_win = jnp.einsum('bgrqk,bgkd->bgrqd', p.astype(q.dtype), kv_win[..., 1, :])
    return gates[..., 0:1] * o_cmp + gates[..., 1:2] * o_slc + gates[..., 2:3] * o_win


def nsa_prompt(q, gates, kv_cmp, kv_slc, kv_win, cmp_pos, w_cmp1, w_cmp2, b_cmp2):
    B, G, R, T, _ = q.shape
    n_ch = T // CMP_STRIDE
    chunks = kv_cmp[:, :, :n_ch * CMP_STRIDE].reshape(B, G, n_ch, CMP_STRIDE, 2, HEAD_DIM)
    kv_cmp_c = compress(chunk_proj(chunks, w_cmp1), cmp_pos, w_cmp1, w_cmp2, b_cmp2)
    n_key_blocks = -(-T // SEL_BLOCK)
    n_blk = max(n_key_blocks, N_SEL)
    slc_blocks = jnp.pad(kv_slc, ((0, 0), (0, 0), (0, n_blk * SEL_BLOCK - T), (0, 0), (0, 0)))
    slc_blocks = slc_blocks.reshape(B, G, n_blk, SEL_BLOCK, 2, HEAD_DIM)
    bi = jnp.arange(B)[:, None, None, None]
    gi = jnp.arange(G)[None, :, None, None]

    def gather_sel(idx):
        return slc_blocks[bi, gi, idx]

    win_pad = jnp.pad(kv_win, ((0, 0), (0, 0), (WINDOW, 0), (0, 0), (0, 0)))

    def block(i):
        start = i * Q_BLOCK
        qb = lax.dynamic_slice_in_dim(q, start, Q_BLOCK, axis=3)
        gb = lax.dynamic_slice_in_dim(gates, start, Q_BLOCK, axis=3)
        pos_q = start + jnp.arange(Q_BLOCK, dtype=jnp.int32)
        kw = lax.dynamic_slice_in_dim(win_pad, start, WINDOW + Q_BLOCK, axis=2)
        pos_w = start - WINDOW + jnp.arange(WINDOW + Q_BLOCK, dtype=jnp.int32)
        return nsa_core(qb, gb, pos_q, kv_cmp_c, n_key_blocks, gather_sel, kw, pos_w)

    o = lax.map(block, jnp.arange(T // Q_BLOCK, dtype=jnp.int32))
    return o.transpose(1, 0, 4, 2, 3, 5).reshape(B, T, G * R * HEAD_DIM)


def nsa_sample(q, gates, kv_cmp_new, kv_slc_new, kv_win_new, cache_kv_cmp, cache_kv_slc, state_kv_win,
               page_table, cmp_pos, w_cmp1, w_cmp2, b_cmp2):
    B, G, R, S, _ = q.shape
    n_pages = page_table.shape[1]
    past_len = n_pages * PAGE_SIZE
    n_phys = cache_kv_slc.shape[0]
    pos_q = past_len + jnp.arange(S, dtype=jnp.int32)
    past = cache_kv_cmp[page_table]
    past_chunks = past.reshape(B, n_pages, G, PAGE_SIZE // CMP_STRIDE, CMP_STRIDE, 2, HEAD_DIM)
    cp_past = chunk_proj(past_chunks, w_cmp1)
    cp_past = cp_past.transpose(0, 2, 1, 3, 4, 5, 6).reshape(B, G, -1, CMP_BLOCK // CMP_STRIDE, 2, CMP_HIDDEN)
    n_new_ch = S // CMP_STRIDE
    new_chunks = kv_cmp_new[:, :, :n_new_ch * CMP_STRIDE].reshape(B, G, n_new_ch, CMP_STRIDE, 2, HEAD_DIM)
    cp = jnp.concatenate([cp_past, chunk_proj(new_chunks, w_cmp1)], axis=2)
    kv_cmp_c = compress(cp, cmp_pos, w_cmp1, w_cmp2, b_cmp2)
    n_key_blocks = -(-(past_len + S) // SEL_BLOCK)
    n_past_blocks = past_len // SEL_BLOCK
    bpp = PAGE_SIZE // SEL_BLOCK
    pool = cache_kv_slc.reshape(n_phys, G, bpp, SEL_BLOCK, 2, HEAD_DIM)
    n_new_blocks = -(-S // SEL_BLOCK)
    new_blocks = jnp.pad(kv_slc_new, ((0, 0), (0, 0), (0, n_new_blocks * SEL_BLOCK - S), (0, 0), (0, 0)))
    new_blocks = new_blocks.reshape(B, G, n_new_blocks, SEL_BLOCK, 2, HEAD_DIM)
    bi = jnp.arange(B)[:, None, None, None]
    gi = jnp.arange(G)[None, :, None, None]

    def gather_sel(idx):
        jp = jnp.minimum(idx, n_past_blocks - 1)
        phys = page_table[bi, jp // bpp]
        from_past = pool[phys, gi, jp % bpp]
        jn = jnp.clip(idx - n_past_blocks, 0, n_new_blocks - 1)
        from_new = new_blocks[bi, gi, jn]
        return jnp.where((idx < n_past_blocks)[..., None, None, None], from_past, from_new)

    win_c = state_kv_win.shape[2]
    kv_w = jnp.concatenate([state_kv_win, kv_win_new], axis=2)
    pos_w = past_len - win_c + jnp.arange(win_c + S, dtype=jnp.int32)
    o = nsa_core(q, gates, pos_q, kv_cmp_c, n_key_blocks, gather_sel, kv_w, pos_w)
    o = o.transpose(0, 3, 1, 2, 4).reshape(B, S, G * R * HEAD_DIM)
    return o, kv_w[:, :, S:]


def ret_chunk(state, q, k, v, log_gamma):
    C = q.shape[1]
    qf, kf, vf = q.astype(jnp.float32), k.astype(jnp.float32), v.astype(jnp.float32)
    n = jnp.arange(C, dtype=jnp.float32)
    diff = n[:, None] - n[None, :]
    decay = jnp.exp(jnp.maximum(diff, 0.0)[None] * log_gamma[:, None, None]) * (diff >= 0)[None]
    inner = jnp.einsum('bnhd,bmhd->bhnm', qf, kf) * decay[None]
    o = jnp.einsum('bhnm,bmhe->bnhe', inner, vf)
    q_decay = jnp.exp((n + 1.0)[:, None] * log_gamma[None, :])[None, :, :, None]
    o = o + jnp.einsum('bnhd,bhde->bnhe', qf, state) * q_decay
    k_decay = jnp.exp((C - 1.0 - n)[:, None] * log_gamma[None, :])[None, :, :, None]
    new_state = jnp.exp(C * log_gamma)[None, :, None, None] * state + jnp.einsum('bmhd,bmhe->bhde', kf * k_decay, vf)
    return new_state, o


def retention_prompt(q, k, v, log_gamma):
    B, T, H, _ = q.shape
    C = min(RET_CHUNK, T)
    n_c = T // C

    def to_chunks(a):
        return a.reshape(B, n_c, C, H, a.shape[-1]).transpose(1, 0, 2, 3, 4)

    def step(state, xs):
        qc, kc, vc = xs
        return ret_chunk(state, qc, kc, vc, log_gamma)

    s0 = jnp.zeros((B, H, RET_DK, RET_DV), jnp.float32)
    s_fin, o = lax.scan(step, s0, (to_chunks(q), to_chunks(k), to_chunks(v)))
    return o.transpose(1, 0, 2, 3, 4).reshape(B, T, H, RET_DV), s_fin


def retention_output(o, g, gn_g, gn_b):
    B, T, H, dv = o.shape
    mu = jnp.mean(o, -1, keepdims=True)
    var = jnp.mean(jnp.square(o - mu), -1, keepdims=True)
    on = ((o - mu) * lax.rsqrt(var + LN_EPS)).reshape(B, T, H * dv)
    on = on * gn_g.astype(jnp.float32) + gn_b.astype(jnp.float32)
    return (jax.nn.silu(g.astype(jnp.float32)) * on).astype(g.dtype)


def merge_branches(o_nsa, o_ret, g_m, w_nsa_proj, w_ret_proj, w_o):
    gate_a, gate_b = jnp.split(jax.nn.sigmoid(g_m), 2, axis=-1)
    m = gate_a * jnp.einsum('btc,cd->btd', o_nsa, w_nsa_proj) + gate_b * jnp.einsum('btc,cd->btd', o_ret, w_ret_proj)
    return jnp.einsum('btd,de->bte', m, w_o)


def hier_moe(x, w_route_group, b_route_group, w_route_exp, b_route_exp, w_e1, w_e3, w_e2):
    shape = x.shape
    xt = x.reshape(-1, D_MODEL)
    n_tok = xt.shape[0]
    p_group = jax.nn.softmax((xt @ w_route_group + b_route_group).astype(jnp.float32), axis=-1)
    pg, grp = lax.top_k(p_group, 1)
    lg_exp = (xt @ w_route_exp + b_route_exp).astype(jnp.float32).reshape(n_tok, N_GROUPS, EXPERTS_PER_GROUP)
    lg_in = jnp.take_along_axis(lg_exp, grp[:, :, None], axis=1)[:, 0]
    w_top, e_loc = lax.top_k(jax.nn.softmax(lg_in, axis=-1), TOP_K)
    w_top = pg * w_top / jnp.sum(w_top, axis=-1, keepdims=True)
    e_glob = grp * EXPERTS_PER_GROUP + e_loc
    n_asg = n_tok * TOP_K
    flat_e = e_glob.reshape(-1)
    flat_w = w_top.reshape(-1)
    flat_tok = jnp.repeat(jnp.arange(n_tok, dtype=jnp.int32), TOP_K)
    order = jnp.argsort(flat_e)
    se, stok, sw = flat_e[order], flat_tok[order], flat_w[order]
    sizes = jnp.bincount(flat_e, length=N_EXPERTS)
    padded = (sizes + MOE_BLOCK - 1) // MOE_BLOCK * MOE_BLOCK
    pad_end = jnp.cumsum(padded)
    pad_start = pad_end - padded
    start = jnp.cumsum(sizes) - sizes
    dest = pad_start[se] + jnp.arange(n_asg, dtype=jnp.int32) - start[se]
    n_blocks = -(-n_asg // MOE_BLOCK) + N_EXPERTS
    n_slots = n_blocks * MOE_BLOCK
    slot_tok = jnp.full((n_slots,), n_tok, jnp.int32).at[dest].set(stok)
    slot_w = jnp.zeros((n_slots,), jnp.float32).at[dest].set(sw)
    block_e = jnp.minimum(jnp.searchsorted(pad_end, jnp.arange(n_blocks, dtype=jnp.int32) * MOE_BLOCK, side='right'), N_EXPERTS - 1)
    x_pad = jnp.concatenate([xt, jnp.zeros((1, D_MODEL), xt.dtype)], axis=0)
    xb = x_pad[slot_tok].reshape(n_blocks, MOE_BLOCK, D_MODEL)

    def expert_block(args):
        xi, e = args
        h = jax.nn.silu(xi @ w_e1[e]) * (xi @ w_e3[e])
        return h @ w_e2[e]

    yb = lax.map(expert_block, (xb, block_e))
    y = yb.reshape(n_slots, D_MODEL) * slot_w[:, None].astype(yb.dtype)
    out = jax.ops.segment_sum(y, slot_tok, num_segments=n_tok + 1)[:n_tok]
    return out.reshape(shape)


def setup_inputs(seed: int = 0) -> dict:
    key = jax.random.key(seed)
    ks = jax.random.split(key, 32)
    f32 = jnp.float32

    def nrm(k, shape, scale):
        return jax.random.normal(k, shape, f32) * scale

    n_pages = PAST_LEN // PAGE_SIZE
    n_used = DEC_BATCH * n_pages
    n_phys = n_used + max(1, n_used // 4)
    win_c = min(WINDOW, PAST_LEN)
    beta = DEEPNORM_BETA
    page_table = jax.random.permutation(ks[6], n_phys)[:n_used].reshape(DEC_BATCH, n_pages).astype(jnp.int32)
    return {
        'x_prompt': nrm(ks[0], (BATCH, SEQ, D_MODEL), 1.0),
        'x_sample': nrm(ks[1], (DEC_BATCH, DEC_SEQ, D_MODEL), 1.0),
        'cache_kv_cmp': nrm(ks[2], (n_phys, NSA_KV_GROUPS, PAGE_SIZE, 2, HEAD_DIM), 1.0),
        'cache_kv_slc': nrm(ks[3], (n_phys, NSA_KV_GROUPS, PAGE_SIZE, 2, HEAD_DIM), 1.0),
        'state_kv_win': nrm(ks[4], (DEC_BATCH, NSA_KV_GROUPS, win_c, 2, HEAD_DIM), 1.0),
        'state_ret': nrm(ks[5], (DEC_BATCH, RET_HEADS, RET_DK, RET_DV), 0.1),
        'page_table': page_table,
        'w_in': nrm(ks[7], (D_MODEL, IN_COLS), D_MODEL ** -0.5),
        'b_in': nrm(ks[8], (IN_COLS,), 0.02),
        'cmp_pos': nrm(ks[9], (2, CMP_BLOCK, HEAD_DIM), 0.5),
        'w_cmp1': nrm(ks[10], (2, CMP_BLOCK, HEAD_DIM, CMP_HIDDEN), (CMP_BLOCK * HEAD_DIM) ** -0.5),
        'w_cmp2': nrm(ks[11], (2, CMP_HIDDEN, HEAD_DIM), (2.0 / CMP_HIDDEN) ** 0.5),
        'b_cmp2': nrm(ks[12], (2, HEAD_DIM), 0.02),
        'ret_gn_g': 1.0 + nrm(ks[13], (RET_HEADS * RET_DV,), 0.02),
        'ret_gn_b': nrm(ks[14], (RET_HEADS * RET_DV,), 0.02),
        'w_nsa_proj': nrm(ks[15], (NSA_HEADS * HEAD_DIM, D_MODEL), (NSA_HEADS * HEAD_DIM) ** -0.5),
        'w_ret_proj': nrm(ks[16], (RET_HEADS * RET_DV, D_MODEL), (RET_HEADS * RET_DV) ** -0.5),
        'w_o': nrm(ks[17], (D_MODEL, D_MODEL), beta * D_MODEL ** -0.5),
        'ln1_g': 1.0 + nrm(ks[18], (D_MODEL,), 0.02),
        'ln1_b': nrm(ks[19], (D_MODEL,), 0.02),
        'w_route_group': nrm(ks[20], (D_MODEL, N_GROUPS), D_MODEL ** -0.5),
        'b_route_group': nrm(ks[21], (N_GROUPS,), 0.01),
        'w_route_exp': nrm(ks[22], (D_MODEL, N_EXPERTS), D_MODEL ** -0.5),
        'b_route_exp': nrm(ks[23], (N_EXPERTS,), 0.01),
        'w_e1': nrm(ks[24], (N_EXPERTS, D_MODEL, EXPERT_FF), D_MODEL ** -0.5),
        'w_e3': nrm(ks[25], (N_EXPERTS, D_MODEL, EXPERT_FF), D_MODEL ** -0.5),
        'w_e2': nrm(ks[26], (N_EXPERTS, EXPERT_FF, D_MODEL), beta * EXPERT_FF ** -0.5),
        'ln2_g': 1.0 + nrm(ks[27], (D_MODEL,), 0.02),
        'ln2_b': nrm(ks[28], (D_MODEL,), 0.02),
    }


def reference(x_prompt, x_sample, cache_kv_cmp, cache_kv_slc, state_kv_win, state_ret, page_table,
              w_in, b_in, cmp_pos, w_cmp1, w_cmp2, b_cmp2, ret_gn_g, ret_gn_b, w_nsa_proj, w_ret_proj, w_o,
              ln1_g, ln1_b, w_route_group, b_route_group, w_route_exp, b_route_exp, w_e1, w_e3, w_e2,
              ln2_g, ln2_b):
    log_gamma = jnp.log1p(-jnp.exp2(-5.0 - jnp.arange(RET_HEADS, dtype=jnp.float32)))

    def finish(x, mix):
        x1 = layer_norm(DEEPNORM_ALPHA * x + mix, ln1_g, ln1_b)
        ff = hier_moe(x1, w_route_group, b_route_group, w_route_exp, b_route_exp, w_e1, w_e3, w_e2)
        return layer_norm(DEEPNORM_ALPHA * x1 + ff, ln2_g, ln2_b)

    T = x_prompt.shape[1]
    pos_p = jnp.arange(T, dtype=jnp.int32)
    q_a, g_a, kv_cmp_p, kv_slc_p, kv_win_p, q_b, k_b, v_b, g_b, g_m = project_inputs(x_prompt, pos_p, w_in, b_in)
    o_nsa = nsa_prompt(q_a, g_a, kv_cmp_p, kv_slc_p, kv_win_p, cmp_pos, w_cmp1, w_cmp2, b_cmp2)
    o_ret, ret_p = retention_prompt(q_b, k_b, v_b, log_gamma)
    mix = merge_branches(o_nsa, retention_output(o_ret, g_b, ret_gn_g, ret_gn_b), g_m, w_nsa_proj, w_ret_proj, w_o)
    y_prompt = finish(x_prompt, mix)
    win_p = kv_win_p[:, :, T - min(WINDOW, T):]
    ret_p = ret_p.astype(x_prompt.dtype)

    S = x_sample.shape[1]
    past_len = page_table.shape[1] * PAGE_SIZE
    pos_s = past_len + jnp.arange(S, dtype=jnp.int32)
    q_a, g_a, kv_cmp_s, kv_slc_s, kv_win_s, q_b, k_b, v_b, g_b, g_m = project_inputs(x_sample, pos_s, w_in, b_in)
    o_nsa, win_s = nsa_sample(q_a, g_a, kv_cmp_s, kv_slc_s, kv_win_s, cache_kv_cmp, cache_kv_slc, state_kv_win,
                              page_table, cmp_pos, w_cmp1, w_cmp2, b_cmp2)
    ret_s, o_ret = ret_chunk(state_ret.astype(jnp.float32), q_b, k_b, v_b, log_gamma)
    mix = merge_branches(o_nsa, retention_output(o_ret, g_b, ret_gn_g, ret_gn_b), g_m, w_nsa_proj, w_ret_proj, w_o)
    y_sample = finish(x_sample, mix)
    ret_s = ret_s.astype(state_ret.dtype)
    return (y_prompt, y_sample, kv_cmp_p, kv_slc_p, win_p, ret_p, kv_cmp_s, kv_slc_s, win_s, ret_s)
```

```python
import functools

import numpy as np
import jax
import jax.numpy as jnp
from jax import lax
from jax.experimental import pallas as pl
from jax.experimental.pallas import tpu as pltpu

F32 = jnp.float32
BF16 = jnp.bfloat16

D_MODEL = 1024
DEPTH = 1
PAGE_SIZE = 128
NSA_HEADS = 8
NSA_KV_GROUPS = 2
NSA_GROUP_HEADS = NSA_HEADS // NSA_KV_GROUPS
HEAD_DIM = 64
CMP_BLOCK = 32
CMP_STRIDE = 16
CMP_HIDDEN = 64
SEL_BLOCK = 64
N_SEL = 16
WINDOW = 512
Q_TILE = 128
FORCE_SCORE = 1.0e3
NEG_INF = -1.0e30
RET_HEADS = 4
RET_DK = 128
RET_DV = 128
RET_CHUNK = 128
ROPE_BASE = 10000.0
N_GROUPS = 4
EXPERTS_PER_GROUP = 8
N_EXPERTS = N_GROUPS * EXPERTS_PER_GROUP
TOP_K = 2
EXPERT_FF = 512
LN_EPS = 1e-5
DEEPNORM_ALPHA = (2.0 * DEPTH) ** 0.25

NSA_Q_COLS = NSA_HEADS * HEAD_DIM
NSA_KV_COLS = 3 * 2 * NSA_KV_GROUPS * HEAD_DIM
NSA_GATE_COLS = 3 * NSA_HEADS
RET_QK_COLS = RET_HEADS * RET_DK
RET_V_COLS = RET_HEADS * RET_DV

LANES = 128

P_QA = 0
P_GA = P_QA + NSA_Q_COLS
P_QB = P_GA + NSA_KV_GROUPS * LANES
P_KB = P_QB + RET_QK_COLS
P_VB = P_KB + RET_QK_COLS
P_GB = P_VB + RET_V_COLS
P_GM = P_GB + RET_V_COLS
VMEM_LIMIT = 56 * 1024 * 1024


def _cparams(sem):
    return pltpu.CompilerParams(dimension_semantics=sem, vmem_limit_bytes=VMEM_LIMIT)


def _proj_kernel(x_ref, w_ref, b_ref, wkv_ref, bkv_ref, cos_ref, sin_ref,
                 qa_ref, kvt_ref, ga_ref, qb_ref, kb_ref, vb_ref, gb_ref, gm_ref):
    x = x_ref[...].astype(BF16)

    def cols(c0, n):
        h = jnp.dot(x, w_ref[:, c0:c0 + n], preferred_element_type=F32)
        return h + b_ref[:, c0:c0 + n]

    qa_ref[...] = cols(P_QA, 512) * (HEAD_DIM ** -0.5)
    ga_ref[...] = jax.nn.sigmoid(cols(P_GA, NSA_KV_GROUPS * LANES))
    cos = cos_ref[...]
    sin = sin_ref[...]
    for h in range(RET_HEADS):
        q = cols(P_QB + h * RET_DK, RET_DK)
        qb_ref[:, h * RET_DK:(h + 1) * RET_DK] = (q * cos + pltpu.roll(q, RET_DK // 2, 1) * sin).astype(BF16)
        k = cols(P_KB + h * RET_DK, RET_DK)
        kb_ref[:, h * RET_DK:(h + 1) * RET_DK] = (k * cos + pltpu.roll(k, RET_DK // 2, 1) * sin) * (RET_DK ** -0.5)
    vb_ref[...] = cols(P_VB, 512).astype(BF16)
    gb_ref[...] = cols(P_GB, 512)
    gm_ref[...] = cols(P_GM, 2048)
    kvt = lax.dot_general(wkv_ref[...], x, (((1,), (1,)), ((), ())), preferred_element_type=F32)
    kvt = kvt + bkv_ref[...]
    for i in range(6):
        kvt_ref[i // 2, 0, i % 2] = kvt[i * LANES:(i + 1) * LANES]


def _proj_weights(w_in, b_in):
    o_kv = NSA_Q_COLS
    o_ga = o_kv + NSA_KV_COLS
    o_qb = o_ga + NSA_GATE_COLS
    per_g = NSA_GATE_COLS // NSA_KV_GROUPS
    ga_cols = []
    for g in range(NSA_KV_GROUPS):
        ga_cols += [np.arange(o_ga + g * per_g, o_ga + (g + 1) * per_g), np.full(LANES - per_g, -1)]
    perm = np.concatenate([
        np.arange(0, NSA_Q_COLS),
        *ga_cols,
        np.arange(o_qb, o_qb + 2 * RET_QK_COLS + 2 * RET_V_COLS + 2 * D_MODEL),
    ])
    keep = jnp.asarray(perm >= 0)
    idx = jnp.asarray(np.maximum(perm, 0))
    w = jnp.where(keep[None, :], w_in[:, idx], 0.0).astype(BF16)
    b = jnp.where(keep, b_in[idx], 0.0)[None, :]
    br, g, kv, d = np.meshgrid(np.arange(3), np.arange(NSA_KV_GROUPS), np.arange(2), np.arange(HEAD_DIM), indexing="ij")
    kv_idx = jnp.asarray((o_kv + br * 256 + kv * 128 + g * 64 + d).reshape(-1))
    wkv = w_in[:, kv_idx].T.astype(BF16)
    bkv = b_in[kv_idx][:, None]
    return w, b, wkv, bkv


def _rope_tables(pos):
    half = RET_DK // 2
    inv = ROPE_BASE ** (-jnp.arange(half, dtype=F32) / half)
    ang = pos.astype(F32)[:, None] * inv[None, :]
    cos, sin = jnp.cos(ang), jnp.sin(ang)
    return jnp.concatenate([cos, cos], -1), jnp.concatenate([-sin, sin], -1)


def _project(x2, n_batch, pw, cos_t, sin_t, tm):
    w, b, wkv, bkv = pw
    n = x2.shape[0]
    t_len = n // n_batch
    nt = t_len // tm
    n_pos_blocks = cos_t.shape[0] // tm
    row = lambda i: (i, 0)
    const = lambda i: (0, 0)
    pos_map = lambda i: (i % n_pos_blocks, 0)
    out_shape = (
        jax.ShapeDtypeStruct((n, 512), F32),
        jax.ShapeDtypeStruct((3, n_batch, NSA_KV_GROUPS, LANES, t_len), F32),
        jax.ShapeDtypeStruct((n, NSA_KV_GROUPS * LANES), F32),
        jax.ShapeDtypeStruct((n, 512), BF16),
        jax.ShapeDtypeStruct((n, 512), F32),
        jax.ShapeDtypeStruct((n, 512), BF16),
        jax.ShapeDtypeStruct((n, 512), F32),
        jax.ShapeDtypeStruct((n, 2048), F32),
    )
    return pl.pallas_call(
        _proj_kernel,
        grid=(n // tm,),
        in_specs=[
            pl.BlockSpec((tm, D_MODEL), row),
            pl.BlockSpec(w.shape, const),
            pl.BlockSpec(b.shape, const),
            pl.BlockSpec(wkv.shape, const),
            pl.BlockSpec(bkv.shape, const),
            pl.BlockSpec((tm, LANES), pos_map),
            pl.BlockSpec((tm, LANES), pos_map),
        ],
        out_specs=(
            pl.BlockSpec((tm, 512), row),
            pl.BlockSpec((3, 1, NSA_KV_GROUPS, LANES, tm), lambda i: (0, i // nt, 0, 0, i % nt)),
            pl.BlockSpec((tm, NSA_KV_GROUPS * LANES), row),
            pl.BlockSpec((tm, 512), row),
            pl.BlockSpec((tm, 512), row),
            pl.BlockSpec((tm, 512), row),
            pl.BlockSpec((tm, 512), row),
            pl.BlockSpec((tm, 2048), row),
        ),
        out_shape=out_shape,
        compiler_params=_cparams(("parallel",)),
        name="proj",
    )(x2, w, b, wkv, bkv, cos_t, sin_t)


XPOSE_CHUNK = 512


def _compress_rows(xs_ref, wc_ref, w2_ref, b2_ref, pos_ref, wpe_ref, kc_ref, kvct_ref, nj):
    pe = jnp.dot(pos_ref[...], wpe_ref[...], preferred_element_type=F32)[0:1]
    ys = []
    for rr in range(4):
        acc = jnp.zeros((nj, 2 * LANES), F32)
        for c in range(CMP_STRIDE):
            xc = xs_ref[pl.ds(CMP_STRIDE * rr + c, nj, stride=SEL_BLOCK), :].astype(BF16)
            acc = acc + jnp.dot(xc, wc_ref[c], preferred_element_type=F32)
        ys.append(acc)
    row = lax.broadcasted_iota(jnp.int32, (nj, LANES), 0)
    for rr in range(4):
        first = ys[rr][:, :LANES]
        if rr < 3:
            second = ys[rr + 1][:, LANES:]
        else:
            nxt = pltpu.roll(ys[0][:, LANES:], nj - 1, 0)
            second = jnp.where(row == nj - 1, 0.0, nxt)
        h = jax.nn.gelu(first + second + pe)
        kv = jnp.dot(h.astype(BF16), w2_ref[...], preferred_element_type=F32) + b2_ref[...]
        lane = lax.broadcasted_iota(jnp.int32, (nj, LANES), 1)
        kc_ref[rr * nj:(rr + 1) * nj, :] = jnp.where(lane < HEAD_DIM, kv, 0.0).astype(BF16)
        kvct_ref[:, rr * nj:(rr + 1) * nj] = kv.T.astype(BF16)


def _compress_kernel(xt_ref, wc_ref, w2_ref, b2_ref, pos_ref, wpe_ref, kc_ref, kvct_ref, xs_ref, *, length):
    def xpose(i, _):
        t0 = pl.multiple_of(i * XPOSE_CHUNK, XPOSE_CHUNK)
        xs_ref[pl.ds(t0, XPOSE_CHUNK), :] = xt_ref[0, 0, :, pl.ds(t0, XPOSE_CHUNK)].T
        return 0

    lax.fori_loop(0, length // XPOSE_CHUNK, xpose, 0)
    _compress_rows(xs_ref, wc_ref, w2_ref, b2_ref, pos_ref, wpe_ref, kc_ref.at[0, 0], kvct_ref.at[0, 0],
                   length // SEL_BLOCK)


PAGES_PER_STEP = 8


def _compress_paged_kernel(pt_ref, *refs, n_pages):
    page_refs = refs[:PAGES_PER_STEP]
    wc_ref, w2_ref, b2_ref, pos_ref, wpe_ref, kc_ref, kvct_ref, xs_ref = refs[PAGES_PER_STEP:]
    s = pl.program_id(1)
    for k in range(PAGES_PER_STEP):
        r0 = pl.multiple_of((s * PAGES_PER_STEP + k) * PAGE_SIZE, PAGE_SIZE)
        for g in range(NSA_KV_GROUPS):
            xs_ref[g, pl.ds(r0, PAGE_SIZE), :] = page_refs[k][0, g].T

    @pl.when(s == n_pages // PAGES_PER_STEP - 1)
    def _():
        for g in range(NSA_KV_GROUPS):
            _compress_rows(xs_ref.at[g], wc_ref, w2_ref, b2_ref, pos_ref, wpe_ref, kc_ref.at[0, g], kvct_ref.at[0, g],
                           n_pages * PAGE_SIZE // SEL_BLOCK)


def _compress_paged(cache_t, page_table, cw):
    wc, w2, b2, pos, wpe = cw
    nb, n_pages = page_table.shape
    g = cache_t.shape[1]
    length = n_pages * PAGE_SIZE
    n_rows = length // CMP_STRIDE
    steps = n_pages // PAGES_PER_STEP
    page_spec = lambda k: pl.BlockSpec(
        (1, g, LANES, PAGE_SIZE), lambda b, s, pt: (pt[b * n_pages + s * PAGES_PER_STEP + k], 0, 0, 0))
    const = lambda a: pl.BlockSpec(a.shape, lambda b, s, pt: (0,) * a.ndim)
    return pl.pallas_call(
        functools.partial(_compress_paged_kernel, n_pages=n_pages),
        grid_spec=pltpu.PrefetchScalarGridSpec(
            num_scalar_prefetch=1,
            grid=(nb, steps),
            in_specs=[page_spec(k) for k in range(PAGES_PER_STEP)] + [const(a) for a in (wc, w2, b2, pos, wpe)],
            out_specs=(
                pl.BlockSpec((1, g, n_rows, LANES), lambda b, s, pt: (b, 0, 0, 0)),
                pl.BlockSpec((1, g, LANES, n_rows), lambda b, s, pt: (b, 0, 0, 0)),
            ),
            scratch_shapes=[pltpu.VMEM((g, length, LANES), F32)],
        ),
        out_shape=(
            jax.ShapeDtypeStruct((nb, g, n_rows, LANES), BF16),
            jax.ShapeDtypeStruct((nb, g, LANES, n_rows), BF16),
        ),
        compiler_params=_cparams(("parallel", "arbitrary")),
        name="compress_paged",
    )(page_table.reshape(-1), *([cache_t] * PAGES_PER_STEP), wc, w2, b2, pos, wpe)


def _compress_weights(cmp_pos, w_cmp1, w_cmp2, b_cmp2):
    eye = jnp.eye(2, dtype=F32)
    w5 = w_cmp1.reshape(2, 2, CMP_STRIDE, HEAD_DIM, CMP_HIDDEN)
    wc = jnp.einsum("vhcde,vw->cvdhwe", w5, eye).reshape(CMP_STRIDE, LANES, 2 * LANES).astype(BF16)
    w2 = jnp.einsum("ved,vw->vewd", w_cmp2, eye).reshape(LANES, LANES).astype(BF16)
    b2 = b_cmp2.reshape(1, LANES)
    wpe = jnp.einsum("vcde,vw->vcdwe", w_cmp1, eye).reshape(2 * CMP_BLOCK * HEAD_DIM, LANES).astype(BF16)
    pos = jnp.broadcast_to(cmp_pos.reshape(1, -1), (8, 2 * CMP_BLOCK * HEAD_DIM)).astype(BF16)
    return wc, w2, b2, pos, wpe


def _compress(kvt, cw):
    wc, w2, b2, pos, wpe = cw
    nb, g, _, length = kvt.shape
    n_rows = length // CMP_STRIDE
    const2 = lambda b, gg: (0, 0)
    return pl.pallas_call(
        functools.partial(_compress_kernel, length=length),
        grid=(nb, g),
        in_specs=[
            pl.BlockSpec((1, 1, LANES, length), lambda b, gg: (b, gg, 0, 0)),
            pl.BlockSpec(wc.shape, lambda b, gg: (0, 0, 0)),
            pl.BlockSpec(w2.shape, const2),
            pl.BlockSpec(b2.shape, const2),
            pl.BlockSpec(pos.shape, const2),
            pl.BlockSpec(wpe.shape, const2),
        ],
        out_specs=(
            pl.BlockSpec((1, 1, n_rows, LANES), lambda b, gg: (b, gg, 0, 0)),
            pl.BlockSpec((1, 1, LANES, n_rows), lambda b, gg: (b, gg, 0, 0)),
        ),
        out_shape=(
            jax.ShapeDtypeStruct((nb, g, n_rows, LANES), BF16),
            jax.ShapeDtypeStruct((nb, g, LANES, n_rows), BF16),
        ),
        scratch_shapes=[pltpu.VMEM((length, LANES), F32)],
        compiler_params=_cparams(("parallel", "parallel")),
        name="compress",
    )(kvt, wc, w2, b2, pos, wpe)


BIAS_BLOCKS = 16
KEY_TILE = 512
PREP_TILE = 1024
N_COLS = NSA_GROUP_HEADS * Q_TILE


def _kv_prep_kernel(slc_ref, win_ref, kslc_ref, kwin_ref, vslc_ref, vwin_ref):
    shape = (PREP_TILE, LANES)
    lane = lax.broadcasted_iota(jnp.int32, shape, 1)
    blk = (lax.broadcasted_iota(jnp.int32, shape, 0) // SEL_BLOCK) % BIAS_BLOCKS
    onehot = jnp.where(lane - HEAD_DIM == blk, 1.0, 0.0)
    ks = slc_ref[0, 0, 0].T
    kslc_ref[0, 0] = jnp.where(lane < HEAD_DIM, ks, onehot).astype(BF16)
    kw = win_ref[0, 0, 0].T
    kwin_ref[0, 0] = jnp.where(lane < HEAD_DIM, kw, 0.0).astype(BF16)
    vslc_ref[0, 0] = slc_ref[0, 0, 0, HEAD_DIM:, :].astype(BF16)
    vwin_ref[0, 0] = win_ref[0, 0, 0, HEAD_DIM:, :].astype(BF16)


def _kv_prep(kvt):
    _, nb, g, _, t_len = kvt.shape
    in_map = lambda br: (lambda b, gg, i: (br, b, gg, 0, i))
    k_spec = pl.BlockSpec((1, 1, PREP_TILE, LANES), lambda b, gg, i: (b, gg, i, 0))
    v_spec = pl.BlockSpec((1, 1, HEAD_DIM, PREP_TILE), lambda b, gg, i: (b, gg, 0, i))
    k_shape = jax.ShapeDtypeStruct((nb, g, t_len, LANES), BF16)
    v_shape = jax.ShapeDtypeStruct((nb, g, HEAD_DIM, t_len), BF16)
    return pl.pallas_call(
        _kv_prep_kernel,
        grid=(nb, g, t_len // PREP_TILE),
        in_specs=[pl.BlockSpec((1, 1, 1, LANES, PREP_TILE), in_map(1)),
                  pl.BlockSpec((1, 1, 1, LANES, PREP_TILE), in_map(2))],
        out_specs=(k_spec, k_spec, v_spec, v_spec),
        out_shape=(k_shape, k_shape, v_shape, v_shape),
        compiler_params=_cparams(("parallel", "parallel", "parallel")),
        name="kv_prep",
    )(kvt, kvt)


def _online_update(carry, s, vt):
    m, l, acc = carry
    m_new = jnp.maximum(m, jnp.max(s, axis=0, keepdims=True))
    p = jnp.exp(s - m_new)
    alpha = jnp.exp(m - m_new)
    l = alpha * l + jnp.sum(p, axis=0, keepdims=True)
    acc = alpha * acc + jnp.dot(vt, p.astype(BF16), preferred_element_type=F32)
    return m_new, l, acc


def _nsa_prompt_kernel(qa_ref, ga_ref, kc_ref, kvct_ref, kslc_ref, vslc_ref, kwin_ref, vwin_ref,
                       o_ref, qm_ref, bias_ref, *, nj):
    qi = pl.program_id(2)

    @pl.when(qi == 0)
    def _():
        qm_ref[...] = jnp.zeros_like(qm_ref)

    q_t = qa_ref[...].T
    for r in range(NSA_GROUP_HEADS):
        qm_ref[0:HEAD_DIM, r * Q_TILE:(r + 1) * Q_TILE] = q_t[r * HEAD_DIM:(r + 1) * HEAD_DIM].astype(BF16)
    ga_t = ga_ref[...].T

    def gate(branch):
        return jnp.concatenate([ga_t[r * 3 + branch:r * 3 + branch + 1] for r in range(NSA_GROUP_HEADS)], axis=1)

    q0 = qi * Q_TILE
    qpos_c = q0 + (lax.broadcasted_iota(jnp.int32, (1, N_COLS), 1) & (Q_TILE - 1))

    n_cmp_rows = 4 * nj
    s = jnp.dot(kc_ref[0, 0], qm_ref[...], preferred_element_type=F32)
    row = lax.broadcasted_iota(jnp.int32, (n_cmp_rows, N_COLS), 0)
    cmp_end = (row % nj) * SEL_BLOCK + (row // nj) * CMP_STRIDE + (CMP_BLOCK - 1)
    vis = cmp_end <= qpos_c
    s = jnp.where(vis, s, NEG_INF)
    m = jnp.max(s, axis=0, keepdims=True)
    p = jnp.where(vis, jnp.exp(s - m), 0.0)
    l = jnp.sum(p, axis=0, keepdims=True)
    pn = p / jnp.where(l > 0.0, l, 1.0)
    o_cmp = jnp.dot(kvct_ref[0, 0, HEAD_DIM:, :], pn.astype(BF16), preferred_element_type=F32)

    p3 = pn[3 * nj:4 * nj]
    jrow4 = lax.broadcasted_iota(jnp.int32, (nj, N_COLS), 0)
    prev = jnp.where(jrow4 == 0, 0.0, pltpu.roll(p3, 1, 0))
    a = pn[0:nj] + pn[nj:2 * nj] + pn[2 * nj:3 * nj] + p3 + prev
    imp = a[:, 0:Q_TILE]
    for r in range(1, NSA_GROUP_HEADS):
        imp = imp + a[:, r * Q_TILE:(r + 1) * Q_TILE]
    jrow = lax.broadcasted_iota(jnp.int32, (nj, Q_TILE), 0)
    cur = (q0 + lax.broadcasted_iota(jnp.int32, (nj, Q_TILE), 1)) // SEL_BLOCK
    work = jnp.where(jrow <= cur, imp, -1.0)
    for forced_blk in (0, cur, cur - 1):
        work = jnp.where(jrow == forced_blk, FORCE_SCORE, work)
    jf = jrow.astype(F32)
    bias = jnp.full((nj, Q_TILE), NEG_INF, F32)
    for _ in range(N_SEL):
        mx = jnp.max(work, axis=0, keepdims=True)
        first = jnp.min(jnp.where(work == mx, jf, 1.0e9), axis=0, keepdims=True)
        hit = jf == first
        bias = jnp.where(hit, 0.0, bias)
        work = jnp.where(hit, -3.0e38, work)
    bias16 = bias.astype(BF16)
    for r in range(NSA_GROUP_HEADS):
        bias_ref[:, r * Q_TILE:(r + 1) * Q_TILE] = bias16

    init = (jnp.full((1, N_COLS), -jnp.inf, F32), jnp.zeros((1, N_COLS), F32), jnp.zeros((HEAD_DIM, N_COLS), F32))

    def slc_step(st, carry):
        b0 = pl.multiple_of((st * KEY_TILE // (SEL_BLOCK * BIAS_BLOCKS)) * BIAS_BLOCKS, BIAS_BLOCKS)
        qm_ref[HEAD_DIM:HEAD_DIM + BIAS_BLOCKS, :] = bias_ref[pl.ds(b0, BIAS_BLOCKS), :]
        k0 = pl.multiple_of(st * KEY_TILE, KEY_TILE)
        sc = jnp.dot(kslc_ref[0, 0, pl.ds(k0, KEY_TILE), :], qm_ref[...], preferred_element_type=F32)
        kpos = k0 + lax.broadcasted_iota(jnp.int32, (KEY_TILE, N_COLS), 0)
        sc = jnp.where(kpos <= qpos_c, sc, NEG_INF)
        return _online_update(carry, sc, vslc_ref[0, 0, :, pl.ds(k0, KEY_TILE)])

    n_slc = (q0 + Q_TILE - 1) // KEY_TILE + 1
    _, l_s, acc_s = lax.fori_loop(0, n_slc, slc_step, init)
    o_slc = acc_s / l_s

    qm = qm_ref[...]

    def win_step(tw, carry):
        k0 = pl.multiple_of(tw * Q_TILE, Q_TILE)
        sc = jnp.dot(kwin_ref[0, 0, pl.ds(k0, Q_TILE), :], qm, preferred_element_type=F32)
        dpos = qpos_c - (k0 + lax.broadcasted_iota(jnp.int32, (Q_TILE, N_COLS), 0))
        sc = jnp.where(dpos >= 0, jnp.where(dpos < WINDOW, sc, NEG_INF), NEG_INF)
        return _online_update(carry, sc, vwin_ref[0, 0, :, pl.ds(k0, Q_TILE)])

    _, l_w, acc_w = lax.fori_loop(jnp.maximum(qi - WINDOW // Q_TILE, 0), qi + 1, win_step, init)
    o_win = acc_w / l_w

    o = gate(0) * o_cmp + gate(1) * o_slc + gate(2) * o_win
    o_rows = jnp.concatenate([o[:, r * Q_TILE:(r + 1) * Q_TILE] for r in range(NSA_GROUP_HEADS)], axis=0)
    o_ref[...] = o_rows.T.astype(BF16)


def _nsa_prompt(qa, ga, kc, kvct, kslc, vslc, kwin, vwin):
    nb, g, t_len, _ = kslc.shape
    nq = t_len // Q_TILE
    nj = t_len // SEL_BLOCK
    n_cmp_rows = kc.shape[2]
    assert nj % BIAS_BLOCKS == 0 and n_cmp_rows == 4 * nj and t_len % KEY_TILE == 0
    tile = lambda b, gg, i: (b * nq + i, gg)
    whole = lambda b, gg, i: (b, gg, 0, 0)
    return pl.pallas_call(
        functools.partial(_nsa_prompt_kernel, nj=nj),
        grid=(nb, g, nq),
        in_specs=[
            pl.BlockSpec((Q_TILE, NSA_GROUP_HEADS * HEAD_DIM), tile),
            pl.BlockSpec((Q_TILE, LANES), tile),
            pl.BlockSpec((1, 1, n_cmp_rows, LANES), whole),
            pl.BlockSpec((1, 1, LANES, n_cmp_rows), whole),
            pl.BlockSpec((1, 1, t_len, LANES), whole),
            pl.BlockSpec((1, 1, HEAD_DIM, t_len), whole),
            pl.BlockSpec((1, 1, t_len, LANES), whole),
            pl.BlockSpec((1, 1, HEAD_DIM, t_len), whole),
        ],
        out_specs=pl.BlockSpec((Q_TILE, NSA_GROUP_HEADS * HEAD_DIM), tile),
        out_shape=jax.ShapeDtypeStruct((nb * t_len, NSA_HEADS * HEAD_DIM), BF16),
        scratch_shapes=[pltpu.VMEM((LANES, N_COLS), BF16), pltpu.VMEM((nj, N_COLS), BF16)],
        compiler_params=_cparams(("parallel", "parallel", "arbitrary")),
        name="nsa_prompt",
    )(qa, ga, kc, kvct, kslc, vslc, kwin, vwin)


HEAD_ROWS = 8
N_PAST_SEL = N_SEL - 1


def _softmax_lanes(s_list, mask_list):
    s_list = [jnp.where(mk, s, NEG_INF) for s, mk in zip(s_list, mask_list)]
    m = s_list[0].max(axis=-1, keepdims=True)
    for s in s_list[1:]:
        m = jnp.maximum(m, s.max(axis=-1, keepdims=True))
    p_list = [jnp.where(mk, jnp.exp(s - m), 0.0) for s, mk in zip(s_list, mask_list)]
    l = p_list[0].sum(axis=-1, keepdims=True)
    for p in p_list[1:]:
        l = l + p.sum(axis=-1, keepdims=True)
    inv = 1.0 / jnp.where(l > 0.0, l, 1.0)
    return [p * inv for p in p_list]


def _dot_nt(a, b):
    return lax.dot_general(a, b, (((1,), (1,)), ((), ())), preferred_element_type=F32)


def _dec_cmp_kernel(q_ref, kvct_ref, o_ref, imp_ref, *, nj):
    q = q_ref[0, 0]
    n = 4 * nj
    s = jnp.dot(q, kvct_ref[0, 0, :HEAD_DIM, :], preferred_element_type=F32)
    lane = lax.broadcasted_iota(jnp.int32, (HEAD_ROWS, n), 1)
    blk = 4 * (lane % nj) + lane // nj
    (pn,) = _softmax_lanes([s], [blk <= n - 2])
    o_ref[0, 0] = _dot_nt(pn.astype(BF16), kvct_ref[0, 0, HEAD_DIM:, :])
    row = lax.broadcasted_iota(jnp.int32, (HEAD_ROWS, n), 0)
    ph = jnp.sum(jnp.where(row < NSA_GROUP_HEADS, pn, 0.0), axis=0, keepdims=True)
    p3 = ph[:, 3 * nj:]
    prev = jnp.where(lane[0:1, :nj] == 0, 0.0, pltpu.roll(p3, 1, 1))
    imp = ph[:, :nj] + ph[:, nj:2 * nj] + ph[:, 2 * nj:3 * nj] + p3 + prev
    imp_ref[0, 0] = jnp.broadcast_to(imp, (HEAD_ROWS, nj))


def _dec_cmp(q8, kvct):
    nb, g, _, n = kvct.shape
    nj = n // 4
    blk = lambda *shape: pl.BlockSpec((1, 1) + shape, lambda b, gg: (b, gg, 0, 0))
    return pl.pallas_call(
        functools.partial(_dec_cmp_kernel, nj=nj),
        grid=(nb, g),
        in_specs=[blk(HEAD_ROWS, HEAD_DIM), blk(LANES, n)],
        out_specs=(blk(HEAD_ROWS, HEAD_DIM), blk(HEAD_ROWS, nj)),
        out_shape=(jax.ShapeDtypeStruct((nb, g, HEAD_ROWS, HEAD_DIM), F32),
                   jax.ShapeDtypeStruct((nb, g, HEAD_ROWS, nj), F32)),
        compiler_params=_cparams(("parallel", "parallel")),
        name="dec_cmp",
    )(q8, kvct)


def _dec_topk_kernel(imp_ref, idx_ref):
    work = imp_ref[...].T
    nj, n = work.shape
    jrow = lax.broadcasted_iota(jnp.int32, (nj, n), 0)
    jf = jrow.astype(F32)
    for forced_blk in (0, nj - 1):
        work = jnp.where(jrow == forced_blk, FORCE_SCORE, work)
    picks = []
    for _ in range(N_PAST_SEL):
        mx = jnp.max(work, axis=0, keepdims=True)
        first = jnp.min(jnp.where(work == mx, jf, 1.0e9), axis=0, keepdims=True)
        picks.append(first)
        work = jnp.where(jf == first, -3.0e38, work)
    picks.append(jnp.full((1, n), float(nj), F32))
    idx_ref[...] = jnp.concatenate(picks, axis=0)


def _dec_topk(imp2):
    n, nj = imp2.shape
    return pl.pallas_call(
        _dec_topk_kernel,
        out_shape=jax.ShapeDtypeStruct((N_SEL, n), F32),
        compiler_params=pltpu.CompilerParams(vmem_limit_bytes=VMEM_LIMIT),
        name="dec_topk",
    )(imp2)


def _dec_attn_kernel(phys_ref, half_ref, q_ref, gate_ref, ocmp_ref, *refs):
    page_refs = refs[:N_PAST_SEL]
    slcnew_ref, winnew_ref, win_ref, o_ref, winout_ref = refs[N_PAST_SEL:]
    b = pl.program_id(0)
    base = (b * pl.num_programs(1) + pl.program_id(1)) * N_PAST_SEL
    q = q_ref[0, 0]
    lane = lax.broadcasted_iota(jnp.int32, (HEAD_ROWS, LANES), 1)
    own = lane == b

    def attend(kv_blocks, masks):
        scores = [jnp.dot(q, kv[:HEAD_DIM].astype(BF16), preferred_element_type=F32) for kv in kv_blocks]
        probs = _softmax_lanes(scores, masks)
        out = jnp.zeros((HEAD_ROWS, HEAD_DIM), F32)
        for p, kv in zip(probs, kv_blocks):
            out = out + _dot_nt(p.astype(BF16), kv[HEAD_DIM:].astype(BF16))
        return out

    pages = [r[0, 0] for r in page_refs] + [slcnew_ref[0, 0, 0]]
    masks = [(lane // SEL_BLOCK) == half_ref[base + k] for k in range(N_PAST_SEL)] + [own]
    o_slc = attend(pages, masks)
    st = win_ref[0, 0]
    win_c = st.shape[1]
    wlane = lax.broadcasted_iota(jnp.int32, (HEAD_ROWS, win_c), 1)
    o_win = attend([st, winnew_ref[0, 0, 0]], [wlane > win_c - WINDOW, own])
    gates = gate_ref[0, 0]
    o_ref[0, 0] = gates[:, 0:1] * ocmp_ref[0, 0] + gates[:, 1:2] * o_slc + gates[:, 2:3] * o_win
    new = winnew_ref[0, 0, 0]
    lane_full = lax.broadcasted_iota(jnp.int32, new.shape, 1)
    new_col = jnp.sum(jnp.where(lane_full == b, new, 0.0), axis=1, keepdims=True)
    col = lax.broadcasted_iota(jnp.int32, st.shape, 1)
    winout_ref[0, 0] = jnp.where(col == win_c - 1, new_col, pltpu.roll(st, win_c - 1, 1))


def _dec_attn(phys, half, q8, gates, o_cmp, slc_t, kvt_new, win_t):
    nb, g, _, win_c = win_t.shape
    blk = lambda *shape: pl.BlockSpec((1, 1) + shape, lambda b, gg, ph, hf: (b, gg, 0, 0))
    page = lambda k: pl.BlockSpec((1, 1, LANES, PAGE_SIZE),
                                  lambda b, gg, ph, hf: (ph[(b * g + gg) * N_PAST_SEL + k], gg, 0, 0))
    new = lambda br: pl.BlockSpec((1, 1, 1, LANES, kvt_new.shape[-1]), lambda b, gg, ph, hf: (br, 0, gg, 0, 0))
    return pl.pallas_call(
        _dec_attn_kernel,
        grid_spec=pltpu.PrefetchScalarGridSpec(
            num_scalar_prefetch=2,
            grid=(nb, g),
            in_specs=[blk(HEAD_ROWS, HEAD_DIM), blk(HEAD_ROWS, LANES), blk(HEAD_ROWS, HEAD_DIM)]
            + [page(k) for k in range(N_PAST_SEL)] + [new(1), new(2), blk(LANES, win_c)],
            out_specs=(blk(HEAD_ROWS, HEAD_DIM), blk(LANES, win_c)),
        ),
        out_shape=(jax.ShapeDtypeStruct((nb, g, HEAD_ROWS, HEAD_DIM), F32),
                   jax.ShapeDtypeStruct((nb, g, LANES, win_c), F32)),
        compiler_params=_cparams(("parallel", "parallel")),
        name="dec_attn",
    )(phys, half, q8, gates, o_cmp, *([slc_t] * N_PAST_SEL), kvt_new, kvt_new, win_t)


RET_STEP_CHUNKS = 8


def _log_gamma():
    return jnp.log1p(-jnp.exp2(-5.0 - jnp.arange(RET_HEADS, dtype=F32)))


def _ret_tables(chunk):
    lg = _log_gamma()
    n = jnp.arange(chunk, dtype=F32)
    diff = n[:, None] - n[None, :]
    decay = jnp.exp(jnp.maximum(diff, 0.0)[None] * lg[:, None, None]) * (diff >= 0)[None]
    rep = lambda col: jnp.broadcast_to(col[:, :, None], (RET_HEADS, chunk, RET_DV))
    qdec = rep(jnp.exp((n + 1.0)[None, :] * lg[:, None]))
    kdec = rep(jnp.exp((chunk - 1.0 - n)[None, :] * lg[:, None]))
    sdec = jnp.broadcast_to(jnp.exp(chunk * lg)[:, None, None], (RET_HEADS, RET_DK, RET_DV))
    return decay.astype(F32), qdec, kdec, sdec


def _group_norm_gate(o, g, gn_g, gn_b):
    mu = jnp.mean(o, axis=-1, keepdims=True)
    var = jnp.mean(jnp.square(o - mu), axis=-1, keepdims=True)
    on = (o - mu) * lax.rsqrt(var + LN_EPS) * gn_g + gn_b
    return (jax.nn.silu(g) * on).astype(BF16)


def _ret_prompt_kernel(qb_ref, kb_ref, vb_ref, gb_ref, decay_ref, qdec_ref, kdec_ref, sdec_ref, gng_ref, gnb_ref,
                       bm_ref, state_ref, st_ref):
    @pl.when(pl.program_id(1) == 0)
    def _():
        st_ref[...] = jnp.zeros_like(st_ref)

    def chunk_step(c, _):
        rows = pl.ds(pl.multiple_of(c * RET_CHUNK, RET_CHUNK), RET_CHUNK)
        for h in range(RET_HEADS):
            cols = slice(h * RET_DK, (h + 1) * RET_DK)
            q = qb_ref[rows, cols]
            k = kb_ref[rows, cols]
            v = vb_ref[rows, cols]
            st = st_ref[h]
            inner = lax.dot_general(q, k.astype(BF16), (((1,), (1,)), ((), ())), preferred_element_type=F32)
            inner = inner * decay_ref[h]
            o = jnp.dot(inner.astype(BF16), v, preferred_element_type=F32)
            o = o + jnp.dot(q, st.astype(BF16), preferred_element_type=F32) * qdec_ref[h]
            kd = (k * kdec_ref[h]).astype(BF16)
            st_ref[h] = sdec_ref[h] * st + lax.dot_general(kd, v, (((0,), (0,)), ((), ())), preferred_element_type=F32)
            bm_ref[rows, cols] = _group_norm_gate(o, gb_ref[rows, cols], gng_ref[:, cols], gnb_ref[:, cols])
        return 0

    lax.fori_loop(0, RET_STEP_CHUNKS, chunk_step, 0)
    state_ref[0] = st_ref[...]


def _ret_prompt(qb, kb, vb, gb, gn_g, gn_b, nb):
    n = qb.shape[0]
    tt = RET_STEP_CHUNKS * RET_CHUNK
    nt = n // nb // tt
    decay, qdec, kdec, sdec = _ret_tables(RET_CHUNK)
    row = pl.BlockSpec((tt, RET_HEADS * RET_DK), lambda b, i: (b * nt + i, 0))
    tab = pl.BlockSpec((RET_HEADS, RET_CHUNK, RET_DV), lambda b, i: (0, 0, 0))
    vec = pl.BlockSpec((1, RET_HEADS * RET_DV), lambda b, i: (0, 0))
    return pl.pallas_call(
        _ret_prompt_kernel,
        grid=(nb, nt),
        in_specs=[row, row, row, row, tab, tab, tab, tab, vec, vec],
        out_specs=(row, pl.BlockSpec((1, RET_HEADS, RET_DK, RET_DV), lambda b, i: (b, 0, 0, 0))),
        out_shape=(jax.ShapeDtypeStruct((n, RET_HEADS * RET_DV), BF16),
                   jax.ShapeDtypeStruct((nb, RET_HEADS, RET_DK, RET_DV), F32)),
        scratch_shapes=[pltpu.VMEM((RET_HEADS, RET_DK, RET_DV), F32)],
        compiler_params=_cparams(("parallel", "arbitrary")),
        name="ret_prompt",
    )(qb, kb, vb, gb, decay, qdec, kdec, sdec, gn_g.reshape(1, -1), gn_b.reshape(1, -1))


def _ret_step_kernel(q_ref, k_ref, v_ref, g_ref, st_ref, qdec_ref, sdec_ref, gng_ref, gnb_ref, bm_ref, stout_ref):
    for h in range(RET_HEADS):
        cols = slice(h * RET_DK, (h + 1) * RET_DK)
        q = q_ref[0, :, cols]
        k = k_ref[0, :, cols].astype(BF16)
        v = v_ref[0, :, cols]
        st = st_ref[0, h]
        qk = jnp.sum(q.astype(F32) * k.astype(F32), axis=-1, keepdims=True)
        o = qk.astype(BF16).astype(F32) * v.astype(F32)
        o = o + jnp.dot(q, st.astype(BF16), preferred_element_type=F32) * qdec_ref[h]
        stout_ref[0, h] = sdec_ref[h] * st + lax.dot_general(k, v, (((0,), (0,)), ((), ())), preferred_element_type=F32)
        bm_ref[0, :, cols] = _group_norm_gate(o, g_ref[0, :, cols], gng_ref[:, cols], gnb_ref[:, cols])


def _ret_step(q8, k8, v8, g8, state, gn_g, gn_b):
    nb = state.shape[0]
    _, qdec, _, sdec = _ret_tables(1)
    tok = pl.BlockSpec((1, HEAD_ROWS, RET_HEADS * RET_DK), lambda b: (b, 0, 0))
    st = pl.BlockSpec((1, RET_HEADS, RET_DK, RET_DV), lambda b: (b, 0, 0, 0))
    vec = pl.BlockSpec((1, RET_HEADS * RET_DV), lambda b: (0, 0))
    return pl.pallas_call(
        _ret_step_kernel,
        grid=(nb,),
        in_specs=[tok, tok, tok, tok, st,
                  pl.BlockSpec((RET_HEADS, 1, RET_DV), lambda b: (0, 0, 0)),
                  pl.BlockSpec((RET_HEADS, RET_DK, RET_DV), lambda b: (0, 0, 0)), vec, vec],
        out_specs=(tok, st),
        out_shape=(jax.ShapeDtypeStruct((nb, HEAD_ROWS, RET_HEADS * RET_DV), BF16),
                   jax.ShapeDtypeStruct(state.shape, F32)),
        compiler_params=_cparams(("parallel",)),
        name="ret_step",
    )(q8, k8, v8, g8, state, qdec, sdec, gn_g.reshape(1, -1), gn_b.reshape(1, -1))


ROUTE_E0 = N_GROUPS
MOE_ROWS = 256


def _layer_norm(v, g, b):
    mu = jnp.mean(v, axis=-1, keepdims=True)
    var = jnp.mean(jnp.square(v - mu), axis=-1, keepdims=True)
    return (v - mu) * lax.rsqrt(var + LN_EPS) * g + b


def _first_lane(hit, lane):
    return jnp.min(jnp.where(hit, lane, 4 * LANES), axis=-1, keepdims=True)


def _merge_kernel(on_ref, bm_ref, gm_ref, x_ref, wn_ref, wr_ref, wo_ref, g1_ref, b1_ref, wrt_ref, brt_ref, tri_ref,
                  x1_ref, route_ref, counts_ref, carry_ref):
    @pl.when(pl.program_id(0) == 0)
    def _():
        carry_ref[...] = jnp.zeros_like(carry_ref)

    gate_a = jax.nn.sigmoid(gm_ref[:, :D_MODEL])
    gate_b = jax.nn.sigmoid(gm_ref[:, D_MODEL:])
    m = gate_a * jnp.dot(on_ref[...], wn_ref[...], preferred_element_type=F32)
    m = m + gate_b * jnp.dot(bm_ref[...], wr_ref[...], preferred_element_type=F32)
    mix = jnp.dot(m.astype(BF16), wo_ref[...], preferred_element_type=F32)
    x1 = _layer_norm(DEEPNORM_ALPHA * x_ref[...] + mix, g1_ref[...], b1_ref[...])
    x1_ref[...] = x1

    lg = jnp.dot(x1.astype(BF16), wrt_ref[...], preferred_element_type=F32) + brt_ref[...]
    lane = lax.broadcasted_iota(jnp.int32, lg.shape, 1)
    gl = jnp.where(lane < N_GROUPS, lg, -jnp.inf)
    ge = jnp.exp(gl - jnp.max(gl, axis=-1, keepdims=True))
    pgrp = ge / jnp.sum(ge, axis=-1, keepdims=True)
    pg = jnp.max(pgrp, axis=-1, keepdims=True)
    grp = _first_lane(pgrp == pg, lane)
    lo = ROUTE_E0 + grp * EXPERTS_PER_GROUP
    in_g = (lane >= lo) & (lane < lo + EXPERTS_PER_GROUP)
    el = jnp.where(in_g, lg, -jnp.inf)
    ee = jnp.exp(el - jnp.max(el, axis=-1, keepdims=True))
    pc = jnp.where(in_g, ee / jnp.sum(ee, axis=-1, keepdims=True), -1.0)
    w1 = jnp.max(pc, axis=-1, keepdims=True)
    i1 = _first_lane(pc == w1, lane)
    pc2 = jnp.where(lane == i1, -1.0, pc)
    w2 = jnp.max(pc2, axis=-1, keepdims=True)
    i2 = _first_lane(pc2 == w2, lane)
    wsum = w1 + w2
    wt1 = pg * w1 / wsum
    wt2 = pg * w2 / wsum
    hit1 = lane == i1
    hit2 = lane == i2
    cnt = jnp.where(hit1, 1.0, 0.0) + jnp.where(hit2, 1.0, 0.0)
    base = carry_ref[0:1, :] + jnp.dot(tri_ref[...], cnt.astype(BF16), preferred_element_type=F32)
    rank1 = jnp.sum(jnp.where(hit1, base, 0.0), axis=-1, keepdims=True)
    rank2 = jnp.sum(jnp.where(hit2, base, 0.0), axis=-1, keepdims=True)
    carry = carry_ref[0:1, :] + jnp.sum(cnt, axis=0, keepdims=True)
    carry_ref[...] = jnp.broadcast_to(carry, carry_ref.shape)
    counts_ref[...] = jnp.broadcast_to(carry, counts_ref.shape)
    fields = ((i1 - ROUTE_E0).astype(F32), (i2 - ROUTE_E0).astype(F32), wt1, wt2, rank1, rank2)
    out = jnp.zeros(lg.shape, F32)
    for idx, val in enumerate(fields):
        out = jnp.where(lane == idx, val, out)
    route_ref[...] = out


def _merge_weights(w_nsa_proj, w_ret_proj, w_o, w_route_group, b_route_group, w_route_exp, b_route_exp):
    pad = LANES - N_GROUPS - N_EXPERTS
    wrt = jnp.concatenate([w_route_group, w_route_exp, jnp.zeros((D_MODEL, pad), F32)], axis=1).astype(BF16)
    brt = jnp.concatenate([b_route_group, b_route_exp, jnp.zeros((pad,), F32)])[None, :]
    return w_nsa_proj.astype(BF16), w_ret_proj.astype(BF16), w_o.astype(BF16), wrt, brt


def _merge(o_nsa, bm, gm, x2, mw, ln_g, ln_b, tm):
    wn, wr, wo, wrt, brt = mw
    n = x2.shape[0]
    tri = (jnp.arange(tm)[:, None] > jnp.arange(tm)[None, :]).astype(BF16)
    row = lambda width: pl.BlockSpec((tm, width), lambda i: (i, 0))
    const = lambda a: pl.BlockSpec(a.shape, lambda i: (0, 0))
    ln_g = ln_g.reshape(1, -1)
    ln_b = ln_b.reshape(1, -1)
    return pl.pallas_call(
        _merge_kernel,
        grid=(n // tm,),
        in_specs=[row(512), row(512), row(2 * D_MODEL), row(D_MODEL), const(wn), const(wr), const(wo),
                  const(ln_g), const(ln_b), const(wrt), const(brt), const(tri)],
        out_specs=(row(D_MODEL), row(LANES), pl.BlockSpec((8, LANES), lambda i: (0, 0))),
        out_shape=(jax.ShapeDtypeStruct((n, D_MODEL), F32), jax.ShapeDtypeStruct((n, LANES), F32),
                   jax.ShapeDtypeStruct((8, LANES), F32)),
        scratch_shapes=[pltpu.VMEM((8, LANES), F32)],
        compiler_params=_cparams(("arbitrary",)),
        name="merge_route",
    )(o_nsa, bm, gm, x2, wn, wr, wo, ln_g, ln_b, wrt, brt, tri)


def _expert_kernel(be_ref, nused_ref, src_ref, dst_ref, x_hbm, w1_ref, w3_ref, w2_ref, y_hbm,
                   xbuf, ybuf, sem_in, sem_out):
    i = pl.program_id(0)

    def gather(r):
        return pltpu.make_async_copy(x_hbm.at[pl.ds(src_ref[0, 0, r], 1), :], xbuf.at[pl.ds(r, 1), :], sem_in)

    def scatter(r):
        return pltpu.make_async_copy(ybuf.at[pl.ds(r, 1), :], y_hbm.at[pl.ds(dst_ref[0, 0, r], 1), :], sem_out)

    @pl.when(i < nused_ref[0])
    def _():
        def start_in(r, _):
            gather(r).start()
            return 0

        def wait_in(r, _):
            gather(r).wait()
            return 0

        lax.fori_loop(0, MOE_ROWS, start_in, 0)
        lax.fori_loop(0, MOE_ROWS, wait_in, 0)
        x = xbuf[...].astype(BF16)
        h = jax.nn.silu(jnp.dot(x, w1_ref[0], preferred_element_type=F32))
        h = h * jnp.dot(x, w3_ref[0], preferred_element_type=F32)
        ybuf[...] = jnp.dot(h.astype(BF16), w2_ref[0], preferred_element_type=F32)

        def start_out(r, _):
            @pl.when(dst_ref[0, 0, r] >= 0)
            def _():
                scatter(r).start()
            return 0

        def wait_out(r, _):
            @pl.when(dst_ref[0, 0, r] >= 0)
            def _():
                scatter(r).wait()
            return 0

        lax.fori_loop(0, MOE_ROWS, start_out, 0)
        lax.fori_loop(0, MOE_ROWS, wait_out, 0)


def _experts(x1, block_e, n_used, slot_src, slot_dst, w1, w3, w2):
    n = x1.shape[0]
    n_blocks = block_e.shape[0]
    slot = pl.BlockSpec((1, 1, MOE_ROWS), lambda i, be, nu: (i, 0, 0), memory_space=pltpu.SMEM)
    wspec = lambda a: pl.BlockSpec((1,) + a.shape[1:], lambda i, be, nu: (be[i], 0, 0))
    return pl.pallas_call(
        _expert_kernel,
        grid_spec=pltpu.PrefetchScalarGridSpec(
            num_scalar_prefetch=2,
            grid=(n_blocks,),
            in_specs=[slot, slot, pl.BlockSpec(memory_space=pl.ANY), wspec(w1), wspec(w3), wspec(w2)],
            out_specs=pl.BlockSpec(memory_space=pl.ANY),
            scratch_shapes=[pltpu.VMEM((MOE_ROWS, D_MODEL), F32), pltpu.VMEM((MOE_ROWS, D_MODEL), F32),
                            pltpu.SemaphoreType.DMA(()), pltpu.SemaphoreType.DMA(())],
        ),
        out_shape=jax.ShapeDtypeStruct((TOP_K * n, D_MODEL), F32),
        compiler_params=_cparams(("arbitrary",)),
        name="experts",
    )(block_e, n_used, slot_src.reshape(n_blocks, 1, MOE_ROWS), slot_dst.reshape(n_blocks, 1, MOE_ROWS), x1, w1, w3, w2)


def _combine_kernel(x1_ref, y_ref, route_ref, g_ref, b_ref, o_ref):
    route = route_ref[...]
    ff = y_ref[:, :D_MODEL] * route[:, 2:3] + y_ref[:, D_MODEL:] * route[:, 3:4]
    o_ref[...] = _layer_norm(DEEPNORM_ALPHA * x1_ref[...] + ff, g_ref[...], b_ref[...])


def _combine(x1, y2, route, ln_g, ln_b, tm):
    n = x1.shape[0]
    row = lambda width: pl.BlockSpec((tm, width), lambda i: (i, 0))
    vec = pl.BlockSpec((1, D_MODEL), lambda i: (0, 0))
    return pl.pallas_call(
        _combine_kernel,
        grid=(n // tm,),
        in_specs=[row(D_MODEL), row(TOP_K * D_MODEL), row(LANES), vec, vec],
        out_specs=row(D_MODEL),
        out_shape=jax.ShapeDtypeStruct((n, D_MODEL), F32),
        compiler_params=_cparams(("parallel",)),
        name="combine_ln2",
    )(x1, y2.reshape(n, TOP_K * D_MODEL), route, ln_g.reshape(1, -1), ln_b.reshape(1, -1))


def _finish(o_nsa, bm, gm, x2, mw, ew, ln1, ln2, tm):
    n = x2.shape[0]
    x1, route, counts = _merge(o_nsa, bm, gm, x2, mw, ln1[0], ln1[1], tm)
    sizes = counts[0, ROUTE_E0:ROUTE_E0 + N_EXPERTS].astype(jnp.int32)
    padded = (sizes + MOE_ROWS - 1) // MOE_ROWS * MOE_ROWS
    pad_end = jnp.cumsum(padded)
    pad_start = pad_end - padded
    e = route[:, 0:TOP_K].astype(jnp.int32)
    rank = route[:, 4:4 + TOP_K].astype(jnp.int32)
    dest = (pad_start[e] + rank).reshape(-1)
    n_blocks = -(-(n * TOP_K) // MOE_ROWS) + N_EXPERTS
    n_slots = n_blocks * MOE_ROWS
    asg = jnp.arange(n * TOP_K, dtype=jnp.int32)
    slot_src = jnp.zeros((n_slots,), jnp.int32).at[dest].set(asg // TOP_K)
    slot_dst = jnp.full((n_slots,), -1, jnp.int32).at[dest].set(asg)
    blk0 = jnp.arange(n_blocks, dtype=jnp.int32) * MOE_ROWS
    block_e = jnp.minimum(jnp.searchsorted(pad_end, blk0, side="right"), N_EXPERTS - 1).astype(jnp.int32)
    n_used = (pad_end[-1:] // MOE_ROWS).astype(jnp.int32)
    y2 = _experts(x1, block_e, n_used, slot_src, slot_dst, *ew)
    return _combine(x1, y2, route, ln2[0], ln2[1], tm)


PROMPT_ROWS = 256


def _feature_major(a):
    lead = a.shape[:-3]
    return jnp.moveaxis(a, -3, -1).reshape(lead + (2 * HEAD_DIM, a.shape[-3]))


def _row_major(a):
    lead = a.shape[:-2]
    return jnp.moveaxis(a.reshape(lead + (2, HEAD_DIM, a.shape[-1])), -1, -3)


def _pad_rows(a):
    return jnp.pad(a[:, None, :], ((0, 0), (0, HEAD_ROWS - 1), (0, 0)))


def kernel(x_prompt, x_sample, cache_kv_cmp, cache_kv_slc, state_kv_win, state_ret, page_table, w_in, b_in, cmp_pos, w_cmp1, w_cmp2, b_cmp2, ret_gn_g, ret_gn_b, w_nsa_proj, w_ret_proj, w_o, ln1_g, ln1_b, w_route_group, b_route_group, w_route_exp, b_route_exp, w_e1, w_e3, w_e2, ln2_g, ln2_b):
    nb, t_len, _ = x_prompt.shape
    ns, s_len, _ = x_sample.shape
    assert s_len == 1 and ns == LANES and t_len % (RET_STEP_CHUNKS * RET_CHUNK) == 0 and t_len >= WINDOW
    g = NSA_KV_GROUPS
    pw = _proj_weights(w_in, b_in)
    cw = _compress_weights(cmp_pos, w_cmp1, w_cmp2, b_cmp2)
    mw = _merge_weights(w_nsa_proj, w_ret_proj, w_o, w_route_group, b_route_group, w_route_exp, b_route_exp)
    ew = (w_e1.astype(BF16), w_e3.astype(BF16), w_e2.astype(BF16))
    ln1 = (ln1_g, ln1_b)
    ln2 = (ln2_g, ln2_b)

    cos_p, sin_p = _rope_tables(jnp.arange(t_len, dtype=jnp.int32))
    x2 = x_prompt.reshape(nb * t_len, D_MODEL)
    qa, kvt, ga, qb, kb, vb, gb, gm = _project(x2, nb, pw, cos_p, sin_p, PROMPT_ROWS)
    kc, kvct = _compress(kvt[0], cw)
    kslc, kwin, vslc, vwin = _kv_prep(kvt)
    o_nsa = _nsa_prompt(qa, ga, kc, kvct, kslc, vslc, kwin, vwin)
    bm, ret_p = _ret_prompt(qb, kb, vb, gb, ret_gn_g, ret_gn_b, nb)
    y_prompt = _finish(o_nsa, bm, gm, x2, mw, ew, ln1, ln2, PROMPT_ROWS).reshape(x_prompt.shape)
    kv_cmp_p = _row_major(kvt[0])
    kv_slc_p = _row_major(kvt[1])
    win_p = _row_major(kvt[2][..., t_len - WINDOW:])

    n_pages = page_table.shape[1]
    past_len = n_pages * PAGE_SIZE
    cos_s, sin_s = _rope_tables(jnp.full((ns,), past_len, jnp.int32))
    xs2 = x_sample.reshape(ns, D_MODEL)
    qa, kvt_s, ga, qb, kb, vb, gb, gm = _project(xs2, 1, pw, cos_s, sin_s, ns)
    _, kvct_s = _compress_paged(_feature_major(cache_kv_cmp), page_table, cw)
    q8 = jnp.pad(qa.reshape(ns, g, NSA_GROUP_HEADS, HEAD_DIM),
                 ((0, 0), (0, 0), (0, HEAD_ROWS - NSA_GROUP_HEADS), (0, 0))).astype(BF16)
    o_cmp, imp = _dec_cmp(q8, kvct_s)
    picks = _dec_topk(imp[:, :, 0, :].reshape(ns * g, -1))
    idx = picks[:N_PAST_SEL].T.astype(jnp.int32)
    blocks_per_page = PAGE_SIZE // SEL_BLOCK
    seq = jnp.arange(ns * g, dtype=jnp.int32)[:, None] // g
    phys = page_table[seq, idx // blocks_per_page]
    half = idx % blocks_per_page
    gates = ga.reshape(ns, g, LANES)[:, :, :3 * NSA_GROUP_HEADS].reshape(ns, g, NSA_GROUP_HEADS, 3)
    gates = jnp.pad(gates, ((0, 0), (0, 0), (0, HEAD_ROWS - NSA_GROUP_HEADS), (0, LANES - 3)))
    o_s, win_t = _dec_attn(phys.reshape(-1), half.reshape(-1), q8, gates, o_cmp, _feature_major(cache_kv_slc),
                           kvt_s, _feature_major(state_kv_win))
    o_nsa_s = o_s[:, :, :NSA_GROUP_HEADS].reshape(ns, NSA_HEADS * HEAD_DIM).astype(BF16)
    bm8, ret_s = _ret_step(_pad_rows(qb), _pad_rows(kb), _pad_rows(vb), _pad_rows(gb), state_ret, ret_gn_g, ret_gn_b)
    y_sample = _finish(o_nsa_s, bm8[:, 0], gm, xs2, mw, ew, ln1, ln2, ns).reshape(x_sample.shape)
    new_rows = jnp.moveaxis(kvt_s[:, 0], -1, 1)
    kv_cmp_s = new_rows[0].reshape(ns, g, 1, 2, HEAD_DIM)
    kv_slc_s = new_rows[1].reshape(ns, g, 1, 2, HEAD_DIM)
    win_s = _row_major(win_t)
    return (y_prompt, y_sample, kv_cmp_p, kv_slc_p, win_p, ret_p, kv_cmp_s, kv_slc_s, win_s, ret_s)
```

```python
import functools

import numpy as np
import jax
import jax.numpy as jnp
from jax import lax
from jax.experimental import pallas as pl
from jax.experimental.pallas import tpu as pltpu

F32 = jnp.float32
BF16 = jnp.bfloat16

D_MODEL = 1024
DEPTH = 1
PAGE_SIZE = 128
NSA_HEADS = 8
NSA_KV_GROUPS = 2
NSA_GROUP_HEADS = NSA_HEADS // NSA_KV_GROUPS
HEAD_DIM = 64
CMP_BLOCK = 32
CMP_STRIDE = 16
CMP_HIDDEN = 64
SEL_BLOCK = 64
N_SEL = 16
WINDOW = 512
Q_TILE = 128
FORCE_SCORE = 1.0e3
NEG_INF = -1.0e30
RET_HEADS = 4
RET_DK = 128
RET_DV = 128
RET_CHUNK = 128
ROPE_BASE = 10000.0
N_GROUPS = 4
EXPERTS_PER_GROUP = 8
N_EXPERTS = N_GROUPS * EXPERTS_PER_GROUP
TOP_K = 2
EXPERT_FF = 512
LN_EPS = 1e-5
DEEPNORM_ALPHA = (2.0 * DEPTH) ** 0.25

NSA_Q_COLS = NSA_HEADS * HEAD_DIM
NSA_KV_COLS = 3 * 2 * NSA_KV_GROUPS * HEAD_DIM
NSA_GATE_COLS = 3 * NSA_HEADS
RET_QK_COLS = RET_HEADS * RET_DK
RET_V_COLS = RET_HEADS * RET_DV

LANES = 128

P_QA = 0
P_GA = P_QA + NSA_Q_COLS
P_QB = P_GA + NSA_KV_GROUPS * LANES
P_KB = P_QB + RET_QK_COLS
P_VB = P_KB + RET_QK_COLS
P_GB = P_VB + RET_V_COLS
P_GM = P_GB + RET_V_COLS
VMEM_LIMIT = 56 * 1024 * 1024


def _cparams(sem):
    return pltpu.CompilerParams(dimension_semantics=sem, vmem_limit_bytes=VMEM_LIMIT)


def _proj_kernel(x_ref, w_ref, b_ref, wkv_ref, bkv_ref, cos_ref, sin_ref,
                 qa_ref, kvt_ref, ga_ref, qb_ref, kb_ref, vb_ref, gb_ref, gm_ref):
    x = x_ref[...].astype(BF16)

    def cols(c0, n):
        h = jnp.dot(x, w_ref[:, c0:c0 + n], preferred_element_type=F32)
        return h + b_ref[:, c0:c0 + n]

    qa_ref[...] = cols(P_QA, 512) * (HEAD_DIM ** -0.5)
    ga_ref[...] = jax.nn.sigmoid(cols(P_GA, NSA_KV_GROUPS * LANES))
    cos = cos_ref[...]
    sin = sin_ref[...]
    for h in range(RET_HEADS):
        q = cols(P_QB + h * RET_DK, RET_DK)
        qb_ref[:, h * RET_DK:(h + 1) * RET_DK] = (q * cos + pltpu.roll(q, RET_DK // 2, 1) * sin).astype(BF16)
        k = cols(P_KB + h * RET_DK, RET_DK)
        kb_ref[:, h * RET_DK:(h + 1) * RET_DK] = (k * cos + pltpu.roll(k, RET_DK // 2, 1) * sin) * (RET_DK ** -0.5)
    vb_ref[...] = cols(P_VB, 512).astype(BF16)
    gb_ref[...] = cols(P_GB, 512)
    gm_ref[...] = cols(P_GM, 2048)
    kvt = lax.dot_general(wkv_ref[...], x, (((1,), (1,)), ((), ())), preferred_element_type=F32)
    kvt = kvt + bkv_ref[...]
    for i in range(6):
        kvt_ref[i // 2, 0, i % 2] = kvt[i * LANES:(i + 1) * LANES]


def _proj_weights(w_in, b_in):
    o_kv = NSA_Q_COLS
    o_ga = o_kv + NSA_KV_COLS
    o_qb = o_ga + NSA_GATE_COLS
    per_g = NSA_GATE_COLS // NSA_KV_GROUPS
    ga_cols = []
    for g in range(NSA_KV_GROUPS):
        ga_cols += [np.arange(o_ga + g * per_g, o_ga + (g + 1) * per_g), np.full(LANES - per_g, -1)]
    perm = np.concatenate([
        np.arange(0, NSA_Q_COLS),
        *ga_cols,
        np.arange(o_qb, o_qb + 2 * RET_QK_COLS + 2 * RET_V_COLS + 2 * D_MODEL),
    ])
    keep = jnp.asarray(perm >= 0)
    idx = jnp.asarray(np.maximum(perm, 0))
    w = jnp.where(keep[None, :], w_in[:, idx], 0.0).astype(BF16)
    b = jnp.where(keep, b_in[idx], 0.0)[None, :]
    br, g, kv, d = np.meshgrid(np.arange(3), np.arange(NSA_KV_GROUPS), np.arange(2), np.arange(HEAD_DIM), indexing="ij")
    kv_idx = jnp.asarray((o_kv + br * 256 + kv * 128 + g * 64 + d).reshape(-1))
    wkv = w_in[:, kv_idx].T.astype(BF16)
    bkv = b_in[kv_idx][:, None]
    return w, b, wkv, bkv


def _rope_tables(pos):
    half = RET_DK // 2
    inv = ROPE_BASE ** (-jnp.arange(half, dtype=F32) / half)
    ang = pos.astype(F32)[:, None] * inv[None, :]
    cos, sin = jnp.cos(ang), jnp.sin(ang)
    return jnp.concatenate([cos, cos], -1), jnp.concatenate([-sin, sin], -1)


def _project(x2, n_batch, pw, cos_t, sin_t, tm):
    w, b, wkv, bkv = pw
    n = x2.shape[0]
    t_len = n // n_batch
    nt = t_len // tm
    n_pos_blocks = cos_t.shape[0] // tm
    row = lambda i: (i, 0)
    const = lambda i: (0, 0)
    pos_map = lambda i: (i % n_pos_blocks, 0)
    out_shape = (
        jax.ShapeDtypeStruct((n, 512), F32),
        jax.ShapeDtypeStruct((3, n_batch, NSA_KV_GROUPS, LANES, t_len), F32),
        jax.ShapeDtypeStruct((n, NSA_KV_GROUPS * LANES), F32),
        jax.ShapeDtypeStruct((n, 512), BF16),
        jax.ShapeDtypeStruct((n, 512), F32),
        jax.ShapeDtypeStruct((n, 512), BF16),
        jax.ShapeDtypeStruct((n, 512), F32),
        jax.ShapeDtypeStruct((n, 2048), F32),
    )
    return pl.pallas_call(
        _proj_kernel,
        grid=(n // tm,),
        in_specs=[
            pl.BlockSpec((tm, D_MODEL), row),
            pl.BlockSpec(w.shape, const),
            pl.BlockSpec(b.shape, const),
            pl.BlockSpec(wkv.shape, const),
            pl.BlockSpec(bkv.shape, const),
            pl.BlockSpec((tm, LANES), pos_map),
            pl.BlockSpec((tm, LANES), pos_map),
        ],
        out_specs=(
            pl.BlockSpec((tm, 512), row),
            pl.BlockSpec((3, 1, NSA_KV_GROUPS, LANES, tm), lambda i: (0, i // nt, 0, 0, i % nt)),
            pl.BlockSpec((tm, NSA_KV_GROUPS * LANES), row),
            pl.BlockSpec((tm, 512), row),
            pl.BlockSpec((tm, 512), row),
            pl.BlockSpec((tm, 512), row),
            pl.BlockSpec((tm, 512), row),
            pl.BlockSpec((tm, 2048), row),
        ),
        out_shape=out_shape,
        compiler_params=_cparams(("parallel",)),
        name="proj",
    )(x2, w, b, wkv, bkv, cos_t, sin_t)


XPOSE_CHUNK = 512


def _compress_rows(xs_ref, wc_ref, w2_ref, b2_ref, pos_ref, wpe_ref, kc_ref, kvct_ref, perm_ref, nj):
    n_blk = 4 * nj
    pe = jnp.dot(pos_ref[...], wpe_ref[...], preferred_element_type=F32)[0:1]
    acc = jnp.zeros((n_blk, 2 * LANES), F32)
    for c2 in range(CMP_STRIDE // 2):
        xa = xs_ref[pl.ds(2 * c2, n_blk, stride=CMP_STRIDE), :]
        xb = xs_ref[pl.ds(2 * c2 + 1, n_blk, stride=CMP_STRIDE), :]
        x2 = jnp.concatenate([xa, xb], axis=1).astype(BF16)
        acc = acc + jnp.dot(x2, wc_ref[c2], preferred_element_type=F32)
    row = lax.broadcasted_iota(jnp.int32, (n_blk, LANES), 0)
    nxt = pltpu.roll(acc[:, LANES:], n_blk - 1, 0)
    second = jnp.where(row == n_blk - 1, 0.0, nxt)
    h = jax.nn.gelu(acc[:, :LANES] + second + pe)
    perm_ref[...] = jnp.dot(h.astype(BF16), w2_ref[...], preferred_element_type=F32) + b2_ref[...]
    lane = lax.broadcasted_iota(jnp.int32, (nj, LANES), 1)
    for rr in range(4):
        kv = perm_ref[pl.ds(rr, nj, stride=4), :]
        kc_ref[rr * nj:(rr + 1) * nj, :] = jnp.where(lane < HEAD_DIM, kv, 0.0).astype(BF16)
        kvct_ref[:, rr * nj:(rr + 1) * nj] = kv.T.astype(BF16)


def _compress_kernel(xt_ref, wc_ref, w2_ref, b2_ref, pos_ref, wpe_ref, kc_ref, kvct_ref, xs_ref, perm_ref, *, length):
    def xpose(i, _):
        t0 = pl.multiple_of(i * XPOSE_CHUNK, XPOSE_CHUNK)
        xs_ref[pl.ds(t0, XPOSE_CHUNK), :] = xt_ref[0, 0, :, pl.ds(t0, XPOSE_CHUNK)].T
        return 0

    lax.fori_loop(0, length // XPOSE_CHUNK, xpose, 0)
    _compress_rows(xs_ref, wc_ref, w2_ref, b2_ref, pos_ref, wpe_ref, kc_ref.at[0, 0], kvct_ref.at[0, 0], perm_ref,
                   length // SEL_BLOCK)


PAGES_PER_STEP = 8


def _compress_paged_kernel(pt_ref, *refs, n_pages):
    page_refs = refs[:PAGES_PER_STEP]
    wc_ref, w2_ref, b2_ref, pos_ref, wpe_ref, kc_ref, kvct_ref, xs_ref, perm_ref = refs[PAGES_PER_STEP:]
    s = pl.program_id(1)
    for k in range(PAGES_PER_STEP):
        r0 = pl.multiple_of((s * PAGES_PER_STEP + k) * PAGE_SIZE, PAGE_SIZE)
        for g in range(NSA_KV_GROUPS):
            xs_ref[g, pl.ds(r0, PAGE_SIZE), :] = page_refs[k][0, g].T

    @pl.when(s == n_pages // PAGES_PER_STEP - 1)
    def _():
        for g in range(NSA_KV_GROUPS):
            _compress_rows(xs_ref.at[g], wc_ref, w2_ref, b2_ref, pos_ref, wpe_ref, kc_ref.at[0, g], kvct_ref.at[0, g],
                           perm_ref, n_pages * PAGE_SIZE // SEL_BLOCK)


def _compress_paged(cache_t, page_table, cw):
    wc, w2, b2, pos, wpe = cw
    nb, n_pages = page_table.shape
    g = cache_t.shape[1]
    length = n_pages * PAGE_SIZE
    n_rows = length // CMP_STRIDE
    steps = n_pages // PAGES_PER_STEP
    page_spec = lambda k: pl.BlockSpec(
        (1, g, LANES, PAGE_SIZE), lambda b, s, pt: (pt[b * n_pages + s * PAGES_PER_STEP + k], 0, 0, 0))
    const = lambda a: pl.BlockSpec(a.shape, lambda b, s, pt: (0,) * a.ndim)
    return pl.pallas_call(
        functools.partial(_compress_paged_kernel, n_pages=n_pages),
        grid_spec=pltpu.PrefetchScalarGridSpec(
            num_scalar_prefetch=1,
            grid=(nb, steps),
            in_specs=[page_spec(k) for k in range(PAGES_PER_STEP)] + [const(a) for a in (wc, w2, b2, pos, wpe)],
            out_specs=(
                pl.BlockSpec((1, g, n_rows, LANES), lambda b, s, pt: (b, 0, 0, 0)),
                pl.BlockSpec((1, g, LANES, n_rows), lambda b, s, pt: (b, 0, 0, 0)),
            ),
            scratch_shapes=[pltpu.VMEM((g, length, LANES), F32), pltpu.VMEM((n_rows, LANES), F32)],
        ),
        out_shape=(
            jax.ShapeDtypeStruct((nb, g, n_rows, LANES), BF16),
            jax.ShapeDtypeStruct((nb, g, LANES, n_rows), BF16),
        ),
        compiler_params=_cparams(("parallel", "arbitrary")),
        name="compress_paged",
    )(page_table.reshape(-1), *([cache_t] * PAGES_PER_STEP), wc, w2, b2, pos, wpe)


def _compress_weights(cmp_pos, w_cmp1, w_cmp2, b_cmp2):
    eye = jnp.eye(2, dtype=F32)
    w5 = w_cmp1.reshape(2, 2, CMP_STRIDE, HEAD_DIM, CMP_HIDDEN)
    wc = jnp.einsum("vhcde,vw->cvdhwe", w5, eye).reshape(CMP_STRIDE // 2, 2 * LANES, 2 * LANES).astype(BF16)
    w2 = jnp.einsum("ved,vw->vewd", w_cmp2, eye).reshape(LANES, LANES).astype(BF16)
    b2 = b_cmp2.reshape(1, LANES)
    wpe = jnp.einsum("vcde,vw->vcdwe", w_cmp1, eye).reshape(2 * CMP_BLOCK * HEAD_DIM, LANES).astype(BF16)
    pos = jnp.broadcast_to(cmp_pos.reshape(1, -1), (8, 2 * CMP_BLOCK * HEAD_DIM)).astype(BF16)
    return wc, w2, b2, pos, wpe


def _compress(kvt, cw):
    wc, w2, b2, pos, wpe = cw
    nb, g, _, length = kvt.shape
    n_rows = length // CMP_STRIDE
    const2 = lambda b, gg: (0, 0)
    return pl.pallas_call(
        functools.partial(_compress_kernel, length=length),
        grid=(nb, g),
        in_specs=[
            pl.BlockSpec((1, 1, LANES, length), lambda b, gg: (b, gg, 0, 0)),
            pl.BlockSpec(wc.shape, lambda b, gg: (0, 0, 0)),
            pl.BlockSpec(w2.shape, const2),
            pl.BlockSpec(b2.shape, const2),
            pl.BlockSpec(pos.shape, const2),
            pl.BlockSpec(wpe.shape, const2),
        ],
        out_specs=(
            pl.BlockSpec((1, 1, n_rows, LANES), lambda b, gg: (b, gg, 0, 0)),
            pl.BlockSpec((1, 1, LANES, n_rows), lambda b, gg: (b, gg, 0, 0)),
        ),
        out_shape=(
            jax.ShapeDtypeStruct((nb, g, n_rows, LANES), BF16),
            jax.ShapeDtypeStruct((nb, g, LANES, n_rows), BF16),
        ),
        scratch_shapes=[pltpu.VMEM((length, LANES), F32), pltpu.VMEM((n_rows, LANES), F32)],
        compiler_params=_cparams(("parallel", "parallel")),
        name="compress",
    )(kvt, wc, w2, b2, pos, wpe)


BIAS_BLOCKS = 16
KEY_TILE = 512
PREP_TILE = 1024
N_COLS = NSA_GROUP_HEADS * Q_TILE
SLC_UNROLL = 4
V_ROWS = HEAD_DIM + 16
WIN_KEYS = WINDOW + Q_TILE
CMP_ROW_CHOICES = (128,)


def _kv_prep_kernel(slc_ref, win_ref, kslc_ref, kwin_ref, vslc_ref, vwin_ref):
    shape = (PREP_TILE, LANES)
    lane = lax.broadcasted_iota(jnp.int32, shape, 1)
    blk = (lax.broadcasted_iota(jnp.int32, shape, 0) // SEL_BLOCK) % BIAS_BLOCKS
    onehot = jnp.where(lane - HEAD_DIM == blk, 1.0, 0.0)
    ks = slc_ref[0, 0, 0].T
    kslc_ref[0, 0] = jnp.where(lane < HEAD_DIM, ks, onehot).astype(BF16)
    kw = win_ref[0, 0, 0].T
    kwin_ref[0, 0] = jnp.where(lane < HEAD_DIM, kw, 0.0).astype(BF16)
    ones = jnp.ones((V_ROWS - HEAD_DIM, PREP_TILE), BF16)
    vslc_ref[0, 0] = jnp.concatenate([slc_ref[0, 0, 0, HEAD_DIM:, :].astype(BF16), ones], axis=0)
    vwin_ref[0, 0] = jnp.concatenate([win_ref[0, 0, 0, HEAD_DIM:, :].astype(BF16), ones], axis=0)


def _kv_prep(kvt):
    _, nb, g, _, t_len = kvt.shape
    in_map = lambda br: (lambda b, gg, i: (br, b, gg, 0, i))
    k_spec = pl.BlockSpec((1, 1, PREP_TILE, LANES), lambda b, gg, i: (b, gg, i, 0))
    v_spec = pl.BlockSpec((1, 1, V_ROWS, PREP_TILE), lambda b, gg, i: (b, gg, 0, i))
    k_shape = jax.ShapeDtypeStruct((nb, g, t_len, LANES), BF16)
    v_shape = jax.ShapeDtypeStruct((nb, g, V_ROWS, t_len), BF16)
    return pl.pallas_call(
        _kv_prep_kernel,
        grid=(nb, g, t_len // PREP_TILE),
        in_specs=[pl.BlockSpec((1, 1, 1, LANES, PREP_TILE), in_map(1)),
                  pl.BlockSpec((1, 1, 1, LANES, PREP_TILE), in_map(2))],
        out_specs=(k_spec, k_spec, v_spec, v_spec),
        out_shape=(k_shape, k_shape, v_shape, v_shape),
        compiler_params=_cparams(("parallel", "parallel", "parallel")),
        name="kv_prep",
    )(kvt, kvt)


def _online_update(carry, s, vt):
    m, acc = carry
    m_new = jnp.maximum(m, jnp.max(s, axis=0, keepdims=True))
    p = jnp.exp(s - m_new)
    acc = jnp.exp(m - m_new) * acc + jnp.dot(vt, p.astype(BF16), preferred_element_type=F32)
    return m_new, acc


def _softmax_out(acc):
    return acc[:HEAD_DIM] / acc[HEAD_DIM:HEAD_DIM + 1]


def _select_blocks(imp, q0):
    jrow = lax.broadcasted_iota(jnp.int32, imp.shape, 0)
    cur = (q0 + lax.broadcasted_iota(jnp.int32, imp.shape, 1)) // SEL_BLOCK
    work = jnp.where(jrow <= cur, imp, -1.0)
    bias = jnp.full(imp.shape, NEG_INF, F32)
    for forced_blk in (0, cur, cur - 1):
        hit = jrow == forced_blk
        bias = jnp.where(hit, 0.0, bias)
        work = jnp.where(hit, -3.0e38, work)
    jf = jrow.astype(F32)
    for _ in range(N_SEL - 3):
        mx = jnp.max(work, axis=0, keepdims=True)
        first = jnp.min(jnp.where(work == mx, jf, 1.0e9), axis=0, keepdims=True)
        hit = jf == first
        bias = jnp.where(hit, 0.0, bias)
        work = jnp.where(hit, -3.0e38, work)
    return bias


def _nsa_prompt_kernel(qa_ref, ga_ref, kc_ref, kvct_ref, kslc_ref, vslc_ref, kwin_ref, vwin_ref,
                       o_ref, qm_ref, bias_ref, ocmp_ref, *, nj):
    qi = pl.program_id(2)

    @pl.when(qi == 0)
    def _():
        qm_ref[...] = jnp.zeros_like(qm_ref)

    q_t = qa_ref[...].T
    for r in range(NSA_GROUP_HEADS):
        qm_ref[0:HEAD_DIM, r * Q_TILE:(r + 1) * Q_TILE] = q_t[r * HEAD_DIM:(r + 1) * HEAD_DIM].astype(BF16)
    ga_t = ga_ref[...].T

    def gate(branch):
        return jnp.concatenate([ga_t[r * 3 + branch:r * 3 + branch + 1] for r in range(NSA_GROUP_HEADS)], axis=1)

    q0 = qi * Q_TILE
    qpos_c = q0 + (lax.broadcasted_iota(jnp.int32, (1, N_COLS), 1) & (Q_TILE - 1))

    def compressed_and_select(n):
        kc = jnp.concatenate([kc_ref[0, 0, rr * nj:rr * nj + n, :] for rr in range(4)], axis=0)
        vct = jnp.concatenate([kvct_ref[0, 0, HEAD_DIM:, rr * nj:rr * nj + n] for rr in range(4)], axis=1)
        s = jnp.dot(kc, qm_ref[...], preferred_element_type=F32)
        row = lax.broadcasted_iota(jnp.int32, (4 * n, N_COLS), 0)
        cmp_end = (row % n) * SEL_BLOCK + (row // n) * CMP_STRIDE + (CMP_BLOCK - 1)
        vis = cmp_end <= qpos_c
        s = jnp.where(vis, s, NEG_INF)
        m = jnp.max(s, axis=0, keepdims=True)
        p = jnp.where(vis, jnp.exp(s - m), 0.0)
        l = jnp.sum(p, axis=0, keepdims=True)
        pn = p / jnp.where(l > 0.0, l, 1.0)
        ocmp_ref[...] = jnp.dot(vct, pn.astype(BF16), preferred_element_type=F32)
        p3 = pn[3 * n:4 * n]
        jrow4 = lax.broadcasted_iota(jnp.int32, (n, N_COLS), 0)
        prev = jnp.where(jrow4 == 0, 0.0, pltpu.roll(p3, 1, 0))
        a = pn[0:n] + pn[n:2 * n] + pn[2 * n:3 * n] + p3 + prev
        imp = a[:, 0:Q_TILE]
        for r in range(1, NSA_GROUP_HEADS):
            imp = imp + a[:, r * Q_TILE:(r + 1) * Q_TILE]
        bias16 = _select_blocks(imp, q0).astype(BF16)
        for r in range(NSA_GROUP_HEADS):
            bias_ref[0:n, r * Q_TILE:(r + 1) * Q_TILE] = bias16

    sizes = [n for n in CMP_ROW_CHOICES if n < nj] + [nj]
    need = 2 * qi + 2
    for v, n in enumerate(sizes):
        lo = sizes[v - 1] if v else 0
        pl.when((need > lo) & ((need <= n) | (v == len(sizes) - 1)))(functools.partial(compressed_and_select, n))
    o_cmp = ocmp_ref[...]


    init =(jnp.full((1, N_COLS), -jnp.inf, F32), jnp.zeros((V_ROWS, N_COLS), F32))
    q_rows = qm_ref[0:HEAD_DIM, :]
    pad_rows = jnp.zeros((LANES - HEAD_DIM - BIAS_BLOCKS, N_COLS), BF16)

    def slc_scores(st):
        b0 = pl.multiple_of((st * KEY_TILE // (SEL_BLOCK * BIAS_BLOCKS)) * BIAS_BLOCKS, BIAS_BLOCKS)
        qm = jnp.concatenate([q_rows, bias_ref[pl.ds(b0, BIAS_BLOCKS), :], pad_rows], axis=0)
        k0 = pl.multiple_of(st * KEY_TILE, KEY_TILE)
        return jnp.dot(kslc_ref[0, 0, pl.ds(k0, KEY_TILE), :], qm, preferred_element_type=F32)

    def slc_absorb(st, carry, sc):
        k0 = pl.multiple_of(st * KEY_TILE, KEY_TILE)
        return _online_update(carry, sc, vslc_ref[0, 0, :, pl.ds(k0, KEY_TILE)])

    def slc_group(i, carry):
        st0 = i * SLC_UNROLL
        sc = slc_scores(st0)
        for u in range(SLC_UNROLL):
            sc_next = slc_scores(st0 + u + 1) if u + 1 < SLC_UNROLL else None
            carry = slc_absorb(st0 + u, carry, sc)
            sc = sc_next
        return carry

    n_full = q0 // KEY_TILE
    n_groups = n_full // SLC_UNROLL
    carry = lax.fori_loop(0, n_groups, slc_group, init)
    carry = lax.fori_loop(n_groups * SLC_UNROLL, n_full, lambda st, c: slc_absorb(st, c, slc_scores(st)), carry)
    kpos = n_full * KEY_TILE + lax.broadcasted_iota(jnp.int32, (KEY_TILE, N_COLS), 0)
    sc_last = jnp.where(kpos <= qpos_c, slc_scores(n_full), NEG_INF)
    _, acc_s = slc_absorb(n_full, carry, sc_last)
    o_slc = _softmax_out(acc_s)

    kw0 = pl.multiple_of(jnp.maximum(q0 - WINDOW, 0), Q_TILE)
    sc = jnp.dot(kwin_ref[0, 0, pl.ds(kw0, WIN_KEYS), :], qm_ref[...], preferred_element_type=F32)
    dpos = qpos_c - (kw0 + lax.broadcasted_iota(jnp.int32, (WIN_KEYS, N_COLS), 0))
    sc = jnp.where(dpos >= 0, jnp.where(dpos < WINDOW, sc, NEG_INF), NEG_INF)
    p = jnp.exp(sc - jnp.max(sc, axis=0, keepdims=True))
    o_win = _softmax_out(jnp.dot(vwin_ref[0, 0, :, pl.ds(kw0, WIN_KEYS)], p.astype(BF16), preferred_element_type=F32))

    o = gate(0) * o_cmp + gate(1) * o_slc + gate(2) * o_win
    o_rows = jnp.concatenate([o[:, r * Q_TILE:(r + 1) * Q_TILE] for r in range(NSA_GROUP_HEADS)], axis=0)
    o_ref[...] = o_rows.T.astype(BF16)


def _nsa_prompt(qa, ga, kc, kvct, kslc, vslc, kwin, vwin):
    nb, g, t_len, _ = kslc.shape
    nq = t_len // Q_TILE
    nj = t_len // SEL_BLOCK
    n_cmp_rows = kc.shape[2]
    assert nj % BIAS_BLOCKS == 0 and n_cmp_rows == 4 * nj and t_len % KEY_TILE == 0 and t_len >= WIN_KEYS
    tile = lambda b, gg, i: (b * nq + i, gg)
    whole = lambda b, gg, i: (b, gg, 0, 0)
    return pl.pallas_call(
        functools.partial(_nsa_prompt_kernel, nj=nj),
        grid=(nb, g, nq),
        in_specs=[
            pl.BlockSpec((Q_TILE, NSA_GROUP_HEADS * HEAD_DIM), tile),
            pl.BlockSpec((Q_TILE, LANES), tile),
            pl.BlockSpec((1, 1, n_cmp_rows, LANES), whole),
            pl.BlockSpec((1, 1, LANES, n_cmp_rows), whole),
            pl.BlockSpec((1, 1, t_len, LANES), whole),
            pl.BlockSpec((1, 1, V_ROWS, t_len), whole),
            pl.BlockSpec((1, 1, t_len, LANES), whole),
            pl.BlockSpec((1, 1, V_ROWS, t_len), whole),
        ],
        out_specs=pl.BlockSpec((Q_TILE, NSA_GROUP_HEADS * HEAD_DIM), tile),
        out_shape=jax.ShapeDtypeStruct((nb * t_len, NSA_HEADS * HEAD_DIM), BF16),
        scratch_shapes=[pltpu.VMEM((LANES, N_COLS), BF16), pltpu.VMEM((nj, N_COLS), BF16),
                        pltpu.VMEM((HEAD_DIM, N_COLS), F32)],
        compiler_params=_cparams(("parallel", "parallel", "arbitrary")),
        name="nsa_prompt",
    )(qa, ga, kc, kvct, kslc, vslc, kwin, vwin)


HEAD_ROWS = 8
N_PAST_SEL = N_SEL - 1


def _softmax_lanes(s_list, mask_list):
    s_list = [jnp.where(mk, s, NEG_INF) for s, mk in zip(s_list, mask_list)]
    m = s_list[0].max(axis=-1, keepdims=True)
    for s in s_list[1:]:
        m = jnp.maximum(m, s.max(axis=-1, keepdims=True))
    p_list = [jnp.where(mk, jnp.exp(s - m), 0.0) for s, mk in zip(s_list, mask_list)]
    l = p_list[0].sum(axis=-1, keepdims=True)
    for p in p_list[1:]:
        l = l + p.sum(axis=-1, keepdims=True)
    inv = 1.0 / jnp.where(l > 0.0, l, 1.0)
    return [p * inv for p in p_list]


def _dot_nt(a, b):
    return lax.dot_general(a, b, (((1,), (1,)), ((), ())), preferred_element_type=F32)


def _dec_cmp_kernel(q_ref, kvct_ref, o_ref, imp_ref, *, nj):
    q = q_ref[0, 0]
    n = 4 * nj
    s = jnp.dot(q, kvct_ref[0, 0, :HEAD_DIM, :], preferred_element_type=F32)
    lane = lax.broadcasted_iota(jnp.int32, (HEAD_ROWS, n), 1)
    blk = 4 * (lane % nj) + lane // nj
    (pn,) = _softmax_lanes([s], [blk <= n - 2])
    o_ref[0, 0] = _dot_nt(pn.astype(BF16), kvct_ref[0, 0, HEAD_DIM:, :])
    row = lax.broadcasted_iota(jnp.int32, (HEAD_ROWS, n), 0)
    ph = jnp.sum(jnp.where(row < NSA_GROUP_HEADS, pn, 0.0), axis=0, keepdims=True)
    p3 = ph[:, 3 * nj:]
    prev = jnp.where(lane[0:1, :nj] == 0, 0.0, pltpu.roll(p3, 1, 1))
    imp = ph[:, :nj] + ph[:, nj:2 * nj] + ph[:, 2 * nj:3 * nj] + p3 + prev
    imp_ref[0, 0] = jnp.broadcast_to(imp, (HEAD_ROWS, nj))


def _dec_cmp(q8, kvct):
    nb, g, _, n = kvct.shape
    nj = n // 4
    blk = lambda *shape: pl.BlockSpec((1, 1) + shape, lambda b, gg: (b, gg, 0, 0))
    return pl.pallas_call(
        functools.partial(_dec_cmp_kernel, nj=nj),
        grid=(nb, g),
        in_specs=[blk(HEAD_ROWS, HEAD_DIM), blk(LANES, n)],
        out_specs=(blk(HEAD_ROWS, HEAD_DIM), blk(HEAD_ROWS, nj)),
        out_shape=(jax.ShapeDtypeStruct((nb, g, HEAD_ROWS, HEAD_DIM), F32),
                   jax.ShapeDtypeStruct((nb, g, HEAD_ROWS, nj), F32)),
        compiler_params=_cparams(("parallel", "parallel")),
        name="dec_cmp",
    )(q8, kvct)


def _dec_topk_kernel(imp_ref, idx_ref):
    work = imp_ref[...].T
    nj, n = work.shape
    jrow = lax.broadcasted_iota(jnp.int32, (nj, n), 0)
    jf = jrow.astype(F32)
    for forced_blk in (0, nj - 1):
        work = jnp.where(jrow == forced_blk, FORCE_SCORE, work)
    picks = []
    for _ in range(N_PAST_SEL):
        mx = jnp.max(work, axis=0, keepdims=True)
        first = jnp.min(jnp.where(work == mx, jf, 1.0e9), axis=0, keepdims=True)
        picks.append(first)
        work = jnp.where(jf == first, -3.0e38, work)
    picks.append(jnp.full((1, n), float(nj), F32))
    idx_ref[...] = jnp.concatenate(picks, axis=0)


def _dec_topk(imp2):
    n, nj = imp2.shape
    return pl.pallas_call(
        _dec_topk_kernel,
        out_shape=jax.ShapeDtypeStruct((N_SEL, n), F32),
        compiler_params=pltpu.CompilerParams(vmem_limit_bytes=VMEM_LIMIT),
        name="dec_topk",
    )(imp2)


def _dec_attn_kernel(phys_ref, half_ref, q_ref, gate_ref, ocmp_ref, *refs):
    page_refs = refs[:N_PAST_SEL]
    slcnew_ref, winnew_ref, win_ref, o_ref, winout_ref = refs[N_PAST_SEL:]
    b = pl.program_id(0)
    base = (b * pl.num_programs(1) + pl.program_id(1)) * N_PAST_SEL
    q = q_ref[0, 0]
    lane = lax.broadcasted_iota(jnp.int32, (HEAD_ROWS, LANES), 1)
    own = lane == b

    def attend(kv_blocks, masks):
        scores = [jnp.dot(q, kv[:HEAD_DIM].astype(BF16), preferred_element_type=F32) for kv in kv_blocks]
        probs = _softmax_lanes(scores, masks)
        out = jnp.zeros((HEAD_ROWS, HEAD_DIM), F32)
        for p, kv in zip(probs, kv_blocks):
            out = out + _dot_nt(p.astype(BF16), kv[HEAD_DIM:].astype(BF16))
        return out

    pages = [r[0, 0] for r in page_refs] + [slcnew_ref[0, 0, 0]]
    masks = [(lane // SEL_BLOCK) == half_ref[base + k] for k in range(N_PAST_SEL)] + [own]
    o_slc = attend(pages, masks)
    st = win_ref[0, 0]
    win_c = st.shape[1]
    wlane = lax.broadcasted_iota(jnp.int32, (HEAD_ROWS, win_c), 1)
    o_win = attend([st, winnew_ref[0, 0, 0]], [wlane > win_c - WINDOW, own])
    gates = gate_ref[0, 0]
    o_ref[0, 0] = gates[:, 0:1] * ocmp_ref[0, 0] + gates[:, 1:2] * o_slc + gates[:, 2:3] * o_win
    new = winnew_ref[0, 0, 0]
    lane_full = lax.broadcasted_iota(jnp.int32, new.shape, 1)
    new_col = jnp.sum(jnp.where(lane_full == b, new, 0.0), axis=1, keepdims=True)
    col = lax.broadcasted_iota(jnp.int32, st.shape, 1)
    winout_ref[0, 0] = jnp.where(col == win_c - 1, new_col, pltpu.roll(st, win_c - 1, 1))


def _dec_attn(phys, half, q8, gates, o_cmp, slc_t, kvt_new, win_t):
    nb, g, _, win_c = win_t.shape
    blk = lambda *shape: pl.BlockSpec((1, 1) + shape, lambda b, gg, ph, hf: (b, gg, 0, 0))
    page = lambda k: pl.BlockSpec((1, 1, LANES, PAGE_SIZE),
                                  lambda b, gg, ph, hf: (ph[(b * g + gg) * N_PAST_SEL + k], gg, 0, 0))
    new = lambda br: pl.BlockSpec((1, 1, 1, LANES, kvt_new.shape[-1]), lambda b, gg, ph, hf: (br, 0, gg, 0, 0))
    return pl.pallas_call(
        _dec_attn_kernel,
        grid_spec=pltpu.PrefetchScalarGridSpec(
            num_scalar_prefetch=2,
            grid=(nb, g),
            in_specs=[blk(HEAD_ROWS, HEAD_DIM), blk(HEAD_ROWS, LANES), blk(HEAD_ROWS, HEAD_DIM)]
            + [page(k) for k in range(N_PAST_SEL)] + [new(1), new(2), blk(LANES, win_c)],
            out_specs=(blk(HEAD_ROWS, HEAD_DIM), blk(LANES, win_c)),
        ),
        out_shape=(jax.ShapeDtypeStruct((nb, g, HEAD_ROWS, HEAD_DIM), F32),
                   jax.ShapeDtypeStruct((nb, g, LANES, win_c), F32)),
        compiler_params=_cparams(("parallel", "parallel")),
        name="dec_attn",
    )(phys, half, q8, gates, o_cmp, *([slc_t] * N_PAST_SEL), kvt_new, kvt_new, win_t)


RET_STEP_CHUNKS = 8


def _log_gamma():
    return jnp.log1p(-jnp.exp2(-5.0 - jnp.arange(RET_HEADS, dtype=F32)))


def _ret_tables(chunk):
    lg = _log_gamma()
    n = jnp.arange(chunk, dtype=F32)
    diff = n[:, None] - n[None, :]
    decay = jnp.exp(jnp.maximum(diff, 0.0)[None] * lg[:, None, None]) * (diff >= 0)[None]
    rep = lambda col: jnp.broadcast_to(col[:, :, None], (RET_HEADS, chunk, RET_DV))
    qdec = rep(jnp.exp((n + 1.0)[None, :] * lg[:, None]))
    kdec = rep(jnp.exp((chunk - 1.0 - n)[None, :] * lg[:, None]))
    sdec = jnp.broadcast_to(jnp.exp(chunk * lg)[:, None, None], (RET_HEADS, RET_DK, RET_DV))
    return decay.astype(F32), qdec, kdec, sdec


def _group_norm_gate(o, g, gn_g, gn_b):
    mu = jnp.mean(o, axis=-1, keepdims=True)
    var = jnp.mean(jnp.square(o - mu), axis=-1, keepdims=True)
    on = (o - mu) * lax.rsqrt(var + LN_EPS) * gn_g + gn_b
    return (jax.nn.silu(g) * on).astype(BF16)


def _ret_prompt_kernel(qb_ref, kb_ref, vb_ref, gb_ref, decay_ref, qdec_ref, kdec_ref, sdec_ref, gng_ref, gnb_ref,
                       bm_ref, state_ref, st_ref):
    @pl.when(pl.program_id(1) == 0)
    def _():
        st_ref[...] = jnp.zeros_like(st_ref)

    def chunk_step(c, _):
        rows = pl.ds(pl.multiple_of(c * RET_CHUNK, RET_CHUNK), RET_CHUNK)
        for h in range(RET_HEADS):
            cols = slice(h * RET_DK, (h + 1) * RET_DK)
            q = qb_ref[rows, cols]
            k = kb_ref[rows, cols]
            v = vb_ref[rows, cols]
            st = st_ref[h]
            inner = lax.dot_general(q, k.astype(BF16), (((1,), (1,)), ((), ())), preferred_element_type=F32)
            inner = inner * decay_ref[h]
            o = jnp.dot(inner.astype(BF16), v, preferred_element_type=F32)
            o = o + jnp.dot(q, st.astype(BF16), preferred_element_type=F32) * qdec_ref[h]
            kd = (k * kdec_ref[h]).astype(BF16)
            st_ref[h] = sdec_ref[h] * st + lax.dot_general(kd, v, (((0,), (0,)), ((), ())), preferred_element_type=F32)
            bm_ref[rows, cols] = _group_norm_gate(o, gb_ref[rows, cols], gng_ref[:, cols], gnb_ref[:, cols])
        return 0

    lax.fori_loop(0, RET_STEP_CHUNKS, chunk_step, 0)
    state_ref[0] = st_ref[...]


def _ret_prompt(qb, kb, vb, gb, gn_g, gn_b, nb):
    n = qb.shape[0]
    tt = RET_STEP_CHUNKS * RET_CHUNK
    nt = n // nb // tt
    decay, qdec, kdec, sdec = _ret_tables(RET_CHUNK)
    row = pl.BlockSpec((tt, RET_HEADS * RET_DK), lambda b, i: (b * nt + i, 0))
    tab = pl.BlockSpec((RET_HEADS, RET_CHUNK, RET_DV), lambda b, i: (0, 0, 0))
    vec = pl.BlockSpec((1, RET_HEADS * RET_DV), lambda b, i: (0, 0))
    return pl.pallas_call(
        _ret_prompt_kernel,
        grid=(nb, nt),
        in_specs=[row, row, row, row, tab, tab, tab, tab, vec, vec],
        out_specs=(row, pl.BlockSpec((1, RET_HEADS, RET_DK, RET_DV), lambda b, i: (b, 0, 0, 0))),
        out_shape=(jax.ShapeDtypeStruct((n, RET_HEADS * RET_DV), BF16),
                   jax.ShapeDtypeStruct((nb, RET_HEADS, RET_DK, RET_DV), F32)),
        scratch_shapes=[pltpu.VMEM((RET_HEADS, RET_DK, RET_DV), F32)],
        compiler_params=_cparams(("parallel", "arbitrary")),
        name="ret_prompt",
    )(qb, kb, vb, gb, decay, qdec, kdec, sdec, gn_g.reshape(1, -1), gn_b.reshape(1, -1))


def _ret_step_kernel(q_ref, k_ref, v_ref, g_ref, st_ref, qdec_ref, sdec_ref, gng_ref, gnb_ref, bm_ref, stout_ref):
    for h in range(RET_HEADS):
        cols = slice(h * RET_DK, (h + 1) * RET_DK)
        q = q_ref[0, :, cols]
        k = k_ref[0, :, cols].astype(BF16)
        v = v_ref[0, :, cols]
        st = st_ref[0, h]
        qk = jnp.sum(q.astype(F32) * k.astype(F32), axis=-1, keepdims=True)
        o = qk.astype(BF16).astype(F32) * v.astype(F32)
        o = o + jnp.dot(q, st.astype(BF16), preferred_element_type=F32) * qdec_ref[h]
        stout_ref[0, h] = sdec_ref[h] * st + lax.dot_general(k, v, (((0,), (0,)), ((), ())), preferred_element_type=F32)
        bm_ref[0, :, cols] = _group_norm_gate(o, g_ref[0, :, cols], gng_ref[:, cols], gnb_ref[:, cols])


def _ret_step(q8, k8, v8, g8, state, gn_g, gn_b):
    nb = state.shape[0]
    _, qdec, _, sdec = _ret_tables(1)
    tok = pl.BlockSpec((1, HEAD_ROWS, RET_HEADS * RET_DK), lambda b: (b, 0, 0))
    st = pl.BlockSpec((1, RET_HEADS, RET_DK, RET_DV), lambda b: (b, 0, 0, 0))
    vec = pl.BlockSpec((1, RET_HEADS * RET_DV), lambda b: (0, 0))
    return pl.pallas_call(
        _ret_step_kernel,
        grid=(nb,),
        in_specs=[tok, tok, tok, tok, st,
                  pl.BlockSpec((RET_HEADS, 1, RET_DV), lambda b: (0, 0, 0)),
                  pl.BlockSpec((RET_HEADS, RET_DK, RET_DV), lambda b: (0, 0, 0)), vec, vec],
        out_specs=(tok, st),
        out_shape=(jax.ShapeDtypeStruct((nb, HEAD_ROWS, RET_HEADS * RET_DV), BF16),
                   jax.ShapeDtypeStruct(state.shape, F32)),
        compiler_params=_cparams(("parallel",)),
        name="ret_step",
    )(q8, k8, v8, g8, state, qdec, sdec, gn_g.reshape(1, -1), gn_b.reshape(1, -1))


ROUTE_E0 = N_GROUPS
MOE_ROWS = 256
SAMPLE_MOE_ROWS = 32


def _layer_norm(v, g, b):
    mu = jnp.mean(v, axis=-1, keepdims=True)
    var = jnp.mean(jnp.square(v - mu), axis=-1, keepdims=True)
    return (v - mu) * lax.rsqrt(var + LN_EPS) * g + b


def _first_lane(hit, lane):
    return jnp.min(jnp.where(hit, lane, 4 * LANES), axis=-1, keepdims=True)


def _merge_kernel(on_ref, bm_ref, gm_ref, x_ref, wn_ref, wr_ref, wo_ref, g1_ref, b1_ref, wrt_ref, brt_ref, tri_ref,
                  x1_ref, route_ref, counts_ref, carry_ref):
    @pl.when(pl.program_id(0) == 0)
    def _():
        carry_ref[...] = jnp.zeros_like(carry_ref)

    gate_a = jax.nn.sigmoid(gm_ref[:, :D_MODEL])
    gate_b = jax.nn.sigmoid(gm_ref[:, D_MODEL:])
    m = gate_a * jnp.dot(on_ref[...], wn_ref[...], preferred_element_type=F32)
    m = m + gate_b * jnp.dot(bm_ref[...], wr_ref[...], preferred_element_type=F32)
    mix = jnp.dot(m.astype(BF16), wo_ref[...], preferred_element_type=F32)
    x1 = _layer_norm(DEEPNORM_ALPHA * x_ref[...] + mix, g1_ref[...], b1_ref[...])
    x1_ref[...] = x1

    lg = jnp.dot(x1.astype(BF16), wrt_ref[...], preferred_element_type=F32) + brt_ref[...]
    lane = lax.broadcasted_iota(jnp.int32, lg.shape, 1)
    gl = jnp.where(lane < N_GROUPS, lg, -jnp.inf)
    ge = jnp.exp(gl - jnp.max(gl, axis=-1, keepdims=True))
    pgrp = ge / jnp.sum(ge, axis=-1, keepdims=True)
    pg = jnp.max(pgrp, axis=-1, keepdims=True)
    grp = _first_lane(pgrp == pg, lane)
    lo = ROUTE_E0 + grp * EXPERTS_PER_GROUP
    in_g = (lane >= lo) & (lane < lo + EXPERTS_PER_GROUP)
    el = jnp.where(in_g, lg, -jnp.inf)
    ee = jnp.exp(el - jnp.max(el, axis=-1, keepdims=True))
    pc = jnp.where(in_g, ee / jnp.sum(ee, axis=-1, keepdims=True), -1.0)
    w1 = jnp.max(pc, axis=-1, keepdims=True)
    i1 = _first_lane(pc == w1, lane)
    pc2 = jnp.where(lane == i1, -1.0, pc)
    w2 = jnp.max(pc2, axis=-1, keepdims=True)
    i2 = _first_lane(pc2 == w2, lane)
    wsum = w1 + w2
    wt1 = pg * w1 / wsum
    wt2 = pg * w2 / wsum
    hit1 = lane == i1
    hit2 = lane == i2
    cnt = jnp.where(hit1, 1.0, 0.0) + jnp.where(hit2, 1.0, 0.0)
    base = carry_ref[0:1, :] + jnp.dot(tri_ref[...], cnt.astype(BF16), preferred_element_type=F32)
    rank1 = jnp.sum(jnp.where(hit1, base, 0.0), axis=-1, keepdims=True)
    rank2 = jnp.sum(jnp.where(hit2, base, 0.0), axis=-1, keepdims=True)
    carry = carry_ref[0:1, :] + jnp.sum(cnt, axis=0, keepdims=True)
    carry_ref[...] = jnp.broadcast_to(carry, carry_ref.shape)
    counts_ref[...] = jnp.broadcast_to(carry, counts_ref.shape)
    fields = ((i1 - ROUTE_E0).astype(F32), (i2 - ROUTE_E0).astype(F32), wt1, wt2, rank1, rank2)
    out = jnp.zeros(lg.shape, F32)
    for idx, val in enumerate(fields):
        out = jnp.where(lane == idx, val, out)
    route_ref[...] = out


def _merge_weights(w_nsa_proj, w_ret_proj, w_o, w_route_group, b_route_group, w_route_exp, b_route_exp):
    pad = LANES - N_GROUPS - N_EXPERTS
    wrt = jnp.concatenate([w_route_group, w_route_exp, jnp.zeros((D_MODEL, pad), F32)], axis=1).astype(BF16)
    brt = jnp.concatenate([b_route_group, b_route_exp, jnp.zeros((pad,), F32)])[None, :]
    return w_nsa_proj.astype(BF16), w_ret_proj.astype(BF16), w_o.astype(BF16), wrt, brt


def _merge(o_nsa, bm, gm, x2, mw, ln_g, ln_b, tm):
    wn, wr, wo, wrt, brt = mw
    n = x2.shape[0]
    tri = (jnp.arange(tm)[:, None] > jnp.arange(tm)[None, :]).astype(BF16)
    row = lambda width: pl.BlockSpec((tm, width), lambda i: (i, 0))
    const = lambda a: pl.BlockSpec(a.shape, lambda i: (0, 0))
    ln_g = ln_g.reshape(1, -1)
    ln_b = ln_b.reshape(1, -1)
    return pl.pallas_call(
        _merge_kernel,
        grid=(n // tm,),
        in_specs=[row(512), row(512), row(2 * D_MODEL), row(D_MODEL), const(wn), const(wr), const(wo),
                  const(ln_g), const(ln_b), const(wrt), const(brt), const(tri)],
        out_specs=(row(D_MODEL), row(LANES), pl.BlockSpec((8, LANES), lambda i: (0, 0))),
        out_shape=(jax.ShapeDtypeStruct((n, D_MODEL), F32), jax.ShapeDtypeStruct((n, LANES), F32),
                   jax.ShapeDtypeStruct((8, LANES), F32)),
        scratch_shapes=[pltpu.VMEM((8, LANES), F32)],
        compiler_params=_cparams(("arbitrary",)),
        name="merge_route",
    )(o_nsa, bm, gm, x2, wn, wr, wo, ln_g, ln_b, wrt, brt, tri)


def _expert_kernel(be_ref, nused_ref, dst_ref, x_hbm, w1_ref, w3_ref, w2_ref, y_hbm,
                   xbuf, ybuf, sem_in, sem_out):
    i = pl.program_id(0)
    rows = xbuf.shape[0]

    def gather(r):
        src = jnp.maximum(dst_ref[0, 0, r], 0) // TOP_K
        return pltpu.make_async_copy(x_hbm.at[pl.ds(src, 1), :], xbuf.at[pl.ds(r, 1), :], sem_in)

    def scatter(r):
        return pltpu.make_async_copy(ybuf.at[pl.ds(r, 1), :], y_hbm.at[pl.ds(dst_ref[0, 0, r], 1), :], sem_out)

    @pl.when(i < nused_ref[0])
    def _():
        def start_in(r, _):
            gather(r).start()
            return 0

        def wait_in(r, _):
            gather(r).wait()
            return 0

        lax.fori_loop(0, rows, start_in, 0)
        lax.fori_loop(0, rows, wait_in, 0)
        x = xbuf[...].astype(BF16)
        h = jax.nn.silu(jnp.dot(x, w1_ref[0], preferred_element_type=F32))
        h = h * jnp.dot(x, w3_ref[0], preferred_element_type=F32)
        ybuf[...] = jnp.dot(h.astype(BF16), w2_ref[0], preferred_element_type=F32)

        def start_out(r, _):
            @pl.when(dst_ref[0, 0, r] >= 0)
            def _():
                scatter(r).start()
            return 0

        def wait_out(r, _):
            @pl.when(dst_ref[0, 0, r] >= 0)
            def _():
                scatter(r).wait()
            return 0

        lax.fori_loop(0, rows, start_out, 0)
        lax.fori_loop(0, rows, wait_out, 0)


def _experts(x1, block_e, n_used, slot_dst, w1, w3, w2, rows):
    n = x1.shape[0]
    n_blocks = block_e.shape[0]
    slot = pl.BlockSpec((1, 1, rows), lambda i, be, nu: (i, 0, 0), memory_space=pltpu.SMEM)
    wspec = lambda a: pl.BlockSpec((1,) + a.shape[1:], lambda i, be, nu: (be[i], 0, 0))
    return pl.pallas_call(
        _expert_kernel,
        grid_spec=pltpu.PrefetchScalarGridSpec(
            num_scalar_prefetch=2,
            grid=(n_blocks,),
            in_specs=[slot, pl.BlockSpec(memory_space=pl.ANY), wspec(w1), wspec(w3), wspec(w2)],
            out_specs=pl.BlockSpec(memory_space=pl.ANY),
            scratch_shapes=[pltpu.VMEM((rows, D_MODEL), F32), pltpu.VMEM((rows, D_MODEL), F32),
                            pltpu.SemaphoreType.DMA(()), pltpu.SemaphoreType.DMA(())],
        ),
        out_shape=jax.ShapeDtypeStruct((TOP_K * n, D_MODEL), F32),
        compiler_params=_cparams(("arbitrary",)),
        name="experts",
    )(block_e, n_used, slot_dst.reshape(n_blocks, 1, rows), x1, w1, w3, w2)


def _combine_kernel(x1_ref, y_ref, route_ref, g_ref, b_ref, o_ref):
    route = route_ref[...]
    ff = y_ref[:, :D_MODEL] * route[:, 2:3] + y_ref[:, D_MODEL:] * route[:, 3:4]
    o_ref[...] = _layer_norm(DEEPNORM_ALPHA * x1_ref[...] + ff, g_ref[...], b_ref[...])


def _combine(x1, y2, route, ln_g, ln_b, tm):
    n = x1.shape[0]
    row = lambda width: pl.BlockSpec((tm, width), lambda i: (i, 0))
    vec = pl.BlockSpec((1, D_MODEL), lambda i: (0, 0))
    return pl.pallas_call(
        _combine_kernel,
        grid=(n // tm,),
        in_specs=[row(D_MODEL), row(TOP_K * D_MODEL), row(LANES), vec, vec],
        out_specs=row(D_MODEL),
        out_shape=jax.ShapeDtypeStruct((n, D_MODEL), F32),
        compiler_params=_cparams(("parallel",)),
        name="combine_ln2",
    )(x1, y2.reshape(n, TOP_K * D_MODEL), route, ln_g.reshape(1, -1), ln_b.reshape(1, -1))


def _finish(o_nsa, bm, gm, x2, mw, ew, ln1, ln2, tm, rows):
    n = x2.shape[0]
    x1, route, counts = _merge(o_nsa, bm, gm, x2, mw, ln1[0], ln1[1], tm)
    sizes = counts[0, ROUTE_E0:ROUTE_E0 + N_EXPERTS].astype(jnp.int32)
    padded = (sizes + rows - 1) // rows * rows
    pad_end = jnp.cumsum(padded)
    pad_start = pad_end - padded
    e = route[:, 0:TOP_K].astype(jnp.int32)
    rank = route[:, 4:4 + TOP_K].astype(jnp.int32)
    dest = (pad_start[e] + rank).reshape(-1)
    n_blocks = -(-(n * TOP_K) // rows) + N_EXPERTS
    asg = jnp.arange(n * TOP_K, dtype=jnp.int32)
    slot_dst = jnp.full((n_blocks * rows,), -1, jnp.int32).at[dest].set(asg)
    blk0 = jnp.arange(n_blocks, dtype=jnp.int32) * rows
    block_e = jnp.sum(blk0[:, None] >= pad_end[None, :], axis=1).astype(jnp.int32)
    block_e = jnp.minimum(block_e, N_EXPERTS - 1)
    n_used = (pad_end[-1:] // rows).astype(jnp.int32)
    y2 = _experts(x1, block_e, n_used, slot_dst, *ew, rows)
    return _combine(x1, y2, route, ln2[0], ln2[1], tm)


PROMPT_ROWS = 256


def _feature_major(a):
    lead = a.shape[:-3]
    return jnp.moveaxis(a, -3, -1).reshape(lead + (2 * HEAD_DIM, a.shape[-3]))


def _row_major(a):
    lead = a.shape[:-2]
    return jnp.moveaxis(a.reshape(lead + (2, HEAD_DIM, a.shape[-1])), -1, -3)


def _pad_rows(a):
    return jnp.pad(a[:, None, :], ((0, 0), (0, HEAD_ROWS - 1), (0, 0)))


def kernel(x_prompt, x_sample, cache_kv_cmp, cache_kv_slc, state_kv_win, state_ret, page_table, w_in, b_in, cmp_pos, w_cmp1, w_cmp2, b_cmp2, ret_gn_g, ret_gn_b, w_nsa_proj, w_ret_proj, w_o, ln1_g, ln1_b, w_route_group, b_route_group, w_route_exp, b_route_exp, w_e1, w_e3, w_e2, ln2_g, ln2_b):
    nb, t_len, _ = x_prompt.shape
    ns, s_len, _ = x_sample.shape
    assert s_len == 1 and ns == LANES and t_len % (RET_STEP_CHUNKS * RET_CHUNK) == 0 and t_len >= WINDOW
    g = NSA_KV_GROUPS
    pw = _proj_weights(w_in, b_in)
    cw = _compress_weights(cmp_pos, w_cmp1, w_cmp2, b_cmp2)
    mw = _merge_weights(w_nsa_proj, w_ret_proj, w_o, w_route_group, b_route_group, w_route_exp, b_route_exp)
    ew = (w_e1.astype(BF16), w_e3.astype(BF16), w_e2.astype(BF16))
    ln1 = (ln1_g, ln1_b)
    ln2 = (ln2_g, ln2_b)

    cos_p, sin_p = _rope_tables(jnp.arange(t_len, dtype=jnp.int32))
    x2 = x_prompt.reshape(nb * t_len, D_MODEL)
    qa, kvt, ga, qb, kb, vb, gb, gm = _project(x2, nb, pw, cos_p, sin_p, PROMPT_ROWS)
    kc, kvct = _compress(kvt[0], cw)
    kslc, kwin, vslc, vwin = _kv_prep(kvt)
    o_nsa = _nsa_prompt(qa, ga, kc, kvct, kslc, vslc, kwin, vwin)
    bm, ret_p = _ret_prompt(qb, kb, vb, gb, ret_gn_g, ret_gn_b, nb)
    y_prompt = _finish(o_nsa, bm, gm, x2, mw, ew, ln1, ln2, PROMPT_ROWS, MOE_ROWS).reshape(x_prompt.shape)
    kv_cmp_p = _row_major(kvt[0])
    kv_slc_p = _row_major(kvt[1])
    win_p = _row_major(kvt[2][..., t_len - WINDOW:])

    n_pages = page_table.shape[1]
    past_len = n_pages * PAGE_SIZE
    cos_s, sin_s = _rope_tables(jnp.full((ns,), past_len, jnp.int32))
    xs2 = x_sample.reshape(ns, D_MODEL)
    qa, kvt_s, ga, qb, kb, vb, gb, gm = _project(xs2, 1, pw, cos_s, sin_s, ns)
    _, kvct_s = _compress_paged(_feature_major(cache_kv_cmp), page_table, cw)
    q8 = jnp.pad(qa.reshape(ns, g, NSA_GROUP_HEADS, HEAD_DIM),
                 ((0, 0), (0, 0), (0, HEAD_ROWS - NSA_GROUP_HEADS), (0, 0))).astype(BF16)
    o_cmp, imp = _dec_cmp(q8, kvct_s)
    picks = _dec_topk(imp[:, :, 0, :].reshape(ns * g, -1))
    idx = picks[:N_PAST_SEL].T.astype(jnp.int32)
    blocks_per_page = PAGE_SIZE // SEL_BLOCK
    seq = jnp.arange(ns * g, dtype=jnp.int32)[:, None] // g
    phys = page_table[seq, idx // blocks_per_page]
    half = idx % blocks_per_page
    gates = ga.reshape(ns, g, LANES)[:, :, :3 * NSA_GROUP_HEADS].reshape(ns, g, NSA_GROUP_HEADS, 3)
    gates = jnp.pad(gates, ((0, 0), (0, 0), (0, HEAD_ROWS - NSA_GROUP_HEADS), (0, LANES - 3)))
    o_s, win_t = _dec_attn(phys.reshape(-1), half.reshape(-1), q8, gates, o_cmp, _feature_major(cache_kv_slc),
                           kvt_s, _feature_major(state_kv_win))
    o_nsa_s = o_s[:, :, :NSA_GROUP_HEADS].reshape(ns, NSA_HEADS * HEAD_DIM).astype(BF16)
    bm8, ret_s = _ret_step(_pad_rows(qb), _pad_rows(kb), _pad_rows(vb), _pad_rows(gb), state_ret, ret_gn_g, ret_gn_b)
    y_sample = _finish(o_nsa_s, bm8[:, 0], gm, xs2, mw, ew, ln1, ln2, ns, SAMPLE_MOE_ROWS).reshape(x_sample.shape)
    new_rows = jnp.moveaxis(kvt_s[:, 0], -1, 1)
    kv_cmp_s = new_rows[0].reshape(ns, g, 1, 2, HEAD_DIM)
    kv_slc_s = new_rows[1].reshape(ns, g, 1, 2, HEAD_DIM)
    win_s = _row_major(win_t)
    return (y_prompt, y_sample, kv_cmp_p, kv_slc_p, win_p, ret_p, kv_cmp_s, kv_slc_s, win_s, ret_s)
```

```python
import functools

import numpy as np
import jax
import jax.numpy as jnp
from jax import lax
from jax.experimental import pallas as pl
from jax.experimental.pallas import tpu as pltpu
from jax.experimental.pallas import tpu_sc as plsc

F32 = jnp.float32
BF16 = jnp.bfloat16

D_MODEL = 1024
DEPTH = 1
PAGE_SIZE = 128
NSA_HEADS = 8
NSA_KV_GROUPS = 2
NSA_GROUP_HEADS = NSA_HEADS // NSA_KV_GROUPS
HEAD_DIM = 64
CMP_BLOCK = 32
CMP_STRIDE = 16
CMP_HIDDEN = 64
SEL_BLOCK = 64
N_SEL = 16
WINDOW = 512
Q_TILE = 128
FORCE_SCORE = 1.0e3
NEG_INF = -1.0e30
RET_HEADS = 4
RET_DK = 128
RET_DV = 128
RET_CHUNK = 128
ROPE_BASE = 10000.0
N_GROUPS = 4
EXPERTS_PER_GROUP = 8
N_EXPERTS = N_GROUPS * EXPERTS_PER_GROUP
TOP_K = 2
EXPERT_FF = 512
LN_EPS = 1e-5
DEEPNORM_ALPHA = (2.0 * DEPTH) ** 0.25

NSA_Q_COLS = NSA_HEADS * HEAD_DIM
NSA_KV_COLS = 3 * 2 * NSA_KV_GROUPS * HEAD_DIM
NSA_GATE_COLS = 3 * NSA_HEADS
RET_QK_COLS = RET_HEADS * RET_DK
RET_V_COLS = RET_HEADS * RET_DV

LANES = 128

P_QA = 0
P_GA = P_QA + NSA_Q_COLS
P_QB = P_GA + NSA_KV_GROUPS * LANES
P_KB = P_QB + RET_QK_COLS
P_VB = P_KB + RET_QK_COLS
P_GB = P_VB + RET_V_COLS
P_GM = P_GB + RET_V_COLS
VMEM_LIMIT = 56 * 1024 * 1024


def _cparams(sem):
    return pltpu.CompilerParams(dimension_semantics=sem, vmem_limit_bytes=VMEM_LIMIT)


def _proj_kernel(x_ref, w_ref, b_ref, wkv_ref, bkv_ref, cos_ref, sin_ref,
                 qa_ref, kcmp_ref, kslc_ref, kwin_ref, ga_ref, qb_ref, kb_ref, vb_ref, gb_ref, gm_ref):
    x = x_ref[...].astype(BF16)

    def cols(c0, n):
        h = jnp.dot(x, w_ref[:, c0:c0 + n], preferred_element_type=F32)
        return h + b_ref[:, c0:c0 + n]

    qa_ref[...] = cols(P_QA, 512) * (HEAD_DIM ** -0.5)
    ga_ref[...] = jax.nn.sigmoid(cols(P_GA, NSA_KV_GROUPS * LANES))
    cos = cos_ref[...]
    sin = sin_ref[...]
    for h in range(RET_HEADS):
        q = cols(P_QB + h * RET_DK, RET_DK)
        qb_ref[:, h * RET_DK:(h + 1) * RET_DK] = (q * cos + pltpu.roll(q, RET_DK // 2, 1) * sin).astype(BF16)
        k = cols(P_KB + h * RET_DK, RET_DK)
        kb_ref[:, h * RET_DK:(h + 1) * RET_DK] = (k * cos + pltpu.roll(k, RET_DK // 2, 1) * sin) * (RET_DK ** -0.5)
    vb_ref[...] = cols(P_VB, 512).astype(BF16)
    gb_ref[...] = cols(P_GB, 512)
    gm_ref[...] = cols(P_GM, 2048)
    kvt = lax.dot_general(wkv_ref[...], x, (((1,), (1,)), ((), ())), preferred_element_type=F32)
    kvt = kvt + bkv_ref[...]
    for br, out_ref in enumerate((kcmp_ref, kslc_ref, kwin_ref)):
        for g in range(NSA_KV_GROUPS):
            i = br * NSA_KV_GROUPS + g
            out_ref[0, g] = kvt[i * LANES:(i + 1) * LANES]


def _proj_weights(w_in, b_in):
    o_kv = NSA_Q_COLS
    o_ga = o_kv + NSA_KV_COLS
    o_qb = o_ga + NSA_GATE_COLS
    per_g = NSA_GATE_COLS // NSA_KV_GROUPS
    ga_cols = []
    for g in range(NSA_KV_GROUPS):
        ga_cols += [np.arange(o_ga + g * per_g, o_ga + (g + 1) * per_g), np.full(LANES - per_g, -1)]
    perm = np.concatenate([
        np.arange(0, NSA_Q_COLS),
        *ga_cols,
        np.arange(o_qb, o_qb + 2 * RET_QK_COLS + 2 * RET_V_COLS + 2 * D_MODEL),
    ])
    keep = jnp.asarray(perm >= 0)
    idx = jnp.asarray(np.maximum(perm, 0))
    w = jnp.where(keep[None, :], w_in[:, idx], 0.0).astype(BF16)
    b = jnp.where(keep, b_in[idx], 0.0)[None, :]
    br, g, kv, d = np.meshgrid(np.arange(3), np.arange(NSA_KV_GROUPS), np.arange(2), np.arange(HEAD_DIM), indexing="ij")
    kv_idx = jnp.asarray((o_kv + br * 256 + kv * 128 + g * 64 + d).reshape(-1))
    wkv = w_in[:, kv_idx].T.astype(BF16)
    bkv = b_in[kv_idx][:, None]
    return w, b, wkv, bkv


def _rope_tables(pos):
    half = RET_DK // 2
    inv = ROPE_BASE ** (-jnp.arange(half, dtype=F32) / half)
    ang = pos.astype(F32)[:, None] * inv[None, :]
    cos, sin = jnp.cos(ang), jnp.sin(ang)
    return jnp.concatenate([cos, cos], -1), jnp.concatenate([-sin, sin], -1)


def _project(x2, n_batch, pw, cos_t, sin_t, tm):
    w, b, wkv, bkv = pw
    n = x2.shape[0]
    t_len = n // n_batch
    nt = t_len // tm
    n_pos_blocks = cos_t.shape[0] // tm
    row = lambda i: (i, 0)
    const = lambda i: (0, 0)
    pos_map = lambda i: (i % n_pos_blocks, 0)
    kvt_shape = jax.ShapeDtypeStruct((n_batch, NSA_KV_GROUPS, LANES, t_len), F32)
    kvt_spec = pl.BlockSpec((1, NSA_KV_GROUPS, LANES, tm), lambda i: (i // nt, 0, 0, i % nt))
    out_shape = (
        jax.ShapeDtypeStruct((n, 512), F32),
        kvt_shape, kvt_shape, kvt_shape,
        jax.ShapeDtypeStruct((n, NSA_KV_GROUPS * LANES), F32),
        jax.ShapeDtypeStruct((n, 512), BF16),
        jax.ShapeDtypeStruct((n, 512), F32),
        jax.ShapeDtypeStruct((n, 512), BF16),
        jax.ShapeDtypeStruct((n, 512), F32),
        jax.ShapeDtypeStruct((n, 2048), F32),
    )
    return pl.pallas_call(
        _proj_kernel,
        grid=(n // tm,),
        in_specs=[
            pl.BlockSpec((tm, D_MODEL), row),
            pl.BlockSpec(w.shape, const),
            pl.BlockSpec(b.shape, const),
            pl.BlockSpec(wkv.shape, const),
            pl.BlockSpec(bkv.shape, const),
            pl.BlockSpec((tm, LANES), pos_map),
            pl.BlockSpec((tm, LANES), pos_map),
        ],
        out_specs=(
            pl.BlockSpec((tm, 512), row),
            kvt_spec, kvt_spec, kvt_spec,
            pl.BlockSpec((tm, NSA_KV_GROUPS * LANES), row),
            pl.BlockSpec((tm, 512), row),
            pl.BlockSpec((tm, 512), row),
            pl.BlockSpec((tm, 512), row),
            pl.BlockSpec((tm, 512), row),
            pl.BlockSpec((tm, 2048), row),
        ),
        out_shape=out_shape,
        compiler_params=_cparams(("parallel",)),
        name="proj",
    )(x2, w, b, wkv, bkv, cos_t, sin_t)


XPOSE_CHUNK = 512


def _compress_rows(xs_ref, wc_ref, w2_ref, b2_ref, pos_ref, wpe_ref, kc_ref, kvct_ref, perm_ref, nj):
    n_blk = 4 * nj
    pe = jnp.dot(pos_ref[...], wpe_ref[...], preferred_element_type=F32)[0:1]
    acc = jnp.zeros((n_blk, 2 * LANES), F32)
    for c2 in range(CMP_STRIDE // 2):
        xa = xs_ref[pl.ds(2 * c2, n_blk, stride=CMP_STRIDE), :]
        xb = xs_ref[pl.ds(2 * c2 + 1, n_blk, stride=CMP_STRIDE), :]
        x2 = jnp.concatenate([xa, xb], axis=1).astype(BF16)
        acc = acc + jnp.dot(x2, wc_ref[c2], preferred_element_type=F32)
    row = lax.broadcasted_iota(jnp.int32, (n_blk, LANES), 0)
    nxt = pltpu.roll(acc[:, LANES:], n_blk - 1, 0)
    second = jnp.where(row == n_blk - 1, 0.0, nxt)
    h = jax.nn.gelu(acc[:, :LANES] + second + pe)
    perm_ref[...] = jnp.dot(h.astype(BF16), w2_ref[...], preferred_element_type=F32) + b2_ref[...]
    lane = lax.broadcasted_iota(jnp.int32, (nj, LANES), 1)
    for rr in range(4):
        kv = perm_ref[pl.ds(rr, nj, stride=4), :]
        kc_ref[rr * nj:(rr + 1) * nj, :] = jnp.where(lane < HEAD_DIM, kv, 0.0).astype(BF16)
        kvct_ref[:, rr * nj:(rr + 1) * nj] = kv.T.astype(BF16)


def _compress_kernel(xt_ref, wc_ref, w2_ref, b2_ref, pos_ref, wpe_ref, kc_ref, kvct_ref, xs_ref, perm_ref, *, length):
    def xpose(i, _):
        t0 = pl.multiple_of(i * XPOSE_CHUNK, XPOSE_CHUNK)
        xs_ref[pl.ds(t0, XPOSE_CHUNK), :] = xt_ref[0, 0, :, pl.ds(t0, XPOSE_CHUNK)].T
        return 0

    lax.fori_loop(0, length // XPOSE_CHUNK, xpose, 0)
    _compress_rows(xs_ref, wc_ref, w2_ref, b2_ref, pos_ref, wpe_ref, kc_ref.at[0, 0], kvct_ref.at[0, 0], perm_ref,
                   length // SEL_BLOCK)


PAGES_PER_STEP = 16


def _compress_paged_kernel(pt_ref, *refs, n_pages, per_step):
    page_refs = refs[:per_step]
    wc_ref, w2_ref, b2_ref, pos_ref, wpe_ref, kc_ref, kvct_ref, xs_ref, perm_ref = refs[per_step:]
    s = pl.program_id(1)
    for k in range(per_step):
        r0 = pl.multiple_of((s * per_step + k) * PAGE_SIZE, PAGE_SIZE)
        for g in range(NSA_KV_GROUPS):
            xs_ref[g, pl.ds(r0, PAGE_SIZE), :] = page_refs[k][0, g].T

    @pl.when(s == n_pages // per_step - 1)
    def _():
        for g in range(NSA_KV_GROUPS):
            _compress_rows(xs_ref.at[g], wc_ref, w2_ref, b2_ref, pos_ref, wpe_ref, kc_ref.at[0, g], kvct_ref.at[0, g],
                           perm_ref, n_pages * PAGE_SIZE // SEL_BLOCK)


def _compress_paged(cache_t, page_table, cw):
    wc, w2, b2, pos, wpe = cw
    nb, n_pages = page_table.shape
    g = cache_t.shape[1]
    length = n_pages * PAGE_SIZE
    n_rows = length // CMP_STRIDE
    per_step = min(PAGES_PER_STEP, n_pages)
    steps = n_pages // per_step
    page_spec = lambda k: pl.BlockSpec(
        (1, g, LANES, PAGE_SIZE), lambda b, s, pt: (pt[b * n_pages + s * per_step + k], 0, 0, 0))
    const = lambda a: pl.BlockSpec(a.shape, lambda b, s, pt: (0,) * a.ndim)
    return pl.pallas_call(
        functools.partial(_compress_paged_kernel, n_pages=n_pages, per_step=per_step),
        grid_spec=pltpu.PrefetchScalarGridSpec(
            num_scalar_prefetch=1,
            grid=(nb, steps),
            in_specs=[page_spec(k) for k in range(per_step)] + [const(a) for a in (wc, w2, b2, pos, wpe)],
            out_specs=(
                pl.BlockSpec((1, g, n_rows, LANES), lambda b, s, pt: (b, 0, 0, 0)),
                pl.BlockSpec((1, g, LANES, n_rows), lambda b, s, pt: (b, 0, 0, 0)),
            ),
            scratch_shapes=[pltpu.VMEM((g, length, LANES), F32), pltpu.VMEM((n_rows, LANES), F32)],
        ),
        out_shape=(
            jax.ShapeDtypeStruct((nb, g, n_rows, LANES), BF16),
            jax.ShapeDtypeStruct((nb, g, LANES, n_rows), BF16),
        ),
        compiler_params=_cparams(("parallel", "arbitrary")),
        name="compress_paged",
    )(page_table.reshape(-1), *([cache_t] * per_step), wc, w2, b2, pos, wpe)


def _compress_weights(cmp_pos, w_cmp1, w_cmp2, b_cmp2):
    eye = jnp.eye(2, dtype=F32)
    w5 = w_cmp1.reshape(2, 2, CMP_STRIDE, HEAD_DIM, CMP_HIDDEN)
    wc = jnp.einsum("vhcde,vw->cvdhwe", w5, eye).reshape(CMP_STRIDE // 2, 2 * LANES, 2 * LANES).astype(BF16)
    w2 = jnp.einsum("ved,vw->vewd", w_cmp2, eye).reshape(LANES, LANES).astype(BF16)
    b2 = b_cmp2.reshape(1, LANES)
    wpe = jnp.einsum("vcde,vw->vcdwe", w_cmp1, eye).reshape(2 * CMP_BLOCK * HEAD_DIM, LANES).astype(BF16)
    pos = jnp.broadcast_to(cmp_pos.reshape(1, -1), (8, 2 * CMP_BLOCK * HEAD_DIM)).astype(BF16)
    return wc, w2, b2, pos, wpe


def _compress(kvt, cw):
    wc, w2, b2, pos, wpe = cw
    nb, g, _, length = kvt.shape
    n_rows = length // CMP_STRIDE
    const2 = lambda b, gg: (0, 0)
    return pl.pallas_call(
        functools.partial(_compress_kernel, length=length),
        grid=(nb, g),
        in_specs=[
            pl.BlockSpec((1, 1, LANES, length), lambda b, gg: (b, gg, 0, 0)),
            pl.BlockSpec(wc.shape, lambda b, gg: (0, 0, 0)),
            pl.BlockSpec(w2.shape, const2),
            pl.BlockSpec(b2.shape, const2),
            pl.BlockSpec(pos.shape, const2),
            pl.BlockSpec(wpe.shape, const2),
        ],
        out_specs=(
            pl.BlockSpec((1, 1, n_rows, LANES), lambda b, gg: (b, gg, 0, 0)),
            pl.BlockSpec((1, 1, LANES, n_rows), lambda b, gg: (b, gg, 0, 0)),
        ),
        out_shape=(
            jax.ShapeDtypeStruct((nb, g, n_rows, LANES), BF16),
            jax.ShapeDtypeStruct((nb, g, LANES, n_rows), BF16),
        ),
        scratch_shapes=[pltpu.VMEM((length, LANES), F32), pltpu.VMEM((n_rows, LANES), F32)],
        compiler_params=_cparams(("parallel", "parallel")),
        name="compress",
    )(kvt, wc, w2, b2, pos, wpe)


BIAS_BLOCKS = 16
KEY_TILE = 512
PREP_TILE = 1024
N_COLS = NSA_GROUP_HEADS * Q_TILE
SLC_UNROLL = 4
V_ROWS = HEAD_DIM + 16
WIN_KEYS = WINDOW + Q_TILE
CMP_ROW_CHOICES = (128,)


def _kv_prep_kernel(slc_ref, win_ref, kslc_ref, kwin_ref, vslc_ref, vwin_ref):
    shape = (PREP_TILE, LANES)
    lane = lax.broadcasted_iota(jnp.int32, shape, 1)
    blk = (lax.broadcasted_iota(jnp.int32, shape, 0) // SEL_BLOCK) % BIAS_BLOCKS
    onehot = jnp.where(lane - HEAD_DIM == blk, 1.0, 0.0)
    ks = slc_ref[0, 0].T
    kslc_ref[0, 0] = jnp.where(lane < HEAD_DIM, ks, onehot).astype(BF16)
    kw = win_ref[0, 0].T
    kwin_ref[0, 0] = jnp.where(lane < HEAD_DIM, kw, 0.0).astype(BF16)
    ones = jnp.ones((V_ROWS - HEAD_DIM, PREP_TILE), BF16)
    vslc_ref[0, 0] = jnp.concatenate([slc_ref[0, 0, HEAD_DIM:, :].astype(BF16), ones], axis=0)
    vwin_ref[0, 0] = jnp.concatenate([win_ref[0, 0, HEAD_DIM:, :].astype(BF16), ones], axis=0)


def _kv_prep(kvt_slc, kvt_win):
    nb, g, _, t_len = kvt_slc.shape
    in_spec = pl.BlockSpec((1, 1, LANES, PREP_TILE), lambda b, gg, i: (b, gg, 0, i))
    k_spec = pl.BlockSpec((1, 1, PREP_TILE, LANES), lambda b, gg, i: (b, gg, i, 0))
    v_spec = pl.BlockSpec((1, 1, V_ROWS, PREP_TILE), lambda b, gg, i: (b, gg, 0, i))
    k_shape = jax.ShapeDtypeStruct((nb, g, t_len, LANES), BF16)
    v_shape = jax.ShapeDtypeStruct((nb, g, V_ROWS, t_len), BF16)
    return pl.pallas_call(
        _kv_prep_kernel,
        grid=(nb, g, t_len // PREP_TILE),
        in_specs=[in_spec, in_spec],
        out_specs=(k_spec, k_spec, v_spec, v_spec),
        out_shape=(k_shape, k_shape, v_shape, v_shape),
        compiler_params=_cparams(("parallel", "parallel", "parallel")),
        name="kv_prep",
    )(kvt_slc, kvt_win)


def _online_update(carry, s, vt):
    m, acc = carry
    m_new = jnp.maximum(m, jnp.max(s, axis=0, keepdims=True))
    p = jnp.exp(s - m_new)
    acc = jnp.exp(m - m_new) * acc + jnp.dot(vt, p.astype(BF16), preferred_element_type=F32)
    return m_new, acc


def _softmax_out(acc):
    return acc[:HEAD_DIM] / acc[HEAD_DIM:HEAD_DIM + 1]


def _select_blocks(imp, q0):
    jrow = lax.broadcasted_iota(jnp.int32, imp.shape, 0)
    cur = (q0 + lax.broadcasted_iota(jnp.int32, imp.shape, 1)) // SEL_BLOCK
    work = jnp.where(jrow <= cur, imp, -1.0)
    bias = jnp.full(imp.shape, NEG_INF, F32)
    for forced_blk in (0, cur, cur - 1):
        hit = jrow == forced_blk
        bias = jnp.where(hit, 0.0, bias)
        work = jnp.where(hit, -3.0e38, work)
    jf = jrow.astype(F32)
    for _ in range(N_SEL - 3):
        mx = jnp.max(work, axis=0, keepdims=True)
        first = jnp.min(jnp.where(work == mx, jf, 1.0e9), axis=0, keepdims=True)
        hit = jf == first
        bias = jnp.where(hit, 0.0, bias)
        work = jnp.where(hit, -3.0e38, work)
    return bias


def _nsa_prompt_kernel(qa_ref, ga_ref, kc_ref, kvct_ref, kslc_ref, vslc_ref, kwin_ref, vwin_ref,
                       o_ref, qm_ref, bias_ref, ocmp_ref, *, nj):
    qi = pl.program_id(2)

    @pl.when(qi == 0)
    def _():
        qm_ref[...] = jnp.zeros_like(qm_ref)

    q_t = qa_ref[...].T
    for r in range(NSA_GROUP_HEADS):
        qm_ref[0:HEAD_DIM, r * Q_TILE:(r + 1) * Q_TILE] = q_t[r * HEAD_DIM:(r + 1) * HEAD_DIM].astype(BF16)
    ga_t = ga_ref[...].T

    def gate(branch):
        return jnp.concatenate([ga_t[r * 3 + branch:r * 3 + branch + 1] for r in range(NSA_GROUP_HEADS)], axis=1)

    q0 = qi * Q_TILE
    qpos_c = q0 + (lax.broadcasted_iota(jnp.int32, (1, N_COLS), 1) & (Q_TILE - 1))

    def compressed_and_select(n):
        kc = jnp.concatenate([kc_ref[0, 0, rr * nj:rr * nj + n, :] for rr in range(4)], axis=0)
        vct = jnp.concatenate([kvct_ref[0, 0, HEAD_DIM:, rr * nj:rr * nj + n] for rr in range(4)], axis=1)
        s = jnp.dot(kc, qm_ref[...], preferred_element_type=F32)
        row = lax.broadcasted_iota(jnp.int32, (4 * n, N_COLS), 0)
        cmp_end = (row % n) * SEL_BLOCK + (row // n) * CMP_STRIDE + (CMP_BLOCK - 1)
        vis = cmp_end <= qpos_c
        s = jnp.where(vis, s, NEG_INF)
        m = jnp.max(s, axis=0, keepdims=True)
        p = jnp.where(vis, jnp.exp(s - m), 0.0)
        l = jnp.sum(p, axis=0, keepdims=True)
        pn = p / jnp.where(l > 0.0, l, 1.0)
        ocmp_ref[...] = jnp.dot(vct, pn.astype(BF16), preferred_element_type=F32)
        p3 = pn[3 * n:4 * n]
        jrow4 = lax.broadcasted_iota(jnp.int32, (n, N_COLS), 0)
        prev = jnp.where(jrow4 == 0, 0.0, pltpu.roll(p3, 1, 0))
        a = pn[0:n] + pn[n:2 * n] + pn[2 * n:3 * n] + p3 + prev
        imp = a[:, 0:Q_TILE]
        for r in range(1, NSA_GROUP_HEADS):
            imp = imp + a[:, r * Q_TILE:(r + 1) * Q_TILE]
        bias16 = _select_blocks(imp, q0).astype(BF16)
        for r in range(NSA_GROUP_HEADS):
            bias_ref[0:n, r * Q_TILE:(r + 1) * Q_TILE] = bias16

    sizes = [n for n in CMP_ROW_CHOICES if n < nj] + [nj]
    need = 2 * qi + 2
    for v, n in enumerate(sizes):
        lo = sizes[v - 1] if v else 0
        pl.when((need > lo) & ((need <= n) | (v == len(sizes) - 1)))(functools.partial(compressed_and_select, n))
    o_cmp = ocmp_ref[...]


    init =(jnp.full((1, N_COLS), -jnp.inf, F32), jnp.zeros((V_ROWS, N_COLS), F32))
    q_rows = qm_ref[0:HEAD_DIM, :]
    pad_rows = jnp.zeros((LANES - HEAD_DIM - BIAS_BLOCKS, N_COLS), BF16)

    def slc_scores(st):
        b0 = pl.multiple_of((st * KEY_TILE // (SEL_BLOCK * BIAS_BLOCKS)) * BIAS_BLOCKS, BIAS_BLOCKS)
        qm = jnp.concatenate([q_rows, bias_ref[pl.ds(b0, BIAS_BLOCKS), :], pad_rows], axis=0)
        k0 = pl.multiple_of(st * KEY_TILE, KEY_TILE)
        return jnp.dot(kslc_ref[0, 0, pl.ds(k0, KEY_TILE), :], qm, preferred_element_type=F32)

    def slc_absorb(st, carry, sc):
        k0 = pl.multiple_of(st * KEY_TILE, KEY_TILE)
        return _online_update(carry, sc, vslc_ref[0, 0, :, pl.ds(k0, KEY_TILE)])

    def slc_group(i, carry):
        st0 = i * SLC_UNROLL
        sc = slc_scores(st0)
        for u in range(SLC_UNROLL):
            sc_next = slc_scores(st0 + u + 1) if u + 1 < SLC_UNROLL else None
            carry = slc_absorb(st0 + u, carry, sc)
            sc = sc_next
        return carry

    n_full = q0 // KEY_TILE
    n_groups = n_full // SLC_UNROLL
    carry = lax.fori_loop(0, n_groups, slc_group, init)
    carry = lax.fori_loop(n_groups * SLC_UNROLL, n_full, lambda st, c: slc_absorb(st, c, slc_scores(st)), carry)
    kpos = n_full * KEY_TILE + lax.broadcasted_iota(jnp.int32, (KEY_TILE, N_COLS), 0)
    sc_last = jnp.where(kpos <= qpos_c, slc_scores(n_full), NEG_INF)
    _, acc_s = slc_absorb(n_full, carry, sc_last)
    o_slc = _softmax_out(acc_s)

    kw0 = pl.multiple_of(jnp.maximum(q0 - WINDOW, 0), Q_TILE)
    sc = jnp.dot(kwin_ref[0, 0, pl.ds(kw0, WIN_KEYS), :], qm_ref[...], preferred_element_type=F32)
    dpos = qpos_c - (kw0 + lax.broadcasted_iota(jnp.int32, (WIN_KEYS, N_COLS), 0))
    sc = jnp.where(dpos >= 0, jnp.where(dpos < WINDOW, sc, NEG_INF), NEG_INF)
    p = jnp.exp(sc - jnp.max(sc, axis=0, keepdims=True))
    o_win = _softmax_out(jnp.dot(vwin_ref[0, 0, :, pl.ds(kw0, WIN_KEYS)], p.astype(BF16), preferred_element_type=F32))

    o = gate(0) * o_cmp + gate(1) * o_slc + gate(2) * o_win
    o_rows = jnp.concatenate([o[:, r * Q_TILE:(r + 1) * Q_TILE] for r in range(NSA_GROUP_HEADS)], axis=0)
    o_ref[...] = o_rows.T.astype(BF16)


def _nsa_prompt(qa, ga, kc, kvct, kslc, vslc, kwin, vwin):
    nb, g, t_len, _ = kslc.shape
    nq = t_len // Q_TILE
    nj = t_len // SEL_BLOCK
    n_cmp_rows = kc.shape[2]
    assert nj % BIAS_BLOCKS == 0 and n_cmp_rows == 4 * nj and t_len % KEY_TILE == 0 and t_len >= WIN_KEYS
    tile = lambda b, gg, i: (b * nq + i, gg)
    whole = lambda b, gg, i: (b, gg, 0, 0)
    return pl.pallas_call(
        functools.partial(_nsa_prompt_kernel, nj=nj),
        grid=(nb, g, nq),
        in_specs=[
            pl.BlockSpec((Q_TILE, NSA_GROUP_HEADS * HEAD_DIM), tile),
            pl.BlockSpec((Q_TILE, LANES), tile),
            pl.BlockSpec((1, 1, n_cmp_rows, LANES), whole),
            pl.BlockSpec((1, 1, LANES, n_cmp_rows), whole),
            pl.BlockSpec((1, 1, t_len, LANES), whole),
            pl.BlockSpec((1, 1, V_ROWS, t_len), whole),
            pl.BlockSpec((1, 1, t_len, LANES), whole),
            pl.BlockSpec((1, 1, V_ROWS, t_len), whole),
        ],
        out_specs=pl.BlockSpec((Q_TILE, NSA_GROUP_HEADS * HEAD_DIM), tile),
        out_shape=jax.ShapeDtypeStruct((nb * t_len, NSA_HEADS * HEAD_DIM), BF16),
        scratch_shapes=[pltpu.VMEM((LANES, N_COLS), BF16), pltpu.VMEM((nj, N_COLS), BF16),
                        pltpu.VMEM((HEAD_DIM, N_COLS), F32)],
        compiler_params=_cparams(("parallel", "parallel", "arbitrary")),
        name="nsa_prompt",
    )(qa, ga, kc, kvct, kslc, vslc, kwin, vwin)


HEAD_ROWS = 8
N_PAST_SEL = N_SEL - 1


def _softmax_lanes(s_list, mask_list):
    s_list = [jnp.where(mk, s, NEG_INF) for s, mk in zip(s_list, mask_list)]
    m = s_list[0].max(axis=-1, keepdims=True)
    for s in s_list[1:]:
        m = jnp.maximum(m, s.max(axis=-1, keepdims=True))
    p_list = [jnp.where(mk, jnp.exp(s - m), 0.0) for s, mk in zip(s_list, mask_list)]
    l = p_list[0].sum(axis=-1, keepdims=True)
    for p in p_list[1:]:
        l = l + p.sum(axis=-1, keepdims=True)
    inv = 1.0 / jnp.where(l > 0.0, l, 1.0)
    return [p * inv for p in p_list]


def _dot_nt(a, b):
    return lax.dot_general(a, b, (((1,), (1,)), ((), ())), preferred_element_type=F32)


def _dec_cmp_kernel(q_ref, kvct_ref, o_ref, imp_ref, *, nj):
    q = q_ref[0, 0]
    n = 4 * nj
    s = jnp.dot(q, kvct_ref[0, 0, :HEAD_DIM, :], preferred_element_type=F32)
    lane = lax.broadcasted_iota(jnp.int32, (HEAD_ROWS, n), 1)
    blk = 4 * (lane % nj) + lane // nj
    (pn,) = _softmax_lanes([s], [blk <= n - 2])
    o_ref[0, 0] = _dot_nt(pn.astype(BF16), kvct_ref[0, 0, HEAD_DIM:, :])
    row = lax.broadcasted_iota(jnp.int32, (HEAD_ROWS, n), 0)
    ph = jnp.sum(jnp.where(row < NSA_GROUP_HEADS, pn, 0.0), axis=0, keepdims=True)
    p3 = ph[:, 3 * nj:]
    prev = jnp.where(lane[0:1, :nj] == 0, 0.0, pltpu.roll(p3, 1, 1))
    imp = ph[:, :nj] + ph[:, nj:2 * nj] + ph[:, 2 * nj:3 * nj] + p3 + prev
    imp_ref[0, 0] = jnp.broadcast_to(imp, (HEAD_ROWS, nj))


def _dec_cmp(q8, kvct):
    nb, g, _, n = kvct.shape
    nj = n // 4
    blk = lambda *shape: pl.BlockSpec((1, 1) + shape, lambda b, gg: (b, gg, 0, 0))
    return pl.pallas_call(
        functools.partial(_dec_cmp_kernel, nj=nj),
        grid=(nb, g),
        in_specs=[blk(HEAD_ROWS, HEAD_DIM), blk(LANES, n)],
        out_specs=(blk(HEAD_ROWS, HEAD_DIM), blk(HEAD_ROWS, nj)),
        out_shape=(jax.ShapeDtypeStruct((nb, g, HEAD_ROWS, HEAD_DIM), F32),
                   jax.ShapeDtypeStruct((nb, g, HEAD_ROWS, nj), F32)),
        compiler_params=_cparams(("parallel", "parallel")),
        name="dec_cmp",
    )(q8, kvct)


def _dec_topk_kernel(imp_ref, idx_ref):
    work = imp_ref[...].T
    nj, n = work.shape
    jrow = lax.broadcasted_iota(jnp.int32, (nj, n), 0)
    jf = jrow.astype(F32)
    for forced_blk in (0, nj - 1):
        work = jnp.where(jrow == forced_blk, FORCE_SCORE, work)
    picks = []
    for _ in range(N_PAST_SEL):
        mx = jnp.max(work, axis=0, keepdims=True)
        first = jnp.min(jnp.where(work == mx, jf, 1.0e9), axis=0, keepdims=True)
        picks.append(first)
        work = jnp.where(jf == first, -3.0e38, work)
    picks.append(jnp.full((1, n), float(nj), F32))
    idx_ref[...] = jnp.concatenate(picks, axis=0)


def _dec_topk(imp2):
    n, nj = imp2.shape
    return pl.pallas_call(
        _dec_topk_kernel,
        out_shape=jax.ShapeDtypeStruct((N_SEL, n), F32),
        compiler_params=pltpu.CompilerParams(vmem_limit_bytes=VMEM_LIMIT),
        name="dec_topk",
    )(imp2)


def _dec_attn_kernel(phys_ref, half_ref, q_ref, gate_ref, ocmp_ref, *refs):
    page_refs = refs[:N_PAST_SEL]
    slcnew_ref, winnew_ref, win_ref, o_ref, winout_ref = refs[N_PAST_SEL:]
    b = pl.program_id(0)
    base = (b * pl.num_programs(1) + pl.program_id(1)) * N_PAST_SEL
    q = q_ref[0, 0]
    lane = lax.broadcasted_iota(jnp.int32, (HEAD_ROWS, LANES), 1)
    own = lane == b

    def attend(kv_blocks, masks):
        scores = [jnp.dot(q, kv[:HEAD_DIM].astype(BF16), preferred_element_type=F32) for kv in kv_blocks]
        probs = _softmax_lanes(scores, masks)
        out = jnp.zeros((HEAD_ROWS, HEAD_DIM), F32)
        for p, kv in zip(probs, kv_blocks):
            out = out + _dot_nt(p.astype(BF16), kv[HEAD_DIM:].astype(BF16))
        return out

    pages = [r[0, 0] for r in page_refs] + [slcnew_ref[0, 0]]
    masks = [(lane // SEL_BLOCK) == half_ref[base + k] for k in range(N_PAST_SEL)] + [own]
    o_slc = attend(pages, masks)
    st = win_ref[0, 0]
    win_c = st.shape[1]
    wlane = lax.broadcasted_iota(jnp.int32, (HEAD_ROWS, win_c), 1)
    o_win = attend([st, winnew_ref[0, 0]], [wlane > win_c - WINDOW, own])
    gates = gate_ref[0, 0]
    o_ref[0, 0] = gates[:, 0:1] * ocmp_ref[0, 0] + gates[:, 1:2] * o_slc + gates[:, 2:3] * o_win
    new = winnew_ref[0, 0]
    lane_full = lax.broadcasted_iota(jnp.int32, new.shape, 1)
    new_col = jnp.sum(jnp.where(lane_full == b, new, 0.0), axis=1, keepdims=True)
    col = lax.broadcasted_iota(jnp.int32, st.shape, 1)
    winout_ref[0, 0] = jnp.where(col == win_c - 1, new_col, pltpu.roll(st, win_c - 1, 1))


def _dec_attn(phys, half, q8, gates, o_cmp, slc_t, slc_new, win_new, win_t):
    nb, g, _, win_c = win_t.shape
    assert slc_new.shape[-1] == nb == LANES
    blk = lambda *shape: pl.BlockSpec((1, 1) + shape, lambda b, gg, ph, hf: (b, gg, 0, 0))
    page = lambda k: pl.BlockSpec((1, 1, LANES, PAGE_SIZE),
                                  lambda b, gg, ph, hf: (ph[(b * g + gg) * N_PAST_SEL + k], gg, 0, 0))
    new = pl.BlockSpec((1, 1, LANES, nb), lambda b, gg, ph, hf: (0, gg, 0, 0))
    return pl.pallas_call(
        _dec_attn_kernel,
        grid_spec=pltpu.PrefetchScalarGridSpec(
            num_scalar_prefetch=2,
            grid=(nb, g),
            in_specs=[blk(HEAD_ROWS, HEAD_DIM), blk(HEAD_ROWS, LANES), blk(HEAD_ROWS, HEAD_DIM)]
            + [page(k) for k in range(N_PAST_SEL)] + [new, new, blk(LANES, win_c)],
            out_specs=(blk(HEAD_ROWS, HEAD_DIM), blk(LANES, win_c)),
        ),
        out_shape=(jax.ShapeDtypeStruct((nb, g, HEAD_ROWS, HEAD_DIM), F32),
                   jax.ShapeDtypeStruct((nb, g, LANES, win_c), F32)),
        compiler_params=_cparams(("parallel", "parallel")),
        name="dec_attn",
    )(phys, half, q8, gates, o_cmp, *([slc_t] * N_PAST_SEL), slc_new, win_new, win_t)


RET_STEP_CHUNKS = 8


def _log_gamma():
    return jnp.log1p(-jnp.exp2(-5.0 - jnp.arange(RET_HEADS, dtype=F32)))


def _ret_tables(chunk):
    lg = _log_gamma()
    n = jnp.arange(chunk, dtype=F32)
    diff = n[:, None] - n[None, :]
    decay = jnp.exp(jnp.maximum(diff, 0.0)[None] * lg[:, None, None]) * (diff >= 0)[None]
    rep = lambda col: jnp.broadcast_to(col[:, :, None], (RET_HEADS, chunk, RET_DV))
    qdec = rep(jnp.exp((n + 1.0)[None, :] * lg[:, None]))
    kdec = rep(jnp.exp((chunk - 1.0 - n)[None, :] * lg[:, None]))
    sdec = jnp.broadcast_to(jnp.exp(chunk * lg)[:, None, None], (RET_HEADS, RET_DK, RET_DV))
    return decay.astype(F32), qdec, kdec, sdec


def _group_norm_gate(o, g, gn_g, gn_b):
    mu = jnp.mean(o, axis=-1, keepdims=True)
    var = jnp.mean(jnp.square(o - mu), axis=-1, keepdims=True)
    on = (o - mu) * lax.rsqrt(var + LN_EPS) * gn_g + gn_b
    return (jax.nn.silu(g) * on).astype(BF16)


def _ret_prompt_kernel(qb_ref, kb_ref, vb_ref, gb_ref, decay_ref, qdec_ref, kdec_ref, sdec_ref, gng_ref, gnb_ref,
                       bm_ref, state_ref, st_ref):
    @pl.when(pl.program_id(1) == 0)
    def _():
        st_ref[...] = jnp.zeros_like(st_ref)

    def chunk_step(c, _):
        rows = pl.ds(pl.multiple_of(c * RET_CHUNK, RET_CHUNK), RET_CHUNK)
        for h in range(RET_HEADS):
            cols = slice(h * RET_DK, (h + 1) * RET_DK)
            q = qb_ref[rows, cols]
            k = kb_ref[rows, cols]
            v = vb_ref[rows, cols]
            st = st_ref[h]
            inner = lax.dot_general(q, k.astype(BF16), (((1,), (1,)), ((), ())), preferred_element_type=F32)
            inner = inner * decay_ref[h]
            o = jnp.dot(inner.astype(BF16), v, preferred_element_type=F32)
            o = o + jnp.dot(q, st.astype(BF16), preferred_element_type=F32) * qdec_ref[h]
            kd = (k * kdec_ref[h]).astype(BF16)
            st_ref[h] = sdec_ref[h] * st + lax.dot_general(kd, v, (((0,), (0,)), ((), ())), preferred_element_type=F32)
            bm_ref[rows, cols] = _group_norm_gate(o, gb_ref[rows, cols], gng_ref[:, cols], gnb_ref[:, cols])
        return 0

    lax.fori_loop(0, RET_STEP_CHUNKS, chunk_step, 0)
    state_ref[0] = st_ref[...]


def _ret_prompt(qb, kb, vb, gb, gn_g, gn_b, nb):
    n = qb.shape[0]
    tt = RET_STEP_CHUNKS * RET_CHUNK
    nt = n // nb // tt
    decay, qdec, kdec, sdec = _ret_tables(RET_CHUNK)
    row = pl.BlockSpec((tt, RET_HEADS * RET_DK), lambda b, i: (b * nt + i, 0))
    tab = pl.BlockSpec((RET_HEADS, RET_CHUNK, RET_DV), lambda b, i: (0, 0, 0))
    vec = pl.BlockSpec((1, RET_HEADS * RET_DV), lambda b, i: (0, 0))
    return pl.pallas_call(
        _ret_prompt_kernel,
        grid=(nb, nt),
        in_specs=[row, row, row, row, tab, tab, tab, tab, vec, vec],
        out_specs=(row, pl.BlockSpec((1, RET_HEADS, RET_DK, RET_DV), lambda b, i: (b, 0, 0, 0))),
        out_shape=(jax.ShapeDtypeStruct((n, RET_HEADS * RET_DV), BF16),
                   jax.ShapeDtypeStruct((nb, RET_HEADS, RET_DK, RET_DV), F32)),
        scratch_shapes=[pltpu.VMEM((RET_HEADS, RET_DK, RET_DV), F32)],
        compiler_params=_cparams(("parallel", "arbitrary")),
        name="ret_prompt",
    )(qb, kb, vb, gb, decay, qdec, kdec, sdec, gn_g.reshape(1, -1), gn_b.reshape(1, -1))


def _ret_step_kernel(q_ref, k_ref, v_ref, g_ref, st_ref, qdec_ref, sdec_ref, gng_ref, gnb_ref, bm_ref, stout_ref):
    for h in range(RET_HEADS):
        cols = slice(h * RET_DK, (h + 1) * RET_DK)
        q = q_ref[0, :, cols]
        k = k_ref[0, :, cols].astype(BF16)
        v = v_ref[0, :, cols]
        st = st_ref[0, h]
        qk = jnp.sum(q.astype(F32) * k.astype(F32), axis=-1, keepdims=True)
        o = qk.astype(BF16).astype(F32) * v.astype(F32)
        o = o + jnp.dot(q, st.astype(BF16), preferred_element_type=F32) * qdec_ref[h]
        stout_ref[0, h] = sdec_ref[h] * st + lax.dot_general(k, v, (((0,), (0,)), ((), ())), preferred_element_type=F32)
        bm_ref[0, :, cols] = _group_norm_gate(o, g_ref[0, :, cols], gng_ref[:, cols], gnb_ref[:, cols])


def _ret_step(q8, k8, v8, g8, state, gn_g, gn_b):
    nb = state.shape[0]
    _, qdec, _, sdec = _ret_tables(1)
    tok = pl.BlockSpec((1, HEAD_ROWS, RET_HEADS * RET_DK), lambda b: (b, 0, 0))
    st = pl.BlockSpec((1, RET_HEADS, RET_DK, RET_DV), lambda b: (b, 0, 0, 0))
    vec = pl.BlockSpec((1, RET_HEADS * RET_DV), lambda b: (0, 0))
    return pl.pallas_call(
        _ret_step_kernel,
        grid=(nb,),
        in_specs=[tok, tok, tok, tok, st,
                  pl.BlockSpec((RET_HEADS, 1, RET_DV), lambda b: (0, 0, 0)),
                  pl.BlockSpec((RET_HEADS, RET_DK, RET_DV), lambda b: (0, 0, 0)), vec, vec],
        out_specs=(tok, st),
        out_shape=(jax.ShapeDtypeStruct((nb, HEAD_ROWS, RET_HEADS * RET_DV), BF16),
                   jax.ShapeDtypeStruct(state.shape, F32)),
        compiler_params=_cparams(("parallel",)),
        name="ret_step",
    )(q8, k8, v8, g8, state, qdec, sdec, gn_g.reshape(1, -1), gn_b.reshape(1, -1))


ROUTE_E0 = N_GROUPS
MOE_ROWS = 256
SAMPLE_MOE_ROWS = 32


def _layer_norm(v, g, b):
    mu = jnp.mean(v, axis=-1, keepdims=True)
    var = jnp.mean(jnp.square(v - mu), axis=-1, keepdims=True)
    return (v - mu) * lax.rsqrt(var + LN_EPS) * g + b


def _first_lane(hit, lane):
    return jnp.min(jnp.where(hit, lane, 4 * LANES), axis=-1, keepdims=True)


def _merge_kernel(on_ref, bm_ref, gm_ref, x_ref, wn_ref, wr_ref, wo_ref, g1_ref, b1_ref, wrt_ref, brt_ref, tri_ref,
                  x1_ref, route_ref, counts_ref, carry_ref):
    @pl.when(pl.program_id(0) == 0)
    def _():
        carry_ref[...] = jnp.zeros_like(carry_ref)

    gate_a = jax.nn.sigmoid(gm_ref[:, :D_MODEL])
    gate_b = jax.nn.sigmoid(gm_ref[:, D_MODEL:])
    m = gate_a * jnp.dot(on_ref[...], wn_ref[...], preferred_element_type=F32)
    m = m + gate_b * jnp.dot(bm_ref[...], wr_ref[...], preferred_element_type=F32)
    mix = jnp.dot(m.astype(BF16), wo_ref[...], preferred_element_type=F32)
    x1 = _layer_norm(DEEPNORM_ALPHA * x_ref[...] + mix, g1_ref[...], b1_ref[...])
    x1_ref[...] = x1

    lg = jnp.dot(x1.astype(BF16), wrt_ref[...], preferred_element_type=F32) + brt_ref[...]
    lane = lax.broadcasted_iota(jnp.int32, lg.shape, 1)
    gl = jnp.where(lane < N_GROUPS, lg, -jnp.inf)
    ge = jnp.exp(gl - jnp.max(gl, axis=-1, keepdims=True))
    pgrp = ge / jnp.sum(ge, axis=-1, keepdims=True)
    pg = jnp.max(pgrp, axis=-1, keepdims=True)
    grp = _first_lane(pgrp == pg, lane)
    lo = ROUTE_E0 + grp * EXPERTS_PER_GROUP
    in_g = (lane >= lo) & (lane < lo + EXPERTS_PER_GROUP)
    el = jnp.where(in_g, lg, -jnp.inf)
    ee = jnp.exp(el - jnp.max(el, axis=-1, keepdims=True))
    pc = jnp.where(in_g, ee / jnp.sum(ee, axis=-1, keepdims=True), -1.0)
    w1 = jnp.max(pc, axis=-1, keepdims=True)
    i1 = _first_lane(pc == w1, lane)
    pc2 = jnp.where(lane == i1, -1.0, pc)
    w2 = jnp.max(pc2, axis=-1, keepdims=True)
    i2 = _first_lane(pc2 == w2, lane)
    wsum = w1 + w2
    wt1 = pg * w1 / wsum
    wt2 = pg * w2 / wsum
    hit1 = lane == i1
    hit2 = lane == i2
    cnt = jnp.where(hit1, 1.0, 0.0) + jnp.where(hit2, 1.0, 0.0)
    base = carry_ref[0:1, :] + jnp.dot(tri_ref[...], cnt.astype(BF16), preferred_element_type=F32)
    rank1 = jnp.sum(jnp.where(hit1, base, 0.0), axis=-1, keepdims=True)
    rank2 = jnp.sum(jnp.where(hit2, base, 0.0), axis=-1, keepdims=True)
    carry = carry_ref[0:1, :] + jnp.sum(cnt, axis=0, keepdims=True)
    carry_ref[...] = jnp.broadcast_to(carry, carry_ref.shape)
    counts_ref[...] = jnp.broadcast_to(carry, counts_ref.shape)
    fields = ((i1 - ROUTE_E0).astype(F32), (i2 - ROUTE_E0).astype(F32), wt1, wt2, rank1, rank2)
    out = jnp.zeros(lg.shape, F32)
    for idx, val in enumerate(fields):
        out = jnp.where(lane == idx, val, out)
    route_ref[...] = out


def _merge_weights(w_nsa_proj, w_ret_proj, w_o, w_route_group, b_route_group, w_route_exp, b_route_exp):
    pad = LANES - N_GROUPS - N_EXPERTS
    wrt = jnp.concatenate([w_route_group, w_route_exp, jnp.zeros((D_MODEL, pad), F32)], axis=1).astype(BF16)
    brt = jnp.concatenate([b_route_group, b_route_exp, jnp.zeros((pad,), F32)])[None, :]
    return w_nsa_proj.astype(BF16), w_ret_proj.astype(BF16), w_o.astype(BF16), wrt, brt


def _merge(o_nsa, bm, gm, x2, mw, ln_g, ln_b, tm):
    wn, wr, wo, wrt, brt = mw
    n = x2.shape[0]
    tri = (jnp.arange(tm)[:, None] > jnp.arange(tm)[None, :]).astype(BF16)
    row = lambda width: pl.BlockSpec((tm, width), lambda i: (i, 0))
    const = lambda a: pl.BlockSpec(a.shape, lambda i: (0, 0))
    ln_g = ln_g.reshape(1, -1)
    ln_b = ln_b.reshape(1, -1)
    return pl.pallas_call(
        _merge_kernel,
        grid=(n // tm,),
        in_specs=[row(512), row(512), row(2 * D_MODEL), row(D_MODEL), const(wn), const(wr), const(wo),
                  const(ln_g), const(ln_b), const(wrt), const(brt), const(tri)],
        out_specs=(row(D_MODEL), row(LANES), pl.BlockSpec((8, LANES), lambda i: (0, 0))),
        out_shape=(jax.ShapeDtypeStruct((n, D_MODEL), F32), jax.ShapeDtypeStruct((n, LANES), F32),
                   jax.ShapeDtypeStruct((8, LANES), F32)),
        scratch_shapes=[pltpu.VMEM((8, LANES), F32)],
        compiler_params=_cparams(("arbitrary",)),
        name="merge_route",
    )(o_nsa, bm, gm, x2, wn, wr, wo, ln_g, ln_b, wrt, brt, tri)


SC_CHUNK_ROWS = 32


def _gather_rows(table, idx):
    info = plsc.get_sparse_core_info()
    n_cores = info.num_cores
    n_workers = n_cores * info.num_subcores
    n_idx, width = idx.shape[0], table.shape[1]
    per_worker = n_idx // n_workers
    chunk = max(c for c in range(8, SC_CHUNK_ROWS + 1, 8) if per_worker % c == 0)
    assert n_idx % (8 * n_workers) == 0
    mesh = plsc.VectorSubcoreMesh(core_axis_name="c", subcore_axis_name="s")

    def body(table_hbm, idx_hbm, out_hbm, idx_v, rows_v, sem):
        worker = lax.axis_index("s") * n_cores + lax.axis_index("c")
        base = worker * per_worker

        @pl.loop(0, per_worker // chunk)
        def _(c):
            off = pl.multiple_of(base + c * chunk, 8)
            pltpu.sync_copy(idx_hbm.at[pl.ds(off, chunk)], idx_v)
            pltpu.async_copy(table_hbm.at[idx_v], rows_v, sem).wait()
            pltpu.sync_copy(rows_v, out_hbm.at[pl.ds(off, chunk)])

    return pl.kernel(
        body, mesh=mesh,
        out_type=jax.ShapeDtypeStruct((n_idx, width), table.dtype),
        scratch_types=[pltpu.VMEM((chunk,), jnp.int32), pltpu.VMEM((chunk, width), table.dtype),
                       pltpu.SemaphoreType.DMA],
    )(table, idx)


def _expert_kernel(be_ref, nused_ref, x_ref, w1_ref, w3_ref, w2_ref, y_ref):
    @pl.when(pl.program_id(0) >= nused_ref[0])
    def _():
        y_ref[...] = jnp.zeros_like(y_ref)

    @pl.when(pl.program_id(0) < nused_ref[0])
    def _():
        x = x_ref[...].astype(BF16)
        h = jax.nn.silu(jnp.dot(x, w1_ref[0], preferred_element_type=F32))
        h = h * jnp.dot(x, w3_ref[0], preferred_element_type=F32)
        y_ref[...] = jnp.dot(h.astype(BF16), w2_ref[0], preferred_element_type=F32)


def _experts(x_sorted, block_e, n_used, w1, w3, w2, rows):
    n_blocks = block_e.shape[0]
    xspec = pl.BlockSpec((rows, D_MODEL), lambda i, be, nu: (i, 0))
    wspec = lambda a: pl.BlockSpec((1,) + a.shape[1:], lambda i, be, nu: (be[i], 0, 0))
    return pl.pallas_call(
        _expert_kernel,
        grid_spec=pltpu.PrefetchScalarGridSpec(
            num_scalar_prefetch=2,
            grid=(n_blocks,),
            in_specs=[xspec, wspec(w1), wspec(w3), wspec(w2)],
            out_specs=xspec,
        ),
        out_shape=jax.ShapeDtypeStruct(x_sorted.shape, F32),
        compiler_params=_cparams(("arbitrary",)),
        name="experts",
    )(block_e, n_used, x_sorted, w1, w3, w2)


def _combine_kernel(x1_ref, y0_ref, y1_ref, route_ref, g_ref, b_ref, o_ref):
    route = route_ref[...]
    ff = y0_ref[...] * route[:, 2:3] + y1_ref[...] * route[:, 3:4]
    o_ref[...] = _layer_norm(DEEPNORM_ALPHA * x1_ref[...] + ff, g_ref[...], b_ref[...])


def _combine(x1, y2, route, ln_g, ln_b, tm):
    n = x1.shape[0]
    row = lambda width: pl.BlockSpec((tm, width), lambda i: (i, 0))
    second = pl.BlockSpec((tm, D_MODEL), lambda i: (i + n // tm, 0))
    vec = pl.BlockSpec((1, D_MODEL), lambda i: (0, 0))
    return pl.pallas_call(
        _combine_kernel,
        grid=(n // tm,),
        in_specs=[row(D_MODEL), row(D_MODEL), second, row(LANES), vec, vec],
        out_specs=row(D_MODEL),
        out_shape=jax.ShapeDtypeStruct((n, D_MODEL), F32),
        compiler_params=_cparams(("parallel",)),
        name="combine_ln2",
    )(x1, y2, y2, route, ln_g.reshape(1, -1), ln_b.reshape(1, -1))


def _finish(o_nsa, bm, gm, x2, mw, ew, ln1, ln2, tm, rows):
    n = x2.shape[0]
    x1, route, counts = _merge(o_nsa, bm, gm, x2, mw, ln1[0], ln1[1], tm)
    sizes = counts[0, ROUTE_E0:ROUTE_E0 + N_EXPERTS].astype(jnp.int32)
    padded = (sizes + rows - 1) // rows * rows
    pad_end = jnp.cumsum(padded)
    pad_start = pad_end - padded
    e = route[:, 0:TOP_K].astype(jnp.int32)
    rank = route[:, 4:4 + TOP_K].astype(jnp.int32)
    dest = (pad_start[e] + rank).T.reshape(-1)
    n_blocks = -(-(n * TOP_K) // rows) + N_EXPERTS
    tok = jnp.tile(jnp.arange(n, dtype=jnp.int32), TOP_K)
    slot_tok = jnp.zeros((n_blocks * rows,), jnp.int32).at[dest].set(tok)
    blk0 = jnp.arange(n_blocks, dtype=jnp.int32) * rows
    block_e = jnp.sum(blk0[:, None] >= pad_end[None, :], axis=1).astype(jnp.int32)
    block_e = jnp.minimum(block_e, N_EXPERTS - 1)
    n_used = (pad_end[-1:] // rows).astype(jnp.int32)
    y_sorted = _experts(_gather_rows(x1, slot_tok), block_e, n_used, *ew, rows)
    return _combine(x1, _gather_rows(y_sorted, dest), route, ln2[0], ln2[1], tm)


PROMPT_ROWS = 256


def _feature_major(a):
    lead = a.shape[:-3]
    return jnp.moveaxis(a, -3, -1).reshape(lead + (2 * HEAD_DIM, a.shape[-3]))


def _row_major(a):
    lead = a.shape[:-2]
    return jnp.moveaxis(a.reshape(lead + (2, HEAD_DIM, a.shape[-1])), -1, -3)


def _pad_rows(a):
    return jnp.pad(a[:, None, :], ((0, 0), (0, HEAD_ROWS - 1), (0, 0)))


def kernel(x_prompt, x_sample, cache_kv_cmp, cache_kv_slc, state_kv_win, state_ret, page_table, w_in, b_in, cmp_pos, w_cmp1, w_cmp2, b_cmp2, ret_gn_g, ret_gn_b, w_nsa_proj, w_ret_proj, w_o, ln1_g, ln1_b, w_route_group, b_route_group, w_route_exp, b_route_exp, w_e1, w_e3, w_e2, ln2_g, ln2_b):
    nb, t_len, _ = x_prompt.shape
    ns, s_len, _ = x_sample.shape
    assert s_len == 1 and ns == LANES and t_len % (RET_STEP_CHUNKS * RET_CHUNK) == 0 and t_len >= WINDOW
    g = NSA_KV_GROUPS
    pw = _proj_weights(w_in, b_in)
    cw = _compress_weights(cmp_pos, w_cmp1, w_cmp2, b_cmp2)
    mw = _merge_weights(w_nsa_proj, w_ret_proj, w_o, w_route_group, b_route_group, w_route_exp, b_route_exp)
    ew = (w_e1.astype(BF16), w_e3.astype(BF16), w_e2.astype(BF16))
    ln1 = (ln1_g, ln1_b)
    ln2 = (ln2_g, ln2_b)

    cos_p, sin_p = _rope_tables(jnp.arange(t_len, dtype=jnp.int32))
    x2 = x_prompt.reshape(nb * t_len, D_MODEL)
    qa, kvt_cmp, kvt_slc, kvt_win, ga, qb, kb, vb, gb, gm = _project(x2, nb, pw, cos_p, sin_p, PROMPT_ROWS)
    kc, kvct = _compress(kvt_cmp, cw)
    kslc, kwin, vslc, vwin = _kv_prep(kvt_slc, kvt_win)
    o_nsa = _nsa_prompt(qa, ga, kc, kvct, kslc, vslc, kwin, vwin)
    bm, ret_p = _ret_prompt(qb, kb, vb, gb, ret_gn_g, ret_gn_b, nb)
    y_prompt = _finish(o_nsa, bm, gm, x2, mw, ew, ln1, ln2, PROMPT_ROWS, MOE_ROWS).reshape(x_prompt.shape)
    kv_cmp_p = _row_major(kvt_cmp)
    kv_slc_p = _row_major(kvt_slc)
    win_p = _row_major(kvt_win[..., t_len - WINDOW:])

    n_pages = page_table.shape[1]
    past_len = n_pages * PAGE_SIZE
    cos_s, sin_s = _rope_tables(jnp.full((ns,), past_len, jnp.int32))
    xs2 = x_sample.reshape(ns, D_MODEL)
    qa, new_cmp, new_slc, new_win, ga, qb, kb, vb, gb, gm = _project(xs2, 1, pw, cos_s, sin_s, ns)
    _, kvct_s = _compress_paged(_feature_major(cache_kv_cmp), page_table, cw)
    q8 = jnp.pad(qa.reshape(ns, g, NSA_GROUP_HEADS, HEAD_DIM),
                 ((0, 0), (0, 0), (0, HEAD_ROWS - NSA_GROUP_HEADS), (0, 0))).astype(BF16)
    o_cmp, imp = _dec_cmp(q8, kvct_s)
    picks = _dec_topk(imp[:, :, 0, :].reshape(ns * g, -1))
    idx = picks[:N_PAST_SEL].T.astype(jnp.int32)
    blocks_per_page = PAGE_SIZE // SEL_BLOCK
    seq = jnp.arange(ns * g, dtype=jnp.int32)[:, None] // g
    phys = page_table[seq, idx // blocks_per_page]
    half = idx % blocks_per_page
    gates = ga.reshape(ns, g, LANES)[:, :, :3 * NSA_GROUP_HEADS].reshape(ns, g, NSA_GROUP_HEADS, 3)
    gates = jnp.pad(gates, ((0, 0), (0, 0), (0, HEAD_ROWS - NSA_GROUP_HEADS), (0, LANES - 3)))
    o_s, win_t = _dec_attn(phys.reshape(-1), half.reshape(-1), q8, gates, o_cmp, _feature_major(cache_kv_slc),
                           new_slc, new_win, _feature_major(state_kv_win))
    o_nsa_s = o_s[:, :, :NSA_GROUP_HEADS].reshape(ns, NSA_HEADS * HEAD_DIM).astype(BF16)
    bm8, ret_s = _ret_step(_pad_rows(qb), _pad_rows(kb), _pad_rows(vb), _pad_rows(gb), state_ret, ret_gn_g, ret_gn_b)
    y_sample = _finish(o_nsa_s, bm8[:, 0], gm, xs2, mw, ew, ln1, ln2, ns, SAMPLE_MOE_ROWS).reshape(x_sample.shape)
    kv_cmp_s = jnp.moveaxis(new_cmp[0], -1, 0).reshape(ns, g, 1, 2, HEAD_DIM)
    kv_slc_s = jnp.moveaxis(new_slc[0], -1, 0).reshape(ns, g, 1, 2, HEAD_DIM)
    win_s = _row_major(win_t)
    return (y_prompt, y_sample, kv_cmp_p, kv_slc_p, win_p, ret_p, kv_cmp_s, kv_slc_s, win_s, ret_s)
```

```python
import functools

import numpy as np
import jax
import jax.numpy as jnp
from jax import lax
from jax.experimental import pallas as pl
from jax.experimental.pallas import tpu as pltpu
from jax.experimental.pallas import tpu_sc as plsc

F32 = jnp.float32
BF16 = jnp.bfloat16

D_MODEL = 1024
DEPTH = 1
PAGE_SIZE = 128
NSA_HEADS = 8
NSA_KV_GROUPS = 2
NSA_GROUP_HEADS = NSA_HEADS // NSA_KV_GROUPS
HEAD_DIM = 64
CMP_BLOCK = 32
CMP_STRIDE = 16
CMP_HIDDEN = 64
SEL_BLOCK = 64
N_SEL = 16
WINDOW = 512
Q_TILE = 128
FORCE_SCORE = 1.0e3
NEG_INF = -1.0e30
RET_HEADS = 4
RET_DK = 128
RET_DV = 128
RET_CHUNK = 128
ROPE_BASE = 10000.0
N_GROUPS = 4
EXPERTS_PER_GROUP = 8
N_EXPERTS = N_GROUPS * EXPERTS_PER_GROUP
TOP_K = 2
EXPERT_FF = 512
LN_EPS = 1e-5
DEEPNORM_ALPHA = (2.0 * DEPTH) ** 0.25

NSA_Q_COLS = NSA_HEADS * HEAD_DIM
NSA_KV_COLS = 3 * 2 * NSA_KV_GROUPS * HEAD_DIM
NSA_GATE_COLS = 3 * NSA_HEADS
RET_QK_COLS = RET_HEADS * RET_DK
RET_V_COLS = RET_HEADS * RET_DV

LANES = 128

P_QA = 0
P_GA = P_QA + NSA_Q_COLS
P_QB = P_GA + NSA_KV_GROUPS * LANES
P_KB = P_QB + RET_QK_COLS
P_VB = P_KB + RET_QK_COLS
P_GB = P_VB + RET_V_COLS
P_GM = P_GB + RET_V_COLS
VMEM_LIMIT = 56 * 1024 * 1024


def _cparams(sem):
    return pltpu.CompilerParams(dimension_semantics=sem, vmem_limit_bytes=VMEM_LIMIT)


def _proj_kernel(x_ref, w_ref, b_ref, wkv_ref, bkv_ref, cos_ref, sin_ref,
                 qa_ref, kcmp_ref, kslc_ref, kwin_ref, ga_ref, qb_ref, kb_ref, vb_ref, gb_ref, gm_ref):
    x = x_ref[...].astype(BF16)

    def cols(c0, n):
        h = jnp.dot(x, w_ref[:, c0:c0 + n], preferred_element_type=F32)
        return h + b_ref[:, c0:c0 + n]

    qa_ref[...] = cols(P_QA, 512) * (HEAD_DIM ** -0.5)
    ga_ref[...] = jax.nn.sigmoid(cols(P_GA, NSA_KV_GROUPS * LANES))
    cos = cos_ref[...]
    sin = sin_ref[...]
    for h in range(RET_HEADS):
        q = cols(P_QB + h * RET_DK, RET_DK)
        qb_ref[:, h * RET_DK:(h + 1) * RET_DK] = (q * cos + pltpu.roll(q, RET_DK // 2, 1) * sin).astype(BF16)
        k = cols(P_KB + h * RET_DK, RET_DK)
        kb_ref[:, h * RET_DK:(h + 1) * RET_DK] = (k * cos + pltpu.roll(k, RET_DK // 2, 1) * sin) * (RET_DK ** -0.5)
    vb_ref[...] = cols(P_VB, 512).astype(BF16)
    gb_ref[...] = cols(P_GB, 512)
    gm_ref[...] = cols(P_GM, 2048)
    kvt = lax.dot_general(wkv_ref[...], x, (((1,), (1,)), ((), ())), preferred_element_type=F32)
    kvt = kvt + bkv_ref[...]
    for br, out_ref in enumerate((kcmp_ref, kslc_ref, kwin_ref)):
        for g in range(NSA_KV_GROUPS):
            i = br * NSA_KV_GROUPS + g
            out_ref[0, g] = kvt[i * LANES:(i + 1) * LANES]


def _proj_weights(w_in, b_in):
    o_kv = NSA_Q_COLS
    o_ga = o_kv + NSA_KV_COLS
    o_qb = o_ga + NSA_GATE_COLS
    per_g = NSA_GATE_COLS // NSA_KV_GROUPS
    ga_cols = []
    for g in range(NSA_KV_GROUPS):
        ga_cols += [np.arange(o_ga + g * per_g, o_ga + (g + 1) * per_g), np.full(LANES - per_g, -1)]
    perm = np.concatenate([
        np.arange(0, NSA_Q_COLS),
        *ga_cols,
        np.arange(o_qb, o_qb + 2 * RET_QK_COLS + 2 * RET_V_COLS + 2 * D_MODEL),
    ])
    keep = jnp.asarray(perm >= 0)
    idx = jnp.asarray(np.maximum(perm, 0))
    w = jnp.where(keep[None, :], w_in[:, idx], 0.0).astype(BF16)
    b = jnp.where(keep, b_in[idx], 0.0)[None, :]
    br, g, kv, d = np.meshgrid(np.arange(3), np.arange(NSA_KV_GROUPS), np.arange(2), np.arange(HEAD_DIM), indexing="ij")
    kv_idx = jnp.asarray((o_kv + br * 256 + kv * 128 + g * 64 + d).reshape(-1))
    wkv = w_in[:, kv_idx].T.astype(BF16)
    bkv = b_in[kv_idx][:, None]
    return w, b, wkv, bkv


def _rope_tables(pos):
    half = RET_DK // 2
    inv = ROPE_BASE ** (-jnp.arange(half, dtype=F32) / half)
    ang = pos.astype(F32)[:, None] * inv[None, :]
    cos, sin = jnp.cos(ang), jnp.sin(ang)
    return jnp.concatenate([cos, cos], -1), jnp.concatenate([-sin, sin], -1)


def _project(x2, n_batch, pw, cos_t, sin_t, tm):
    w, b, wkv, bkv = pw
    n = x2.shape[0]
    t_len = n // n_batch
    nt = t_len // tm
    n_pos_blocks = cos_t.shape[0] // tm
    row = lambda i: (i, 0)
    const = lambda i: (0, 0)
    pos_map = lambda i: (i % n_pos_blocks, 0)
    kvt_shape = jax.ShapeDtypeStruct((n_batch, NSA_KV_GROUPS, LANES, t_len), F32)
    kvt_spec = pl.BlockSpec((1, NSA_KV_GROUPS, LANES, tm), lambda i: (i // nt, 0, 0, i % nt))
    out_shape = (
        jax.ShapeDtypeStruct((n, 512), F32),
        kvt_shape, kvt_shape, kvt_shape,
        jax.ShapeDtypeStruct((n, NSA_KV_GROUPS * LANES), F32),
        jax.ShapeDtypeStruct((n, 512), BF16),
        jax.ShapeDtypeStruct((n, 512), F32),
        jax.ShapeDtypeStruct((n, 512), BF16),
        jax.ShapeDtypeStruct((n, 512), F32),
        jax.ShapeDtypeStruct((n, 2048), F32),
    )
    return pl.pallas_call(
        _proj_kernel,
        grid=(n // tm,),
        in_specs=[
            pl.BlockSpec((tm, D_MODEL), row),
            pl.BlockSpec(w.shape, const),
            pl.BlockSpec(b.shape, const),
            pl.BlockSpec(wkv.shape, const),
            pl.BlockSpec(bkv.shape, const),
            pl.BlockSpec((tm, LANES), pos_map),
            pl.BlockSpec((tm, LANES), pos_map),
        ],
        out_specs=(
            pl.BlockSpec((tm, 512), row),
            kvt_spec, kvt_spec, kvt_spec,
            pl.BlockSpec((tm, NSA_KV_GROUPS * LANES), row),
            pl.BlockSpec((tm, 512), row),
            pl.BlockSpec((tm, 512), row),
            pl.BlockSpec((tm, 512), row),
            pl.BlockSpec((tm, 512), row),
            pl.BlockSpec((tm, 2048), row),
        ),
        out_shape=out_shape,
        compiler_params=_cparams(("parallel",)),
        name="proj",
    )(x2, w, b, wkv, bkv, cos_t, sin_t)


XPOSE_CHUNK = 512


def _compress_rows(xs_ref, wc_ref, w2_ref, b2_ref, pos_ref, wpe_ref, kc_ref, kvct_ref, perm_ref, nj):
    n_blk = 4 * nj
    pe = jnp.dot(pos_ref[...], wpe_ref[...], preferred_element_type=F32)[0:1]
    acc = jnp.zeros((n_blk, 2 * LANES), F32)
    for c2 in range(CMP_STRIDE // 2):
        xa = xs_ref[pl.ds(2 * c2, n_blk, stride=CMP_STRIDE), :]
        xb = xs_ref[pl.ds(2 * c2 + 1, n_blk, stride=CMP_STRIDE), :]
        x2 = jnp.concatenate([xa, xb], axis=1).astype(BF16)
        acc = acc + jnp.dot(x2, wc_ref[c2], preferred_element_type=F32)
    row = lax.broadcasted_iota(jnp.int32, (n_blk, LANES), 0)
    nxt = pltpu.roll(acc[:, LANES:], n_blk - 1, 0)
    second = jnp.where(row == n_blk - 1, 0.0, nxt)
    h = jax.nn.gelu(acc[:, :LANES] + second + pe)
    perm_ref[...] = jnp.dot(h.astype(BF16), w2_ref[...], preferred_element_type=F32) + b2_ref[...]
    lane = lax.broadcasted_iota(jnp.int32, (nj, LANES), 1)
    for rr in range(4):
        kv = perm_ref[pl.ds(rr, nj, stride=4), :]
        kc_ref[rr * nj:(rr + 1) * nj, :] = jnp.where(lane < HEAD_DIM, kv, 0.0).astype(BF16)
        kvct_ref[:, rr * nj:(rr + 1) * nj] = kv.T.astype(BF16)


def _compress_kernel(xt_ref, wc_ref, w2_ref, b2_ref, pos_ref, wpe_ref, kc_ref, kvct_ref, xs_ref, perm_ref, *, length):
    def xpose(i, _):
        t0 = pl.multiple_of(i * XPOSE_CHUNK, XPOSE_CHUNK)
        xs_ref[pl.ds(t0, XPOSE_CHUNK), :] = xt_ref[0, 0, :, pl.ds(t0, XPOSE_CHUNK)].T
        return 0

    lax.fori_loop(0, length // XPOSE_CHUNK, xpose, 0)
    _compress_rows(xs_ref, wc_ref, w2_ref, b2_ref, pos_ref, wpe_ref, kc_ref.at[0, 0], kvct_ref.at[0, 0], perm_ref,
                   length // SEL_BLOCK)


PAGES_PER_STEP = 16


def _compress_paged_kernel(pt_ref, *refs, n_pages, per_step):
    page_refs = refs[:per_step]
    wc_ref, w2_ref, b2_ref, pos_ref, wpe_ref, kc_ref, kvct_ref, xs_ref, perm_ref = refs[per_step:]
    s = pl.program_id(1)
    for k in range(per_step):
        r0 = pl.multiple_of((s * per_step + k) * PAGE_SIZE, PAGE_SIZE)
        for g in range(NSA_KV_GROUPS):
            xs_ref[g, pl.ds(r0, PAGE_SIZE), :] = page_refs[k][0, g].T

    @pl.when(s == n_pages // per_step - 1)
    def _():
        for g in range(NSA_KV_GROUPS):
            _compress_rows(xs_ref.at[g], wc_ref, w2_ref, b2_ref, pos_ref, wpe_ref, kc_ref.at[0, g], kvct_ref.at[0, g],
                           perm_ref, n_pages * PAGE_SIZE // SEL_BLOCK)


def _compress_paged(cache_t, page_table, cw):
    wc, w2, b2, pos, wpe = cw
    nb, n_pages = page_table.shape
    g = cache_t.shape[1]
    length = n_pages * PAGE_SIZE
    n_rows = length // CMP_STRIDE
    per_step = min(PAGES_PER_STEP, n_pages)
    steps = n_pages // per_step
    page_spec = lambda k: pl.BlockSpec(
        (1, g, LANES, PAGE_SIZE), lambda b, s, pt: (pt[b * n_pages + s * per_step + k], 0, 0, 0))
    const = lambda a: pl.BlockSpec(a.shape, lambda b, s, pt: (0,) * a.ndim)
    return pl.pallas_call(
        functools.partial(_compress_paged_kernel, n_pages=n_pages, per_step=per_step),
        grid_spec=pltpu.PrefetchScalarGridSpec(
            num_scalar_prefetch=1,
            grid=(nb, steps),
            in_specs=[page_spec(k) for k in range(per_step)] + [const(a) for a in (wc, w2, b2, pos, wpe)],
            out_specs=(
                pl.BlockSpec((1, g, n_rows, LANES), lambda b, s, pt: (b, 0, 0, 0)),
                pl.BlockSpec((1, g, LANES, n_rows), lambda b, s, pt: (b, 0, 0, 0)),
            ),
            scratch_shapes=[pltpu.VMEM((g, length, LANES), F32), pltpu.VMEM((n_rows, LANES), F32)],
        ),
        out_shape=(
            jax.ShapeDtypeStruct((nb, g, n_rows, LANES), BF16),
            jax.ShapeDtypeStruct((nb, g, LANES, n_rows), BF16),
        ),
        compiler_params=_cparams(("parallel", "arbitrary")),
        name="compress_paged",
    )(page_table.reshape(-1), *([cache_t] * per_step), wc, w2, b2, pos, wpe)


def _compress_weights(cmp_pos, w_cmp1, w_cmp2, b_cmp2):
    eye = jnp.eye(2, dtype=F32)
    w5 = w_cmp1.reshape(2, 2, CMP_STRIDE, HEAD_DIM, CMP_HIDDEN)
    wc = jnp.einsum("vhcde,vw->cvdhwe", w5, eye).reshape(CMP_STRIDE // 2, 2 * LANES, 2 * LANES).astype(BF16)
    w2 = jnp.einsum("ved,vw->vewd", w_cmp2, eye).reshape(LANES, LANES).astype(BF16)
    b2 = b_cmp2.reshape(1, LANES)
    wpe = jnp.einsum("vcde,vw->vcdwe", w_cmp1, eye).reshape(2 * CMP_BLOCK * HEAD_DIM, LANES).astype(BF16)
    pos = jnp.broadcast_to(cmp_pos.reshape(1, -1), (8, 2 * CMP_BLOCK * HEAD_DIM)).astype(BF16)
    return wc, w2, b2, pos, wpe


def _compress(kvt, cw):
    wc, w2, b2, pos, wpe = cw
    nb, g, _, length = kvt.shape
    n_rows = length // CMP_STRIDE
    const2 = lambda b, gg: (0, 0)
    return pl.pallas_call(
        functools.partial(_compress_kernel, length=length),
        grid=(nb, g),
        in_specs=[
            pl.BlockSpec((1, 1, LANES, length), lambda b, gg: (b, gg, 0, 0)),
            pl.BlockSpec(wc.shape, lambda b, gg: (0, 0, 0)),
            pl.BlockSpec(w2.shape, const2),
            pl.BlockSpec(b2.shape, const2),
            pl.BlockSpec(pos.shape, const2),
            pl.BlockSpec(wpe.shape, const2),
        ],
        out_specs=(
            pl.BlockSpec((1, 1, n_rows, LANES), lambda b, gg: (b, gg, 0, 0)),
            pl.BlockSpec((1, 1, LANES, n_rows), lambda b, gg: (b, gg, 0, 0)),
        ),
        out_shape=(
            jax.ShapeDtypeStruct((nb, g, n_rows, LANES), BF16),
            jax.ShapeDtypeStruct((nb, g, LANES, n_rows), BF16),
        ),
        scratch_shapes=[pltpu.VMEM((length, LANES), F32), pltpu.VMEM((n_rows, LANES), F32)],
        compiler_params=_cparams(("parallel", "parallel")),
        name="compress",
    )(kvt, wc, w2, b2, pos, wpe)


BIAS_BLOCKS = 16
KEY_TILE = 512
PREP_TILE = 1024
N_COLS = NSA_GROUP_HEADS * Q_TILE
SLC_UNROLL = 4
V_ROWS = HEAD_DIM + 16
WIN_KEYS = WINDOW + Q_TILE
CMP_ROW_CHOICES = (64, 128, 192)


def _kv_prep_kernel(slc_ref, win_ref, kslc_ref, kwin_ref, vslc_ref, vwin_ref):
    shape = (PREP_TILE, LANES)
    lane = lax.broadcasted_iota(jnp.int32, shape, 1)
    blk = (lax.broadcasted_iota(jnp.int32, shape, 0) // SEL_BLOCK) % BIAS_BLOCKS
    onehot = jnp.where(lane - HEAD_DIM == blk, 1.0, 0.0)
    ks = slc_ref[0, 0].T
    kslc_ref[0, 0] = jnp.where(lane < HEAD_DIM, ks, onehot).astype(BF16)
    kw = win_ref[0, 0].T
    kwin_ref[0, 0] = jnp.where(lane < HEAD_DIM, kw, 0.0).astype(BF16)
    ones = jnp.ones((V_ROWS - HEAD_DIM, PREP_TILE), BF16)
    vslc_ref[0, 0] = jnp.concatenate([slc_ref[0, 0, HEAD_DIM:, :].astype(BF16), ones], axis=0)
    vwin_ref[0, 0] = jnp.concatenate([win_ref[0, 0, HEAD_DIM:, :].astype(BF16), ones], axis=0)


def _kv_prep(kvt_slc, kvt_win):
    nb, g, _, t_len = kvt_slc.shape
    in_spec = pl.BlockSpec((1, 1, LANES, PREP_TILE), lambda b, gg, i: (b, gg, 0, i))
    k_spec = pl.BlockSpec((1, 1, PREP_TILE, LANES), lambda b, gg, i: (b, gg, i, 0))
    v_spec = pl.BlockSpec((1, 1, V_ROWS, PREP_TILE), lambda b, gg, i: (b, gg, 0, i))
    k_shape = jax.ShapeDtypeStruct((nb, g, t_len, LANES), BF16)
    v_shape = jax.ShapeDtypeStruct((nb, g, V_ROWS, t_len), BF16)
    return pl.pallas_call(
        _kv_prep_kernel,
        grid=(nb, g, t_len // PREP_TILE),
        in_specs=[in_spec, in_spec],
        out_specs=(k_spec, k_spec, v_spec, v_spec),
        out_shape=(k_shape, k_shape, v_shape, v_shape),
        compiler_params=_cparams(("parallel", "parallel", "parallel")),
        name="kv_prep",
    )(kvt_slc, kvt_win)


def _online_update(carry, s, vt):
    m, acc = carry
    m_new = jnp.maximum(m, jnp.max(s, axis=0, keepdims=True))
    p = jnp.exp(s - m_new)
    acc = jnp.exp(m - m_new) * acc + jnp.dot(vt, p.astype(BF16), preferred_element_type=F32)
    return m_new, acc


def _softmax_out(acc):
    return acc[:HEAD_DIM] / acc[HEAD_DIM:HEAD_DIM + 1]


def _select_blocks(imp, q0):
    jrow = lax.broadcasted_iota(jnp.int32, imp.shape, 0)
    cur = (q0 + lax.broadcasted_iota(jnp.int32, imp.shape, 1)) // SEL_BLOCK
    work = jnp.where(jrow <= cur, imp, -1.0)
    bias = jnp.full(imp.shape, NEG_INF, F32)
    for forced_blk in (0, cur, cur - 1):
        hit = jrow == forced_blk
        bias = jnp.where(hit, 0.0, bias)
        work = jnp.where(hit, -3.0e38, work)
    jf = jrow.astype(F32)
    for _ in range(N_SEL - 3):
        mx = jnp.max(work, axis=0, keepdims=True)
        first = jnp.min(jnp.where(work == mx, jf, 1.0e9), axis=0, keepdims=True)
        hit = jf == first
        bias = jnp.where(hit, 0.0, bias)
        work = jnp.where(hit, -3.0e38, work)
    return bias


def _nsa_prompt_kernel(qa_ref, ga_ref, kc_ref, kvct_ref, kslc_ref, vslc_ref, kwin_ref, vwin_ref,
                       o_ref, qm_ref, bias_ref, ocmp_ref, mtail_ref, acctail_ref, *, nj):
    qi = pl.program_id(2)

    @pl.when(qi == 0)
    def _():
        qm_ref[...] = jnp.zeros_like(qm_ref)

    q_t = qa_ref[...].T
    for r in range(NSA_GROUP_HEADS):
        qm_ref[0:HEAD_DIM, r * Q_TILE:(r + 1) * Q_TILE] = q_t[r * HEAD_DIM:(r + 1) * HEAD_DIM].astype(BF16)
    ga_t = ga_ref[...].T

    def gate(branch):
        return jnp.concatenate([ga_t[r * 3 + branch:r * 3 + branch + 1] for r in range(NSA_GROUP_HEADS)], axis=1)

    q0 = qi * Q_TILE
    qpos_c = q0 + (lax.broadcasted_iota(jnp.int32, (1, N_COLS), 1) & (Q_TILE - 1))

    def compressed_and_select(n):
        kc = jnp.concatenate([kc_ref[0, 0, rr * nj:rr * nj + n, :] for rr in range(4)], axis=0)
        vct = jnp.concatenate([kvct_ref[0, 0, HEAD_DIM:, rr * nj:rr * nj + n] for rr in range(4)], axis=1)
        s = jnp.dot(kc, qm_ref[...], preferred_element_type=F32)
        row = lax.broadcasted_iota(jnp.int32, (4 * n, N_COLS), 0)
        cmp_end = (row % n) * SEL_BLOCK + (row // n) * CMP_STRIDE + (CMP_BLOCK - 1)
        vis = cmp_end <= qpos_c
        s = jnp.where(vis, s, NEG_INF)
        m = jnp.max(s, axis=0, keepdims=True)
        p = jnp.where(vis, jnp.exp(s - m), 0.0)
        l = jnp.sum(p, axis=0, keepdims=True)
        pn = p / jnp.where(l > 0.0, l, 1.0)
        ocmp_ref[...] = jnp.dot(vct, pn.astype(BF16), preferred_element_type=F32)
        p3 = pn[3 * n:4 * n]
        jrow4 = lax.broadcasted_iota(jnp.int32, (n, N_COLS), 0)
        prev = jnp.where(jrow4 == 0, 0.0, pltpu.roll(p3, 1, 0))
        a = pn[0:n] + pn[n:2 * n] + pn[2 * n:3 * n] + p3 + prev
        imp = a[:, 0:Q_TILE]
        for r in range(1, NSA_GROUP_HEADS):
            imp = imp + a[:, r * Q_TILE:(r + 1) * Q_TILE]
        bias16 = _select_blocks(imp, q0).astype(BF16)
        for r in range(NSA_GROUP_HEADS):
            bias_ref[0:n, r * Q_TILE:(r + 1) * Q_TILE] = bias16

    sizes = [n for n in CMP_ROW_CHOICES if n < nj] + [nj]
    need = 2 * qi + 2
    for v, n in enumerate(sizes):
        lo = sizes[v - 1] if v else 0
        pl.when((need > lo) & ((need <= n) | (v == len(sizes) - 1)))(functools.partial(compressed_and_select, n))
    o_cmp = ocmp_ref[...]


    init =(jnp.full((1, N_COLS), -jnp.inf, F32), jnp.zeros((V_ROWS, N_COLS), F32))
    q_rows = qm_ref[0:HEAD_DIM, :]
    pad_rows = jnp.zeros((LANES - HEAD_DIM - BIAS_BLOCKS, N_COLS), BF16)

    def slc_scores(st):
        b0 = pl.multiple_of((st * KEY_TILE // (SEL_BLOCK * BIAS_BLOCKS)) * BIAS_BLOCKS, BIAS_BLOCKS)
        qm = jnp.concatenate([q_rows, bias_ref[pl.ds(b0, BIAS_BLOCKS), :], pad_rows], axis=0)
        k0 = pl.multiple_of(st * KEY_TILE, KEY_TILE)
        return jnp.dot(kslc_ref[0, 0, pl.ds(k0, KEY_TILE), :], qm, preferred_element_type=F32)

    def slc_absorb(st, carry, sc):
        k0 = pl.multiple_of(st * KEY_TILE, KEY_TILE)
        return _online_update(carry, sc, vslc_ref[0, 0, :, pl.ds(k0, KEY_TILE)])

    def slc_group(i, carry):
        st0 = i * SLC_UNROLL
        sc = slc_scores(st0)
        for u in range(SLC_UNROLL):
            sc_next = slc_scores(st0 + u + 1) if u + 1 < SLC_UNROLL else None
            carry = slc_absorb(st0 + u, carry, sc)
            sc = sc_next
        return carry

    n_tiles = q0 // KEY_TILE + 1
    tail = (n_tiles - 1) % SLC_UNROLL + 1
    m_s, acc_s = lax.fori_loop(0, (n_tiles - tail) // SLC_UNROLL, slc_group, init)
    mtail_ref[...] = m_s
    acctail_ref[...] = acc_s

    def slc_tail(size):
        st0 = n_tiles - size
        carry = (mtail_ref[...], acctail_ref[...])
        sc = slc_scores(st0)
        for u in range(size):
            sc_next = slc_scores(st0 + u + 1) if u + 1 < size else None
            if u == size - 1:
                kpos = (st0 + u) * KEY_TILE + lax.broadcasted_iota(jnp.int32, (KEY_TILE, N_COLS), 0)
                sc = jnp.where(kpos <= qpos_c, sc, NEG_INF)
            carry = slc_absorb(st0 + u, carry, sc)
            sc = sc_next
        acctail_ref[...] = carry[1]

    for size in range(1, SLC_UNROLL + 1):
        pl.when(tail == size)(functools.partial(slc_tail, size))
    o_slc = _softmax_out(acctail_ref[...])

    kw0 = pl.multiple_of(jnp.maximum(q0 - WINDOW, 0), Q_TILE)
    sc = jnp.dot(kwin_ref[0, 0, pl.ds(kw0, WIN_KEYS), :], qm_ref[...], preferred_element_type=F32)
    dpos = qpos_c - (kw0 + lax.broadcasted_iota(jnp.int32, (WIN_KEYS, N_COLS), 0))
    sc = jnp.where(dpos >= 0, jnp.where(dpos < WINDOW, sc, NEG_INF), NEG_INF)
    p = jnp.exp(sc - jnp.max(sc, axis=0, keepdims=True))
    o_win = _softmax_out(jnp.dot(vwin_ref[0, 0, :, pl.ds(kw0, WIN_KEYS)], p.astype(BF16), preferred_element_type=F32))

    o = gate(0) * o_cmp + gate(1) * o_slc + gate(2) * o_win
    o_rows = jnp.concatenate([o[:, r * Q_TILE:(r + 1) * Q_TILE] for r in range(NSA_GROUP_HEADS)], axis=0)
    o_ref[...] = o_rows.T.astype(BF16)


def _nsa_prompt(qa, ga, kc, kvct, kslc, vslc, kwin, vwin):
    nb, g, t_len, _ = kslc.shape
    nq = t_len // Q_TILE
    nj = t_len // SEL_BLOCK
    n_cmp_rows = kc.shape[2]
    assert nj % BIAS_BLOCKS == 0 and n_cmp_rows == 4 * nj and t_len % KEY_TILE == 0 and t_len >= WIN_KEYS
    tile = lambda b, gg, i: (b * nq + i, gg)
    whole = lambda b, gg, i: (b, gg, 0, 0)
    return pl.pallas_call(
        functools.partial(_nsa_prompt_kernel, nj=nj),
        grid=(nb, g, nq),
        in_specs=[
            pl.BlockSpec((Q_TILE, NSA_GROUP_HEADS * HEAD_DIM), tile),
            pl.BlockSpec((Q_TILE, LANES), tile),
            pl.BlockSpec((1, 1, n_cmp_rows, LANES), whole),
            pl.BlockSpec((1, 1, LANES, n_cmp_rows), whole),
            pl.BlockSpec((1, 1, t_len, LANES), whole),
            pl.BlockSpec((1, 1, V_ROWS, t_len), whole),
            pl.BlockSpec((1, 1, t_len, LANES), whole),
            pl.BlockSpec((1, 1, V_ROWS, t_len), whole),
        ],
        out_specs=pl.BlockSpec((Q_TILE, NSA_GROUP_HEADS * HEAD_DIM), tile),
        out_shape=jax.ShapeDtypeStruct((nb * t_len, NSA_HEADS * HEAD_DIM), BF16),
        scratch_shapes=[pltpu.VMEM((LANES, N_COLS), BF16), pltpu.VMEM((nj, N_COLS), BF16),
                        pltpu.VMEM((HEAD_DIM, N_COLS), F32), pltpu.VMEM((1, N_COLS), F32),
                        pltpu.VMEM((V_ROWS, N_COLS), F32)],
        compiler_params=_cparams(("parallel", "parallel", "arbitrary")),
        name="nsa_prompt",
    )(qa, ga, kc, kvct, kslc, vslc, kwin, vwin)


HEAD_ROWS = 8
N_PAST_SEL = N_SEL - 1


def _softmax_lanes(s_list, mask_list):
    s_list = [jnp.where(mk, s, NEG_INF) for s, mk in zip(s_list, mask_list)]
    m = s_list[0].max(axis=-1, keepdims=True)
    for s in s_list[1:]:
        m = jnp.maximum(m, s.max(axis=-1, keepdims=True))
    p_list = [jnp.where(mk, jnp.exp(s - m), 0.0) for s, mk in zip(s_list, mask_list)]
    l = p_list[0].sum(axis=-1, keepdims=True)
    for p in p_list[1:]:
        l = l + p.sum(axis=-1, keepdims=True)
    inv = 1.0 / jnp.where(l > 0.0, l, 1.0)
    return [p * inv for p in p_list]


def _dot_nt(a, b):
    return lax.dot_general(a, b, (((1,), (1,)), ((), ())), preferred_element_type=F32)


def _dec_cmp_kernel(q_ref, kvct_ref, o_ref, imp_ref, *, nj):
    q = q_ref[0, 0]
    n = 4 * nj
    s = jnp.dot(q, kvct_ref[0, 0, :HEAD_DIM, :], preferred_element_type=F32)
    lane = lax.broadcasted_iota(jnp.int32, (HEAD_ROWS, n), 1)
    blk = 4 * (lane % nj) + lane // nj
    (pn,) = _softmax_lanes([s], [blk <= n - 2])
    o_ref[0, 0] = _dot_nt(pn.astype(BF16), kvct_ref[0, 0, HEAD_DIM:, :])
    row = lax.broadcasted_iota(jnp.int32, (HEAD_ROWS, n), 0)
    ph = jnp.sum(jnp.where(row < NSA_GROUP_HEADS, pn, 0.0), axis=0, keepdims=True)
    p3 = ph[:, 3 * nj:]
    prev = jnp.where(lane[0:1, :nj] == 0, 0.0, pltpu.roll(p3, 1, 1))
    imp = ph[:, :nj] + ph[:, nj:2 * nj] + ph[:, 2 * nj:3 * nj] + p3 + prev
    imp_ref[0, 0] = jnp.broadcast_to(imp, (HEAD_ROWS, nj))


def _dec_cmp(q8, kvct):
    nb, g, _, n = kvct.shape
    nj = n // 4
    blk = lambda *shape: pl.BlockSpec((1, 1) + shape, lambda b, gg: (b, gg, 0, 0))
    return pl.pallas_call(
        functools.partial(_dec_cmp_kernel, nj=nj),
        grid=(nb, g),
        in_specs=[blk(HEAD_ROWS, HEAD_DIM), blk(LANES, n)],
        out_specs=(blk(HEAD_ROWS, HEAD_DIM), blk(HEAD_ROWS, nj)),
        out_shape=(jax.ShapeDtypeStruct((nb, g, HEAD_ROWS, HEAD_DIM), F32),
                   jax.ShapeDtypeStruct((nb, g, HEAD_ROWS, nj), F32)),
        compiler_params=_cparams(("parallel", "parallel")),
        name="dec_cmp",
    )(q8, kvct)


def _dec_topk_kernel(imp_ref, idx_ref):
    work = imp_ref[...].T
    nj, n = work.shape
    jrow = lax.broadcasted_iota(jnp.int32, (nj, n), 0)
    jf = jrow.astype(F32)
    for forced_blk in (0, nj - 1):
        work = jnp.where(jrow == forced_blk, FORCE_SCORE, work)
    picks = []
    for _ in range(N_PAST_SEL):
        mx = jnp.max(work, axis=0, keepdims=True)
        first = jnp.min(jnp.where(work == mx, jf, 1.0e9), axis=0, keepdims=True)
        picks.append(first)
        work = jnp.where(jf == first, -3.0e38, work)
    picks.append(jnp.full((1, n), float(nj), F32))
    idx_ref[...] = jnp.concatenate(picks, axis=0)


def _dec_topk(imp2):
    n, nj = imp2.shape
    return pl.pallas_call(
        _dec_topk_kernel,
        out_shape=jax.ShapeDtypeStruct((N_SEL, n), F32),
        compiler_params=pltpu.CompilerParams(vmem_limit_bytes=VMEM_LIMIT),
        name="dec_topk",
    )(imp2)


def _dec_attn_kernel(phys_ref, half_ref, q_ref, gate_ref, ocmp_ref, *refs):
    page_refs = refs[:N_PAST_SEL]
    slcnew_ref, winnew_ref, win_ref, o_ref, winout_ref = refs[N_PAST_SEL:]
    b = pl.program_id(0)
    base = (b * pl.num_programs(1) + pl.program_id(1)) * N_PAST_SEL
    q = q_ref[0, 0]
    lane = lax.broadcasted_iota(jnp.int32, (HEAD_ROWS, LANES), 1)
    own = lane == b

    def attend(kv_blocks, masks):
        scores = [jnp.dot(q, kv[:HEAD_DIM].astype(BF16), preferred_element_type=F32) for kv in kv_blocks]
        probs = _softmax_lanes(scores, masks)
        out = jnp.zeros((HEAD_ROWS, HEAD_DIM), F32)
        for p, kv in zip(probs, kv_blocks):
            out = out + _dot_nt(p.astype(BF16), kv[HEAD_DIM:].astype(BF16))
        return out

    pages = [r[0, 0] for r in page_refs] + [slcnew_ref[0, 0]]
    masks = [(lane // SEL_BLOCK) == half_ref[base + k] for k in range(N_PAST_SEL)] + [own]
    o_slc = attend(pages, masks)
    st = win_ref[0, 0]
    win_c = st.shape[1]
    wlane = lax.broadcasted_iota(jnp.int32, (HEAD_ROWS, win_c), 1)
    o_win = attend([st, winnew_ref[0, 0]], [wlane > win_c - WINDOW, own])
    gates = gate_ref[0, 0]
    o_ref[0, 0] = gates[:, 0:1] * ocmp_ref[0, 0] + gates[:, 1:2] * o_slc + gates[:, 2:3] * o_win
    new = winnew_ref[0, 0]
    lane_full = lax.broadcasted_iota(jnp.int32, new.shape, 1)
    new_col = jnp.sum(jnp.where(lane_full == b, new, 0.0), axis=1, keepdims=True)
    col = lax.broadcasted_iota(jnp.int32, st.shape, 1)
    winout_ref[0, 0] = jnp.where(col == win_c - 1, new_col, pltpu.roll(st, win_c - 1, 1))


def _dec_attn(phys, half, q8, gates, o_cmp, slc_t, slc_new, win_new, win_t):
    nb, g, _, win_c = win_t.shape
    assert slc_new.shape[-1] == nb == LANES
    blk = lambda *shape: pl.BlockSpec((1, 1) + shape, lambda b, gg, ph, hf: (b, gg, 0, 0))
    page = lambda k: pl.BlockSpec((1, 1, LANES, PAGE_SIZE),
                                  lambda b, gg, ph, hf: (ph[(b * g + gg) * N_PAST_SEL + k], gg, 0, 0))
    new = pl.BlockSpec((1, 1, LANES, nb), lambda b, gg, ph, hf: (0, gg, 0, 0))
    return pl.pallas_call(
        _dec_attn_kernel,
        grid_spec=pltpu.PrefetchScalarGridSpec(
            num_scalar_prefetch=2,
            grid=(nb, g),
            in_specs=[blk(HEAD_ROWS, HEAD_DIM), blk(HEAD_ROWS, LANES), blk(HEAD_ROWS, HEAD_DIM)]
            + [page(k) for k in range(N_PAST_SEL)] + [new, new, blk(LANES, win_c)],
            out_specs=(blk(HEAD_ROWS, HEAD_DIM), blk(LANES, win_c)),
        ),
        out_shape=(jax.ShapeDtypeStruct((nb, g, HEAD_ROWS, HEAD_DIM), F32),
                   jax.ShapeDtypeStruct((nb, g, LANES, win_c), F32)),
        compiler_params=_cparams(("parallel", "parallel")),
        name="dec_attn",
    )(phys, half, q8, gates, o_cmp, *([slc_t] * N_PAST_SEL), slc_new, win_new, win_t)


RET_STEP_CHUNKS = 8


def _log_gamma():
    return jnp.log1p(-jnp.exp2(-5.0 - jnp.arange(RET_HEADS, dtype=F32)))


def _ret_tables(chunk):
    lg = _log_gamma()
    n = jnp.arange(chunk, dtype=F32)
    diff = n[:, None] - n[None, :]
    decay = jnp.exp(jnp.maximum(diff, 0.0)[None] * lg[:, None, None]) * (diff >= 0)[None]
    rep = lambda col: jnp.broadcast_to(col[:, :, None], (RET_HEADS, chunk, RET_DV))
    qdec = rep(jnp.exp((n + 1.0)[None, :] * lg[:, None]))
    kdec = rep(jnp.exp((chunk - 1.0 - n)[None, :] * lg[:, None]))
    sdec = jnp.broadcast_to(jnp.exp(chunk * lg)[:, None, None], (RET_HEADS, RET_DK, RET_DV))
    return decay.astype(F32), qdec, kdec, sdec


def _group_norm_gate(o, g, gn_g, gn_b):
    mu = jnp.mean(o, axis=-1, keepdims=True)
    var = jnp.mean(jnp.square(o - mu), axis=-1, keepdims=True)
    on = (o - mu) * lax.rsqrt(var + LN_EPS) * gn_g + gn_b
    return (jax.nn.silu(g) * on).astype(BF16)


def _ret_prompt_kernel(qb_ref, kb_ref, vb_ref, gb_ref, decay_ref, qdec_ref, kdec_ref, sdec_ref, gng_ref, gnb_ref,
                       bm_ref, state_ref, st_ref):
    @pl.when(pl.program_id(1) == 0)
    def _():
        st_ref[...] = jnp.zeros_like(st_ref)

    def chunk_step(c, _):
        rows = pl.ds(pl.multiple_of(c * RET_CHUNK, RET_CHUNK), RET_CHUNK)
        for h in range(RET_HEADS):
            cols = slice(h * RET_DK, (h + 1) * RET_DK)
            q = qb_ref[rows, cols]
            k = kb_ref[rows, cols]
            v = vb_ref[rows, cols]
            st = st_ref[h]
            inner = lax.dot_general(q, k.astype(BF16), (((1,), (1,)), ((), ())), preferred_element_type=F32)
            inner = inner * decay_ref[h]
            o = jnp.dot(inner.astype(BF16), v, preferred_element_type=F32)
            o = o + jnp.dot(q, st.astype(BF16), preferred_element_type=F32) * qdec_ref[h]
            kd = (k * kdec_ref[h]).astype(BF16)
            st_ref[h] = sdec_ref[h] * st + lax.dot_general(kd, v, (((0,), (0,)), ((), ())), preferred_element_type=F32)
            bm_ref[rows, cols] = _group_norm_gate(o, gb_ref[rows, cols], gng_ref[:, cols], gnb_ref[:, cols])
        return 0

    lax.fori_loop(0, RET_STEP_CHUNKS, chunk_step, 0)
    state_ref[0] = st_ref[...]


def _ret_prompt(qb, kb, vb, gb, gn_g, gn_b, nb):
    n = qb.shape[0]
    tt = RET_STEP_CHUNKS * RET_CHUNK
    nt = n // nb // tt
    decay, qdec, kdec, sdec = _ret_tables(RET_CHUNK)
    row = pl.BlockSpec((tt, RET_HEADS * RET_DK), lambda b, i: (b * nt + i, 0))
    tab = pl.BlockSpec((RET_HEADS, RET_CHUNK, RET_DV), lambda b, i: (0, 0, 0))
    vec = pl.BlockSpec((1, RET_HEADS * RET_DV), lambda b, i: (0, 0))
    return pl.pallas_call(
        _ret_prompt_kernel,
        grid=(nb, nt),
        in_specs=[row, row, row, row, tab, tab, tab, tab, vec, vec],
        out_specs=(row, pl.BlockSpec((1, RET_HEADS, RET_DK, RET_DV), lambda b, i: (b, 0, 0, 0))),
        out_shape=(jax.ShapeDtypeStruct((n, RET_HEADS * RET_DV), BF16),
                   jax.ShapeDtypeStruct((nb, RET_HEADS, RET_DK, RET_DV), F32)),
        scratch_shapes=[pltpu.VMEM((RET_HEADS, RET_DK, RET_DV), F32)],
        compiler_params=_cparams(("parallel", "arbitrary")),
        name="ret_prompt",
    )(qb, kb, vb, gb, decay, qdec, kdec, sdec, gn_g.reshape(1, -1), gn_b.reshape(1, -1))


def _ret_step_kernel(q_ref, k_ref, v_ref, g_ref, st_ref, qdec_ref, sdec_ref, gng_ref, gnb_ref, bm_ref, stout_ref):
    for h in range(RET_HEADS):
        cols = slice(h * RET_DK, (h + 1) * RET_DK)
        q = q_ref[0, :, cols]
        k = k_ref[0, :, cols].astype(BF16)
        v = v_ref[0, :, cols]
        st = st_ref[0, h]
        qk = jnp.sum(q.astype(F32) * k.astype(F32), axis=-1, keepdims=True)
        o = qk.astype(BF16).astype(F32) * v.astype(F32)
        o = o + jnp.dot(q, st.astype(BF16), preferred_element_type=F32) * qdec_ref[h]
        stout_ref[0, h] = sdec_ref[h] * st + lax.dot_general(k, v, (((0,), (0,)), ((), ())), preferred_element_type=F32)
        bm_ref[0, :, cols] = _group_norm_gate(o, g_ref[0, :, cols], gng_ref[:, cols], gnb_ref[:, cols])


def _ret_step(q8, k8, v8, g8, state, gn_g, gn_b):
    nb = state.shape[0]
    _, qdec, _, sdec = _ret_tables(1)
    tok = pl.BlockSpec((1, HEAD_ROWS, RET_HEADS * RET_DK), lambda b: (b, 0, 0))
    st = pl.BlockSpec((1, RET_HEADS, RET_DK, RET_DV), lambda b: (b, 0, 0, 0))
    vec = pl.BlockSpec((1, RET_HEADS * RET_DV), lambda b: (0, 0))
    return pl.pallas_call(
        _ret_step_kernel,
        grid=(nb,),
        in_specs=[tok, tok, tok, tok, st,
                  pl.BlockSpec((RET_HEADS, 1, RET_DV), lambda b: (0, 0, 0)),
                  pl.BlockSpec((RET_HEADS, RET_DK, RET_DV), lambda b: (0, 0, 0)), vec, vec],
        out_specs=(tok, st),
        out_shape=(jax.ShapeDtypeStruct((nb, HEAD_ROWS, RET_HEADS * RET_DV), BF16),
                   jax.ShapeDtypeStruct(state.shape, F32)),
        compiler_params=_cparams(("parallel",)),
        name="ret_step",
    )(q8, k8, v8, g8, state, qdec, sdec, gn_g.reshape(1, -1), gn_b.reshape(1, -1))


ROUTE_E0 = N_GROUPS
MOE_ROWS = 256
SAMPLE_MOE_ROWS = 32


def _layer_norm(v, g, b):
    mu = jnp.mean(v, axis=-1, keepdims=True)
    var = jnp.mean(jnp.square(v - mu), axis=-1, keepdims=True)
    return (v - mu) * lax.rsqrt(var + LN_EPS) * g + b


def _first_lane(hit, lane):
    return jnp.min(jnp.where(hit, lane, 4 * LANES), axis=-1, keepdims=True)


def _merge_kernel(on_ref, bm_ref, gm_ref, x_ref, wn_ref, wr_ref, wo_ref, g1_ref, b1_ref, wrt_ref, brt_ref, tri_ref,
                  x1_ref, route_ref, counts_ref, carry_ref):
    @pl.when(pl.program_id(0) == 0)
    def _():
        carry_ref[...] = jnp.zeros_like(carry_ref)

    gate_a = jax.nn.sigmoid(gm_ref[:, :D_MODEL])
    gate_b = jax.nn.sigmoid(gm_ref[:, D_MODEL:])
    m = gate_a * jnp.dot(on_ref[...], wn_ref[...], preferred_element_type=F32)
    m = m + gate_b * jnp.dot(bm_ref[...], wr_ref[...], preferred_element_type=F32)
    mix = jnp.dot(m.astype(BF16), wo_ref[...], preferred_element_type=F32)
    x1 = _layer_norm(DEEPNORM_ALPHA * x_ref[...] + mix, g1_ref[...], b1_ref[...])
    x1_ref[...] = x1

    lg = jnp.dot(x1.astype(BF16), wrt_ref[...], preferred_element_type=F32) + brt_ref[...]
    lane = lax.broadcasted_iota(jnp.int32, lg.shape, 1)
    gl = jnp.where(lane < N_GROUPS, lg, -jnp.inf)
    ge = jnp.exp(gl - jnp.max(gl, axis=-1, keepdims=True))
    pgrp = ge / jnp.sum(ge, axis=-1, keepdims=True)
    pg = jnp.max(pgrp, axis=-1, keepdims=True)
    grp = _first_lane(pgrp == pg, lane)
    lo = ROUTE_E0 + grp * EXPERTS_PER_GROUP
    in_g = (lane >= lo) & (lane < lo + EXPERTS_PER_GROUP)
    el = jnp.where(in_g, lg, -jnp.inf)
    ee = jnp.exp(el - jnp.max(el, axis=-1, keepdims=True))
    pc = jnp.where(in_g, ee / jnp.sum(ee, axis=-1, keepdims=True), -1.0)
    w1 = jnp.max(pc, axis=-1, keepdims=True)
    i1 = _first_lane(pc == w1, lane)
    pc2 = jnp.where(lane == i1, -1.0, pc)
    w2 = jnp.max(pc2, axis=-1, keepdims=True)
    i2 = _first_lane(pc2 == w2, lane)
    wsum = w1 + w2
    wt1 = pg * w1 / wsum
    wt2 = pg * w2 / wsum
    hit1 = lane == i1
    hit2 = lane == i2
    cnt = jnp.where(hit1, 1.0, 0.0) + jnp.where(hit2, 1.0, 0.0)
    base = carry_ref[0:1, :] + jnp.dot(tri_ref[...], cnt.astype(BF16), preferred_element_type=F32)
    rank1 = jnp.sum(jnp.where(hit1, base, 0.0), axis=-1, keepdims=True)
    rank2 = jnp.sum(jnp.where(hit2, base, 0.0), axis=-1, keepdims=True)
    carry = carry_ref[0:1, :] + jnp.sum(cnt, axis=0, keepdims=True)
    carry_ref[...] = jnp.broadcast_to(carry, carry_ref.shape)
    counts_ref[...] = jnp.broadcast_to(carry, counts_ref.shape)
    fields = ((i1 - ROUTE_E0).astype(F32), (i2 - ROUTE_E0).astype(F32), wt1, wt2, rank1, rank2)
    out = jnp.zeros(lg.shape, F32)
    for idx, val in enumerate(fields):
        out = jnp.where(lane == idx, val, out)
    route_ref[...] = out


def _merge_weights(w_nsa_proj, w_ret_proj, w_o, w_route_group, b_route_group, w_route_exp, b_route_exp):
    pad = LANES - N_GROUPS - N_EXPERTS
    wrt = jnp.concatenate([w_route_group, w_route_exp, jnp.zeros((D_MODEL, pad), F32)], axis=1).astype(BF16)
    brt = jnp.concatenate([b_route_group, b_route_exp, jnp.zeros((pad,), F32)])[None, :]
    return w_nsa_proj.astype(BF16), w_ret_proj.astype(BF16), w_o.astype(BF16), wrt, brt


def _merge(o_nsa, bm, gm, x2, mw, ln_g, ln_b, tm):
    wn, wr, wo, wrt, brt = mw
    n = x2.shape[0]
    tri = (jnp.arange(tm)[:, None] > jnp.arange(tm)[None, :]).astype(BF16)
    row = lambda width: pl.BlockSpec((tm, width), lambda i: (i, 0))
    const = lambda a: pl.BlockSpec(a.shape, lambda i: (0, 0))
    ln_g = ln_g.reshape(1, -1)
    ln_b = ln_b.reshape(1, -1)
    return pl.pallas_call(
        _merge_kernel,
        grid=(n // tm,),
        in_specs=[row(512), row(512), row(2 * D_MODEL), row(D_MODEL), const(wn), const(wr), const(wo),
                  const(ln_g), const(ln_b), const(wrt), const(brt), const(tri)],
        out_specs=(row(D_MODEL), row(LANES), pl.BlockSpec((8, LANES), lambda i: (0, 0))),
        out_shape=(jax.ShapeDtypeStruct((n, D_MODEL), F32), jax.ShapeDtypeStruct((n, LANES), F32),
                   jax.ShapeDtypeStruct((8, LANES), F32)),
        scratch_shapes=[pltpu.VMEM((8, LANES), F32)],
        compiler_params=_cparams(("arbitrary",)),
        name="merge_route",
    )(o_nsa, bm, gm, x2, wn, wr, wo, ln_g, ln_b, wrt, brt, tri)


SC_CHUNK_ROWS = 32


def _gather_rows(table, idx):
    info = plsc.get_sparse_core_info()
    n_cores = info.num_cores
    n_workers = n_cores * info.num_subcores
    n_idx, width = idx.shape[0], table.shape[1]
    per_worker = n_idx // n_workers
    chunk = max(c for c in range(8, SC_CHUNK_ROWS + 1, 8) if per_worker % c == 0)
    assert n_idx % (8 * n_workers) == 0
    mesh = plsc.VectorSubcoreMesh(core_axis_name="c", subcore_axis_name="s")

    def body(table_hbm, idx_hbm, out_hbm, idx_v, rows_v, sem):
        worker = lax.axis_index("s") * n_cores + lax.axis_index("c")
        base = worker * per_worker

        @pl.loop(0, per_worker // chunk)
        def _(c):
            off = pl.multiple_of(base + c * chunk, 8)
            pltpu.sync_copy(idx_hbm.at[pl.ds(off, chunk)], idx_v)
            pltpu.async_copy(table_hbm.at[idx_v], rows_v, sem).wait()
            pltpu.sync_copy(rows_v, out_hbm.at[pl.ds(off, chunk)])

    return pl.kernel(
        body, mesh=mesh,
        out_type=jax.ShapeDtypeStruct((n_idx, width), table.dtype),
        scratch_types=[pltpu.VMEM((chunk,), jnp.int32), pltpu.VMEM((chunk, width), table.dtype),
                       pltpu.SemaphoreType.DMA],
    )(table, idx)


def _expert_kernel(be_ref, nused_ref, x_ref, w1_ref, w3_ref, w2_ref, y_ref):
    @pl.when(pl.program_id(0) >= nused_ref[0])
    def _():
        y_ref[...] = jnp.zeros_like(y_ref)

    @pl.when(pl.program_id(0) < nused_ref[0])
    def _():
        x = x_ref[...].astype(BF16)
        h = jax.nn.silu(jnp.dot(x, w1_ref[0], preferred_element_type=F32))
        h = h * jnp.dot(x, w3_ref[0], preferred_element_type=F32)
        y_ref[...] = jnp.dot(h.astype(BF16), w2_ref[0], preferred_element_type=F32)


def _experts(x_sorted, block_e, n_used, w1, w3, w2, rows):
    n_blocks = block_e.shape[0]
    xspec = pl.BlockSpec((rows, D_MODEL), lambda i, be, nu: (i, 0))
    wspec = lambda a: pl.BlockSpec((1,) + a.shape[1:], lambda i, be, nu: (be[i], 0, 0))
    return pl.pallas_call(
        _expert_kernel,
        grid_spec=pltpu.PrefetchScalarGridSpec(
            num_scalar_prefetch=2,
            grid=(n_blocks,),
            in_specs=[xspec, wspec(w1), wspec(w3), wspec(w2)],
            out_specs=xspec,
        ),
        out_shape=jax.ShapeDtypeStruct(x_sorted.shape, F32),
        compiler_params=_cparams(("arbitrary",)),
        name="experts",
    )(block_e, n_used, x_sorted, w1, w3, w2)


def _combine_kernel(x1_ref, y0_ref, y1_ref, route_ref, g_ref, b_ref, o_ref):
    route = route_ref[...]
    ff = y0_ref[...] * route[:, 2:3] + y1_ref[...] * route[:, 3:4]
    o_ref[...] = _layer_norm(DEEPNORM_ALPHA * x1_ref[...] + ff, g_ref[...], b_ref[...])


def _combine(x1, y2, route, ln_g, ln_b, tm):
    n = x1.shape[0]
    row = lambda width: pl.BlockSpec((tm, width), lambda i: (i, 0))
    second = pl.BlockSpec((tm, D_MODEL), lambda i: (i + n // tm, 0))
    vec = pl.BlockSpec((1, D_MODEL), lambda i: (0, 0))
    return pl.pallas_call(
        _combine_kernel,
        grid=(n // tm,),
        in_specs=[row(D_MODEL), row(D_MODEL), second, row(LANES), vec, vec],
        out_specs=row(D_MODEL),
        out_shape=jax.ShapeDtypeStruct((n, D_MODEL), F32),
        compiler_params=_cparams(("parallel",)),
        name="combine_ln2",
    )(x1, y2, y2, route, ln_g.reshape(1, -1), ln_b.reshape(1, -1))


def _finish(o_nsa, bm, gm, x2, mw, ew, ln1, ln2, tm, rows):
    n = x2.shape[0]
    x1, route, counts = _merge(o_nsa, bm, gm, x2, mw, ln1[0], ln1[1], tm)
    sizes = counts[0, ROUTE_E0:ROUTE_E0 + N_EXPERTS].astype(jnp.int32)
    padded = (sizes + rows - 1) // rows * rows
    pad_end = jnp.cumsum(padded)
    pad_start = pad_end - padded
    e = route[:, 0:TOP_K].astype(jnp.int32)
    rank = route[:, 4:4 + TOP_K].astype(jnp.int32)
    dest = (pad_start[e] + rank).T.reshape(-1)
    n_blocks = -(-(n * TOP_K) // rows) + N_EXPERTS
    tok = jnp.tile(jnp.arange(n, dtype=jnp.int32), TOP_K)
    slot_tok = jnp.zeros((n_blocks * rows,), jnp.int32).at[dest].set(tok)
    blk0 = jnp.arange(n_blocks, dtype=jnp.int32) * rows
    block_e = jnp.sum(blk0[:, None] >= pad_end[None, :], axis=1).astype(jnp.int32)
    block_e = jnp.minimum(block_e, N_EXPERTS - 1)
    n_used = (pad_end[-1:] // rows).astype(jnp.int32)
    y_sorted = _experts(_gather_rows(x1, slot_tok), block_e, n_used, *ew, rows)
    return _combine(x1, _gather_rows(y_sorted, dest), route, ln2[0], ln2[1], tm)


PROMPT_ROWS = 256


def _feature_major(a):
    lead = a.shape[:-3]
    return jnp.moveaxis(a, -3, -1).reshape(lead + (2 * HEAD_DIM, a.shape[-3]))


def _row_major(a):
    lead = a.shape[:-2]
    return jnp.moveaxis(a.reshape(lead + (2, HEAD_DIM, a.shape[-1])), -1, -3)


def _pad_rows(a):
    return jnp.pad(a[:, None, :], ((0, 0), (0, HEAD_ROWS - 1), (0, 0)))


def kernel(x_prompt, x_sample, cache_kv_cmp, cache_kv_slc, state_kv_win, state_ret, page_table, w_in, b_in, cmp_pos, w_cmp1, w_cmp2, b_cmp2, ret_gn_g, ret_gn_b, w_nsa_proj, w_ret_proj, w_o, ln1_g, ln1_b, w_route_group, b_route_group, w_route_exp, b_route_exp, w_e1, w_e3, w_e2, ln2_g, ln2_b):
    nb, t_len, _ = x_prompt.shape
    ns, s_len, _ = x_sample.shape
    assert s_len == 1 and ns == LANES and t_len % (RET_STEP_CHUNKS * RET_CHUNK) == 0 and t_len >= WINDOW
    g = NSA_KV_GROUPS
    pw = _proj_weights(w_in, b_in)
    cw = _compress_weights(cmp_pos, w_cmp1, w_cmp2, b_cmp2)
    mw = _merge_weights(w_nsa_proj, w_ret_proj, w_o, w_route_group, b_route_group, w_route_exp, b_route_exp)
    ew = (w_e1.astype(BF16), w_e3.astype(BF16), w_e2.astype(BF16))
    ln1 = (ln1_g, ln1_b)
    ln2 = (ln2_g, ln2_b)

    cos_p, sin_p = _rope_tables(jnp.arange(t_len, dtype=jnp.int32))
    x2 = x_prompt.reshape(nb * t_len, D_MODEL)
    qa, kvt_cmp, kvt_slc, kvt_win, ga, qb, kb, vb, gb, gm = _project(x2, nb, pw, cos_p, sin_p, PROMPT_ROWS)
    kc, kvct = _compress(kvt_cmp, cw)
    kslc, kwin, vslc, vwin = _kv_prep(kvt_slc, kvt_win)
    o_nsa = _nsa_prompt(qa, ga, kc, kvct, kslc, vslc, kwin, vwin)
    bm, ret_p = _ret_prompt(qb, kb, vb, gb, ret_gn_g, ret_gn_b, nb)
    y_prompt = _finish(o_nsa, bm, gm, x2, mw, ew, ln1, ln2, PROMPT_ROWS, MOE_ROWS).reshape(x_prompt.shape)
    kv_cmp_p = _row_major(kvt_cmp)
    kv_slc_p = _row_major(kvt_slc)
    win_p = _row_major(kvt_win[..., t_len - WINDOW:])

    n_pages = page_table.shape[1]
    past_len = n_pages * PAGE_SIZE
    cos_s, sin_s = _rope_tables(jnp.full((ns,), past_len, jnp.int32))
    xs2 = x_sample.reshape(ns, D_MODEL)
    qa, new_cmp, new_slc, new_win, ga, qb, kb, vb, gb, gm = _project(xs2, 1, pw, cos_s, sin_s, ns)
    _, kvct_s = _compress_paged(_feature_major(cache_kv_cmp), page_table, cw)
    q8 = jnp.pad(qa.reshape(ns, g, NSA_GROUP_HEADS, HEAD_DIM),
                 ((0, 0), (0, 0), (0, HEAD_ROWS - NSA_GROUP_HEADS), (0, 0))).astype(BF16)
    o_cmp, imp = _dec_cmp(q8, kvct_s)
    picks = _dec_topk(imp[:, :, 0, :].reshape(ns * g, -1))
    idx = picks[:N_PAST_SEL].T.astype(jnp.int32)
    blocks_per_page = PAGE_SIZE // SEL_BLOCK
    seq = jnp.arange(ns * g, dtype=jnp.int32)[:, None] // g
    phys = page_table[seq, idx // blocks_per_page]
    half = idx % blocks_per_page
    gates = ga.reshape(ns, g, LANES)[:, :, :3 * NSA_GROUP_HEADS].reshape(ns, g, NSA_GROUP_HEADS, 3)
    gates = jnp.pad(gates, ((0, 0), (0, 0), (0, HEAD_ROWS - NSA_GROUP_HEADS), (0, LANES - 3)))
    o_s, win_t = _dec_attn(phys.reshape(-1), half.reshape(-1), q8, gates, o_cmp, _feature_major(cache_kv_slc),
                           new_slc, new_win, _feature_major(state_kv_win))
    o_nsa_s = o_s[:, :, :NSA_GROUP_HEADS].reshape(ns, NSA_HEADS * HEAD_DIM).astype(BF16)
    bm8, ret_s = _ret_step(_pad_rows(qb), _pad_rows(kb), _pad_rows(vb), _pad_rows(gb), state_ret, ret_gn_g, ret_gn_b)
    y_sample = _finish(o_nsa_s, bm8[:, 0], gm, xs2, mw, ew, ln1, ln2, ns, SAMPLE_MOE_ROWS).reshape(x_sample.shape)
    kv_cmp_s = jnp.moveaxis(new_cmp[0], -1, 0).reshape(ns, g, 1, 2, HEAD_DIM)
    kv_slc_s = jnp.moveaxis(new_slc[0], -1, 0).reshape(ns, g, 1, 2, HEAD_DIM)
    win_s = _row_major(win_t)
    return (y_prompt, y_sample, kv_cmp_p, kv_slc_p, win_p, ret_p, kv_cmp_s, kv_slc_s, win_s, ret_s)
```

```python
import functools

import numpy as np
import jax
import jax.numpy as jnp
from jax import lax
from jax.experimental import pallas as pl
from jax.experimental.pallas import tpu as pltpu
from jax.experimental.pallas import tpu_sc as plsc

F32 = jnp.float32
BF16 = jnp.bfloat16

D_MODEL = 1024
DEPTH = 1
PAGE_SIZE = 128
NSA_HEADS = 8
NSA_KV_GROUPS = 2
NSA_GROUP_HEADS = NSA_HEADS // NSA_KV_GROUPS
HEAD_DIM = 64
CMP_BLOCK = 32
CMP_STRIDE = 16
CMP_HIDDEN = 64
SEL_BLOCK = 64
N_SEL = 16
WINDOW = 512
Q_TILE = 256
FORCE_SCORE = 1.0e3
NEG_INF = -1.0e30
RET_HEADS = 4
RET_DK = 128
RET_DV = 128
RET_CHUNK = 128
ROPE_BASE = 10000.0
N_GROUPS = 4
EXPERTS_PER_GROUP = 8
N_EXPERTS = N_GROUPS * EXPERTS_PER_GROUP
TOP_K = 2
EXPERT_FF = 512
LN_EPS = 1e-5
DEEPNORM_ALPHA = (2.0 * DEPTH) ** 0.25

NSA_Q_COLS = NSA_HEADS * HEAD_DIM
NSA_KV_COLS = 3 * 2 * NSA_KV_GROUPS * HEAD_DIM
NSA_GATE_COLS = 3 * NSA_HEADS
RET_QK_COLS = RET_HEADS * RET_DK
RET_V_COLS = RET_HEADS * RET_DV

LANES = 128

P_QA = 0
P_GA = P_QA + NSA_Q_COLS
P_QB = P_GA + NSA_KV_GROUPS * LANES
P_KB = P_QB + RET_QK_COLS
P_VB = P_KB + RET_QK_COLS
P_GB = P_VB + RET_V_COLS
P_GM = P_GB + RET_V_COLS
VMEM_LIMIT = 56 * 1024 * 1024


def _cparams(sem):
    return pltpu.CompilerParams(dimension_semantics=sem, vmem_limit_bytes=VMEM_LIMIT)


def _proj_kernel(x_ref, w_ref, b_ref, wkv_ref, bkv_ref, cos_ref, sin_ref,
                 qa_ref, kcmp_ref, kslc_ref, kwin_ref, ga_ref, qb_ref, kb_ref, vb_ref, gb_ref, gm_ref):
    x = x_ref[...].astype(BF16)

    def cols(c0, n):
        h = jnp.dot(x, w_ref[:, c0:c0 + n], preferred_element_type=F32)
        return h + b_ref[:, c0:c0 + n]

    qa_ref[...] = cols(P_QA, 512) * (HEAD_DIM ** -0.5)
    ga_ref[...] = jax.nn.sigmoid(cols(P_GA, NSA_KV_GROUPS * LANES))
    cos = cos_ref[...]
    sin = sin_ref[...]
    for h in range(RET_HEADS):
        q = cols(P_QB + h * RET_DK, RET_DK)
        qb_ref[:, h * RET_DK:(h + 1) * RET_DK] = (q * cos + pltpu.roll(q, RET_DK // 2, 1) * sin).astype(BF16)
        k = cols(P_KB + h * RET_DK, RET_DK)
        kb_ref[:, h * RET_DK:(h + 1) * RET_DK] = (k * cos + pltpu.roll(k, RET_DK // 2, 1) * sin) * (RET_DK ** -0.5)
    vb_ref[...] = cols(P_VB, 512).astype(BF16)
    gb_ref[...] = cols(P_GB, 512)
    gm_ref[...] = cols(P_GM, 2048)
    kvt = lax.dot_general(wkv_ref[...], x, (((1,), (1,)), ((), ())), preferred_element_type=F32)
    kvt = kvt + bkv_ref[...]
    for br, out_ref in enumerate((kcmp_ref, kslc_ref, kwin_ref)):
        for g in range(NSA_KV_GROUPS):
            i = br * NSA_KV_GROUPS + g
            out_ref[0, g] = kvt[i * LANES:(i + 1) * LANES]


def _proj_weights(w_in, b_in):
    o_kv = NSA_Q_COLS
    o_ga = o_kv + NSA_KV_COLS
    o_qb = o_ga + NSA_GATE_COLS
    per_g = NSA_GATE_COLS // NSA_KV_GROUPS
    ga_cols = []
    for g in range(NSA_KV_GROUPS):
        ga_cols += [np.arange(o_ga + g * per_g, o_ga + (g + 1) * per_g), np.full(LANES - per_g, -1)]
    perm = np.concatenate([
        np.arange(0, NSA_Q_COLS),
        *ga_cols,
        np.arange(o_qb, o_qb + 2 * RET_QK_COLS + 2 * RET_V_COLS + 2 * D_MODEL),
    ])
    keep = jnp.asarray(perm >= 0)
    idx = jnp.asarray(np.maximum(perm, 0))
    w = jnp.where(keep[None, :], w_in[:, idx], 0.0).astype(BF16)
    b = jnp.where(keep, b_in[idx], 0.0)[None, :]
    br, g, kv, d = np.meshgrid(np.arange(3), np.arange(NSA_KV_GROUPS), np.arange(2), np.arange(HEAD_DIM), indexing="ij")
    kv_idx = jnp.asarray((o_kv + br * 256 + kv * 128 + g * 64 + d).reshape(-1))
    wkv = w_in[:, kv_idx].T.astype(BF16)
    bkv = b_in[kv_idx][:, None]
    return w, b, wkv, bkv


def _rope_tables(pos):
    half = RET_DK // 2
    inv = ROPE_BASE ** (-jnp.arange(half, dtype=F32) / half)
    ang = pos.astype(F32)[:, None] * inv[None, :]
    cos, sin = jnp.cos(ang), jnp.sin(ang)
    return jnp.concatenate([cos, cos], -1), jnp.concatenate([-sin, sin], -1)


def _project(x2, n_batch, pw, cos_t, sin_t, tm):
    w, b, wkv, bkv = pw
    n = x2.shape[0]
    t_len = n // n_batch
    nt = t_len // tm
    n_pos_blocks = cos_t.shape[0] // tm
    row = lambda i: (i, 0)
    const = lambda i: (0, 0)
    pos_map = lambda i: (i % n_pos_blocks, 0)
    kvt_shape = jax.ShapeDtypeStruct((n_batch, NSA_KV_GROUPS, LANES, t_len), F32)
    kvt_spec = pl.BlockSpec((1, NSA_KV_GROUPS, LANES, tm), lambda i: (i // nt, 0, 0, i % nt))
    out_shape = (
        jax.ShapeDtypeStruct((n, 512), F32),
        kvt_shape, kvt_shape, kvt_shape,
        jax.ShapeDtypeStruct((n, NSA_KV_GROUPS * LANES), F32),
        jax.ShapeDtypeStruct((n, 512), BF16),
        jax.ShapeDtypeStruct((n, 512), F32),
        jax.ShapeDtypeStruct((n, 512), BF16),
        jax.ShapeDtypeStruct((n, 512), F32),
        jax.ShapeDtypeStruct((n, 2048), F32),
    )
    return pl.pallas_call(
        _proj_kernel,
        grid=(n // tm,),
        in_specs=[
            pl.BlockSpec((tm, D_MODEL), row),
            pl.BlockSpec(w.shape, const),
            pl.BlockSpec(b.shape, const),
            pl.BlockSpec(wkv.shape, const),
            pl.BlockSpec(bkv.shape, const),
            pl.BlockSpec((tm, LANES), pos_map),
            pl.BlockSpec((tm, LANES), pos_map),
        ],
        out_specs=(
            pl.BlockSpec((tm, 512), row),
            kvt_spec, kvt_spec, kvt_spec,
            pl.BlockSpec((tm, NSA_KV_GROUPS * LANES), row),
            pl.BlockSpec((tm, 512), row),
            pl.BlockSpec((tm, 512), row),
            pl.BlockSpec((tm, 512), row),
            pl.BlockSpec((tm, 512), row),
            pl.BlockSpec((tm, 2048), row),
        ),
        out_shape=out_shape,
        compiler_params=_cparams(("parallel",)),
        name="proj",
    )(x2, w, b, wkv, bkv, cos_t, sin_t)


XPOSE_CHUNK = 512


def _compress_rows(xs_ref, wc_ref, w2_ref, b2_ref, pos_ref, wpe_ref, kc_ref, kvct_ref, perm_ref, nj):
    n_blk = 4 * nj
    pe = jnp.dot(pos_ref[...], wpe_ref[...], preferred_element_type=F32)[0:1]
    acc = jnp.zeros((n_blk, 2 * LANES), F32)
    for c2 in range(CMP_STRIDE // 2):
        xa = xs_ref[pl.ds(2 * c2, n_blk, stride=CMP_STRIDE), :]
        xb = xs_ref[pl.ds(2 * c2 + 1, n_blk, stride=CMP_STRIDE), :]
        x2 = jnp.concatenate([xa, xb], axis=1).astype(BF16)
        acc = acc + jnp.dot(x2, wc_ref[c2], preferred_element_type=F32)
    row = lax.broadcasted_iota(jnp.int32, (n_blk, LANES), 0)
    nxt = pltpu.roll(acc[:, LANES:], n_blk - 1, 0)
    second = jnp.where(row == n_blk - 1, 0.0, nxt)
    h = jax.nn.gelu(acc[:, :LANES] + second + pe)
    perm_ref[...] = jnp.dot(h.astype(BF16), w2_ref[...], preferred_element_type=F32) + b2_ref[...]
    lane = lax.broadcasted_iota(jnp.int32, (nj, LANES), 1)
    for rr in range(4):
        kv = perm_ref[pl.ds(rr, nj, stride=4), :]
        kc_ref[rr * nj:(rr + 1) * nj, :] = jnp.where(lane < HEAD_DIM, kv, 0.0).astype(BF16)
        kvct_ref[:, rr * nj:(rr + 1) * nj] = kv.T.astype(BF16)


def _compress_kernel(xt_ref, wc_ref, w2_ref, b2_ref, pos_ref, wpe_ref, kc_ref, kvct_ref, xs_ref, perm_ref, *, length):
    def xpose(i, _):
        t0 = pl.multiple_of(i * XPOSE_CHUNK, XPOSE_CHUNK)
        xs_ref[pl.ds(t0, XPOSE_CHUNK), :] = xt_ref[0, 0, :, pl.ds(t0, XPOSE_CHUNK)].T
        return 0

    lax.fori_loop(0, length // XPOSE_CHUNK, xpose, 0)
    _compress_rows(xs_ref, wc_ref, w2_ref, b2_ref, pos_ref, wpe_ref, kc_ref.at[0, 0], kvct_ref.at[0, 0], perm_ref,
                   length // SEL_BLOCK)


PAGES_PER_STEP = 16


def _compress_paged_kernel(pt_ref, *refs, n_pages, per_step):
    page_refs = refs[:per_step]
    wc_ref, w2_ref, b2_ref, pos_ref, wpe_ref, kc_ref, kvct_ref, xs_ref, perm_ref = refs[per_step:]
    s = pl.program_id(1)
    for k in range(per_step):
        r0 = pl.multiple_of((s * per_step + k) * PAGE_SIZE, PAGE_SIZE)
        for g in range(NSA_KV_GROUPS):
            xs_ref[g, pl.ds(r0, PAGE_SIZE), :] = page_refs[k][0, g].T

    @pl.when(s == n_pages // per_step - 1)
    def _():
        for g in range(NSA_KV_GROUPS):
            _compress_rows(xs_ref.at[g], wc_ref, w2_ref, b2_ref, pos_ref, wpe_ref, kc_ref.at[0, g], kvct_ref.at[0, g],
                           perm_ref, n_pages * PAGE_SIZE // SEL_BLOCK)


def _compress_paged(cache_t, page_table, cw):
    wc, w2, b2, pos, wpe = cw
    nb, n_pages = page_table.shape
    g = cache_t.shape[1]
    length = n_pages * PAGE_SIZE
    n_rows = length // CMP_STRIDE
    per_step = min(PAGES_PER_STEP, n_pages)
    steps = n_pages // per_step
    page_spec = lambda k: pl.BlockSpec(
        (1, g, LANES, PAGE_SIZE), lambda b, s, pt: (pt[b * n_pages + s * per_step + k], 0, 0, 0))
    const = lambda a: pl.BlockSpec(a.shape, lambda b, s, pt: (0,) * a.ndim)
    return pl.pallas_call(
        functools.partial(_compress_paged_kernel, n_pages=n_pages, per_step=per_step),
        grid_spec=pltpu.PrefetchScalarGridSpec(
            num_scalar_prefetch=1,
            grid=(nb, steps),
            in_specs=[page_spec(k) for k in range(per_step)] + [const(a) for a in (wc, w2, b2, pos, wpe)],
            out_specs=(
                pl.BlockSpec((1, g, n_rows, LANES), lambda b, s, pt: (b, 0, 0, 0)),
                pl.BlockSpec((1, g, LANES, n_rows), lambda b, s, pt: (b, 0, 0, 0)),
            ),
            scratch_shapes=[pltpu.VMEM((g, length, LANES), F32), pltpu.VMEM((n_rows, LANES), F32)],
        ),
        out_shape=(
            jax.ShapeDtypeStruct((nb, g, n_rows, LANES), BF16),
            jax.ShapeDtypeStruct((nb, g, LANES, n_rows), BF16),
        ),
        compiler_params=_cparams(("parallel", "arbitrary")),
        name="compress_paged",
    )(page_table.reshape(-1), *([cache_t] * per_step), wc, w2, b2, pos, wpe)


def _compress_weights(cmp_pos, w_cmp1, w_cmp2, b_cmp2):
    eye = jnp.eye(2, dtype=F32)
    w5 = w_cmp1.reshape(2, 2, CMP_STRIDE, HEAD_DIM, CMP_HIDDEN)
    wc = jnp.einsum("vhcde,vw->cvdhwe", w5, eye).reshape(CMP_STRIDE // 2, 2 * LANES, 2 * LANES).astype(BF16)
    w2 = jnp.einsum("ved,vw->vewd", w_cmp2, eye).reshape(LANES, LANES).astype(BF16)
    b2 = b_cmp2.reshape(1, LANES)
    wpe = jnp.einsum("vcde,vw->vcdwe", w_cmp1, eye).reshape(2 * CMP_BLOCK * HEAD_DIM, LANES).astype(BF16)
    pos = jnp.broadcast_to(cmp_pos.reshape(1, -1), (8, 2 * CMP_BLOCK * HEAD_DIM)).astype(BF16)
    return wc, w2, b2, pos, wpe


def _compress(kvt, cw):
    wc, w2, b2, pos, wpe = cw
    nb, g, _, length = kvt.shape
    n_rows = length // CMP_STRIDE
    const2 = lambda b, gg: (0, 0)
    return pl.pallas_call(
        functools.partial(_compress_kernel, length=length),
        grid=(nb, g),
        in_specs=[
            pl.BlockSpec((1, 1, LANES, length), lambda b, gg: (b, gg, 0, 0)),
            pl.BlockSpec(wc.shape, lambda b, gg: (0, 0, 0)),
            pl.BlockSpec(w2.shape, const2),
            pl.BlockSpec(b2.shape, const2),
            pl.BlockSpec(pos.shape, const2),
            pl.BlockSpec(wpe.shape, const2),
        ],
        out_specs=(
            pl.BlockSpec((1, 1, n_rows, LANES), lambda b, gg: (b, gg, 0, 0)),
            pl.BlockSpec((1, 1, LANES, n_rows), lambda b, gg: (b, gg, 0, 0)),
        ),
        out_shape=(
            jax.ShapeDtypeStruct((nb, g, n_rows, LANES), BF16),
            jax.ShapeDtypeStruct((nb, g, LANES, n_rows), BF16),
        ),
        scratch_shapes=[pltpu.VMEM((length, LANES), F32), pltpu.VMEM((n_rows, LANES), F32)],
        compiler_params=_cparams(("parallel", "parallel")),
        name="compress",
    )(kvt, wc, w2, b2, pos, wpe)


BIAS_BLOCKS = 16
KEY_TILE = 512
PREP_TILE = 1024
N_COLS = NSA_GROUP_HEADS * Q_TILE
SLC_UNROLL = 4
V_ROWS = HEAD_DIM + 16
WIN_KEYS = WINDOW + Q_TILE
CMP_ROW_CHOICES = (64, 128, 192)


def _kv_prep_kernel(slc_ref, win_ref, kslc_ref, kwin_ref, vslc_ref, vwin_ref):
    shape = (PREP_TILE, LANES)
    lane = lax.broadcasted_iota(jnp.int32, shape, 1)
    blk = (lax.broadcasted_iota(jnp.int32, shape, 0) // SEL_BLOCK) % BIAS_BLOCKS
    onehot = jnp.where(lane - HEAD_DIM == blk, 1.0, 0.0)
    ks = slc_ref[0, 0].T
    kslc_ref[0, 0] = jnp.where(lane < HEAD_DIM, ks, onehot).astype(BF16)
    kw = win_ref[0, 0].T
    kwin_ref[0, 0] = jnp.where(lane < HEAD_DIM, kw, 0.0).astype(BF16)
    ones = jnp.ones((V_ROWS - HEAD_DIM, PREP_TILE), BF16)
    vslc_ref[0, 0] = jnp.concatenate([slc_ref[0, 0, HEAD_DIM:, :].astype(BF16), ones], axis=0)
    vwin_ref[0, 0] = jnp.concatenate([win_ref[0, 0, HEAD_DIM:, :].astype(BF16), ones], axis=0)


def _kv_prep(kvt_slc, kvt_win):
    nb, g, _, t_len = kvt_slc.shape
    in_spec = pl.BlockSpec((1, 1, LANES, PREP_TILE), lambda b, gg, i: (b, gg, 0, i))
    k_spec = pl.BlockSpec((1, 1, PREP_TILE, LANES), lambda b, gg, i: (b, gg, i, 0))
    v_spec = pl.BlockSpec((1, 1, V_ROWS, PREP_TILE), lambda b, gg, i: (b, gg, 0, i))
    k_shape = jax.ShapeDtypeStruct((nb, g, t_len, LANES), BF16)
    v_shape = jax.ShapeDtypeStruct((nb, g, V_ROWS, t_len), BF16)
    return pl.pallas_call(
        _kv_prep_kernel,
        grid=(nb, g, t_len // PREP_TILE),
        in_specs=[in_spec, in_spec],
        out_specs=(k_spec, k_spec, v_spec, v_spec),
        out_shape=(k_shape, k_shape, v_shape, v_shape),
        compiler_params=_cparams(("parallel", "parallel", "parallel")),
        name="kv_prep",
    )(kvt_slc, kvt_win)


def _online_update(carry, s, vt):
    m, acc = carry
    m_new = jnp.maximum(m, jnp.max(s, axis=0, keepdims=True))
    p = jnp.exp(s - m_new)
    acc = jnp.exp(m - m_new) * acc + jnp.dot(vt, p.astype(BF16), preferred_element_type=F32)
    return m_new, acc


def _softmax_out(acc):
    return acc[:HEAD_DIM] / acc[HEAD_DIM:HEAD_DIM + 1]


def _select_blocks(imp, q0):
    jrow = lax.broadcasted_iota(jnp.int32, imp.shape, 0)
    cur = (q0 + lax.broadcasted_iota(jnp.int32, imp.shape, 1)) // SEL_BLOCK
    work = jnp.where(jrow <= cur, imp, -1.0)
    bias = jnp.full(imp.shape, NEG_INF, F32)
    for forced_blk in (0, cur, cur - 1):
        hit = jrow == forced_blk
        bias = jnp.where(hit, 0.0, bias)
        work = jnp.where(hit, -3.0e38, work)
    jf = jrow.astype(F32)
    for _ in range(N_SEL - 3):
        mx = jnp.max(work, axis=0, keepdims=True)
        first = jnp.min(jnp.where(work == mx, jf, 1.0e9), axis=0, keepdims=True)
        hit = jf == first
        bias = jnp.where(hit, 0.0, bias)
        work = jnp.where(hit, -3.0e38, work)
    return bias


def _nsa_prompt_kernel(qa_ref, ga_ref, kc_ref, kvct_ref, kslc_ref, vslc_ref, kwin_ref, vwin_ref,
                       o_ref, qm_ref, bias_ref, ocmp_ref, mtail_ref, acctail_ref, *, nj):
    qi = pl.program_id(2)

    @pl.when(qi == 0)
    def _():
        qm_ref[...] = jnp.zeros_like(qm_ref)

    q_t = qa_ref[...].T
    for r in range(NSA_GROUP_HEADS):
        qm_ref[0:HEAD_DIM, r * Q_TILE:(r + 1) * Q_TILE] = q_t[r * HEAD_DIM:(r + 1) * HEAD_DIM].astype(BF16)
    ga_t = ga_ref[...].T

    def gate(branch):
        return jnp.concatenate([ga_t[r * 3 + branch:r * 3 + branch + 1] for r in range(NSA_GROUP_HEADS)], axis=1)

    q0 = qi * Q_TILE
    qpos_c = q0 + (lax.broadcasted_iota(jnp.int32, (1, N_COLS), 1) & (Q_TILE - 1))

    def compressed_and_select(n):
        kc = jnp.concatenate([kc_ref[0, 0, rr * nj:rr * nj + n, :] for rr in range(4)], axis=0)
        vct = jnp.concatenate([kvct_ref[0, 0, HEAD_DIM:, rr * nj:rr * nj + n] for rr in range(4)], axis=1)
        s = jnp.dot(kc, qm_ref[...], preferred_element_type=F32)
        row = lax.broadcasted_iota(jnp.int32, (4 * n, N_COLS), 0)
        cmp_end = (row % n) * SEL_BLOCK + (row // n) * CMP_STRIDE + (CMP_BLOCK - 1)
        vis = cmp_end <= qpos_c
        s = jnp.where(vis, s, NEG_INF)
        m = jnp.max(s, axis=0, keepdims=True)
        p = jnp.where(vis, jnp.exp(s - m), 0.0)
        l = jnp.sum(p, axis=0, keepdims=True)
        pn = p / jnp.where(l > 0.0, l, 1.0)
        ocmp_ref[...] = jnp.dot(vct, pn.astype(BF16), preferred_element_type=F32)
        p3 = pn[3 * n:4 * n]
        jrow4 = lax.broadcasted_iota(jnp.int32, (n, N_COLS), 0)
        prev = jnp.where(jrow4 == 0, 0.0, pltpu.roll(p3, 1, 0))
        a = pn[0:n] + pn[n:2 * n] + pn[2 * n:3 * n] + p3 + prev
        imp = a[:, 0:Q_TILE]
        for r in range(1, NSA_GROUP_HEADS):
            imp = imp + a[:, r * Q_TILE:(r + 1) * Q_TILE]
        bias16 = _select_blocks(imp, q0).astype(BF16)
        for r in range(NSA_GROUP_HEADS):
            bias_ref[0:n, r * Q_TILE:(r + 1) * Q_TILE] = bias16

    sizes = [n for n in CMP_ROW_CHOICES if n < nj] + [nj]
    need = (q0 + Q_TILE) // SEL_BLOCK
    for v, n in enumerate(sizes):
        lo = sizes[v - 1] if v else 0
        pl.when((need > lo) & ((need <= n) | (v == len(sizes) - 1)))(functools.partial(compressed_and_select, n))
    o_cmp = ocmp_ref[...]


    init =(jnp.full((1, N_COLS), -jnp.inf, F32), jnp.zeros((V_ROWS, N_COLS), F32))
    q_rows = qm_ref[0:HEAD_DIM, :]
    pad_rows = jnp.zeros((LANES - HEAD_DIM - BIAS_BLOCKS, N_COLS), BF16)

    def slc_scores(st):
        b0 = pl.multiple_of((st * KEY_TILE // (SEL_BLOCK * BIAS_BLOCKS)) * BIAS_BLOCKS, BIAS_BLOCKS)
        qm = jnp.concatenate([q_rows, bias_ref[pl.ds(b0, BIAS_BLOCKS), :], pad_rows], axis=0)
        k0 = pl.multiple_of(st * KEY_TILE, KEY_TILE)
        return jnp.dot(kslc_ref[0, 0, pl.ds(k0, KEY_TILE), :], qm, preferred_element_type=F32)

    def slc_absorb(st, carry, sc):
        k0 = pl.multiple_of(st * KEY_TILE, KEY_TILE)
        return _online_update(carry, sc, vslc_ref[0, 0, :, pl.ds(k0, KEY_TILE)])

    def slc_group(i, carry):
        st0 = i * SLC_UNROLL
        sc = slc_scores(st0)
        for u in range(SLC_UNROLL):
            sc_next = slc_scores(st0 + u + 1) if u + 1 < SLC_UNROLL else None
            carry = slc_absorb(st0 + u, carry, sc)
            sc = sc_next
        return carry

    n_tiles = q0 // KEY_TILE + 1
    tail = (n_tiles - 1) % SLC_UNROLL + 1
    m_s, acc_s = lax.fori_loop(0, (n_tiles - tail) // SLC_UNROLL, slc_group, init)
    mtail_ref[...] = m_s
    acctail_ref[...] = acc_s

    def slc_tail(size):
        st0 = n_tiles - size
        carry = (mtail_ref[...], acctail_ref[...])
        sc = slc_scores(st0)
        for u in range(size):
            sc_next = slc_scores(st0 + u + 1) if u + 1 < size else None
            if u == size - 1:
                kpos = (st0 + u) * KEY_TILE + lax.broadcasted_iota(jnp.int32, (KEY_TILE, N_COLS), 0)
                sc = jnp.where(kpos <= qpos_c, sc, NEG_INF)
            carry = slc_absorb(st0 + u, carry, sc)
            sc = sc_next
        acctail_ref[...] = carry[1]

    for size in range(1, SLC_UNROLL + 1):
        pl.when(tail == size)(functools.partial(slc_tail, size))
    o_slc = _softmax_out(acctail_ref[...])

    kw0 = pl.multiple_of(jnp.maximum(q0 - WINDOW, 0), Q_TILE)
    sc = jnp.dot(kwin_ref[0, 0, pl.ds(kw0, WIN_KEYS), :], qm_ref[...], preferred_element_type=F32)
    dpos = qpos_c - (kw0 + lax.broadcasted_iota(jnp.int32, (WIN_KEYS, N_COLS), 0))
    sc = jnp.where(dpos >= 0, jnp.where(dpos < WINDOW, sc, NEG_INF), NEG_INF)
    p = jnp.exp(sc - jnp.max(sc, axis=0, keepdims=True))
    o_win = _softmax_out(jnp.dot(vwin_ref[0, 0, :, pl.ds(kw0, WIN_KEYS)], p.astype(BF16), preferred_element_type=F32))

    o = gate(0) * o_cmp + gate(1) * o_slc + gate(2) * o_win
    o_rows = jnp.concatenate([o[:, r * Q_TILE:(r + 1) * Q_TILE] for r in range(NSA_GROUP_HEADS)], axis=0)
    o_ref[...] = o_rows.T.astype(BF16)


def _nsa_prompt(qa, ga, kc, kvct, kslc, vslc, kwin, vwin):
    nb, g, t_len, _ = kslc.shape
    nq = t_len // Q_TILE
    nj = t_len // SEL_BLOCK
    n_cmp_rows = kc.shape[2]
    assert nj % BIAS_BLOCKS == 0 and n_cmp_rows == 4 * nj and t_len % KEY_TILE == 0 and t_len >= WIN_KEYS
    tile = lambda b, gg, i: (b * nq + i, gg)
    whole = lambda b, gg, i: (b, gg, 0, 0)
    return pl.pallas_call(
        functools.partial(_nsa_prompt_kernel, nj=nj),
        grid=(nb, g, nq),
        in_specs=[
            pl.BlockSpec((Q_TILE, NSA_GROUP_HEADS * HEAD_DIM), tile),
            pl.BlockSpec((Q_TILE, LANES), tile),
            pl.BlockSpec((1, 1, n_cmp_rows, LANES), whole),
            pl.BlockSpec((1, 1, LANES, n_cmp_rows), whole),
            pl.BlockSpec((1, 1, t_len, LANES), whole),
            pl.BlockSpec((1, 1, V_ROWS, t_len), whole),
            pl.BlockSpec((1, 1, t_len, LANES), whole),
            pl.BlockSpec((1, 1, V_ROWS, t_len), whole),
        ],
        out_specs=pl.BlockSpec((Q_TILE, NSA_GROUP_HEADS * HEAD_DIM), tile),
        out_shape=jax.ShapeDtypeStruct((nb * t_len, NSA_HEADS * HEAD_DIM), BF16),
        scratch_shapes=[pltpu.VMEM((LANES, N_COLS), BF16), pltpu.VMEM((nj, N_COLS), BF16),
                        pltpu.VMEM((HEAD_DIM, N_COLS), F32), pltpu.VMEM((1, N_COLS), F32),
                        pltpu.VMEM((V_ROWS, N_COLS), F32)],
        compiler_params=_cparams(("parallel", "parallel", "arbitrary")),
        name="nsa_prompt",
    )(qa, ga, kc, kvct, kslc, vslc, kwin, vwin)


HEAD_ROWS = 8
N_PAST_SEL = N_SEL - 1


def _softmax_lanes(s_list, mask_list):
    s_list = [jnp.where(mk, s, NEG_INF) for s, mk in zip(s_list, mask_list)]
    m = s_list[0].max(axis=-1, keepdims=True)
    for s in s_list[1:]:
        m = jnp.maximum(m, s.max(axis=-1, keepdims=True))
    p_list = [jnp.where(mk, jnp.exp(s - m), 0.0) for s, mk in zip(s_list, mask_list)]
    l = p_list[0].sum(axis=-1, keepdims=True)
    for p in p_list[1:]:
        l = l + p.sum(axis=-1, keepdims=True)
    inv = 1.0 / jnp.where(l > 0.0, l, 1.0)
    return [p * inv for p in p_list]


def _dot_nt(a, b):
    return lax.dot_general(a, b, (((1,), (1,)), ((), ())), preferred_element_type=F32)


def _dec_cmp_kernel(q_ref, kvct_ref, o_ref, imp_ref, *, nj):
    q = q_ref[0, 0]
    n = 4 * nj
    s = jnp.dot(q, kvct_ref[0, 0, :HEAD_DIM, :], preferred_element_type=F32)
    lane = lax.broadcasted_iota(jnp.int32, (HEAD_ROWS, n), 1)
    blk = 4 * (lane % nj) + lane // nj
    (pn,) = _softmax_lanes([s], [blk <= n - 2])
    o_ref[0, 0] = _dot_nt(pn.astype(BF16), kvct_ref[0, 0, HEAD_DIM:, :])
    row = lax.broadcasted_iota(jnp.int32, (HEAD_ROWS, n), 0)
    ph = jnp.sum(jnp.where(row < NSA_GROUP_HEADS, pn, 0.0), axis=0, keepdims=True)
    p3 = ph[:, 3 * nj:]
    prev = jnp.where(lane[0:1, :nj] == 0, 0.0, pltpu.roll(p3, 1, 1))
    imp = ph[:, :nj] + ph[:, nj:2 * nj] + ph[:, 2 * nj:3 * nj] + p3 + prev
    imp_ref[0, 0] = jnp.broadcast_to(imp, (HEAD_ROWS, nj))


def _dec_cmp(q8, kvct):
    nb, g, _, n = kvct.shape
    nj = n // 4
    blk = lambda *shape: pl.BlockSpec((1, 1) + shape, lambda b, gg: (b, gg, 0, 0))
    return pl.pallas_call(
        functools.partial(_dec_cmp_kernel, nj=nj),
        grid=(nb, g),
        in_specs=[blk(HEAD_ROWS, HEAD_DIM), blk(LANES, n)],
        out_specs=(blk(HEAD_ROWS, HEAD_DIM), blk(HEAD_ROWS, nj)),
        out_shape=(jax.ShapeDtypeStruct((nb, g, HEAD_ROWS, HEAD_DIM), F32),
                   jax.ShapeDtypeStruct((nb, g, HEAD_ROWS, nj), F32)),
        compiler_params=_cparams(("parallel", "parallel")),
        name="dec_cmp",
    )(q8, kvct)


def _dec_topk_kernel(imp_ref, idx_ref):
    work = imp_ref[...].T
    nj, n = work.shape
    jrow = lax.broadcasted_iota(jnp.int32, (nj, n), 0)
    jf = jrow.astype(F32)
    for forced_blk in (0, nj - 1):
        work = jnp.where(jrow == forced_blk, FORCE_SCORE, work)
    picks = []
    for _ in range(N_PAST_SEL):
        mx = jnp.max(work, axis=0, keepdims=True)
        first = jnp.min(jnp.where(work == mx, jf, 1.0e9), axis=0, keepdims=True)
        picks.append(first)
        work = jnp.where(jf == first, -3.0e38, work)
    picks.append(jnp.full((1, n), float(nj), F32))
    idx_ref[...] = jnp.concatenate(picks, axis=0)


def _dec_topk(imp2):
    n, nj = imp2.shape
    return pl.pallas_call(
        _dec_topk_kernel,
        out_shape=jax.ShapeDtypeStruct((N_SEL, n), F32),
        compiler_params=pltpu.CompilerParams(vmem_limit_bytes=VMEM_LIMIT),
        name="dec_topk",
    )(imp2)


def _dec_attn_kernel(phys_ref, half_ref, q_ref, gate_ref, ocmp_ref, *refs):
    page_refs = refs[:N_PAST_SEL]
    slcnew_ref, winnew_ref, win_ref, o_ref, winout_ref = refs[N_PAST_SEL:]
    b = pl.program_id(0)
    base = (b * pl.num_programs(1) + pl.program_id(1)) * N_PAST_SEL
    q = q_ref[0, 0]
    lane = lax.broadcasted_iota(jnp.int32, (HEAD_ROWS, LANES), 1)
    own = lane == b

    def attend(kv_blocks, masks):
        scores = [jnp.dot(q, kv[:HEAD_DIM].astype(BF16), preferred_element_type=F32) for kv in kv_blocks]
        probs = _softmax_lanes(scores, masks)
        out = jnp.zeros((HEAD_ROWS, HEAD_DIM), F32)
        for p, kv in zip(probs, kv_blocks):
            out = out + _dot_nt(p.astype(BF16), kv[HEAD_DIM:].astype(BF16))
        return out

    pages = [r[0, 0] for r in page_refs] + [slcnew_ref[0, 0]]
    masks = [(lane // SEL_BLOCK) == half_ref[base + k] for k in range(N_PAST_SEL)] + [own]
    o_slc = attend(pages, masks)
    st = win_ref[0, 0]
    win_c = st.shape[1]
    wlane = lax.broadcasted_iota(jnp.int32, (HEAD_ROWS, win_c), 1)
    o_win = attend([st, winnew_ref[0, 0]], [wlane > win_c - WINDOW, own])
    gates = gate_ref[0, 0]
    o_ref[0, 0] = gates[:, 0:1] * ocmp_ref[0, 0] + gates[:, 1:2] * o_slc + gates[:, 2:3] * o_win
    new = winnew_ref[0, 0]
    lane_full = lax.broadcasted_iota(jnp.int32, new.shape, 1)
    new_col = jnp.sum(jnp.where(lane_full == b, new, 0.0), axis=1, keepdims=True)
    col = lax.broadcasted_iota(jnp.int32, st.shape, 1)
    winout_ref[0, 0] = jnp.where(col == win_c - 1, new_col, pltpu.roll(st, win_c - 1, 1))


def _dec_attn(phys, half, q8, gates, o_cmp, slc_t, slc_new, win_new, win_t):
    nb, g, _, win_c = win_t.shape
    assert slc_new.shape[-1] == nb == LANES
    blk = lambda *shape: pl.BlockSpec((1, 1) + shape, lambda b, gg, ph, hf: (b, gg, 0, 0))
    page = lambda k: pl.BlockSpec((1, 1, LANES, PAGE_SIZE),
                                  lambda b, gg, ph, hf: (ph[(b * g + gg) * N_PAST_SEL + k], gg, 0, 0))
    new = pl.BlockSpec((1, 1, LANES, nb), lambda b, gg, ph, hf: (0, gg, 0, 0))
    return pl.pallas_call(
        _dec_attn_kernel,
        grid_spec=pltpu.PrefetchScalarGridSpec(
            num_scalar_prefetch=2,
            grid=(nb, g),
            in_specs=[blk(HEAD_ROWS, HEAD_DIM), blk(HEAD_ROWS, LANES), blk(HEAD_ROWS, HEAD_DIM)]
            + [page(k) for k in range(N_PAST_SEL)] + [new, new, blk(LANES, win_c)],
            out_specs=(blk(HEAD_ROWS, HEAD_DIM), blk(LANES, win_c)),
        ),
        out_shape=(jax.ShapeDtypeStruct((nb, g, HEAD_ROWS, HEAD_DIM), F32),
                   jax.ShapeDtypeStruct((nb, g, LANES, win_c), F32)),
        compiler_params=_cparams(("parallel", "parallel")),
        name="dec_attn",
    )(phys, half, q8, gates, o_cmp, *([slc_t] * N_PAST_SEL), slc_new, win_new, win_t)


RET_STEP_CHUNKS = 8


def _log_gamma():
    return jnp.log1p(-jnp.exp2(-5.0 - jnp.arange(RET_HEADS, dtype=F32)))


def _ret_tables(chunk):
    lg = _log_gamma()
    n = jnp.arange(chunk, dtype=F32)
    diff = n[:, None] - n[None, :]
    decay = jnp.exp(jnp.maximum(diff, 0.0)[None] * lg[:, None, None]) * (diff >= 0)[None]
    rep = lambda col: jnp.broadcast_to(col[:, :, None], (RET_HEADS, chunk, RET_DV))
    qdec = rep(jnp.exp((n + 1.0)[None, :] * lg[:, None]))
    kdec = rep(jnp.exp((chunk - 1.0 - n)[None, :] * lg[:, None]))
    sdec = jnp.broadcast_to(jnp.exp(chunk * lg)[:, None, None], (RET_HEADS, RET_DK, RET_DV))
    return decay.astype(F32), qdec, kdec, sdec


def _group_norm_gate(o, g, gn_g, gn_b):
    mu = jnp.mean(o, axis=-1, keepdims=True)
    var = jnp.mean(jnp.square(o - mu), axis=-1, keepdims=True)
    on = (o - mu) * lax.rsqrt(var + LN_EPS) * gn_g + gn_b
    return (jax.nn.silu(g) * on).astype(BF16)


def _ret_prompt_kernel(qb_ref, kb_ref, vb_ref, gb_ref, decay_ref, qdec_ref, kdec_ref, sdec_ref, gng_ref, gnb_ref,
                       bm_ref, state_ref, st_ref):
    @pl.when(pl.program_id(1) == 0)
    def _():
        st_ref[...] = jnp.zeros_like(st_ref)

    def chunk_step(c, _):
        rows = pl.ds(pl.multiple_of(c * RET_CHUNK, RET_CHUNK), RET_CHUNK)
        for h in range(RET_HEADS):
            cols = slice(h * RET_DK, (h + 1) * RET_DK)
            q = qb_ref[rows, cols]
            k = kb_ref[rows, cols]
            v = vb_ref[rows, cols]
            st = st_ref[h]
            inner = lax.dot_general(q, k.astype(BF16), (((1,), (1,)), ((), ())), preferred_element_type=F32)
            inner = inner * decay_ref[h]
            o = jnp.dot(inner.astype(BF16), v, preferred_element_type=F32)
            o = o + jnp.dot(q, st.astype(BF16), preferred_element_type=F32) * qdec_ref[h]
            kd = (k * kdec_ref[h]).astype(BF16)
            st_ref[h] = sdec_ref[h] * st + lax.dot_general(kd, v, (((0,), (0,)), ((), ())), preferred_element_type=F32)
            bm_ref[rows, cols] = _group_norm_gate(o, gb_ref[rows, cols], gng_ref[:, cols], gnb_ref[:, cols])
        return 0

    lax.fori_loop(0, RET_STEP_CHUNKS, chunk_step, 0)
    state_ref[0] = st_ref[...]


def _ret_prompt(qb, kb, vb, gb, gn_g, gn_b, nb):
    n = qb.shape[0]
    tt = RET_STEP_CHUNKS * RET_CHUNK
    nt = n // nb // tt
    decay, qdec, kdec, sdec = _ret_tables(RET_CHUNK)
    row = pl.BlockSpec((tt, RET_HEADS * RET_DK), lambda b, i: (b * nt + i, 0))
    tab = pl.BlockSpec((RET_HEADS, RET_CHUNK, RET_DV), lambda b, i: (0, 0, 0))
    vec = pl.BlockSpec((1, RET_HEADS * RET_DV), lambda b, i: (0, 0))
    return pl.pallas_call(
        _ret_prompt_kernel,
        grid=(nb, nt),
        in_specs=[row, row, row, row, tab, tab, tab, tab, vec, vec],
        out_specs=(row, pl.BlockSpec((1, RET_HEADS, RET_DK, RET_DV), lambda b, i: (b, 0, 0, 0))),
        out_shape=(jax.ShapeDtypeStruct((n, RET_HEADS * RET_DV), BF16),
                   jax.ShapeDtypeStruct((nb, RET_HEADS, RET_DK, RET_DV), F32)),
        scratch_shapes=[pltpu.VMEM((RET_HEADS, RET_DK, RET_DV), F32)],
        compiler_params=_cparams(("parallel", "arbitrary")),
        name="ret_prompt",
    )(qb, kb, vb, gb, decay, qdec, kdec, sdec, gn_g.reshape(1, -1), gn_b.reshape(1, -1))


def _ret_step_kernel(q_ref, k_ref, v_ref, g_ref, st_ref, qdec_ref, sdec_ref, gng_ref, gnb_ref, bm_ref, stout_ref):
    for s in range(q_ref.shape[0]):
        for h in range(RET_HEADS):
            cols = slice(h * RET_DK, (h + 1) * RET_DK)
            q = q_ref[s, :, cols]
            k = k_ref[s, :, cols].astype(BF16)
            v = v_ref[s, :, cols]
            st = st_ref[s, h]
            qk = jnp.sum(q.astype(F32) * k.astype(F32), axis=-1, keepdims=True)
            o = qk.astype(BF16).astype(F32) * v.astype(F32)
            o = o + jnp.dot(q, st.astype(BF16), preferred_element_type=F32) * qdec_ref[h]
            stout_ref[s, h] = sdec_ref[h] * st + lax.dot_general(k, v, (((0,), (0,)), ((), ())),
                                                                 preferred_element_type=F32)
            bm_ref[s, :, cols] = _group_norm_gate(o, g_ref[s, :, cols], gng_ref[:, cols], gnb_ref[:, cols])


RET_STEP_SEQS = 4


def _ret_step(q8, k8, v8, g8, state, gn_g, gn_b):
    nb = state.shape[0]
    ns = RET_STEP_SEQS if nb % RET_STEP_SEQS == 0 else 1
    _, qdec, _, sdec = _ret_tables(1)
    tok = pl.BlockSpec((ns, HEAD_ROWS, RET_HEADS * RET_DK), lambda b: (b, 0, 0))
    st = pl.BlockSpec((ns, RET_HEADS, RET_DK, RET_DV), lambda b: (b, 0, 0, 0))
    vec = pl.BlockSpec((1, RET_HEADS * RET_DV), lambda b: (0, 0))
    return pl.pallas_call(
        _ret_step_kernel,
        grid=(nb // ns,),
        in_specs=[tok, tok, tok, tok, st,
                  pl.BlockSpec((RET_HEADS, 1, RET_DV), lambda b: (0, 0, 0)),
                  pl.BlockSpec((RET_HEADS, RET_DK, RET_DV), lambda b: (0, 0, 0)), vec, vec],
        out_specs=(tok, st),
        out_shape=(jax.ShapeDtypeStruct((nb, HEAD_ROWS, RET_HEADS * RET_DV), BF16),
                   jax.ShapeDtypeStruct(state.shape, F32)),
        compiler_params=_cparams(("parallel",)),
        name="ret_step",
    )(q8, k8, v8, g8, state, qdec, sdec, gn_g.reshape(1, -1), gn_b.reshape(1, -1))


ROUTE_E0 = N_GROUPS
MOE_ROWS = 256
SAMPLE_MOE_ROWS = 32


def _layer_norm(v, g, b):
    mu = jnp.mean(v, axis=-1, keepdims=True)
    var = jnp.mean(jnp.square(v - mu), axis=-1, keepdims=True)
    return (v - mu) * lax.rsqrt(var + LN_EPS) * g + b


def _first_lane(hit, lane):
    return jnp.min(jnp.where(hit, lane, 4 * LANES), axis=-1, keepdims=True)


def _merge_kernel(on_ref, bm_ref, gm_ref, x_ref, wn_ref, wr_ref, wo_ref, g1_ref, b1_ref, wrt_ref, brt_ref, tri_ref,
                  x1_ref, route_ref, counts_ref, carry_ref):
    @pl.when(pl.program_id(0) == 0)
    def _():
        carry_ref[...] = jnp.zeros_like(carry_ref)

    gate_a = jax.nn.sigmoid(gm_ref[:, :D_MODEL])
    gate_b = jax.nn.sigmoid(gm_ref[:, D_MODEL:])
    m = gate_a * jnp.dot(on_ref[...], wn_ref[...], preferred_element_type=F32)
    m = m + gate_b * jnp.dot(bm_ref[...], wr_ref[...], preferred_element_type=F32)
    mix = jnp.dot(m.astype(BF16), wo_ref[...], preferred_element_type=F32)
    x1 = _layer_norm(DEEPNORM_ALPHA * x_ref[...] + mix, g1_ref[...], b1_ref[...])
    x1_ref[...] = x1

    lg = jnp.dot(x1, wrt_ref[...], preferred_element_type=F32, precision=lax.Precision.HIGHEST) + brt_ref[...]
    lane = lax.broadcasted_iota(jnp.int32, lg.shape, 1)
    gl = jnp.where(lane < N_GROUPS, lg, -jnp.inf)
    ge = jnp.exp(gl - jnp.max(gl, axis=-1, keepdims=True))
    pgrp = ge / jnp.sum(ge, axis=-1, keepdims=True)
    pg = jnp.max(pgrp, axis=-1, keepdims=True)
    grp = _first_lane(pgrp == pg, lane)
    lo = ROUTE_E0 + grp * EXPERTS_PER_GROUP
    in_g = (lane >= lo) & (lane < lo + EXPERTS_PER_GROUP)
    el = jnp.where(in_g, lg, -jnp.inf)
    ee = jnp.exp(el - jnp.max(el, axis=-1, keepdims=True))
    pc = jnp.where(in_g, ee / jnp.sum(ee, axis=-1, keepdims=True), -1.0)
    w1 = jnp.max(pc, axis=-1, keepdims=True)
    i1 = _first_lane(pc == w1, lane)
    pc2 = jnp.where(lane == i1, -1.0, pc)
    w2 = jnp.max(pc2, axis=-1, keepdims=True)
    i2 = _first_lane(pc2 == w2, lane)
    wsum = w1 + w2
    wt1 = pg * w1 / wsum
    wt2 = pg * w2 / wsum
    hit1 = lane == i1
    hit2 = lane == i2
    cnt = jnp.where(hit1, 1.0, 0.0) + jnp.where(hit2, 1.0, 0.0)
    base = carry_ref[0:1, :] + jnp.dot(tri_ref[...], cnt.astype(BF16), preferred_element_type=F32)
    rank1 = jnp.sum(jnp.where(hit1, base, 0.0), axis=-1, keepdims=True)
    rank2 = jnp.sum(jnp.where(hit2, base, 0.0), axis=-1, keepdims=True)
    carry = carry_ref[0:1, :] + jnp.sum(cnt, axis=0, keepdims=True)
    carry_ref[...] = jnp.broadcast_to(carry, carry_ref.shape)
    counts_ref[...] = jnp.broadcast_to(carry, counts_ref.shape)
    fields = ((i1 - ROUTE_E0).astype(F32), (i2 - ROUTE_E0).astype(F32), wt1, wt2, rank1, rank2)
    out = jnp.zeros(lg.shape, F32)
    for idx, val in enumerate(fields):
        out = jnp.where(lane == idx, val, out)
    route_ref[...] = out


def _merge_weights(w_nsa_proj, w_ret_proj, w_o, w_route_group, b_route_group, w_route_exp, b_route_exp):
    pad = LANES - N_GROUPS - N_EXPERTS
    wrt = jnp.concatenate([w_route_group, w_route_exp, jnp.zeros((D_MODEL, pad), F32)], axis=1)
    brt = jnp.concatenate([b_route_group, b_route_exp, jnp.zeros((pad,), F32)])[None, :]
    return w_nsa_proj.astype(BF16), w_ret_proj.astype(BF16), w_o.astype(BF16), wrt, brt


def _merge(o_nsa, bm, gm, x2, mw, ln_g, ln_b, tm):
    wn, wr, wo, wrt, brt = mw
    n = x2.shape[0]
    tri = (jnp.arange(tm)[:, None] > jnp.arange(tm)[None, :]).astype(BF16)
    row = lambda width: pl.BlockSpec((tm, width), lambda i: (i, 0))
    const = lambda a: pl.BlockSpec(a.shape, lambda i: (0, 0))
    ln_g = ln_g.reshape(1, -1)
    ln_b = ln_b.reshape(1, -1)
    return pl.pallas_call(
        _merge_kernel,
        grid=(n // tm,),
        in_specs=[row(512), row(512), row(2 * D_MODEL), row(D_MODEL), const(wn), const(wr), const(wo),
                  const(ln_g), const(ln_b), const(wrt), const(brt), const(tri)],
        out_specs=(row(D_MODEL), row(LANES), pl.BlockSpec((8, LANES), lambda i: (0, 0))),
        out_shape=(jax.ShapeDtypeStruct((n, D_MODEL), F32), jax.ShapeDtypeStruct((n, LANES), F32),
                   jax.ShapeDtypeStruct((8, LANES), F32)),
        scratch_shapes=[pltpu.VMEM((8, LANES), F32)],
        compiler_params=_cparams(("arbitrary",)),
        name="merge_route",
    )(o_nsa, bm, gm, x2, wn, wr, wo, ln_g, ln_b, wrt, brt, tri)


SC_CHUNK_ROWS = 32


def _gather_rows(table, idx):
    info = plsc.get_sparse_core_info()
    n_cores = info.num_cores
    n_workers = n_cores * info.num_subcores
    n_idx, width = idx.shape[0], table.shape[1]
    per_worker = n_idx // n_workers
    chunk = max(c for c in range(8, SC_CHUNK_ROWS + 1, 8) if per_worker % c == 0)
    assert n_idx % (8 * n_workers) == 0
    mesh = plsc.VectorSubcoreMesh(core_axis_name="c", subcore_axis_name="s")

    def body(table_hbm, idx_hbm, out_hbm, idx_v, rows_v, sem):
        worker = lax.axis_index("s") * n_cores + lax.axis_index("c")
        base = worker * per_worker

        @pl.loop(0, per_worker // chunk)
        def _(c):
            off = pl.multiple_of(base + c * chunk, 8)
            pltpu.sync_copy(idx_hbm.at[pl.ds(off, chunk)], idx_v)
            pltpu.async_copy(table_hbm.at[idx_v], rows_v, sem).wait()
            pltpu.sync_copy(rows_v, out_hbm.at[pl.ds(off, chunk)])

    return pl.kernel(
        body, mesh=mesh,
        out_type=jax.ShapeDtypeStruct((n_idx, width), table.dtype),
        scratch_types=[pltpu.VMEM((chunk,), jnp.int32), pltpu.VMEM((chunk, width), table.dtype),
                       pltpu.SemaphoreType.DMA],
    )(table, idx)


def _expert_kernel(be_ref, nused_ref, x_ref, w1_ref, w3_ref, w2_ref, y_ref):
    @pl.when(pl.program_id(0) >= nused_ref[0])
    def _():
        y_ref[...] = jnp.zeros_like(y_ref)

    @pl.when(pl.program_id(0) < nused_ref[0])
    def _():
        x = x_ref[...].astype(BF16)
        h = jax.nn.silu(jnp.dot(x, w1_ref[0], preferred_element_type=F32))
        h = h * jnp.dot(x, w3_ref[0], preferred_element_type=F32)
        y_ref[...] = jnp.dot(h.astype(BF16), w2_ref[0], preferred_element_type=F32)


def _experts(x_sorted, block_e, n_used, w1, w3, w2, rows):
    n_blocks = block_e.shape[0]
    xspec = pl.BlockSpec((rows, D_MODEL), lambda i, be, nu: (i, 0))
    wspec = lambda a: pl.BlockSpec((1,) + a.shape[1:], lambda i, be, nu: (be[i], 0, 0))
    return pl.pallas_call(
        _expert_kernel,
        grid_spec=pltpu.PrefetchScalarGridSpec(
            num_scalar_prefetch=2,
            grid=(n_blocks,),
            in_specs=[xspec, wspec(w1), wspec(w3), wspec(w2)],
            out_specs=xspec,
        ),
        out_shape=jax.ShapeDtypeStruct(x_sorted.shape, F32),
        compiler_params=_cparams(("arbitrary",)),
        name="experts",
    )(block_e, n_used, x_sorted, w1, w3, w2)


def _combine_kernel(x1_ref, y0_ref, y1_ref, route_ref, g_ref, b_ref, o_ref):
    route = route_ref[...]
    ff = y0_ref[...] * route[:, 2:3] + y1_ref[...] * route[:, 3:4]
    o_ref[...] = _layer_norm(DEEPNORM_ALPHA * x1_ref[...] + ff, g_ref[...], b_ref[...])


def _combine(x1, y2, route, ln_g, ln_b, tm):
    n = x1.shape[0]
    row = lambda width: pl.BlockSpec((tm, width), lambda i: (i, 0))
    second = pl.BlockSpec((tm, D_MODEL), lambda i: (i + n // tm, 0))
    vec = pl.BlockSpec((1, D_MODEL), lambda i: (0, 0))
    return pl.pallas_call(
        _combine_kernel,
        grid=(n // tm,),
        in_specs=[row(D_MODEL), row(D_MODEL), second, row(LANES), vec, vec],
        out_specs=row(D_MODEL),
        out_shape=jax.ShapeDtypeStruct((n, D_MODEL), F32),
        compiler_params=_cparams(("parallel",)),
        name="combine_ln2",
    )(x1, y2, y2, route, ln_g.reshape(1, -1), ln_b.reshape(1, -1))


def _finish(o_nsa, bm, gm, x2, mw, ew, ln1, ln2, tm, rows):
    n = x2.shape[0]
    x1, route, counts = _merge(o_nsa, bm, gm, x2, mw, ln1[0], ln1[1], tm)
    sizes = counts[0, ROUTE_E0:ROUTE_E0 + N_EXPERTS].astype(jnp.int32)
    padded = (sizes + rows - 1) // rows * rows
    pad_end = jnp.cumsum(padded)
    pad_start = pad_end - padded
    e = route[:, 0:TOP_K].astype(jnp.int32)
    rank = route[:, 4:4 + TOP_K].astype(jnp.int32)
    dest = (pad_start[e] + rank).T.reshape(-1)
    n_blocks = -(-(n * TOP_K) // rows) + N_EXPERTS
    tok = jnp.tile(jnp.arange(n, dtype=jnp.int32), TOP_K)
    slot_tok = jnp.zeros((n_blocks * rows,), jnp.int32).at[dest].set(tok)
    blk0 = jnp.arange(n_blocks, dtype=jnp.int32) * rows
    block_e = jnp.sum(blk0[:, None] >= pad_end[None, :], axis=1).astype(jnp.int32)
    block_e = jnp.minimum(block_e, N_EXPERTS - 1)
    n_used = (pad_end[-1:] // rows).astype(jnp.int32)
    y_sorted = _experts(_gather_rows(x1, slot_tok), block_e, n_used, *ew, rows)
    return _combine(x1, _gather_rows(y_sorted, dest), route, ln2[0], ln2[1], tm)


PROMPT_ROWS = 256


def _feature_major(a):
    lead = a.shape[:-3]
    return jnp.moveaxis(a, -3, -1).reshape(lead + (2 * HEAD_DIM, a.shape[-3]))


def _row_major(a):
    lead = a.shape[:-2]
    return jnp.moveaxis(a.reshape(lead + (2, HEAD_DIM, a.shape[-1])), -1, -3)


def _pad_rows(a):
    return jnp.pad(a[:, None, :], ((0, 0), (0, HEAD_ROWS - 1), (0, 0)))


def kernel(x_prompt, x_sample, cache_kv_cmp, cache_kv_slc, state_kv_win, state_ret, page_table, w_in, b_in, cmp_pos, w_cmp1, w_cmp2, b_cmp2, ret_gn_g, ret_gn_b, w_nsa_proj, w_ret_proj, w_o, ln1_g, ln1_b, w_route_group, b_route_group, w_route_exp, b_route_exp, w_e1, w_e3, w_e2, ln2_g, ln2_b):
    nb, t_len, _ = x_prompt.shape
    ns, s_len, _ = x_sample.shape
    assert s_len == 1 and ns == LANES and t_len % (RET_STEP_CHUNKS * RET_CHUNK) == 0 and t_len >= WINDOW
    g = NSA_KV_GROUPS
    pw = _proj_weights(w_in, b_in)
    cw = _compress_weights(cmp_pos, w_cmp1, w_cmp2, b_cmp2)
    mw = _merge_weights(w_nsa_proj, w_ret_proj, w_o, w_route_group, b_route_group, w_route_exp, b_route_exp)
    ew = (w_e1.astype(BF16), w_e3.astype(BF16), w_e2.astype(BF16))
    ln1 = (ln1_g, ln1_b)
    ln2 = (ln2_g, ln2_b)

    cos_p, sin_p = _rope_tables(jnp.arange(t_len, dtype=jnp.int32))
    x2 = x_prompt.reshape(nb * t_len, D_MODEL)
    qa, kvt_cmp, kvt_slc, kvt_win, ga, qb, kb, vb, gb, gm = _project(x2, nb, pw, cos_p, sin_p, PROMPT_ROWS)
    kc, kvct = _compress(kvt_cmp, cw)
    kslc, kwin, vslc, vwin = _kv_prep(kvt_slc, kvt_win)
    o_nsa = _nsa_prompt(qa, ga, kc, kvct, kslc, vslc, kwin, vwin)
    bm, ret_p = _ret_prompt(qb, kb, vb, gb, ret_gn_g, ret_gn_b, nb)
    y_prompt = _finish(o_nsa, bm, gm, x2, mw, ew, ln1, ln2, PROMPT_ROWS, MOE_ROWS).reshape(x_prompt.shape)
    kv_cmp_p = _row_major(kvt_cmp)
    kv_slc_p = _row_major(kvt_slc)
    win_p = _row_major(kvt_win[..., t_len - WINDOW:])

    n_pages = page_table.shape[1]
    past_len = n_pages * PAGE_SIZE
    cos_s, sin_s = _rope_tables(jnp.full((ns,), past_len, jnp.int32))
    xs2 = x_sample.reshape(ns, D_MODEL)
    qa, new_cmp, new_slc, new_win, ga, qb, kb, vb, gb, gm = _project(xs2, 1, pw, cos_s, sin_s, ns)
    _, kvct_s = _compress_paged(_feature_major(cache_kv_cmp), page_table, cw)
    q8 = jnp.pad(qa.reshape(ns, g, NSA_GROUP_HEADS, HEAD_DIM),
                 ((0, 0), (0, 0), (0, HEAD_ROWS - NSA_GROUP_HEADS), (0, 0))).astype(BF16)
    o_cmp, imp = _dec_cmp(q8, kvct_s)
    picks = _dec_topk(imp[:, :, 0, :].reshape(ns * g, -1))
    idx = picks[:N_PAST_SEL].T.astype(jnp.int32)
    blocks_per_page = PAGE_SIZE // SEL_BLOCK
    seq = jnp.arange(ns * g, dtype=jnp.int32)[:, None] // g
    phys = page_table[seq, idx // blocks_per_page]
    half = idx % blocks_per_page
    gates = ga.reshape(ns, g, LANES)[:, :, :3 * NSA_GROUP_HEADS].reshape(ns, g, NSA_GROUP_HEADS, 3)
    gates = jnp.pad(gates, ((0, 0), (0, 0), (0, HEAD_ROWS - NSA_GROUP_HEADS), (0, LANES - 3)))
    o_s, win_t = _dec_attn(phys.reshape(-1), half.reshape(-1), q8, gates, o_cmp, _feature_major(cache_kv_slc),
                           new_slc, new_win, _feature_major(state_kv_win))
    o_nsa_s = o_s[:, :, :NSA_GROUP_HEADS].reshape(ns, NSA_HEADS * HEAD_DIM).astype(BF16)
    bm8, ret_s = _ret_step(_pad_rows(qb), _pad_rows(kb), _pad_rows(vb), _pad_rows(gb), state_ret, ret_gn_g, ret_gn_b)
    y_sample = _finish(o_nsa_s, bm8[:, 0], gm, xs2, mw, ew, ln1, ln2, ns, SAMPLE_MOE_ROWS).reshape(x_sample.shape)
    kv_cmp_s = jnp.moveaxis(new_cmp[0], -1, 0).reshape(ns, g, 1, 2, HEAD_DIM)
    kv_slc_s = jnp.moveaxis(new_slc[0], -1, 0).reshape(ns, g, 1, 2, HEAD_DIM)
    win_s = _row_major(win_t)
    return (y_prompt, y_sample, kv_cmp_p, kv_slc_p, win_p, ret_p, kv_cmp_s, kv_slc_s, win_s, ret_s)
```

```python
import functools

import numpy as np
import jax
import jax.numpy as jnp
from jax import lax
from jax.experimental import pallas as pl
from jax.experimental.pallas import tpu as pltpu
from jax.experimental.pallas import tpu_sc as plsc

F32 = jnp.float32
BF16 = jnp.bfloat16

D_MODEL = 1024
DEPTH = 1
PAGE_SIZE = 128
NSA_HEADS = 8
NSA_KV_GROUPS = 2
NSA_GROUP_HEADS = NSA_HEADS // NSA_KV_GROUPS
HEAD_DIM = 64
CMP_BLOCK = 32
CMP_STRIDE = 16
CMP_HIDDEN = 64
SEL_BLOCK = 64
N_SEL = 16
WINDOW = 512
Q_TILE = 256
FORCE_SCORE = 1.0e3
NEG_INF = -1.0e30
RET_HEADS = 4
RET_DK = 128
RET_DV = 128
RET_CHUNK = 128
ROPE_BASE = 10000.0
N_GROUPS = 4
EXPERTS_PER_GROUP = 8
N_EXPERTS = N_GROUPS * EXPERTS_PER_GROUP
TOP_K = 2
EXPERT_FF = 512
LN_EPS = 1e-5
DEEPNORM_ALPHA = (2.0 * DEPTH) ** 0.25

NSA_Q_COLS = NSA_HEADS * HEAD_DIM
NSA_KV_COLS = 3 * 2 * NSA_KV_GROUPS * HEAD_DIM
NSA_GATE_COLS = 3 * NSA_HEADS
RET_QK_COLS = RET_HEADS * RET_DK
RET_V_COLS = RET_HEADS * RET_DV

LANES = 128

P_QA = 0
P_GA = P_QA + NSA_Q_COLS
P_QB = P_GA + NSA_KV_GROUPS * LANES
P_KB = P_QB + RET_QK_COLS
P_VB = P_KB + RET_QK_COLS
P_GB = P_VB + RET_V_COLS
P_GM = P_GB + RET_V_COLS
VMEM_LIMIT = 56 * 1024 * 1024


def _cparams(sem):
    return pltpu.CompilerParams(dimension_semantics=sem, vmem_limit_bytes=VMEM_LIMIT)


def _proj_kernel(x_ref, w_ref, b_ref, wkv_ref, bkv_ref, cos_ref, sin_ref,
                 qa_ref, kcmp_ref, kslc_ref, kwin_ref, ga_ref, qb_ref, kb_ref, vb_ref, gb_ref, gm_ref):
    x = x_ref[...].astype(BF16)

    def cols(c0, n):
        h = jnp.dot(x, w_ref[:, c0:c0 + n], preferred_element_type=F32)
        return h + b_ref[:, c0:c0 + n]

    qa_ref[...] = cols(P_QA, 512) * (HEAD_DIM ** -0.5)
    ga_ref[...] = jax.nn.sigmoid(cols(P_GA, NSA_KV_GROUPS * LANES))
    cos = cos_ref[...]
    sin = sin_ref[...]
    for h in range(RET_HEADS):
        q = cols(P_QB + h * RET_DK, RET_DK)
        qb_ref[:, h * RET_DK:(h + 1) * RET_DK] = (q * cos + pltpu.roll(q, RET_DK // 2, 1) * sin).astype(BF16)
        k = cols(P_KB + h * RET_DK, RET_DK)
        kb_ref[:, h * RET_DK:(h + 1) * RET_DK] = (k * cos + pltpu.roll(k, RET_DK // 2, 1) * sin) * (RET_DK ** -0.5)
    vb_ref[...] = cols(P_VB, 512).astype(BF16)
    gb_ref[...] = cols(P_GB, 512)
    gm_ref[...] = cols(P_GM, 2048)
    kvt = lax.dot_general(wkv_ref[...], x, (((1,), (1,)), ((), ())), preferred_element_type=F32)
    kvt = kvt + bkv_ref[...]
    for br, out_ref in enumerate((kcmp_ref, kslc_ref, kwin_ref)):
        for g in range(NSA_KV_GROUPS):
            i = br * NSA_KV_GROUPS + g
            out_ref[0, g] = kvt[i * LANES:(i + 1) * LANES]


def _proj_weights(w_in, b_in):
    o_kv = NSA_Q_COLS
    o_ga = o_kv + NSA_KV_COLS
    o_qb = o_ga + NSA_GATE_COLS
    per_g = NSA_GATE_COLS // NSA_KV_GROUPS
    ga_cols = []
    for g in range(NSA_KV_GROUPS):
        ga_cols += [np.arange(o_ga + g * per_g, o_ga + (g + 1) * per_g), np.full(LANES - per_g, -1)]
    perm = np.concatenate([
        np.arange(0, NSA_Q_COLS),
        *ga_cols,
        np.arange(o_qb, o_qb + 2 * RET_QK_COLS + 2 * RET_V_COLS + 2 * D_MODEL),
    ])
    keep = jnp.asarray(perm >= 0)
    idx = jnp.asarray(np.maximum(perm, 0))
    w = jnp.where(keep[None, :], w_in[:, idx], 0.0).astype(BF16)
    b = jnp.where(keep, b_in[idx], 0.0)[None, :]
    br, g, kv, d = np.meshgrid(np.arange(3), np.arange(NSA_KV_GROUPS), np.arange(2), np.arange(HEAD_DIM), indexing="ij")
    kv_idx = jnp.asarray((o_kv + br * 256 + kv * 128 + g * 64 + d).reshape(-1))
    wkv = w_in[:, kv_idx].T.astype(BF16)
    bkv = b_in[kv_idx][:, None]
    return w, b, wkv, bkv


def _rope_tables(pos):
    half = RET_DK // 2
    inv = ROPE_BASE ** (-jnp.arange(half, dtype=F32) / half)
    ang = pos.astype(F32)[:, None] * inv[None, :]
    cos, sin = jnp.cos(ang), jnp.sin(ang)
    return jnp.concatenate([cos, cos], -1), jnp.concatenate([-sin, sin], -1)


def _project(x2, n_batch, pw, cos_t, sin_t, tm):
    w, b, wkv, bkv = pw
    n = x2.shape[0]
    t_len = n // n_batch
    nt = t_len // tm
    n_pos_blocks = cos_t.shape[0] // tm
    row = lambda i: (i, 0)
    const = lambda i: (0, 0)
    pos_map = lambda i: (i % n_pos_blocks, 0)
    kvt_shape = jax.ShapeDtypeStruct((n_batch, NSA_KV_GROUPS, LANES, t_len), F32)
    kvt_spec = pl.BlockSpec((1, NSA_KV_GROUPS, LANES, tm), lambda i: (i // nt, 0, 0, i % nt))
    out_shape = (
        jax.ShapeDtypeStruct((n, 512), F32),
        kvt_shape, kvt_shape, kvt_shape,
        jax.ShapeDtypeStruct((n, NSA_KV_GROUPS * LANES), F32),
        jax.ShapeDtypeStruct((n, 512), BF16),
        jax.ShapeDtypeStruct((n, 512), F32),
        jax.ShapeDtypeStruct((n, 512), BF16),
        jax.ShapeDtypeStruct((n, 512), F32),
        jax.ShapeDtypeStruct((n, 2048), F32),
    )
    return pl.pallas_call(
        _proj_kernel,
        grid=(n // tm,),
        in_specs=[
            pl.BlockSpec((tm, D_MODEL), row),
            pl.BlockSpec(w.shape, const),
            pl.BlockSpec(b.shape, const),
            pl.BlockSpec(wkv.shape, const),
            pl.BlockSpec(bkv.shape, const),
            pl.BlockSpec((tm, LANES), pos_map),
            pl.BlockSpec((tm, LANES), pos_map),
        ],
        out_specs=(
            pl.BlockSpec((tm, 512), row),
            kvt_spec, kvt_spec, kvt_spec,
            pl.BlockSpec((tm, NSA_KV_GROUPS * LANES), row),
            pl.BlockSpec((tm, 512), row),
            pl.BlockSpec((tm, 512), row),
            pl.BlockSpec((tm, 512), row),
            pl.BlockSpec((tm, 512), row),
            pl.BlockSpec((tm, 2048), row),
        ),
        out_shape=out_shape,
        compiler_params=_cparams(("parallel",)),
        name="proj",
    )(x2, w, b, wkv, bkv, cos_t, sin_t)


XPOSE_CHUNK = 512


def _compress_rows(xs_ref, wc_ref, w2_ref, b2_ref, pos_ref, wpe_ref, kc_ref, kvct_ref, perm_ref, nj):
    n_blk = 4 * nj
    pe = jnp.dot(pos_ref[...], wpe_ref[...], preferred_element_type=F32)[0:1]
    acc = jnp.zeros((n_blk, 2 * LANES), F32)
    for c2 in range(CMP_STRIDE // 2):
        xa = xs_ref[pl.ds(2 * c2, n_blk, stride=CMP_STRIDE), :]
        xb = xs_ref[pl.ds(2 * c2 + 1, n_blk, stride=CMP_STRIDE), :]
        x2 = jnp.concatenate([xa, xb], axis=1).astype(BF16)
        acc = acc + jnp.dot(x2, wc_ref[c2], preferred_element_type=F32)
    row = lax.broadcasted_iota(jnp.int32, (n_blk, LANES), 0)
    nxt = pltpu.roll(acc[:, LANES:], n_blk - 1, 0)
    second = jnp.where(row == n_blk - 1, 0.0, nxt)
    h = jax.nn.gelu(acc[:, :LANES] + second + pe)
    perm_ref[...] = jnp.dot(h.astype(BF16), w2_ref[...], preferred_element_type=F32) + b2_ref[...]
    lane = lax.broadcasted_iota(jnp.int32, (nj, LANES), 1)
    for rr in range(4):
        kv = perm_ref[pl.ds(rr, nj, stride=4), :]
        kc_ref[rr * nj:(rr + 1) * nj, :] = jnp.where(lane < HEAD_DIM, kv, 0.0).astype(BF16)
        kvct_ref[:, rr * nj:(rr + 1) * nj] = kv.T.astype(BF16)


def _compress_kernel(xt_ref, wc_ref, w2_ref, b2_ref, pos_ref, wpe_ref, kc_ref, kvct_ref, xs_ref, perm_ref, *, length):
    def xpose(i, _):
        t0 = pl.multiple_of(i * XPOSE_CHUNK, XPOSE_CHUNK)
        xs_ref[pl.ds(t0, XPOSE_CHUNK), :] = xt_ref[0, 0, :, pl.ds(t0, XPOSE_CHUNK)].T
        return 0

    lax.fori_loop(0, length // XPOSE_CHUNK, xpose, 0)
    _compress_rows(xs_ref, wc_ref, w2_ref, b2_ref, pos_ref, wpe_ref, kc_ref.at[0, 0], kvct_ref.at[0, 0], perm_ref,
                   length // SEL_BLOCK)


PAGES_PER_STEP = 16


def _compress_paged_kernel(pt_ref, *refs, n_pages, per_step):
    page_refs = refs[:per_step]
    (q_ref, wc_ref, w2_ref, b2_ref, pos_ref, wpe_ref, o_ref, imp_ref,
     xs_ref, perm_ref, kc_ref, kvct_ref) = refs[per_step:]
    s = pl.program_id(1)
    for k in range(per_step):
        r0 = pl.multiple_of((s * per_step + k) * PAGE_SIZE, PAGE_SIZE)
        for g in range(NSA_KV_GROUPS):
            xs_ref[g, pl.ds(r0, PAGE_SIZE), :] = page_refs[k][0, g].T

    @pl.when(s == n_pages // per_step - 1)
    def _():
        nj = n_pages * PAGE_SIZE // SEL_BLOCK
        for g in range(NSA_KV_GROUPS):
            _compress_rows(xs_ref.at[g], wc_ref, w2_ref, b2_ref, pos_ref, wpe_ref, kc_ref, kvct_ref, perm_ref, nj)
            o_ref[0, g], imp_ref[0, g] = _dec_compressed(q_ref[0, g], kvct_ref, nj)


def _compress_paged_attend(cache_t, page_table, cw, q8):
    wc, w2, b2, pos, wpe = cw
    nb, n_pages = page_table.shape
    g = cache_t.shape[1]
    length = n_pages * PAGE_SIZE
    n_rows = length // CMP_STRIDE
    nj = length // SEL_BLOCK
    per_step = min(PAGES_PER_STEP, n_pages)
    steps = n_pages // per_step
    page_spec = lambda k: pl.BlockSpec(
        (1, g, LANES, PAGE_SIZE), lambda b, s, pt: (pt[b * n_pages + s * per_step + k], 0, 0, 0))
    const = lambda a: pl.BlockSpec(a.shape, lambda b, s, pt: (0,) * a.ndim)
    per_seq = lambda *shape: pl.BlockSpec((1, g) + shape, lambda b, s, pt: (b, 0, 0, 0))
    return pl.pallas_call(
        functools.partial(_compress_paged_kernel, n_pages=n_pages, per_step=per_step),
        grid_spec=pltpu.PrefetchScalarGridSpec(
            num_scalar_prefetch=1,
            grid=(nb, steps),
            in_specs=[page_spec(k) for k in range(per_step)] + [per_seq(HEAD_ROWS, HEAD_DIM)]
            + [const(a) for a in (wc, w2, b2, pos, wpe)],
            out_specs=(per_seq(HEAD_ROWS, HEAD_DIM), per_seq(HEAD_ROWS, nj)),
            scratch_shapes=[pltpu.VMEM((g, length, LANES), F32), pltpu.VMEM((n_rows, LANES), F32),
                            pltpu.VMEM((n_rows, LANES), BF16), pltpu.VMEM((LANES, n_rows), BF16)],
        ),
        out_shape=(
            jax.ShapeDtypeStruct((nb, g, HEAD_ROWS, HEAD_DIM), F32),
            jax.ShapeDtypeStruct((nb, g, HEAD_ROWS, nj), F32),
        ),
        compiler_params=_cparams(("parallel", "arbitrary")),
        name="compress_paged",
    )(page_table.reshape(-1), *([cache_t] * per_step), q8, wc, w2, b2, pos, wpe)


def _compress_weights(cmp_pos, w_cmp1, w_cmp2, b_cmp2):
    eye = jnp.eye(2, dtype=F32)
    w5 = w_cmp1.reshape(2, 2, CMP_STRIDE, HEAD_DIM, CMP_HIDDEN)
    wc = jnp.einsum("vhcde,vw->cvdhwe", w5, eye).reshape(CMP_STRIDE // 2, 2 * LANES, 2 * LANES).astype(BF16)
    w2 = jnp.einsum("ved,vw->vewd", w_cmp2, eye).reshape(LANES, LANES).astype(BF16)
    b2 = b_cmp2.reshape(1, LANES)
    wpe = jnp.einsum("vcde,vw->vcdwe", w_cmp1, eye).reshape(2 * CMP_BLOCK * HEAD_DIM, LANES).astype(BF16)
    pos = jnp.broadcast_to(cmp_pos.reshape(1, -1), (8, 2 * CMP_BLOCK * HEAD_DIM)).astype(BF16)
    return wc, w2, b2, pos, wpe


def _compress(kvt, cw):
    wc, w2, b2, pos, wpe = cw
    nb, g, _, length = kvt.shape
    n_rows = length // CMP_STRIDE
    const2 = lambda b, gg: (0, 0)
    return pl.pallas_call(
        functools.partial(_compress_kernel, length=length),
        grid=(nb, g),
        in_specs=[
            pl.BlockSpec((1, 1, LANES, length), lambda b, gg: (b, gg, 0, 0)),
            pl.BlockSpec(wc.shape, lambda b, gg: (0, 0, 0)),
            pl.BlockSpec(w2.shape, const2),
            pl.BlockSpec(b2.shape, const2),
            pl.BlockSpec(pos.shape, const2),
            pl.BlockSpec(wpe.shape, const2),
        ],
        out_specs=(
            pl.BlockSpec((1, 1, n_rows, LANES), lambda b, gg: (b, gg, 0, 0)),
            pl.BlockSpec((1, 1, LANES, n_rows), lambda b, gg: (b, gg, 0, 0)),
        ),
        out_shape=(
            jax.ShapeDtypeStruct((nb, g, n_rows, LANES), BF16),
            jax.ShapeDtypeStruct((nb, g, LANES, n_rows), BF16),
        ),
        scratch_shapes=[pltpu.VMEM((length, LANES), F32), pltpu.VMEM((n_rows, LANES), F32)],
        compiler_params=_cparams(("parallel", "parallel")),
        name="compress",
    )(kvt, wc, w2, b2, pos, wpe)


BIAS_BLOCKS = 16
KEY_TILE = 512
PREP_TILE = 1024
N_COLS = NSA_GROUP_HEADS * Q_TILE
SLC_UNROLL = 4
V_ROWS = HEAD_DIM + 16
WIN_KEYS = WINDOW + Q_TILE
CMP_ROW_CHOICES = (64, 128, 192)


def _kv_prep_kernel(slc_ref, win_ref, kslc_ref, kwin_ref, vslc_ref, vwin_ref):
    shape = (PREP_TILE, LANES)
    lane = lax.broadcasted_iota(jnp.int32, shape, 1)
    blk = (lax.broadcasted_iota(jnp.int32, shape, 0) // SEL_BLOCK) % BIAS_BLOCKS
    onehot = jnp.where(lane - HEAD_DIM == blk, 1.0, 0.0)
    ks = slc_ref[0, 0].T
    kslc_ref[0, 0] = jnp.where(lane < HEAD_DIM, ks, onehot).astype(BF16)
    kw = win_ref[0, 0].T
    kwin_ref[0, 0] = jnp.where(lane < HEAD_DIM, kw, 0.0).astype(BF16)
    ones = jnp.ones((V_ROWS - HEAD_DIM, PREP_TILE), BF16)
    vslc_ref[0, 0] = jnp.concatenate([slc_ref[0, 0, HEAD_DIM:, :].astype(BF16), ones], axis=0)
    vwin_ref[0, 0] = jnp.concatenate([win_ref[0, 0, HEAD_DIM:, :].astype(BF16), ones], axis=0)


def _kv_prep(kvt_slc, kvt_win):
    nb, g, _, t_len = kvt_slc.shape
    in_spec = pl.BlockSpec((1, 1, LANES, PREP_TILE), lambda b, gg, i: (b, gg, 0, i))
    k_spec = pl.BlockSpec((1, 1, PREP_TILE, LANES), lambda b, gg, i: (b, gg, i, 0))
    v_spec = pl.BlockSpec((1, 1, V_ROWS, PREP_TILE), lambda b, gg, i: (b, gg, 0, i))
    k_shape = jax.ShapeDtypeStruct((nb, g, t_len, LANES), BF16)
    v_shape = jax.ShapeDtypeStruct((nb, g, V_ROWS, t_len), BF16)
    return pl.pallas_call(
        _kv_prep_kernel,
        grid=(nb, g, t_len // PREP_TILE),
        in_specs=[in_spec, in_spec],
        out_specs=(k_spec, k_spec, v_spec, v_spec),
        out_shape=(k_shape, k_shape, v_shape, v_shape),
        compiler_params=_cparams(("parallel", "parallel", "parallel")),
        name="kv_prep",
    )(kvt_slc, kvt_win)


def _online_update(carry, s, vt):
    m, acc = carry
    m_new = jnp.maximum(m, jnp.max(s, axis=0, keepdims=True))
    p = jnp.exp(s - m_new)
    acc = jnp.exp(m - m_new) * acc + jnp.dot(vt, p.astype(BF16), preferred_element_type=F32)
    return m_new, acc


def _softmax_out(acc):
    return acc[:HEAD_DIM] / acc[HEAD_DIM:HEAD_DIM + 1]


def _select_blocks(imp, q0):
    jrow = lax.broadcasted_iota(jnp.int32, imp.shape, 0)
    cur = (q0 + lax.broadcasted_iota(jnp.int32, imp.shape, 1)) // SEL_BLOCK
    work = jnp.where(jrow <= cur, imp, -1.0)
    bias = jnp.full(imp.shape, NEG_INF, F32)
    for forced_blk in (0, cur, cur - 1):
        hit = jrow == forced_blk
        bias = jnp.where(hit, 0.0, bias)
        work = jnp.where(hit, -3.0e38, work)
    jf = jrow.astype(F32)
    for _ in range(N_SEL - 3):
        mx = jnp.max(work, axis=0, keepdims=True)
        first = jnp.min(jnp.where(work == mx, jf, 1.0e9), axis=0, keepdims=True)
        hit = jf == first
        bias = jnp.where(hit, 0.0, bias)
        work = jnp.where(hit, -3.0e38, work)
    return bias


def _nsa_prompt_kernel(qa_ref, ga_ref, kc_ref, kvct_ref, kslc_ref, vslc_ref, kwin_ref, vwin_ref,
                       o_ref, qm_ref, bias_ref, ocmp_ref, mtail_ref, acctail_ref, *, nj):
    qi = pl.program_id(2)

    @pl.when(qi == 0)
    def _():
        qm_ref[...] = jnp.zeros_like(qm_ref)

    q_t = qa_ref[...].T
    for r in range(NSA_GROUP_HEADS):
        qm_ref[0:HEAD_DIM, r * Q_TILE:(r + 1) * Q_TILE] = q_t[r * HEAD_DIM:(r + 1) * HEAD_DIM].astype(BF16)
    ga_t = ga_ref[...].T

    def gate(branch):
        return jnp.concatenate([ga_t[r * 3 + branch:r * 3 + branch + 1] for r in range(NSA_GROUP_HEADS)], axis=1)

    q0 = qi * Q_TILE
    qpos_c = q0 + (lax.broadcasted_iota(jnp.int32, (1, N_COLS), 1) & (Q_TILE - 1))

    def compressed_and_select(n):
        kc = jnp.concatenate([kc_ref[0, 0, rr * nj:rr * nj + n, :] for rr in range(4)], axis=0)
        vct = jnp.concatenate([kvct_ref[0, 0, HEAD_DIM:, rr * nj:rr * nj + n] for rr in range(4)], axis=1)
        s = jnp.dot(kc, qm_ref[...], preferred_element_type=F32)
        row = lax.broadcasted_iota(jnp.int32, (4 * n, N_COLS), 0)
        cmp_end = (row % n) * SEL_BLOCK + (row // n) * CMP_STRIDE + (CMP_BLOCK - 1)
        vis = cmp_end <= qpos_c
        s = jnp.where(vis, s, NEG_INF)
        m = jnp.max(s, axis=0, keepdims=True)
        p = jnp.where(vis, jnp.exp(s - m), 0.0)
        l = jnp.sum(p, axis=0, keepdims=True)
        pn = p / jnp.where(l > 0.0, l, 1.0)
        ocmp_ref[...] = jnp.dot(vct, pn.astype(BF16), preferred_element_type=F32)
        p3 = pn[3 * n:4 * n]
        jrow4 = lax.broadcasted_iota(jnp.int32, (n, N_COLS), 0)
        prev = jnp.where(jrow4 == 0, 0.0, pltpu.roll(p3, 1, 0))
        a = pn[0:n] + pn[n:2 * n] + pn[2 * n:3 * n] + p3 + prev
        imp = a[:, 0:Q_TILE]
        for r in range(1, NSA_GROUP_HEADS):
            imp = imp + a[:, r * Q_TILE:(r + 1) * Q_TILE]
        bias16 = _select_blocks(imp, q0).astype(BF16)
        for r in range(NSA_GROUP_HEADS):
            bias_ref[0:n, r * Q_TILE:(r + 1) * Q_TILE] = bias16

    sizes = [n for n in CMP_ROW_CHOICES if n < nj] + [nj]
    need = (q0 + Q_TILE) // SEL_BLOCK
    for v, n in enumerate(sizes):
        lo = sizes[v - 1] if v else 0
        pl.when((need > lo) & ((need <= n) | (v == len(sizes) - 1)))(functools.partial(compressed_and_select, n))
    o_cmp = ocmp_ref[...]


    init =(jnp.full((1, N_COLS), -jnp.inf, F32), jnp.zeros((V_ROWS, N_COLS), F32))
    q_rows = qm_ref[0:HEAD_DIM, :]
    pad_rows = jnp.zeros((LANES - HEAD_DIM - BIAS_BLOCKS, N_COLS), BF16)

    def slc_scores(st):
        b0 = pl.multiple_of((st * KEY_TILE // (SEL_BLOCK * BIAS_BLOCKS)) * BIAS_BLOCKS, BIAS_BLOCKS)
        qm = jnp.concatenate([q_rows, bias_ref[pl.ds(b0, BIAS_BLOCKS), :], pad_rows], axis=0)
        k0 = pl.multiple_of(st * KEY_TILE, KEY_TILE)
        return jnp.dot(kslc_ref[0, 0, pl.ds(k0, KEY_TILE), :], qm, preferred_element_type=F32)

    def slc_absorb(st, carry, sc):
        k0 = pl.multiple_of(st * KEY_TILE, KEY_TILE)
        return _online_update(carry, sc, vslc_ref[0, 0, :, pl.ds(k0, KEY_TILE)])

    def slc_group(i, carry):
        st0 = i * SLC_UNROLL
        sc = slc_scores(st0)
        for u in range(SLC_UNROLL):
            sc_next = slc_scores(st0 + u + 1) if u + 1 < SLC_UNROLL else None
            carry = slc_absorb(st0 + u, carry, sc)
            sc = sc_next
        return carry

    n_tiles = q0 // KEY_TILE + 1
    tail = (n_tiles - 1) % SLC_UNROLL + 1
    m_s, acc_s = lax.fori_loop(0, (n_tiles - tail) // SLC_UNROLL, slc_group, init)
    mtail_ref[...] = m_s
    acctail_ref[...] = acc_s

    def slc_tail(size):
        st0 = n_tiles - size
        carry = (mtail_ref[...], acctail_ref[...])
        sc = slc_scores(st0)
        for u in range(size):
            sc_next = slc_scores(st0 + u + 1) if u + 1 < size else None
            if u == size - 1:
                kpos = (st0 + u) * KEY_TILE + lax.broadcasted_iota(jnp.int32, (KEY_TILE, N_COLS), 0)
                sc = jnp.where(kpos <= qpos_c, sc, NEG_INF)
            carry = slc_absorb(st0 + u, carry, sc)
            sc = sc_next
        acctail_ref[...] = carry[1]

    for size in range(1, SLC_UNROLL + 1):
        pl.when(tail == size)(functools.partial(slc_tail, size))
    o_slc = _softmax_out(acctail_ref[...])

    kw0 = pl.multiple_of(jnp.maximum(q0 - WINDOW, 0), Q_TILE)
    sc = jnp.dot(kwin_ref[0, 0, pl.ds(kw0, WIN_KEYS), :], qm_ref[...], preferred_element_type=F32)
    dpos = qpos_c - (kw0 + lax.broadcasted_iota(jnp.int32, (WIN_KEYS, N_COLS), 0))
    sc = jnp.where(dpos >= 0, jnp.where(dpos < WINDOW, sc, NEG_INF), NEG_INF)
    p = jnp.exp(sc - jnp.max(sc, axis=0, keepdims=True))
    o_win = _softmax_out(jnp.dot(vwin_ref[0, 0, :, pl.ds(kw0, WIN_KEYS)], p.astype(BF16), preferred_element_type=F32))

    o = gate(0) * o_cmp + gate(1) * o_slc + gate(2) * o_win
    o_rows = jnp.concatenate([o[:, r * Q_TILE:(r + 1) * Q_TILE] for r in range(NSA_GROUP_HEADS)], axis=0)
    o_ref[...] = o_rows.T.astype(BF16)


def _nsa_prompt(qa, ga, kc, kvct, kslc, vslc, kwin, vwin):
    nb, g, t_len, _ = kslc.shape
    nq = t_len // Q_TILE
    nj = t_len // SEL_BLOCK
    n_cmp_rows = kc.shape[2]
    assert nj % BIAS_BLOCKS == 0 and n_cmp_rows == 4 * nj and t_len % KEY_TILE == 0 and t_len >= WIN_KEYS
    tile = lambda b, gg, i: (b * nq + i, gg)
    whole = lambda b, gg, i: (b, gg, 0, 0)
    return pl.pallas_call(
        functools.partial(_nsa_prompt_kernel, nj=nj),
        grid=(nb, g, nq),
        in_specs=[
            pl.BlockSpec((Q_TILE, NSA_GROUP_HEADS * HEAD_DIM), tile),
            pl.BlockSpec((Q_TILE, LANES), tile),
            pl.BlockSpec((1, 1, n_cmp_rows, LANES), whole),
            pl.BlockSpec((1, 1, LANES, n_cmp_rows), whole),
            pl.BlockSpec((1, 1, t_len, LANES), whole),
            pl.BlockSpec((1, 1, V_ROWS, t_len), whole),
            pl.BlockSpec((1, 1, t_len, LANES), whole),
            pl.BlockSpec((1, 1, V_ROWS, t_len), whole),
        ],
        out_specs=pl.BlockSpec((Q_TILE, NSA_GROUP_HEADS * HEAD_DIM), tile),
        out_shape=jax.ShapeDtypeStruct((nb * t_len, NSA_HEADS * HEAD_DIM), BF16),
        scratch_shapes=[pltpu.VMEM((LANES, N_COLS), BF16), pltpu.VMEM((nj, N_COLS), BF16),
                        pltpu.VMEM((HEAD_DIM, N_COLS), F32), pltpu.VMEM((1, N_COLS), F32),
                        pltpu.VMEM((V_ROWS, N_COLS), F32)],
        compiler_params=_cparams(("parallel", "parallel", "arbitrary")),
        name="nsa_prompt",
    )(qa, ga, kc, kvct, kslc, vslc, kwin, vwin)


HEAD_ROWS = 8
N_PAST_SEL = N_SEL - 1


def _softmax_lanes(s_list, mask_list):
    s_list = [jnp.where(mk, s, NEG_INF) for s, mk in zip(s_list, mask_list)]
    m = s_list[0].max(axis=-1, keepdims=True)
    for s in s_list[1:]:
        m = jnp.maximum(m, s.max(axis=-1, keepdims=True))
    p_list = [jnp.where(mk, jnp.exp(s - m), 0.0) for s, mk in zip(s_list, mask_list)]
    l = p_list[0].sum(axis=-1, keepdims=True)
    for p in p_list[1:]:
        l = l + p.sum(axis=-1, keepdims=True)
    inv = 1.0 / jnp.where(l > 0.0, l, 1.0)
    return [p * inv for p in p_list]


def _dot_nt(a, b):
    return lax.dot_general(a, b, (((1,), (1,)), ((), ())), preferred_element_type=F32)


def _dec_compressed(q, kvct_ref, nj):
    n = 4 * nj
    s = jnp.dot(q, kvct_ref[:HEAD_DIM, :], preferred_element_type=F32)
    lane = lax.broadcasted_iota(jnp.int32, (HEAD_ROWS, n), 1)
    blk = 4 * (lane % nj) + lane // nj
    (pn,) = _softmax_lanes([s], [blk <= n - 2])
    out = _dot_nt(pn.astype(BF16), kvct_ref[HEAD_DIM:, :])
    row = lax.broadcasted_iota(jnp.int32, (HEAD_ROWS, n), 0)
    ph = jnp.sum(jnp.where(row < NSA_GROUP_HEADS, pn, 0.0), axis=0, keepdims=True)
    p3 = ph[:, 3 * nj:]
    prev = jnp.where(lane[0:1, :nj] == 0, 0.0, pltpu.roll(p3, 1, 1))
    imp = ph[:, :nj] + ph[:, nj:2 * nj] + ph[:, 2 * nj:3 * nj] + p3 + prev
    return out, jnp.broadcast_to(imp, (HEAD_ROWS, nj))


def _dec_topk_kernel(imp_ref, idx_ref):
    work = imp_ref[...].T
    nj, n = work.shape
    jrow = lax.broadcasted_iota(jnp.int32, (nj, n), 0)
    jf = jrow.astype(F32)
    for forced_blk in (0, nj - 1):
        work = jnp.where(jrow == forced_blk, FORCE_SCORE, work)
    picks = []
    for _ in range(N_PAST_SEL):
        mx = jnp.max(work, axis=0, keepdims=True)
        first = jnp.min(jnp.where(work == mx, jf, 1.0e9), axis=0, keepdims=True)
        picks.append(first)
        work = jnp.where(jf == first, -3.0e38, work)
    picks.append(jnp.full((1, n), float(nj), F32))
    idx_ref[...] = jnp.concatenate(picks, axis=0)


def _dec_topk(imp2):
    n, nj = imp2.shape
    return pl.pallas_call(
        _dec_topk_kernel,
        out_shape=jax.ShapeDtypeStruct((N_SEL, n), F32),
        compiler_params=pltpu.CompilerParams(vmem_limit_bytes=VMEM_LIMIT),
        name="dec_topk",
    )(imp2)


def _dec_attn_kernel(phys_ref, half_ref, q_ref, gate_ref, ocmp_ref, *refs):
    page_refs = refs[:N_PAST_SEL]
    slcnew_ref, winnew_ref, win_ref, o_ref, winout_ref = refs[N_PAST_SEL:]
    b = pl.program_id(0)
    base = (b * pl.num_programs(1) + pl.program_id(1)) * N_PAST_SEL
    q = q_ref[0, 0]
    lane = lax.broadcasted_iota(jnp.int32, (HEAD_ROWS, LANES), 1)
    own = lane == b

    def attend(kv_blocks, masks):
        scores = [jnp.dot(q, kv[:HEAD_DIM].astype(BF16), preferred_element_type=F32) for kv in kv_blocks]
        probs = _softmax_lanes(scores, masks)
        out = jnp.zeros((HEAD_ROWS, HEAD_DIM), F32)
        for p, kv in zip(probs, kv_blocks):
            out = out + _dot_nt(p.astype(BF16), kv[HEAD_DIM:].astype(BF16))
        return out

    pages = [r[0, 0] for r in page_refs] + [slcnew_ref[0, 0]]
    masks = [(lane // SEL_BLOCK) == half_ref[base + k] for k in range(N_PAST_SEL)] + [own]
    o_slc = attend(pages, masks)
    st = win_ref[0, 0]
    win_c = st.shape[1]
    wlane = lax.broadcasted_iota(jnp.int32, (HEAD_ROWS, win_c), 1)
    o_win = attend([st, winnew_ref[0, 0]], [wlane > win_c - WINDOW, own])
    gates = gate_ref[0, 0]
    o_ref[0, 0] = gates[:, 0:1] * ocmp_ref[0, 0] + gates[:, 1:2] * o_slc + gates[:, 2:3] * o_win
    new = winnew_ref[0, 0]
    lane_full = lax.broadcasted_iota(jnp.int32, new.shape, 1)
    new_col = jnp.sum(jnp.where(lane_full == b, new, 0.0), axis=1, keepdims=True)
    col = lax.broadcasted_iota(jnp.int32, st.shape, 1)
    winout_ref[0, 0] = jnp.where(col == win_c - 1, new_col, pltpu.roll(st, win_c - 1, 1))


def _dec_attn(phys, half, q8, gates, o_cmp, slc_t, slc_new, win_new, win_t):
    nb, g, _, win_c = win_t.shape
    assert slc_new.shape[-1] == nb == LANES
    blk = lambda *shape: pl.BlockSpec((1, 1) + shape, lambda b, gg, ph, hf: (b, gg, 0, 0))
    page = lambda k: pl.BlockSpec((1, 1, LANES, PAGE_SIZE),
                                  lambda b, gg, ph, hf: (ph[(b * g + gg) * N_PAST_SEL + k], gg, 0, 0))
    new = pl.BlockSpec((1, 1, LANES, nb), lambda b, gg, ph, hf: (0, gg, 0, 0))
    return pl.pallas_call(
        _dec_attn_kernel,
        grid_spec=pltpu.PrefetchScalarGridSpec(
            num_scalar_prefetch=2,
            grid=(nb, g),
            in_specs=[blk(HEAD_ROWS, HEAD_DIM), blk(HEAD_ROWS, LANES), blk(HEAD_ROWS, HEAD_DIM)]
            + [page(k) for k in range(N_PAST_SEL)] + [new, new, blk(LANES, win_c)],
            out_specs=(blk(HEAD_ROWS, HEAD_DIM), blk(LANES, win_c)),
        ),
        out_shape=(jax.ShapeDtypeStruct((nb, g, HEAD_ROWS, HEAD_DIM), F32),
                   jax.ShapeDtypeStruct((nb, g, LANES, win_c), F32)),
        compiler_params=_cparams(("parallel", "parallel")),
        name="dec_attn",
    )(phys, half, q8, gates, o_cmp, *([slc_t] * N_PAST_SEL), slc_new, win_new, win_t)


RET_STEP_CHUNKS = 8


def _log_gamma():
    return jnp.log1p(-jnp.exp2(-5.0 - jnp.arange(RET_HEADS, dtype=F32)))


def _ret_tables(chunk):
    lg = _log_gamma()
    n = jnp.arange(chunk, dtype=F32)
    diff = n[:, None] - n[None, :]
    decay = jnp.exp(jnp.maximum(diff, 0.0)[None] * lg[:, None, None]) * (diff >= 0)[None]
    rep = lambda col: jnp.broadcast_to(col[:, :, None], (RET_HEADS, chunk, RET_DV))
    qdec = rep(jnp.exp((n + 1.0)[None, :] * lg[:, None]))
    kdec = rep(jnp.exp((chunk - 1.0 - n)[None, :] * lg[:, None]))
    sdec = jnp.broadcast_to(jnp.exp(chunk * lg)[:, None, None], (RET_HEADS, RET_DK, RET_DV))
    return decay.astype(F32), qdec, kdec, sdec


def _group_norm_gate(o, g, gn_g, gn_b):
    mu = jnp.mean(o, axis=-1, keepdims=True)
    var = jnp.mean(jnp.square(o - mu), axis=-1, keepdims=True)
    on = (o - mu) * lax.rsqrt(var + LN_EPS) * gn_g + gn_b
    return (jax.nn.silu(g) * on).astype(BF16)


def _ret_prompt_kernel(qb_ref, kb_ref, vb_ref, gb_ref, decay_ref, qdec_ref, kdec_ref, sdec_ref, gng_ref, gnb_ref,
                       bm_ref, state_ref, st_ref):
    @pl.when(pl.program_id(1) == 0)
    def _():
        st_ref[...] = jnp.zeros_like(st_ref)

    def chunk_step(c, _):
        rows = pl.ds(pl.multiple_of(c * RET_CHUNK, RET_CHUNK), RET_CHUNK)
        for h in range(RET_HEADS):
            cols = slice(h * RET_DK, (h + 1) * RET_DK)
            q = qb_ref[rows, cols]
            k = kb_ref[rows, cols]
            v = vb_ref[rows, cols]
            st = st_ref[h]
            inner = lax.dot_general(q, k.astype(BF16), (((1,), (1,)), ((), ())), preferred_element_type=F32)
            inner = inner * decay_ref[h]
            o = jnp.dot(inner.astype(BF16), v, preferred_element_type=F32)
            o = o + jnp.dot(q, st.astype(BF16), preferred_element_type=F32) * qdec_ref[h]
            kd = (k * kdec_ref[h]).astype(BF16)
            st_ref[h] = sdec_ref[h] * st + lax.dot_general(kd, v, (((0,), (0,)), ((), ())), preferred_element_type=F32)
            bm_ref[rows, cols] = _group_norm_gate(o, gb_ref[rows, cols], gng_ref[:, cols], gnb_ref[:, cols])
        return 0

    lax.fori_loop(0, RET_STEP_CHUNKS, chunk_step, 0)
    state_ref[0] = st_ref[...]


def _ret_prompt(qb, kb, vb, gb, gn_g, gn_b, nb):
    n = qb.shape[0]
    tt = RET_STEP_CHUNKS * RET_CHUNK
    nt = n // nb // tt
    decay, qdec, kdec, sdec = _ret_tables(RET_CHUNK)
    row = pl.BlockSpec((tt, RET_HEADS * RET_DK), lambda b, i: (b * nt + i, 0))
    tab = pl.BlockSpec((RET_HEADS, RET_CHUNK, RET_DV), lambda b, i: (0, 0, 0))
    vec = pl.BlockSpec((1, RET_HEADS * RET_DV), lambda b, i: (0, 0))
    return pl.pallas_call(
        _ret_prompt_kernel,
        grid=(nb, nt),
        in_specs=[row, row, row, row, tab, tab, tab, tab, vec, vec],
        out_specs=(row, pl.BlockSpec((1, RET_HEADS, RET_DK, RET_DV), lambda b, i: (b, 0, 0, 0))),
        out_shape=(jax.ShapeDtypeStruct((n, RET_HEADS * RET_DV), BF16),
                   jax.ShapeDtypeStruct((nb, RET_HEADS, RET_DK, RET_DV), F32)),
        scratch_shapes=[pltpu.VMEM((RET_HEADS, RET_DK, RET_DV), F32)],
        compiler_params=_cparams(("parallel", "arbitrary")),
        name="ret_prompt",
    )(qb, kb, vb, gb, decay, qdec, kdec, sdec, gn_g.reshape(1, -1), gn_b.reshape(1, -1))


def _ret_step_kernel(q_ref, k_ref, v_ref, g_ref, st_ref, qdec_ref, sdec_ref, gng_ref, gnb_ref, bm_ref, stout_ref):
    for s in range(q_ref.shape[0]):
        for h in range(RET_HEADS):
            cols = slice(h * RET_DK, (h + 1) * RET_DK)
            q = q_ref[s, :, cols]
            k = k_ref[s, :, cols].astype(BF16)
            v = v_ref[s, :, cols]
            st = st_ref[s, h]
            qk = jnp.sum(q.astype(F32) * k.astype(F32), axis=-1, keepdims=True)
            o = qk.astype(BF16).astype(F32) * v.astype(F32)
            o = o + jnp.dot(q, st.astype(BF16), preferred_element_type=F32) * qdec_ref[h]
            stout_ref[s, h] = sdec_ref[h] * st + lax.dot_general(k, v, (((0,), (0,)), ((), ())),
                                                                 preferred_element_type=F32)
            bm_ref[s, :, cols] = _group_norm_gate(o, g_ref[s, :, cols], gng_ref[:, cols], gnb_ref[:, cols])


RET_STEP_SEQS = 4


def _ret_step(q8, k8, v8, g8, state, gn_g, gn_b):
    nb = state.shape[0]
    ns = RET_STEP_SEQS if nb % RET_STEP_SEQS == 0 else 1
    _, qdec, _, sdec = _ret_tables(1)
    tok = pl.BlockSpec((ns, HEAD_ROWS, RET_HEADS * RET_DK), lambda b: (b, 0, 0))
    st = pl.BlockSpec((ns, RET_HEADS, RET_DK, RET_DV), lambda b: (b, 0, 0, 0))
    vec = pl.BlockSpec((1, RET_HEADS * RET_DV), lambda b: (0, 0))
    return pl.pallas_call(
        _ret_step_kernel,
        grid=(nb // ns,),
        in_specs=[tok, tok, tok, tok, st,
                  pl.BlockSpec((RET_HEADS, 1, RET_DV), lambda b: (0, 0, 0)),
                  pl.BlockSpec((RET_HEADS, RET_DK, RET_DV), lambda b: (0, 0, 0)), vec, vec],
        out_specs=(tok, st),
        out_shape=(jax.ShapeDtypeStruct((nb, HEAD_ROWS, RET_HEADS * RET_DV), BF16),
                   jax.ShapeDtypeStruct(state.shape, F32)),
        compiler_params=_cparams(("parallel",)),
        name="ret_step",
    )(q8, k8, v8, g8, state, qdec, sdec, gn_g.reshape(1, -1), gn_b.reshape(1, -1))


ROUTE_E0 = N_GROUPS
MOE_ROWS = 256
SAMPLE_MOE_ROWS = 32


def _layer_norm(v, g, b):
    mu = jnp.mean(v, axis=-1, keepdims=True)
    var = jnp.mean(jnp.square(v - mu), axis=-1, keepdims=True)
    return (v - mu) * lax.rsqrt(var + LN_EPS) * g + b


def _first_lane(hit, lane):
    return jnp.min(jnp.where(hit, lane, 4 * LANES), axis=-1, keepdims=True)


def _merge_kernel(on_ref, bm_ref, gm_ref, x_ref, wn_ref, wr_ref, wo_ref, g1_ref, b1_ref, wrt_ref, brt_ref, tri_ref,
                  x1_ref, route_ref, counts_ref, carry_ref):
    @pl.when(pl.program_id(0) == 0)
    def _():
        carry_ref[...] = jnp.zeros_like(carry_ref)

    gate_a = jax.nn.sigmoid(gm_ref[:, :D_MODEL])
    gate_b = jax.nn.sigmoid(gm_ref[:, D_MODEL:])
    m = gate_a * jnp.dot(on_ref[...], wn_ref[...], preferred_element_type=F32)
    m = m + gate_b * jnp.dot(bm_ref[...], wr_ref[...], preferred_element_type=F32)
    mix = jnp.dot(m.astype(BF16), wo_ref[...], preferred_element_type=F32)
    x1 = _layer_norm(DEEPNORM_ALPHA * x_ref[...] + mix, g1_ref[...], b1_ref[...])
    x1_ref[...] = x1

    x_hi = x1.astype(BF16)
    x_lo = (x1 - x_hi.astype(F32)).astype(BF16)
    hi_part = jnp.dot(x_hi, wrt_ref[...], preferred_element_type=F32)
    lg = hi_part[:, :LANES] + hi_part[:, LANES:] + brt_ref[...]
    lg = lg + jnp.dot(x_lo, wrt_ref[:, :LANES], preferred_element_type=F32)
    lane = lax.broadcasted_iota(jnp.int32, lg.shape, 1)
    gl = jnp.where(lane < N_GROUPS, lg, -jnp.inf)
    ge = jnp.exp(gl - jnp.max(gl, axis=-1, keepdims=True))
    pgrp = ge / jnp.sum(ge, axis=-1, keepdims=True)
    pg = jnp.max(pgrp, axis=-1, keepdims=True)
    grp = _first_lane(pgrp == pg, lane)
    lo = ROUTE_E0 + grp * EXPERTS_PER_GROUP
    in_g = (lane >= lo) & (lane < lo + EXPERTS_PER_GROUP)
    el = jnp.where(in_g, lg, -jnp.inf)
    ee = jnp.exp(el - jnp.max(el, axis=-1, keepdims=True))
    pc = jnp.where(in_g, ee / jnp.sum(ee, axis=-1, keepdims=True), -1.0)
    w1 = jnp.max(pc, axis=-1, keepdims=True)
    i1 = _first_lane(pc == w1, lane)
    pc2 = jnp.where(lane == i1, -1.0, pc)
    w2 = jnp.max(pc2, axis=-1, keepdims=True)
    i2 = _first_lane(pc2 == w2, lane)
    wsum = w1 + w2
    wt1 = pg * w1 / wsum
    wt2 = pg * w2 / wsum
    hit1 = lane == i1
    hit2 = lane == i2
    cnt = jnp.where(hit1, 1.0, 0.0) + jnp.where(hit2, 1.0, 0.0)
    base = carry_ref[0:1, :] + jnp.dot(tri_ref[...], cnt.astype(BF16), preferred_element_type=F32)
    rank1 = jnp.sum(jnp.where(hit1, base, 0.0), axis=-1, keepdims=True)
    rank2 = jnp.sum(jnp.where(hit2, base, 0.0), axis=-1, keepdims=True)
    carry = carry_ref[0:1, :] + jnp.sum(cnt, axis=0, keepdims=True)
    carry_ref[...] = jnp.broadcast_to(carry, carry_ref.shape)
    counts_ref[...] = jnp.broadcast_to(carry, counts_ref.shape)
    fields = ((i1 - ROUTE_E0).astype(F32), (i2 - ROUTE_E0).astype(F32), wt1, wt2, rank1, rank2)
    out = jnp.zeros(lg.shape, F32)
    for idx, val in enumerate(fields):
        out = jnp.where(lane == idx, val, out)
    route_ref[...] = out


def _merge_weights(w_nsa_proj, w_ret_proj, w_o, w_route_group, b_route_group, w_route_exp, b_route_exp):
    pad = LANES - N_GROUPS - N_EXPERTS
    wrt = jnp.concatenate([w_route_group, w_route_exp, jnp.zeros((D_MODEL, pad), F32)], axis=1)
    w_hi = wrt.astype(BF16)
    w_lo = (wrt - w_hi.astype(F32)).astype(BF16)
    wrt = jnp.concatenate([w_hi, w_lo], axis=1)
    brt = jnp.concatenate([b_route_group, b_route_exp, jnp.zeros((pad,), F32)])[None, :]
    return w_nsa_proj.astype(BF16), w_ret_proj.astype(BF16), w_o.astype(BF16), wrt, brt


def _merge(o_nsa, bm, gm, x2, mw, ln_g, ln_b, tm):
    wn, wr, wo, wrt, brt = mw
    n = x2.shape[0]
    tri = (jnp.arange(tm)[:, None] > jnp.arange(tm)[None, :]).astype(BF16)
    row = lambda width: pl.BlockSpec((tm, width), lambda i: (i, 0))
    const = lambda a: pl.BlockSpec(a.shape, lambda i: (0, 0))
    ln_g = ln_g.reshape(1, -1)
    ln_b = ln_b.reshape(1, -1)
    return pl.pallas_call(
        _merge_kernel,
        grid=(n // tm,),
        in_specs=[row(512), row(512), row(2 * D_MODEL), row(D_MODEL), const(wn), const(wr), const(wo),
                  const(ln_g), const(ln_b), const(wrt), const(brt), const(tri)],
        out_specs=(row(D_MODEL), row(LANES), pl.BlockSpec((8, LANES), lambda i: (0, 0))),
        out_shape=(jax.ShapeDtypeStruct((n, D_MODEL), F32), jax.ShapeDtypeStruct((n, LANES), F32),
                   jax.ShapeDtypeStruct((8, LANES), F32)),
        scratch_shapes=[pltpu.VMEM((8, LANES), F32)],
        compiler_params=_cparams(("arbitrary",)),
        name="merge_route",
    )(o_nsa, bm, gm, x2, wn, wr, wo, ln_g, ln_b, wrt, brt, tri)


SC_CHUNK_ROWS = 32


def _gather_rows(table, idx):
    info = plsc.get_sparse_core_info()
    n_cores = info.num_cores
    n_workers = n_cores * info.num_subcores
    n_idx, width = idx.shape[0], table.shape[1]
    per_worker = n_idx // n_workers
    chunk = max(c for c in range(8, SC_CHUNK_ROWS + 1, 8) if per_worker % c == 0)
    assert n_idx % (8 * n_workers) == 0
    mesh = plsc.VectorSubcoreMesh(core_axis_name="c", subcore_axis_name="s")

    def body(table_hbm, idx_hbm, out_hbm, idx_v, rows_v, sem):
        worker = lax.axis_index("s") * n_cores + lax.axis_index("c")
        base = worker * per_worker

        @pl.loop(0, per_worker // chunk)
        def _(c):
            off = pl.multiple_of(base + c * chunk, 8)
            pltpu.sync_copy(idx_hbm.at[pl.ds(off, chunk)], idx_v)
            pltpu.async_copy(table_hbm.at[idx_v], rows_v, sem).wait()
            pltpu.sync_copy(rows_v, out_hbm.at[pl.ds(off, chunk)])

    return pl.kernel(
        body, mesh=mesh,
        out_type=jax.ShapeDtypeStruct((n_idx, width), table.dtype),
        scratch_types=[pltpu.VMEM((chunk,), jnp.int32), pltpu.VMEM((chunk, width), table.dtype),
                       pltpu.SemaphoreType.DMA],
    )(table, idx)


def _expert_kernel(be_ref, nused_ref, x_ref, w1_ref, w3_ref, w2_ref, y_ref):
    @pl.when(pl.program_id(0) >= nused_ref[0])
    def _():
        y_ref[...] = jnp.zeros_like(y_ref)

    @pl.when(pl.program_id(0) < nused_ref[0])
    def _():
        x = x_ref[...].astype(BF16)
        h = jax.nn.silu(jnp.dot(x, w1_ref[0].astype(BF16), preferred_element_type=F32))
        h = h * jnp.dot(x, w3_ref[0].astype(BF16), preferred_element_type=F32)
        y_ref[...] = jnp.dot(h.astype(BF16), w2_ref[0].astype(BF16), preferred_element_type=F32)


def _experts(x_sorted, block_e, n_used, w1, w3, w2, rows):
    n_blocks = block_e.shape[0]
    xspec = pl.BlockSpec((rows, D_MODEL), lambda i, be, nu: (i, 0))
    wspec = lambda a: pl.BlockSpec((1,) + a.shape[1:], lambda i, be, nu: (be[i], 0, 0))
    return pl.pallas_call(
        _expert_kernel,
        grid_spec=pltpu.PrefetchScalarGridSpec(
            num_scalar_prefetch=2,
            grid=(n_blocks,),
            in_specs=[xspec, wspec(w1), wspec(w3), wspec(w2)],
            out_specs=xspec,
        ),
        out_shape=jax.ShapeDtypeStruct(x_sorted.shape, F32),
        compiler_params=_cparams(("arbitrary",)),
        name="experts",
    )(block_e, n_used, x_sorted, w1, w3, w2)


def _combine_kernel(x1_ref, y0_ref, y1_ref, route_ref, g_ref, b_ref, o_ref):
    route = route_ref[...]
    ff = y0_ref[...] * route[:, 2:3] + y1_ref[...] * route[:, 3:4]
    o_ref[...] = _layer_norm(DEEPNORM_ALPHA * x1_ref[...] + ff, g_ref[...], b_ref[...])


def _combine(x1, y2, route, ln_g, ln_b, tm):
    n = x1.shape[0]
    row = lambda width: pl.BlockSpec((tm, width), lambda i: (i, 0))
    second = pl.BlockSpec((tm, D_MODEL), lambda i: (i + n // tm, 0))
    vec = pl.BlockSpec((1, D_MODEL), lambda i: (0, 0))
    return pl.pallas_call(
        _combine_kernel,
        grid=(n // tm,),
        in_specs=[row(D_MODEL), row(D_MODEL), second, row(LANES), vec, vec],
        out_specs=row(D_MODEL),
        out_shape=jax.ShapeDtypeStruct((n, D_MODEL), F32),
        compiler_params=_cparams(("parallel",)),
        name="combine_ln2",
    )(x1, y2, y2, route, ln_g.reshape(1, -1), ln_b.reshape(1, -1))


def _finish(o_nsa, bm, gm, x2, mw, ew, ln1, ln2, tm, rows):
    n = x2.shape[0]
    x1, route, counts = _merge(o_nsa, bm, gm, x2, mw, ln1[0], ln1[1], tm)
    sizes = counts[0, ROUTE_E0:ROUTE_E0 + N_EXPERTS].astype(jnp.int32)
    padded = (sizes + rows - 1) // rows * rows
    pad_end = jnp.cumsum(padded)
    pad_start = pad_end - padded
    e = route[:, 0:TOP_K].astype(jnp.int32)
    rank = route[:, 4:4 + TOP_K].astype(jnp.int32)
    dest = (pad_start[e] + rank).T.reshape(-1)
    n_blocks = -(-(n * TOP_K) // rows) + N_EXPERTS
    tok = jnp.tile(jnp.arange(n, dtype=jnp.int32), TOP_K)
    slot_tok = jnp.zeros((n_blocks * rows,), jnp.int32).at[dest].set(tok)
    blk0 = jnp.arange(n_blocks, dtype=jnp.int32) * rows
    block_e = jnp.sum(blk0[:, None] >= pad_end[None, :], axis=1).astype(jnp.int32)
    block_e = jnp.minimum(block_e, N_EXPERTS - 1)
    n_used = (pad_end[-1:] // rows).astype(jnp.int32)
    y_sorted = _experts(_gather_rows(x1, slot_tok), block_e, n_used, *ew, rows)
    return _combine(x1, _gather_rows(y_sorted, dest), route, ln2[0], ln2[1], tm)


PROMPT_ROWS = 256


def _feature_major(a):
    lead = a.shape[:-3]
    return jnp.moveaxis(a, -3, -1).reshape(lead + (2 * HEAD_DIM, a.shape[-3]))


def _row_major(a):
    lead = a.shape[:-2]
    return jnp.moveaxis(a.reshape(lead + (2, HEAD_DIM, a.shape[-1])), -1, -3)


def _pad_rows(a):
    return jnp.pad(a[:, None, :], ((0, 0), (0, HEAD_ROWS - 1), (0, 0)))


def kernel(x_prompt, x_sample, cache_kv_cmp, cache_kv_slc, state_kv_win, state_ret, page_table, w_in, b_in, cmp_pos, w_cmp1, w_cmp2, b_cmp2, ret_gn_g, ret_gn_b, w_nsa_proj, w_ret_proj, w_o, ln1_g, ln1_b, w_route_group, b_route_group, w_route_exp, b_route_exp, w_e1, w_e3, w_e2, ln2_g, ln2_b):
    nb, t_len, _ = x_prompt.shape
    ns, s_len, _ = x_sample.shape
    assert s_len == 1 and ns == LANES and t_len % (RET_STEP_CHUNKS * RET_CHUNK) == 0 and t_len >= WINDOW
    g = NSA_KV_GROUPS
    pw = _proj_weights(w_in, b_in)
    cw = _compress_weights(cmp_pos, w_cmp1, w_cmp2, b_cmp2)
    mw = _merge_weights(w_nsa_proj, w_ret_proj, w_o, w_route_group, b_route_group, w_route_exp, b_route_exp)
    ew = (w_e1, w_e3, w_e2)
    ln1 = (ln1_g, ln1_b)
    ln2 = (ln2_g, ln2_b)

    cos_p, sin_p = _rope_tables(jnp.arange(t_len, dtype=jnp.int32))
    x2 = x_prompt.reshape(nb * t_len, D_MODEL)
    qa, kvt_cmp, kvt_slc, kvt_win, ga, qb, kb, vb, gb, gm = _project(x2, nb, pw, cos_p, sin_p, PROMPT_ROWS)
    kc, kvct = _compress(kvt_cmp, cw)
    kslc, kwin, vslc, vwin = _kv_prep(kvt_slc, kvt_win)
    o_nsa = _nsa_prompt(qa, ga, kc, kvct, kslc, vslc, kwin, vwin)
    bm, ret_p = _ret_prompt(qb, kb, vb, gb, ret_gn_g, ret_gn_b, nb)
    y_prompt = _finish(o_nsa, bm, gm, x2, mw, ew, ln1, ln2, PROMPT_ROWS, MOE_ROWS).reshape(x_prompt.shape)
    kv_cmp_p = _row_major(kvt_cmp)
    kv_slc_p = _row_major(kvt_slc)
    win_p = _row_major(kvt_win[..., t_len - WINDOW:])

    n_pages = page_table.shape[1]
    past_len = n_pages * PAGE_SIZE
    cos_s, sin_s = _rope_tables(jnp.full((ns,), past_len, jnp.int32))
    xs2 = x_sample.reshape(ns, D_MODEL)
    qa, new_cmp, new_slc, new_win, ga, qb, kb, vb, gb, gm = _project(xs2, 1, pw, cos_s, sin_s, ns)
    q8 = jnp.pad(qa.reshape(ns, g, NSA_GROUP_HEADS, HEAD_DIM),
                 ((0, 0), (0, 0), (0, HEAD_ROWS - NSA_GROUP_HEADS), (0, 0))).astype(BF16)
    o_cmp, imp = _compress_paged_attend(_feature_major(cache_kv_cmp), page_table, cw, q8)
    picks = _dec_topk(imp[:, :, 0, :].reshape(ns * g, -1))
    idx = picks[:N_PAST_SEL].T.astype(jnp.int32)
    blocks_per_page = PAGE_SIZE // SEL_BLOCK
    seq = jnp.arange(ns * g, dtype=jnp.int32)[:, None] // g
    phys = page_table[seq, idx // blocks_per_page]
    half = idx % blocks_per_page
    gates = ga.reshape(ns, g, LANES)[:, :, :3 * NSA_GROUP_HEADS].reshape(ns, g, NSA_GROUP_HEADS, 3)
    gates = jnp.pad(gates, ((0, 0), (0, 0), (0, HEAD_ROWS - NSA_GROUP_HEADS), (0, LANES - 3)))
    o_s, win_t = _dec_attn(phys.reshape(-1), half.reshape(-1), q8, gates, o_cmp, _feature_major(cache_kv_slc),
                           new_slc, new_win, _feature_major(state_kv_win))
    o_nsa_s = o_s[:, :, :NSA_GROUP_HEADS].reshape(ns, NSA_HEADS * HEAD_DIM).astype(BF16)
    bm8, ret_s = _ret_step(_pad_rows(qb), _pad_rows(kb), _pad_rows(vb), _pad_rows(gb), state_ret, ret_gn_g, ret_gn_b)
    y_sample = _finish(o_nsa_s, bm8[:, 0], gm, xs2, mw, ew, ln1, ln2, ns, SAMPLE_MOE_ROWS).reshape(x_sample.shape)
    kv_cmp_s = jnp.moveaxis(new_cmp[0], -1, 0).reshape(ns, g, 1, 2, HEAD_DIM)
    kv_slc_s = jnp.moveaxis(new_slc[0], -1, 0).reshape(ns, g, 1, 2, HEAD_DIM)
    win_s = _row_major(win_t)
    return (y_prompt, y_sample, kv_cmp_p, kv_slc_p, win_p, ret_p, kv_cmp_s, kv_slc_s, win_s, ret_s)
```

```python
import functools

import numpy as np
import jax
import jax.numpy as jnp
from jax import lax
from jax.experimental import pallas as pl
from jax.experimental.pallas import tpu as pltpu
from jax.experimental.pallas import tpu_sc as plsc

F32 = jnp.float32
BF16 = jnp.bfloat16

D_MODEL = 1024
DEPTH = 1
PAGE_SIZE = 128
NSA_HEADS = 8
NSA_KV_GROUPS = 2
NSA_GROUP_HEADS = NSA_HEADS // NSA_KV_GROUPS
HEAD_DIM = 64
CMP_BLOCK = 32
CMP_STRIDE = 16
CMP_HIDDEN = 64
SEL_BLOCK = 64
N_SEL = 16
WINDOW = 512
Q_TILE = 256
FORCE_SCORE = 1.0e3
NEG_INF = -1.0e30
RET_HEADS = 4
RET_DK = 128
RET_DV = 128
RET_CHUNK = 128
ROPE_BASE = 10000.0
N_GROUPS = 4
EXPERTS_PER_GROUP = 8
N_EXPERTS = N_GROUPS * EXPERTS_PER_GROUP
TOP_K = 2
EXPERT_FF = 512
LN_EPS = 1e-5
DEEPNORM_ALPHA = (2.0 * DEPTH) ** 0.25

NSA_Q_COLS = NSA_HEADS * HEAD_DIM
NSA_KV_COLS = 3 * 2 * NSA_KV_GROUPS * HEAD_DIM
NSA_GATE_COLS = 3 * NSA_HEADS
RET_QK_COLS = RET_HEADS * RET_DK
RET_V_COLS = RET_HEADS * RET_DV

LANES = 128

P_QA = 0
P_GA = P_QA + NSA_Q_COLS
P_QB = P_GA + NSA_KV_GROUPS * LANES
P_KB = P_QB + RET_QK_COLS
P_VB = P_KB + RET_QK_COLS
P_GB = P_VB + RET_V_COLS
P_GM = P_GB + RET_V_COLS
VMEM_LIMIT = 56 * 1024 * 1024


def _cparams(sem):
    return pltpu.CompilerParams(dimension_semantics=sem, vmem_limit_bytes=VMEM_LIMIT)


def _proj_kernel(x_ref, w_ref, b_ref, wkv_ref, bkv_ref, cos_ref, sin_ref,
                 qa_ref, kcmp_ref, kslc_ref, kwin_ref, ga_ref, qb_ref, kb_ref, vb_ref, gb_ref, gm_ref):
    x = x_ref[...].astype(BF16)

    def cols(c0, n):
        h = jnp.dot(x, w_ref[:, c0:c0 + n], preferred_element_type=F32)
        return h + b_ref[:, c0:c0 + n]

    qa_ref[...] = cols(P_QA, 512) * (HEAD_DIM ** -0.5)
    ga_ref[...] = jax.nn.sigmoid(cols(P_GA, NSA_KV_GROUPS * LANES))
    cos = cos_ref[...]
    sin = sin_ref[...]
    for h in range(RET_HEADS):
        q = cols(P_QB + h * RET_DK, RET_DK)
        qb_ref[:, h * RET_DK:(h + 1) * RET_DK] = (q * cos + pltpu.roll(q, RET_DK // 2, 1) * sin).astype(BF16)
        k = cols(P_KB + h * RET_DK, RET_DK)
        kb_ref[:, h * RET_DK:(h + 1) * RET_DK] = (k * cos + pltpu.roll(k, RET_DK // 2, 1) * sin) * (RET_DK ** -0.5)
    vb_ref[...] = cols(P_VB, 512).astype(BF16)
    gb_ref[...] = cols(P_GB, 512)
    gm_ref[...] = cols(P_GM, 2048)
    kvt = lax.dot_general(wkv_ref[...], x, (((1,), (1,)), ((), ())), preferred_element_type=F32)
    kvt = kvt + bkv_ref[...]
    for br, out_ref in enumerate((kcmp_ref, kslc_ref, kwin_ref)):
        for g in range(NSA_KV_GROUPS):
            i = br * NSA_KV_GROUPS + g
            out_ref[0, g] = kvt[i * LANES:(i + 1) * LANES]


def _proj_weights(w_in, b_in):
    o_kv = NSA_Q_COLS
    o_ga = o_kv + NSA_KV_COLS
    o_qb = o_ga + NSA_GATE_COLS
    per_g = NSA_GATE_COLS // NSA_KV_GROUPS
    ga_cols = []
    for g in range(NSA_KV_GROUPS):
        ga_cols += [np.arange(o_ga + g * per_g, o_ga + (g + 1) * per_g), np.full(LANES - per_g, -1)]
    perm = np.concatenate([
        np.arange(0, NSA_Q_COLS),
        *ga_cols,
        np.arange(o_qb, o_qb + 2 * RET_QK_COLS + 2 * RET_V_COLS + 2 * D_MODEL),
    ])
    keep = jnp.asarray(perm >= 0)
    idx = jnp.asarray(np.maximum(perm, 0))
    w = jnp.where(keep[None, :], w_in[:, idx], 0.0).astype(BF16)
    b = jnp.where(keep, b_in[idx], 0.0)[None, :]
    br, g, kv, d = np.meshgrid(np.arange(3), np.arange(NSA_KV_GROUPS), np.arange(2), np.arange(HEAD_DIM), indexing="ij")
    kv_idx = jnp.asarray((o_kv + br * 256 + kv * 128 + g * 64 + d).reshape(-1))
    wkv = w_in[:, kv_idx].T.astype(BF16)
    bkv = b_in[kv_idx][:, None]
    return w, b, wkv, bkv


def _rope_tables(pos):
    half = RET_DK // 2
    inv = ROPE_BASE ** (-jnp.arange(half, dtype=F32) / half)
    ang = pos.astype(F32)[:, None] * inv[None, :]
    cos, sin = jnp.cos(ang), jnp.sin(ang)
    return jnp.concatenate([cos, cos], -1), jnp.concatenate([-sin, sin], -1)


def _project(x2, n_batch, pw, cos_t, sin_t, tm):
    w, b, wkv, bkv = pw
    n = x2.shape[0]
    t_len = n // n_batch
    nt = t_len // tm
    n_pos_blocks = cos_t.shape[0] // tm
    row = lambda i: (i, 0)
    const = lambda i: (0, 0)
    pos_map = lambda i: (i % n_pos_blocks, 0)
    kvt_shape = jax.ShapeDtypeStruct((n_batch, NSA_KV_GROUPS, LANES, t_len), F32)
    kvt_spec = pl.BlockSpec((1, NSA_KV_GROUPS, LANES, tm), lambda i: (i // nt, 0, 0, i % nt))
    out_shape = (
        jax.ShapeDtypeStruct((n, 512), F32),
        kvt_shape, kvt_shape, kvt_shape,
        jax.ShapeDtypeStruct((n, NSA_KV_GROUPS * LANES), F32),
        jax.ShapeDtypeStruct((n, 512), BF16),
        jax.ShapeDtypeStruct((n, 512), F32),
        jax.ShapeDtypeStruct((n, 512), BF16),
        jax.ShapeDtypeStruct((n, 512), F32),
        jax.ShapeDtypeStruct((n, 2048), F32),
    )
    return pl.pallas_call(
        _proj_kernel,
        grid=(n // tm,),
        in_specs=[
            pl.BlockSpec((tm, D_MODEL), row),
            pl.BlockSpec(w.shape, const),
            pl.BlockSpec(b.shape, const),
            pl.BlockSpec(wkv.shape, const),
            pl.BlockSpec(bkv.shape, const),
            pl.BlockSpec((tm, LANES), pos_map),
            pl.BlockSpec((tm, LANES), pos_map),
        ],
        out_specs=(
            pl.BlockSpec((tm, 512), row),
            kvt_spec, kvt_spec, kvt_spec,
            pl.BlockSpec((tm, NSA_KV_GROUPS * LANES), row),
            pl.BlockSpec((tm, 512), row),
            pl.BlockSpec((tm, 512), row),
            pl.BlockSpec((tm, 512), row),
            pl.BlockSpec((tm, 512), row),
            pl.BlockSpec((tm, 2048), row),
        ),
        out_shape=out_shape,
        compiler_params=_cparams(("parallel",)),
        name="proj",
    )(x2, w, b, wkv, bkv, cos_t, sin_t)


XPOSE_CHUNK = 512


def _compress_rows(xs_ref, wc_ref, w2_ref, b2_ref, pos_ref, wpe_ref, kc_ref, kvct_ref, perm_ref, nj):
    n_blk = 4 * nj
    pe = jnp.dot(pos_ref[...], wpe_ref[...], preferred_element_type=F32)[0:1]
    acc = jnp.zeros((n_blk, 2 * LANES), F32)
    for c2 in range(CMP_STRIDE // 2):
        xa = xs_ref[pl.ds(2 * c2, n_blk, stride=CMP_STRIDE), :]
        xb = xs_ref[pl.ds(2 * c2 + 1, n_blk, stride=CMP_STRIDE), :]
        x2 = jnp.concatenate([xa, xb], axis=1).astype(BF16)
        acc = acc + jnp.dot(x2, wc_ref[c2], preferred_element_type=F32)
    row = lax.broadcasted_iota(jnp.int32, (n_blk, LANES), 0)
    nxt = pltpu.roll(acc[:, LANES:], n_blk - 1, 0)
    second = jnp.where(row == n_blk - 1, 0.0, nxt)
    h = jax.nn.gelu(acc[:, :LANES] + second + pe)
    perm_ref[...] = jnp.dot(h.astype(BF16), w2_ref[...], preferred_element_type=F32) + b2_ref[...]
    lane = lax.broadcasted_iota(jnp.int32, (nj, LANES), 1)
    for rr in range(4):
        kv = perm_ref[pl.ds(rr, nj, stride=4), :]
        kc_ref[rr * nj:(rr + 1) * nj, :] = jnp.where(lane < HEAD_DIM, kv, 0.0).astype(BF16)
        kvct_ref[:, rr * nj:(rr + 1) * nj] = kv.T.astype(BF16)


def _compress_kernel(xt_ref, wc_ref, w2_ref, b2_ref, pos_ref, wpe_ref, kc_ref, kvct_ref, xs_ref, perm_ref, *, length):
    def xpose(i, _):
        t0 = pl.multiple_of(i * XPOSE_CHUNK, XPOSE_CHUNK)
        xs_ref[pl.ds(t0, XPOSE_CHUNK), :] = xt_ref[0, 0, :, pl.ds(t0, XPOSE_CHUNK)].T
        return 0

    lax.fori_loop(0, length // XPOSE_CHUNK, xpose, 0)
    _compress_rows(xs_ref, wc_ref, w2_ref, b2_ref, pos_ref, wpe_ref, kc_ref.at[0, 0], kvct_ref.at[0, 0], perm_ref,
                   length // SEL_BLOCK)


PAGES_PER_STEP = 16


def _compress_paged_kernel(pt_ref, *refs, n_pages, per_step):
    page_refs = refs[:per_step]
    (q_ref, wc_ref, w2_ref, b2_ref, pos_ref, wpe_ref, o_ref, imp_ref,
     xs_ref, perm_ref, kc_ref, kvct_ref) = refs[per_step:]
    s = pl.program_id(1)
    for k in range(per_step):
        r0 = pl.multiple_of((s * per_step + k) * PAGE_SIZE, PAGE_SIZE)
        for g in range(NSA_KV_GROUPS):
            xs_ref[g, pl.ds(r0, PAGE_SIZE), :] = page_refs[k][0, g].T

    @pl.when(s == n_pages // per_step - 1)
    def _():
        nj = n_pages * PAGE_SIZE // SEL_BLOCK
        for g in range(NSA_KV_GROUPS):
            _compress_rows(xs_ref.at[g], wc_ref, w2_ref, b2_ref, pos_ref, wpe_ref, kc_ref, kvct_ref, perm_ref, nj)
            o_ref[0, g], imp_ref[0, g] = _dec_compressed(q_ref[0, g], kvct_ref, nj)


def _compress_paged_attend(cache_t, page_table, cw, q8):
    wc, w2, b2, pos, wpe = cw
    nb, n_pages = page_table.shape
    g = cache_t.shape[1]
    length = n_pages * PAGE_SIZE
    n_rows = length // CMP_STRIDE
    nj = length // SEL_BLOCK
    per_step = min(PAGES_PER_STEP, n_pages)
    steps = n_pages // per_step
    page_spec = lambda k: pl.BlockSpec(
        (1, g, LANES, PAGE_SIZE), lambda b, s, pt: (pt[b * n_pages + s * per_step + k], 0, 0, 0))
    const = lambda a: pl.BlockSpec(a.shape, lambda b, s, pt: (0,) * a.ndim)
    per_seq = lambda *shape: pl.BlockSpec((1, g) + shape, lambda b, s, pt: (b, 0, 0, 0))
    return pl.pallas_call(
        functools.partial(_compress_paged_kernel, n_pages=n_pages, per_step=per_step),
        grid_spec=pltpu.PrefetchScalarGridSpec(
            num_scalar_prefetch=1,
            grid=(nb, steps),
            in_specs=[page_spec(k) for k in range(per_step)] + [per_seq(HEAD_ROWS, HEAD_DIM)]
            + [const(a) for a in (wc, w2, b2, pos, wpe)],
            out_specs=(per_seq(HEAD_ROWS, HEAD_DIM), per_seq(HEAD_ROWS, nj)),
            scratch_shapes=[pltpu.VMEM((g, length, LANES), F32), pltpu.VMEM((n_rows, LANES), F32),
                            pltpu.VMEM((n_rows, LANES), BF16), pltpu.VMEM((LANES, n_rows), BF16)],
        ),
        out_shape=(
            jax.ShapeDtypeStruct((nb, g, HEAD_ROWS, HEAD_DIM), F32),
            jax.ShapeDtypeStruct((nb, g, HEAD_ROWS, nj), F32),
        ),
        compiler_params=_cparams(("parallel", "arbitrary")),
        name="compress_paged",
    )(page_table.reshape(-1), *([cache_t] * per_step), q8, wc, w2, b2, pos, wpe)


def _compress_weights(cmp_pos, w_cmp1, w_cmp2, b_cmp2):
    eye = jnp.eye(2, dtype=F32)
    w5 = w_cmp1.reshape(2, 2, CMP_STRIDE, HEAD_DIM, CMP_HIDDEN)
    wc = jnp.einsum("vhcde,vw->cvdhwe", w5, eye).reshape(CMP_STRIDE // 2, 2 * LANES, 2 * LANES).astype(BF16)
    w2 = jnp.einsum("ved,vw->vewd", w_cmp2, eye).reshape(LANES, LANES).astype(BF16)
    b2 = b_cmp2.reshape(1, LANES)
    wpe = jnp.einsum("vcde,vw->vcdwe", w_cmp1, eye).reshape(2 * CMP_BLOCK * HEAD_DIM, LANES).astype(BF16)
    pos = jnp.broadcast_to(cmp_pos.reshape(1, -1), (8, 2 * CMP_BLOCK * HEAD_DIM)).astype(BF16)
    return wc, w2, b2, pos, wpe


def _compress(kvt, cw):
    wc, w2, b2, pos, wpe = cw
    nb, g, _, length = kvt.shape
    n_rows = length // CMP_STRIDE
    const2 = lambda b, gg: (0, 0)
    return pl.pallas_call(
        functools.partial(_compress_kernel, length=length),
        grid=(nb, g),
        in_specs=[
            pl.BlockSpec((1, 1, LANES, length), lambda b, gg: (b, gg, 0, 0)),
            pl.BlockSpec(wc.shape, lambda b, gg: (0, 0, 0)),
            pl.BlockSpec(w2.shape, const2),
            pl.BlockSpec(b2.shape, const2),
            pl.BlockSpec(pos.shape, const2),
            pl.BlockSpec(wpe.shape, const2),
        ],
        out_specs=(
            pl.BlockSpec((1, 1, n_rows, LANES), lambda b, gg: (b, gg, 0, 0)),
            pl.BlockSpec((1, 1, LANES, n_rows), lambda b, gg: (b, gg, 0, 0)),
        ),
        out_shape=(
            jax.ShapeDtypeStruct((nb, g, n_rows, LANES), BF16),
            jax.ShapeDtypeStruct((nb, g, LANES, n_rows), BF16),
        ),
        scratch_shapes=[pltpu.VMEM((length, LANES), F32), pltpu.VMEM((n_rows, LANES), F32)],
        compiler_params=_cparams(("parallel", "parallel")),
        name="compress",
    )(kvt, wc, w2, b2, pos, wpe)


BIAS_BLOCKS = 16
KEY_TILE = 512
PREP_TILE = 1024
N_COLS = NSA_GROUP_HEADS * Q_TILE
SLC_UNROLL = 4
V_ROWS = HEAD_DIM + 16
WIN_KEYS = WINDOW + Q_TILE
CMP_ROW_CHOICES = (64, 128, 192)


def _kv_prep_kernel(slc_ref, win_ref, kslc_ref, kwin_ref, vslc_ref, vwin_ref):
    shape = (PREP_TILE, LANES)
    lane = lax.broadcasted_iota(jnp.int32, shape, 1)
    blk = (lax.broadcasted_iota(jnp.int32, shape, 0) // SEL_BLOCK) % BIAS_BLOCKS
    onehot = jnp.where(lane - HEAD_DIM == blk, 1.0, 0.0)
    ks = slc_ref[0, 0].T
    kslc_ref[0, 0] = jnp.where(lane < HEAD_DIM, ks, onehot).astype(BF16)
    kw = win_ref[0, 0].T
    kwin_ref[0, 0] = jnp.where(lane < HEAD_DIM, kw, 0.0).astype(BF16)
    ones = jnp.ones((V_ROWS - HEAD_DIM, PREP_TILE), BF16)
    vslc_ref[0, 0] = jnp.concatenate([slc_ref[0, 0, HEAD_DIM:, :].astype(BF16), ones], axis=0)
    vwin_ref[0, 0] = jnp.concatenate([win_ref[0, 0, HEAD_DIM:, :].astype(BF16), ones], axis=0)


def _kv_prep(kvt_slc, kvt_win):
    nb, g, _, t_len = kvt_slc.shape
    in_spec = pl.BlockSpec((1, 1, LANES, PREP_TILE), lambda b, gg, i: (b, gg, 0, i))
    k_spec = pl.BlockSpec((1, 1, PREP_TILE, LANES), lambda b, gg, i: (b, gg, i, 0))
    v_spec = pl.BlockSpec((1, 1, V_ROWS, PREP_TILE), lambda b, gg, i: (b, gg, 0, i))
    k_shape = jax.ShapeDtypeStruct((nb, g, t_len, LANES), BF16)
    v_shape = jax.ShapeDtypeStruct((nb, g, V_ROWS, t_len), BF16)
    return pl.pallas_call(
        _kv_prep_kernel,
        grid=(nb, g, t_len // PREP_TILE),
        in_specs=[in_spec, in_spec],
        out_specs=(k_spec, k_spec, v_spec, v_spec),
        out_shape=(k_shape, k_shape, v_shape, v_shape),
        compiler_params=_cparams(("parallel", "parallel", "parallel")),
        name="kv_prep",
    )(kvt_slc, kvt_win)


def _online_update(carry, s, vt):
    m, acc = carry
    m_new = jnp.maximum(m, jnp.max(s, axis=0, keepdims=True))
    p = jnp.exp(s - m_new)
    acc = jnp.exp(m - m_new) * acc + jnp.dot(vt, p.astype(BF16), preferred_element_type=F32)
    return m_new, acc


def _softmax_out(acc):
    return acc[:HEAD_DIM] / acc[HEAD_DIM:HEAD_DIM + 1]


def _select_blocks(imp, q0):
    jrow = lax.broadcasted_iota(jnp.int32, imp.shape, 0)
    cur = (q0 + lax.broadcasted_iota(jnp.int32, imp.shape, 1)) // SEL_BLOCK
    work = jnp.where(jrow <= cur, imp, -1.0)
    bias = jnp.full(imp.shape, NEG_INF, F32)
    for forced_blk in (0, cur, cur - 1):
        hit = jrow == forced_blk
        bias = jnp.where(hit, 0.0, bias)
        work = jnp.where(hit, -3.0e38, work)
    jf = jrow.astype(F32)
    for _ in range(N_SEL - 3):
        mx = jnp.max(work, axis=0, keepdims=True)
        first = jnp.min(jnp.where(work == mx, jf, 1.0e9), axis=0, keepdims=True)
        hit = jf == first
        bias = jnp.where(hit, 0.0, bias)
        work = jnp.where(hit, -3.0e38, work)
    return bias


def _nsa_prompt_kernel(qa_ref, ga_ref, kc_ref, kvct_ref, kslc_ref, vslc_ref, kwin_ref, vwin_ref,
                       o_ref, qm_ref, bias_ref, ocmp_ref, mtail_ref, acctail_ref, *, nj):
    qi = pl.program_id(2)

    @pl.when(qi == 0)
    def _():
        qm_ref[...] = jnp.zeros_like(qm_ref)

    q_t = qa_ref[...].T
    for r in range(NSA_GROUP_HEADS):
        qm_ref[0:HEAD_DIM, r * Q_TILE:(r + 1) * Q_TILE] = q_t[r * HEAD_DIM:(r + 1) * HEAD_DIM].astype(BF16)
    ga_t = ga_ref[...].T

    def gate(branch):
        return jnp.concatenate([ga_t[r * 3 + branch:r * 3 + branch + 1] for r in range(NSA_GROUP_HEADS)], axis=1)

    q0 = qi * Q_TILE
    qpos_c = q0 + (lax.broadcasted_iota(jnp.int32, (1, N_COLS), 1) & (Q_TILE - 1))

    def compressed_and_select(n):
        kc = jnp.concatenate([kc_ref[0, 0, rr * nj:rr * nj + n, :] for rr in range(4)], axis=0)
        vct = jnp.concatenate([kvct_ref[0, 0, HEAD_DIM:, rr * nj:rr * nj + n] for rr in range(4)], axis=1)
        s = jnp.dot(kc, qm_ref[...], preferred_element_type=F32)
        row = lax.broadcasted_iota(jnp.int32, (4 * n, N_COLS), 0)
        cmp_end = (row % n) * SEL_BLOCK + (row // n) * CMP_STRIDE + (CMP_BLOCK - 1)
        vis = cmp_end <= qpos_c
        s = jnp.where(vis, s, NEG_INF)
        m = jnp.max(s, axis=0, keepdims=True)
        p = jnp.where(vis, jnp.exp(s - m), 0.0)
        l = jnp.sum(p, axis=0, keepdims=True)
        pn = p / jnp.where(l > 0.0, l, 1.0)
        ocmp_ref[...] = jnp.dot(vct, pn.astype(BF16), preferred_element_type=F32)
        p3 = pn[3 * n:4 * n]
        jrow4 = lax.broadcasted_iota(jnp.int32, (n, N_COLS), 0)
        prev = jnp.where(jrow4 == 0, 0.0, pltpu.roll(p3, 1, 0))
        a = pn[0:n] + pn[n:2 * n] + pn[2 * n:3 * n] + p3 + prev
        imp = a[:, 0:Q_TILE]
        for r in range(1, NSA_GROUP_HEADS):
            imp = imp + a[:, r * Q_TILE:(r + 1) * Q_TILE]
        bias16 = _select_blocks(imp, q0).astype(BF16)
        for r in range(NSA_GROUP_HEADS):
            bias_ref[0:n, r * Q_TILE:(r + 1) * Q_TILE] = bias16

    sizes = [n for n in CMP_ROW_CHOICES if n < nj] + [nj]
    need = (q0 + Q_TILE) // SEL_BLOCK
    for v, n in enumerate(sizes):
        lo = sizes[v - 1] if v else 0
        pl.when((need > lo) & ((need <= n) | (v == len(sizes) - 1)))(functools.partial(compressed_and_select, n))
    o_cmp = ocmp_ref[...]


    init =(jnp.full((1, N_COLS), -jnp.inf, F32), jnp.zeros((V_ROWS, N_COLS), F32))
    q_rows = qm_ref[0:HEAD_DIM, :]
    pad_rows = jnp.zeros((LANES - HEAD_DIM - BIAS_BLOCKS, N_COLS), BF16)

    def slc_scores(st):
        b0 = pl.multiple_of((st * KEY_TILE // (SEL_BLOCK * BIAS_BLOCKS)) * BIAS_BLOCKS, BIAS_BLOCKS)
        qm = jnp.concatenate([q_rows, bias_ref[pl.ds(b0, BIAS_BLOCKS), :], pad_rows], axis=0)
        k0 = pl.multiple_of(st * KEY_TILE, KEY_TILE)
        return jnp.dot(kslc_ref[0, 0, pl.ds(k0, KEY_TILE), :], qm, preferred_element_type=F32)

    def slc_absorb(st, carry, sc):
        k0 = pl.multiple_of(st * KEY_TILE, KEY_TILE)
        return _online_update(carry, sc, vslc_ref[0, 0, :, pl.ds(k0, KEY_TILE)])

    def slc_group(i, carry):
        st0 = i * SLC_UNROLL
        sc = slc_scores(st0)
        for u in range(SLC_UNROLL):
            sc_next = slc_scores(st0 + u + 1) if u + 1 < SLC_UNROLL else None
            carry = slc_absorb(st0 + u, carry, sc)
            sc = sc_next
        return carry

    n_tiles = q0 // KEY_TILE + 1
    tail = (n_tiles - 1) % SLC_UNROLL + 1
    m_s, acc_s = lax.fori_loop(0, (n_tiles - tail) // SLC_UNROLL, slc_group, init)
    mtail_ref[...] = m_s
    acctail_ref[...] = acc_s

    def slc_tail(size):
        st0 = n_tiles - size
        carry = (mtail_ref[...], acctail_ref[...])
        sc = slc_scores(st0)
        for u in range(size):
            sc_next = slc_scores(st0 + u + 1) if u + 1 < size else None
            if u == size - 1:
                kpos = (st0 + u) * KEY_TILE + lax.broadcasted_iota(jnp.int32, (KEY_TILE, N_COLS), 0)
                sc = jnp.where(kpos <= qpos_c, sc, NEG_INF)
            carry = slc_absorb(st0 + u, carry, sc)
            sc = sc_next
        acctail_ref[...] = carry[1]

    for size in range(1, SLC_UNROLL + 1):
        pl.when(tail == size)(functools.partial(slc_tail, size))
    o_slc = _softmax_out(acctail_ref[...])

    kw0 = pl.multiple_of(jnp.maximum(q0 - WINDOW, 0), Q_TILE)
    sc = jnp.dot(kwin_ref[0, 0, pl.ds(kw0, WIN_KEYS), :], qm_ref[...], preferred_element_type=F32)
    dpos = qpos_c - (kw0 + lax.broadcasted_iota(jnp.int32, (WIN_KEYS, N_COLS), 0))
    sc = jnp.where(dpos >= 0, jnp.where(dpos < WINDOW, sc, NEG_INF), NEG_INF)
    p = jnp.exp(sc - jnp.max(sc, axis=0, keepdims=True))
    o_win = _softmax_out(jnp.dot(vwin_ref[0, 0, :, pl.ds(kw0, WIN_KEYS)], p.astype(BF16), preferred_element_type=F32))

    o = gate(0) * o_cmp + gate(1) * o_slc + gate(2) * o_win
    o_rows = jnp.concatenate([o[:, r * Q_TILE:(r + 1) * Q_TILE] for r in range(NSA_GROUP_HEADS)], axis=0)
    o_ref[...] = o_rows.T.astype(BF16)


def _nsa_prompt(qa, ga, kc, kvct, kslc, vslc, kwin, vwin):
    nb, g, t_len, _ = kslc.shape
    nq = t_len // Q_TILE
    nj = t_len // SEL_BLOCK
    n_cmp_rows = kc.shape[2]
    assert nj % BIAS_BLOCKS == 0 and n_cmp_rows == 4 * nj and t_len % KEY_TILE == 0 and t_len >= WIN_KEYS
    tile = lambda b, gg, i: (b * nq + i, gg)
    whole = lambda b, gg, i: (b, gg, 0, 0)
    return pl.pallas_call(
        functools.partial(_nsa_prompt_kernel, nj=nj),
        grid=(nb, g, nq),
        in_specs=[
            pl.BlockSpec((Q_TILE, NSA_GROUP_HEADS * HEAD_DIM), tile),
            pl.BlockSpec((Q_TILE, LANES), tile),
            pl.BlockSpec((1, 1, n_cmp_rows, LANES), whole),
            pl.BlockSpec((1, 1, LANES, n_cmp_rows), whole),
            pl.BlockSpec((1, 1, t_len, LANES), whole),
            pl.BlockSpec((1, 1, V_ROWS, t_len), whole),
            pl.BlockSpec((1, 1, t_len, LANES), whole),
            pl.BlockSpec((1, 1, V_ROWS, t_len), whole),
        ],
        out_specs=pl.BlockSpec((Q_TILE, NSA_GROUP_HEADS * HEAD_DIM), tile),
        out_shape=jax.ShapeDtypeStruct((nb * t_len, NSA_HEADS * HEAD_DIM), BF16),
        scratch_shapes=[pltpu.VMEM((LANES, N_COLS), BF16), pltpu.VMEM((nj, N_COLS), BF16),
                        pltpu.VMEM((HEAD_DIM, N_COLS), F32), pltpu.VMEM((1, N_COLS), F32),
                        pltpu.VMEM((V_ROWS, N_COLS), F32)],
        compiler_params=_cparams(("parallel", "parallel", "arbitrary")),
        name="nsa_prompt",
    )(qa, ga, kc, kvct, kslc, vslc, kwin, vwin)


HEAD_ROWS = 8
N_PAST_SEL = N_SEL - 1


def _softmax_lanes(s_list, mask_list):
    s_list = [jnp.where(mk, s, NEG_INF) for s, mk in zip(s_list, mask_list)]
    m = s_list[0].max(axis=-1, keepdims=True)
    for s in s_list[1:]:
        m = jnp.maximum(m, s.max(axis=-1, keepdims=True))
    p_list = [jnp.where(mk, jnp.exp(s - m), 0.0) for s, mk in zip(s_list, mask_list)]
    l = p_list[0].sum(axis=-1, keepdims=True)
    for p in p_list[1:]:
        l = l + p.sum(axis=-1, keepdims=True)
    inv = 1.0 / jnp.where(l > 0.0, l, 1.0)
    return [p * inv for p in p_list]


def _dot_nt(a, b):
    return lax.dot_general(a, b, (((1,), (1,)), ((), ())), preferred_element_type=F32)


def _dec_compressed(q, kvct_ref, nj):
    n = 4 * nj
    s = jnp.dot(q, kvct_ref[:HEAD_DIM, :], preferred_element_type=F32)
    lane = lax.broadcasted_iota(jnp.int32, (HEAD_ROWS, n), 1)
    blk = 4 * (lane % nj) + lane // nj
    (pn,) = _softmax_lanes([s], [blk <= n - 2])
    out = _dot_nt(pn.astype(BF16), kvct_ref[HEAD_DIM:, :])
    row = lax.broadcasted_iota(jnp.int32, (HEAD_ROWS, n), 0)
    ph = jnp.sum(jnp.where(row < NSA_GROUP_HEADS, pn, 0.0), axis=0, keepdims=True)
    p3 = ph[:, 3 * nj:]
    prev = jnp.where(lane[0:1, :nj] == 0, 0.0, pltpu.roll(p3, 1, 1))
    imp = ph[:, :nj] + ph[:, nj:2 * nj] + ph[:, 2 * nj:3 * nj] + p3 + prev
    return out, jnp.broadcast_to(imp, (HEAD_ROWS, nj))


def _dec_topk_kernel(imp_ref, idx_ref):
    work = imp_ref[...].T
    nj, n = work.shape
    jrow = lax.broadcasted_iota(jnp.int32, (nj, n), 0)
    jf = jrow.astype(F32)
    for forced_blk in (0, nj - 1):
        work = jnp.where(jrow == forced_blk, FORCE_SCORE, work)
    picks = []
    for _ in range(N_PAST_SEL):
        mx = jnp.max(work, axis=0, keepdims=True)
        first = jnp.min(jnp.where(work == mx, jf, 1.0e9), axis=0, keepdims=True)
        picks.append(first)
        work = jnp.where(jf == first, -3.0e38, work)
    picks.append(jnp.full((1, n), float(nj), F32))
    idx_ref[...] = jnp.concatenate(picks, axis=0)


def _dec_topk(imp2):
    n, nj = imp2.shape
    return pl.pallas_call(
        _dec_topk_kernel,
        out_shape=jax.ShapeDtypeStruct((N_SEL, n), F32),
        compiler_params=pltpu.CompilerParams(vmem_limit_bytes=VMEM_LIMIT),
        name="dec_topk",
    )(imp2)


def _dec_attn_kernel(phys_ref, half_ref, q_ref, gate_ref, ocmp_ref, *refs):
    page_refs = refs[:N_PAST_SEL]
    slcnew_ref, winnew_ref, win_ref, o_ref, winout_ref = refs[N_PAST_SEL:]
    b = pl.program_id(0)
    base = (b * pl.num_programs(1) + pl.program_id(1)) * N_PAST_SEL
    q = q_ref[0, 0]
    lane = lax.broadcasted_iota(jnp.int32, (HEAD_ROWS, LANES), 1)
    own = lane == b

    def attend(kv_blocks, masks):
        scores = [jnp.dot(q, kv[:HEAD_DIM].astype(BF16), preferred_element_type=F32) for kv in kv_blocks]
        probs = _softmax_lanes(scores, masks)
        out = jnp.zeros((HEAD_ROWS, HEAD_DIM), F32)
        for p, kv in zip(probs, kv_blocks):
            out = out + _dot_nt(p.astype(BF16), kv[HEAD_DIM:].astype(BF16))
        return out

    pages = [r[0, 0] for r in page_refs] + [slcnew_ref[0, 0]]
    masks = [(lane // SEL_BLOCK) == half_ref[base + k] for k in range(N_PAST_SEL)] + [own]
    o_slc = attend(pages, masks)
    st = win_ref[0, 0]
    win_c = st.shape[1]
    wlane = lax.broadcasted_iota(jnp.int32, (HEAD_ROWS, win_c), 1)
    o_win = attend([st, winnew_ref[0, 0]], [wlane > win_c - WINDOW, own])
    gates = gate_ref[0, 0]
    o_ref[0, 0] = gates[:, 0:1] * ocmp_ref[0, 0] + gates[:, 1:2] * o_slc + gates[:, 2:3] * o_win
    new = winnew_ref[0, 0]
    lane_full = lax.broadcasted_iota(jnp.int32, new.shape, 1)
    new_col = jnp.sum(jnp.where(lane_full == b, new, 0.0), axis=1, keepdims=True)
    col = lax.broadcasted_iota(jnp.int32, st.shape, 1)
    winout_ref[0, 0] = jnp.where(col == win_c - 1, new_col, pltpu.roll(st, win_c - 1, 1))


def _dec_attn(phys, half, q8, gates, o_cmp, slc_t, slc_new, win_new, win_t):
    nb, g, _, win_c = win_t.shape
    assert slc_new.shape[-1] == nb == LANES
    blk = lambda *shape: pl.BlockSpec((1, 1) + shape, lambda b, gg, ph, hf: (b, gg, 0, 0))
    page = lambda k: pl.BlockSpec((1, 1, LANES, PAGE_SIZE),
                                  lambda b, gg, ph, hf: (ph[(b * g + gg) * N_PAST_SEL + k], gg, 0, 0))
    new = pl.BlockSpec((1, 1, LANES, nb), lambda b, gg, ph, hf: (0, gg, 0, 0))
    return pl.pallas_call(
        _dec_attn_kernel,
        grid_spec=pltpu.PrefetchScalarGridSpec(
            num_scalar_prefetch=2,
            grid=(nb, g),
            in_specs=[blk(HEAD_ROWS, HEAD_DIM), blk(HEAD_ROWS, LANES), blk(HEAD_ROWS, HEAD_DIM)]
            + [page(k) for k in range(N_PAST_SEL)] + [new, new, blk(LANES, win_c)],
            out_specs=(blk(HEAD_ROWS, HEAD_DIM), blk(LANES, win_c)),
        ),
        out_shape=(jax.ShapeDtypeStruct((nb, g, HEAD_ROWS, HEAD_DIM), F32),
                   jax.ShapeDtypeStruct((nb, g, LANES, win_c), F32)),
        compiler_params=_cparams(("parallel", "parallel")),
        name="dec_attn",
    )(phys, half, q8, gates, o_cmp, *([slc_t] * N_PAST_SEL), slc_new, win_new, win_t)


RET_STEP_CHUNKS = 8


def _log_gamma():
    return jnp.log1p(-jnp.exp2(-5.0 - jnp.arange(RET_HEADS, dtype=F32)))


def _ret_tables(chunk):
    lg = _log_gamma()
    n = jnp.arange(chunk, dtype=F32)
    diff = n[:, None] - n[None, :]
    decay = jnp.exp(jnp.maximum(diff, 0.0)[None] * lg[:, None, None]) * (diff >= 0)[None]
    rep = lambda col: jnp.broadcast_to(col[:, :, None], (RET_HEADS, chunk, RET_DV))
    qdec = rep(jnp.exp((n + 1.0)[None, :] * lg[:, None]))
    kdec = rep(jnp.exp((chunk - 1.0 - n)[None, :] * lg[:, None]))
    sdec = jnp.broadcast_to(jnp.exp(chunk * lg)[:, None, None], (RET_HEADS, RET_DK, RET_DV))
    return decay.astype(F32), qdec, kdec, sdec


def _group_norm_gate(o, g, gn_g, gn_b):
    mu = jnp.mean(o, axis=-1, keepdims=True)
    var = jnp.mean(jnp.square(o - mu), axis=-1, keepdims=True)
    on = (o - mu) * lax.rsqrt(var + LN_EPS) * gn_g + gn_b
    return (jax.nn.silu(g) * on).astype(BF16)


def _ret_prompt_kernel(qb_ref, kb_ref, vb_ref, gb_ref, decay_ref, qdec_ref, kdec_ref, sdec_ref, gng_ref, gnb_ref,
                       bm_ref, state_ref, st_ref):
    @pl.when(pl.program_id(1) == 0)
    def _():
        st_ref[...] = jnp.zeros_like(st_ref)

    def chunk_step(c, _):
        rows = pl.ds(pl.multiple_of(c * RET_CHUNK, RET_CHUNK), RET_CHUNK)
        for h in range(RET_HEADS):
            cols = slice(h * RET_DK, (h + 1) * RET_DK)
            q = qb_ref[rows, cols]
            k = kb_ref[rows, cols]
            v = vb_ref[rows, cols]
            st = st_ref[h]
            inner = lax.dot_general(q, k.astype(BF16), (((1,), (1,)), ((), ())), preferred_element_type=F32)
            inner = inner * decay_ref[h]
            o = jnp.dot(inner.astype(BF16), v, preferred_element_type=F32)
            o = o + jnp.dot(q, st.astype(BF16), preferred_element_type=F32) * qdec_ref[h]
            kd = (k * kdec_ref[h]).astype(BF16)
            st_ref[h] = sdec_ref[h] * st + lax.dot_general(kd, v, (((0,), (0,)), ((), ())), preferred_element_type=F32)
            bm_ref[rows, cols] = _group_norm_gate(o, gb_ref[rows, cols], gng_ref[:, cols], gnb_ref[:, cols])
        return 0

    lax.fori_loop(0, RET_STEP_CHUNKS, chunk_step, 0)
    state_ref[0] = st_ref[...]


def _ret_prompt(qb, kb, vb, gb, gn_g, gn_b, nb):
    n = qb.shape[0]
    tt = RET_STEP_CHUNKS * RET_CHUNK
    nt = n // nb // tt
    decay, qdec, kdec, sdec = _ret_tables(RET_CHUNK)
    row = pl.BlockSpec((tt, RET_HEADS * RET_DK), lambda b, i: (b * nt + i, 0))
    tab = pl.BlockSpec((RET_HEADS, RET_CHUNK, RET_DV), lambda b, i: (0, 0, 0))
    vec = pl.BlockSpec((1, RET_HEADS * RET_DV), lambda b, i: (0, 0))
    return pl.pallas_call(
        _ret_prompt_kernel,
        grid=(nb, nt),
        in_specs=[row, row, row, row, tab, tab, tab, tab, vec, vec],
        out_specs=(row, pl.BlockSpec((1, RET_HEADS, RET_DK, RET_DV), lambda b, i: (b, 0, 0, 0))),
        out_shape=(jax.ShapeDtypeStruct((n, RET_HEADS * RET_DV), BF16),
                   jax.ShapeDtypeStruct((nb, RET_HEADS, RET_DK, RET_DV), F32)),
        scratch_shapes=[pltpu.VMEM((RET_HEADS, RET_DK, RET_DV), F32)],
        compiler_params=_cparams(("parallel", "arbitrary")),
        name="ret_prompt",
    )(qb, kb, vb, gb, decay, qdec, kdec, sdec, gn_g.reshape(1, -1), gn_b.reshape(1, -1))


def _ret_step_kernel(q_ref, k_ref, v_ref, g_ref, st_ref, qdec_ref, sdec_ref, gng_ref, gnb_ref, bm_ref, stout_ref):
    for s in range(q_ref.shape[0]):
        for h in range(RET_HEADS):
            cols = slice(h * RET_DK, (h + 1) * RET_DK)
            q = q_ref[s, :, cols]
            k = k_ref[s, :, cols].astype(BF16)
            v = v_ref[s, :, cols]
            st = st_ref[s, h]
            qk = jnp.sum(q.astype(F32) * k.astype(F32), axis=-1, keepdims=True)
            o = qk.astype(BF16).astype(F32) * v.astype(F32)
            o = o + jnp.dot(q, st.astype(BF16), preferred_element_type=F32) * qdec_ref[h]
            stout_ref[s, h] = sdec_ref[h] * st + lax.dot_general(k, v, (((0,), (0,)), ((), ())),
                                                                 preferred_element_type=F32)
            bm_ref[s, :, cols] = _group_norm_gate(o, g_ref[s, :, cols], gng_ref[:, cols], gnb_ref[:, cols])


RET_STEP_SEQS = 4


def _ret_step(q8, k8, v8, g8, state, gn_g, gn_b):
    nb = state.shape[0]
    ns = RET_STEP_SEQS if nb % RET_STEP_SEQS == 0 else 1
    _, qdec, _, sdec = _ret_tables(1)
    tok = pl.BlockSpec((ns, HEAD_ROWS, RET_HEADS * RET_DK), lambda b: (b, 0, 0))
    st = pl.BlockSpec((ns, RET_HEADS, RET_DK, RET_DV), lambda b: (b, 0, 0, 0))
    vec = pl.BlockSpec((1, RET_HEADS * RET_DV), lambda b: (0, 0))
    return pl.pallas_call(
        _ret_step_kernel,
        grid=(nb // ns,),
        in_specs=[tok, tok, tok, tok, st,
                  pl.BlockSpec((RET_HEADS, 1, RET_DV), lambda b: (0, 0, 0)),
                  pl.BlockSpec((RET_HEADS, RET_DK, RET_DV), lambda b: (0, 0, 0)), vec, vec],
        out_specs=(tok, st),
        out_shape=(jax.ShapeDtypeStruct((nb, HEAD_ROWS, RET_HEADS * RET_DV), BF16),
                   jax.ShapeDtypeStruct(state.shape, F32)),
        compiler_params=_cparams(("parallel",)),
        name="ret_step",
    )(q8, k8, v8, g8, state, qdec, sdec, gn_g.reshape(1, -1), gn_b.reshape(1, -1))


ROUTE_E0 = N_GROUPS
MOE_ROWS = 256
SAMPLE_MOE_ROWS = 32


def _layer_norm(v, g, b):
    mu = jnp.mean(v, axis=-1, keepdims=True)
    var = jnp.mean(jnp.square(v - mu), axis=-1, keepdims=True)
    return (v - mu) * lax.rsqrt(var + LN_EPS) * g + b


def _first_lane(hit, lane):
    return jnp.min(jnp.where(hit, lane, 4 * LANES), axis=-1, keepdims=True)


def _merge_kernel(on_ref, bm_ref, gm_ref, x_ref, wn_ref, wr_ref, wo_ref, g1_ref, b1_ref, wrt_ref, brt_ref, tri_ref,
                  x1_ref, route_ref, counts_ref, carry_ref):
    @pl.when(pl.program_id(0) == 0)
    def _():
        carry_ref[...] = jnp.zeros_like(carry_ref)

    gate_a = jax.nn.sigmoid(gm_ref[:, :D_MODEL])
    gate_b = jax.nn.sigmoid(gm_ref[:, D_MODEL:])
    m = gate_a * jnp.dot(on_ref[...], wn_ref[...], preferred_element_type=F32)
    m = m + gate_b * jnp.dot(bm_ref[...], wr_ref[...], preferred_element_type=F32)
    mix = jnp.dot(m.astype(BF16), wo_ref[...], preferred_element_type=F32)
    x1 = _layer_norm(DEEPNORM_ALPHA * x_ref[...] + mix, g1_ref[...], b1_ref[...])
    x1_ref[...] = x1

    x_hi = x1.astype(BF16)
    x_lo = (x1 - x_hi.astype(F32)).astype(BF16)
    hi_part = jnp.dot(x_hi, wrt_ref[...], preferred_element_type=F32)
    lg = hi_part[:, :LANES] + hi_part[:, LANES:] + brt_ref[...]
    lg = lg + jnp.dot(x_lo, wrt_ref[:, :LANES], preferred_element_type=F32)
    lane = lax.broadcasted_iota(jnp.int32, lg.shape, 1)
    gl = jnp.where(lane < N_GROUPS, lg, -jnp.inf)
    ge = jnp.exp(gl - jnp.max(gl, axis=-1, keepdims=True))
    pgrp = ge / jnp.sum(ge, axis=-1, keepdims=True)
    pg = jnp.max(pgrp, axis=-1, keepdims=True)
    grp = _first_lane(pgrp == pg, lane)
    lo = ROUTE_E0 + grp * EXPERTS_PER_GROUP
    in_g = (lane >= lo) & (lane < lo + EXPERTS_PER_GROUP)
    el = jnp.where(in_g, lg, -jnp.inf)
    ee = jnp.exp(el - jnp.max(el, axis=-1, keepdims=True))
    pc = jnp.where(in_g, ee / jnp.sum(ee, axis=-1, keepdims=True), -1.0)
    w1 = jnp.max(pc, axis=-1, keepdims=True)
    i1 = _first_lane(pc == w1, lane)
    pc2 = jnp.where(lane == i1, -1.0, pc)
    w2 = jnp.max(pc2, axis=-1, keepdims=True)
    i2 = _first_lane(pc2 == w2, lane)
    wsum = w1 + w2
    wt1 = pg * w1 / wsum
    wt2 = pg * w2 / wsum
    hit1 = lane == i1
    hit2 = lane == i2
    cnt = jnp.where(hit1, 1.0, 0.0) + jnp.where(hit2, 1.0, 0.0)
    base = carry_ref[0:1, :] + jnp.dot(tri_ref[...], cnt.astype(BF16), preferred_element_type=F32)
    rank1 = jnp.sum(jnp.where(hit1, base, 0.0), axis=-1, keepdims=True)
    rank2 = jnp.sum(jnp.where(hit2, base, 0.0), axis=-1, keepdims=True)
    carry = carry_ref[0:1, :] + jnp.sum(cnt, axis=0, keepdims=True)
    carry_ref[...] = jnp.broadcast_to(carry, carry_ref.shape)
    counts_ref[...] = jnp.broadcast_to(carry, counts_ref.shape)
    fields = ((i1 - ROUTE_E0).astype(F32), (i2 - ROUTE_E0).astype(F32), wt1, wt2, rank1, rank2)
    out = jnp.zeros(lg.shape, F32)
    for idx, val in enumerate(fields):
        out = jnp.where(lane == idx, val, out)
    route_ref[...] = out


def _merge_weights(w_nsa_proj, w_ret_proj, w_o, w_route_group, b_route_group, w_route_exp, b_route_exp):
    pad = LANES - N_GROUPS - N_EXPERTS
    wrt = jnp.concatenate([w_route_group, w_route_exp, jnp.zeros((D_MODEL, pad), F32)], axis=1)
    w_hi = wrt.astype(BF16)
    w_lo = (wrt - w_hi.astype(F32)).astype(BF16)
    wrt = jnp.concatenate([w_hi, w_lo], axis=1)
    brt = jnp.concatenate([b_route_group, b_route_exp, jnp.zeros((pad,), F32)])[None, :]
    return w_nsa_proj.astype(BF16), w_ret_proj.astype(BF16), w_o.astype(BF16), wrt, brt


def _merge(o_nsa, bm, gm, x2, mw, ln_g, ln_b, tm):
    wn, wr, wo, wrt, brt = mw
    n = x2.shape[0]
    tri = (jnp.arange(tm)[:, None] > jnp.arange(tm)[None, :]).astype(BF16)
    row = lambda width: pl.BlockSpec((tm, width), lambda i: (i, 0))
    const = lambda a: pl.BlockSpec(a.shape, lambda i: (0, 0))
    ln_g = ln_g.reshape(1, -1)
    ln_b = ln_b.reshape(1, -1)
    return pl.pallas_call(
        _merge_kernel,
        grid=(n // tm,),
        in_specs=[row(512), row(512), row(2 * D_MODEL), row(D_MODEL), const(wn), const(wr), const(wo),
                  const(ln_g), const(ln_b), const(wrt), const(brt), const(tri)],
        out_specs=(row(D_MODEL), row(LANES), pl.BlockSpec((8, LANES), lambda i: (0, 0))),
        out_shape=(jax.ShapeDtypeStruct((n, D_MODEL), F32), jax.ShapeDtypeStruct((n, LANES), F32),
                   jax.ShapeDtypeStruct((8, LANES), F32)),
        scratch_shapes=[pltpu.VMEM((8, LANES), F32)],
        compiler_params=_cparams(("arbitrary",)),
        name="merge_route",
    )(o_nsa, bm, gm, x2, wn, wr, wo, ln_g, ln_b, wrt, brt, tri)


SC_CHUNK_ROWS = 32


def _gather_rows(table, idx):
    info = plsc.get_sparse_core_info()
    n_cores = info.num_cores
    n_workers = n_cores * info.num_subcores
    n_idx, width = idx.shape[0], table.shape[1]
    per_worker = n_idx // n_workers
    chunk = max(c for c in range(8, SC_CHUNK_ROWS + 1, 8) if per_worker % c == 0)
    assert n_idx % (8 * n_workers) == 0
    mesh = plsc.VectorSubcoreMesh(core_axis_name="c", subcore_axis_name="s")

    def body(table_hbm, idx_hbm, out_hbm, idx_v, rows_v, sem):
        worker = lax.axis_index("s") * n_cores + lax.axis_index("c")
        base = worker * per_worker

        @pl.loop(0, per_worker // chunk)
        def _(c):
            off = pl.multiple_of(base + c * chunk, 8)
            pltpu.sync_copy(idx_hbm.at[pl.ds(off, chunk)], idx_v)
            pltpu.async_copy(table_hbm.at[idx_v], rows_v, sem).wait()
            pltpu.sync_copy(rows_v, out_hbm.at[pl.ds(off, chunk)])

    return pl.kernel(
        body, mesh=mesh,
        out_type=jax.ShapeDtypeStruct((n_idx, width), table.dtype),
        scratch_types=[pltpu.VMEM((chunk,), jnp.int32), pltpu.VMEM((chunk, width), table.dtype),
                       pltpu.SemaphoreType.DMA],
    )(table, idx)


def _scatter_rows(rows, dest, n_slots):
    info = plsc.get_sparse_core_info()
    n_cores = info.num_cores
    n_workers = n_cores * info.num_subcores
    n, width = rows.shape
    n_idx = dest.shape[0]
    per_worker = n_idx // n_workers
    chunk = max(c for c in range(8, SC_CHUNK_ROWS + 1, 8) if per_worker % c == 0)
    assert n_idx % (8 * n_workers) == 0 and n % per_worker == 0
    dest2 = dest.reshape(n_idx // chunk, chunk)
    mesh = plsc.VectorSubcoreMesh(core_axis_name="c", subcore_axis_name="s")

    def body(rows_hbm, dest_hbm, out_hbm, idx_v, rows_v):
        worker = lax.axis_index("s") * n_cores + lax.axis_index("c")
        base = worker * per_worker

        @pl.loop(0, per_worker // chunk)
        def _(c):
            off = base + c * chunk
            pltpu.sync_copy(dest_hbm.at[pl.ds(off // chunk, 1)], idx_v)
            pltpu.sync_copy(rows_hbm.at[pl.ds(pl.multiple_of(lax.rem(off, n), 8), chunk)], rows_v)
            pltpu.sync_copy(rows_v, out_hbm.at[idx_v.at[0]])

    return pl.kernel(
        body, mesh=mesh,
        out_type=jax.ShapeDtypeStruct((n_slots, width), rows.dtype),
        scratch_types=[pltpu.VMEM((1, chunk), jnp.int32), pltpu.VMEM((chunk, width), rows.dtype)],
    )(rows, dest2)


def _expert_kernel(be_ref, nused_ref, valid_ref, x_ref, w1_ref, w3_ref, w2_ref, y_ref):
    i = pl.program_id(0)

    @pl.when(i >= nused_ref[0])
    def _():
        y_ref[...] = jnp.zeros_like(y_ref)

    @pl.when(i < nused_ref[0])
    def _():
        row = lax.broadcasted_iota(jnp.int32, x_ref.shape, 0)
        x = jnp.where(row < valid_ref[i], x_ref[...], 0.0).astype(BF16)
        h = jax.nn.silu(jnp.dot(x, w1_ref[0].astype(BF16), preferred_element_type=F32))
        h = h * jnp.dot(x, w3_ref[0].astype(BF16), preferred_element_type=F32)
        y_ref[...] = jnp.dot(h.astype(BF16), w2_ref[0].astype(BF16), preferred_element_type=F32)


def _experts(x_sorted, block_e, n_used, valid, w1, w3, w2, rows):
    n_blocks = block_e.shape[0]
    xspec = pl.BlockSpec((rows, D_MODEL), lambda i, be, nu, va: (i, 0))
    wspec = lambda a: pl.BlockSpec((1,) + a.shape[1:], lambda i, be, nu, va: (be[i], 0, 0))
    return pl.pallas_call(
        _expert_kernel,
        grid_spec=pltpu.PrefetchScalarGridSpec(
            num_scalar_prefetch=3,
            grid=(n_blocks,),
            in_specs=[xspec, wspec(w1), wspec(w3), wspec(w2)],
            out_specs=xspec,
        ),
        out_shape=jax.ShapeDtypeStruct(x_sorted.shape, F32),
        compiler_params=_cparams(("arbitrary",)),
        name="experts",
    )(block_e, n_used, valid, x_sorted, w1, w3, w2)


def _combine_kernel(x1_ref, y0_ref, y1_ref, route_ref, g_ref, b_ref, o_ref):
    route = route_ref[...]
    ff = y0_ref[...] * route[:, 2:3] + y1_ref[...] * route[:, 3:4]
    o_ref[...] = _layer_norm(DEEPNORM_ALPHA * x1_ref[...] + ff, g_ref[...], b_ref[...])


def _combine(x1, y2, route, ln_g, ln_b, tm):
    n = x1.shape[0]
    row = lambda width: pl.BlockSpec((tm, width), lambda i: (i, 0))
    second = pl.BlockSpec((tm, D_MODEL), lambda i: (i + n // tm, 0))
    vec = pl.BlockSpec((1, D_MODEL), lambda i: (0, 0))
    return pl.pallas_call(
        _combine_kernel,
        grid=(n // tm,),
        in_specs=[row(D_MODEL), row(D_MODEL), second, row(LANES), vec, vec],
        out_specs=row(D_MODEL),
        out_shape=jax.ShapeDtypeStruct((n, D_MODEL), F32),
        compiler_params=_cparams(("parallel",)),
        name="combine_ln2",
    )(x1, y2, y2, route, ln_g.reshape(1, -1), ln_b.reshape(1, -1))


def _finish(o_nsa, bm, gm, x2, mw, ew, ln1, ln2, tm, rows):
    n = x2.shape[0]
    x1, route, counts = _merge(o_nsa, bm, gm, x2, mw, ln1[0], ln1[1], tm)
    sizes = counts[0, ROUTE_E0:ROUTE_E0 + N_EXPERTS].astype(jnp.int32)
    padded = (sizes + rows - 1) // rows * rows
    pad_end = jnp.cumsum(padded)
    pad_start = pad_end - padded
    e = route[:, 0:TOP_K].astype(jnp.int32)
    rank = route[:, 4:4 + TOP_K].astype(jnp.int32)
    dest = (pad_start[e] + rank).T.reshape(-1)
    n_blocks = -(-(n * TOP_K) // rows) + N_EXPERTS
    blk0 = jnp.arange(n_blocks, dtype=jnp.int32) * rows
    block_e = jnp.sum(blk0[:, None] >= pad_end[None, :], axis=1).astype(jnp.int32)
    block_e = jnp.minimum(block_e, N_EXPERTS - 1)
    n_used = (pad_end[-1:] // rows).astype(jnp.int32)
    valid = jnp.clip(sizes[block_e] - (blk0 - pad_start[block_e]), 0, rows).astype(jnp.int32)
    y_sorted = _experts(_scatter_rows(x1, dest, n_blocks * rows), block_e, n_used, valid, *ew, rows)
    return _combine(x1, _gather_rows(y_sorted, dest), route, ln2[0], ln2[1], tm)


PROMPT_ROWS = 256


def _feature_major(a):
    lead = a.shape[:-3]
    return jnp.moveaxis(a, -3, -1).reshape(lead + (2 * HEAD_DIM, a.shape[-3]))


def _row_major(a):
    lead = a.shape[:-2]
    return jnp.moveaxis(a.reshape(lead + (2, HEAD_DIM, a.shape[-1])), -1, -3)


def _pad_rows(a):
    return jnp.pad(a[:, None, :], ((0, 0), (0, HEAD_ROWS - 1), (0, 0)))


def kernel(x_prompt, x_sample, cache_kv_cmp, cache_kv_slc, state_kv_win, state_ret, page_table, w_in, b_in, cmp_pos, w_cmp1, w_cmp2, b_cmp2, ret_gn_g, ret_gn_b, w_nsa_proj, w_ret_proj, w_o, ln1_g, ln1_b, w_route_group, b_route_group, w_route_exp, b_route_exp, w_e1, w_e3, w_e2, ln2_g, ln2_b):
    nb, t_len, _ = x_prompt.shape
    ns, s_len, _ = x_sample.shape
    assert s_len == 1 and ns == LANES and t_len % (RET_STEP_CHUNKS * RET_CHUNK) == 0 and t_len >= WINDOW
    g = NSA_KV_GROUPS
    pw = _proj_weights(w_in, b_in)
    cw = _compress_weights(cmp_pos, w_cmp1, w_cmp2, b_cmp2)
    mw = _merge_weights(w_nsa_proj, w_ret_proj, w_o, w_route_group, b_route_group, w_route_exp, b_route_exp)
    ew = (w_e1, w_e3, w_e2)
    ln1 = (ln1_g, ln1_b)
    ln2 = (ln2_g, ln2_b)

    cos_p, sin_p = _rope_tables(jnp.arange(t_len, dtype=jnp.int32))
    x2 = x_prompt.reshape(nb * t_len, D_MODEL)
    qa, kvt_cmp, kvt_slc, kvt_win, ga, qb, kb, vb, gb, gm = _project(x2, nb, pw, cos_p, sin_p, PROMPT_ROWS)
    kc, kvct = _compress(kvt_cmp, cw)
    kslc, kwin, vslc, vwin = _kv_prep(kvt_slc, kvt_win)
    o_nsa = _nsa_prompt(qa, ga, kc, kvct, kslc, vslc, kwin, vwin)
    bm, ret_p = _ret_prompt(qb, kb, vb, gb, ret_gn_g, ret_gn_b, nb)
    y_prompt = _finish(o_nsa, bm, gm, x2, mw, ew, ln1, ln2, PROMPT_ROWS, MOE_ROWS).reshape(x_prompt.shape)
    kv_cmp_p = _row_major(kvt_cmp)
    kv_slc_p = _row_major(kvt_slc)
    win_p = _row_major(kvt_win[..., t_len - WINDOW:])

    n_pages = page_table.shape[1]
    past_len = n_pages * PAGE_SIZE
    cos_s, sin_s = _rope_tables(jnp.full((ns,), past_len, jnp.int32))
    xs2 = x_sample.reshape(ns, D_MODEL)
    qa, new_cmp, new_slc, new_win, ga, qb, kb, vb, gb, gm = _project(xs2, 1, pw, cos_s, sin_s, ns)
    q8 = jnp.pad(qa.reshape(ns, g, NSA_GROUP_HEADS, HEAD_DIM),
                 ((0, 0), (0, 0), (0, HEAD_ROWS - NSA_GROUP_HEADS), (0, 0))).astype(BF16)
    o_cmp, imp = _compress_paged_attend(_feature_major(cache_kv_cmp), page_table, cw, q8)
    picks = _dec_topk(imp[:, :, 0, :].reshape(ns * g, -1))
    idx = picks[:N_PAST_SEL].T.astype(jnp.int32)
    blocks_per_page = PAGE_SIZE // SEL_BLOCK
    seq = jnp.arange(ns * g, dtype=jnp.int32)[:, None] // g
    phys = page_table[seq, idx // blocks_per_page]
    half = idx % blocks_per_page
    gates = ga.reshape(ns, g, LANES)[:, :, :3 * NSA_GROUP_HEADS].reshape(ns, g, NSA_GROUP_HEADS, 3)
    gates = jnp.pad(gates, ((0, 0), (0, 0), (0, HEAD_ROWS - NSA_GROUP_HEADS), (0, LANES - 3)))
    o_s, win_t = _dec_attn(phys.reshape(-1), half.reshape(-1), q8, gates, o_cmp, _feature_major(cache_kv_slc),
                           new_slc, new_win, _feature_major(state_kv_win))
    o_nsa_s = o_s[:, :, :NSA_GROUP_HEADS].reshape(ns, NSA_HEADS * HEAD_DIM).astype(BF16)
    bm8, ret_s = _ret_step(_pad_rows(qb), _pad_rows(kb), _pad_rows(vb), _pad_rows(gb), state_ret, ret_gn_g, ret_gn_b)
    y_sample = _finish(o_nsa_s, bm8[:, 0], gm, xs2, mw, ew, ln1, ln2, ns, SAMPLE_MOE_ROWS).reshape(x_sample.shape)
    kv_cmp_s = jnp.moveaxis(new_cmp[0], -1, 0).reshape(ns, g, 1, 2, HEAD_DIM)
    kv_slc_s = jnp.moveaxis(new_slc[0], -1, 0).reshape(ns, g, 1, 2, HEAD_DIM)
    win_s = _row_major(win_t)
    return (y_prompt, y_sample, kv_cmp_p, kv_slc_p, win_p, ret_p, kv_cmp_s, kv_slc_s, win_s, ret_s)
```

```python
import functools

import numpy as np
import jax
import jax.numpy as jnp
from jax import lax
from jax.experimental import pallas as pl
from jax.experimental.pallas import tpu as pltpu
from jax.experimental.pallas import tpu_sc as plsc

F32 = jnp.float32
BF16 = jnp.bfloat16

D_MODEL = 1024
DEPTH = 1
PAGE_SIZE = 128
NSA_HEADS = 8
NSA_KV_GROUPS = 2
NSA_GROUP_HEADS = NSA_HEADS // NSA_KV_GROUPS
HEAD_DIM = 64
CMP_BLOCK = 32
CMP_STRIDE = 16
CMP_HIDDEN = 64
SEL_BLOCK = 64
N_SEL = 16
WINDOW = 512
Q_TILE = 256
FORCE_SCORE = 1.0e3
NEG_INF = -1.0e30
RET_HEADS = 4
RET_DK = 128
RET_DV = 128
RET_CHUNK = 128
ROPE_BASE = 10000.0
N_GROUPS = 4
EXPERTS_PER_GROUP = 8
N_EXPERTS = N_GROUPS * EXPERTS_PER_GROUP
TOP_K = 2
EXPERT_FF = 512
LN_EPS = 1e-5
DEEPNORM_ALPHA = (2.0 * DEPTH) ** 0.25

NSA_Q_COLS = NSA_HEADS * HEAD_DIM
NSA_KV_COLS = 3 * 2 * NSA_KV_GROUPS * HEAD_DIM
NSA_GATE_COLS = 3 * NSA_HEADS
RET_QK_COLS = RET_HEADS * RET_DK
RET_V_COLS = RET_HEADS * RET_DV

LANES = 128

P_QA = 0
P_GA = P_QA + NSA_Q_COLS
P_QB = P_GA + NSA_KV_GROUPS * LANES
P_KB = P_QB + RET_QK_COLS
P_VB = P_KB + RET_QK_COLS
P_GB = P_VB + RET_V_COLS
P_GM = P_GB + RET_V_COLS
VMEM_LIMIT = 56 * 1024 * 1024


def _cparams(sem):
    return pltpu.CompilerParams(dimension_semantics=sem, vmem_limit_bytes=VMEM_LIMIT)


def _proj_kernel(x_ref, w_ref, b_ref, wkv_ref, bkv_ref, cos_ref, sin_ref,
                 qa_ref, kcmp_ref, kslc_ref, kwin_ref, ga_ref, qb_ref, kb_ref, vb_ref, gb_ref, gm_ref):
    x = x_ref[...].astype(BF16)

    def cols(c0, n):
        h = jnp.dot(x, w_ref[:, c0:c0 + n], preferred_element_type=F32)
        return h + b_ref[:, c0:c0 + n]

    qa_ref[...] = cols(P_QA, 512) * (HEAD_DIM ** -0.5)
    ga_ref[...] = jax.nn.sigmoid(cols(P_GA, NSA_KV_GROUPS * LANES))
    cos = cos_ref[...]
    sin = sin_ref[...]
    for h in range(RET_HEADS):
        q = cols(P_QB + h * RET_DK, RET_DK)
        qb_ref[:, h * RET_DK:(h + 1) * RET_DK] = (q * cos + pltpu.roll(q, RET_DK // 2, 1) * sin).astype(BF16)
        k = cols(P_KB + h * RET_DK, RET_DK)
        kb_ref[:, h * RET_DK:(h + 1) * RET_DK] = (k * cos + pltpu.roll(k, RET_DK // 2, 1) * sin) * (RET_DK ** -0.5)
    vb_ref[...] = cols(P_VB, 512).astype(BF16)
    gb_ref[...] = cols(P_GB, 512)
    gm_ref[...] = cols(P_GM, 2048)
    kvt = lax.dot_general(wkv_ref[...], x, (((1,), (1,)), ((), ())), preferred_element_type=F32)
    kvt = kvt + bkv_ref[...]
    for br, out_ref in enumerate((kcmp_ref, kslc_ref, kwin_ref)):
        for g in range(NSA_KV_GROUPS):
            i = br * NSA_KV_GROUPS + g
            out_ref[0, g] = kvt[i * LANES:(i + 1) * LANES]


def _proj_weights(w_in, b_in):
    o_kv = NSA_Q_COLS
    o_ga = o_kv + NSA_KV_COLS
    o_qb = o_ga + NSA_GATE_COLS
    per_g = NSA_GATE_COLS // NSA_KV_GROUPS
    ga_cols = []
    for g in range(NSA_KV_GROUPS):
        ga_cols += [np.arange(o_ga + g * per_g, o_ga + (g + 1) * per_g), np.full(LANES - per_g, -1)]
    perm = np.concatenate([
        np.arange(0, NSA_Q_COLS),
        *ga_cols,
        np.arange(o_qb, o_qb + 2 * RET_QK_COLS + 2 * RET_V_COLS + 2 * D_MODEL),
    ])
    keep = jnp.asarray(perm >= 0)
    idx = jnp.asarray(np.maximum(perm, 0))
    w = jnp.where(keep[None, :], w_in[:, idx], 0.0).astype(BF16)
    b = jnp.where(keep, b_in[idx], 0.0)[None, :]
    br, g, kv, d = np.meshgrid(np.arange(3), np.arange(NSA_KV_GROUPS), np.arange(2), np.arange(HEAD_DIM), indexing="ij")
    kv_idx = jnp.asarray((o_kv + br * 256 + kv * 128 + g * 64 + d).reshape(-1))
    wkv = w_in[:, kv_idx].T.astype(BF16)
    bkv = b_in[kv_idx][:, None]
    return w, b, wkv, bkv


def _rope_tables(pos):
    half = RET_DK // 2
    inv = ROPE_BASE ** (-jnp.arange(half, dtype=F32) / half)
    ang = pos.astype(F32)[:, None] * inv[None, :]
    cos, sin = jnp.cos(ang), jnp.sin(ang)
    return jnp.concatenate([cos, cos], -1), jnp.concatenate([-sin, sin], -1)


def _project(x2, n_batch, pw, cos_t, sin_t, tm):
    w, b, wkv, bkv = pw
    n = x2.shape[0]
    t_len = n // n_batch
    nt = t_len // tm
    n_pos_blocks = cos_t.shape[0] // tm
    row = lambda i: (i, 0)
    const = lambda i: (0, 0)
    pos_map = lambda i: (i % n_pos_blocks, 0)
    kvt_shape = jax.ShapeDtypeStruct((n_batch, NSA_KV_GROUPS, LANES, t_len), F32)
    kvt_spec = pl.BlockSpec((1, NSA_KV_GROUPS, LANES, tm), lambda i: (i // nt, 0, 0, i % nt))
    out_shape = (
        jax.ShapeDtypeStruct((n, 512), F32),
        kvt_shape, kvt_shape, kvt_shape,
        jax.ShapeDtypeStruct((n, NSA_KV_GROUPS * LANES), F32),
        jax.ShapeDtypeStruct((n, 512), BF16),
        jax.ShapeDtypeStruct((n, 512), F32),
        jax.ShapeDtypeStruct((n, 512), BF16),
        jax.ShapeDtypeStruct((n, 512), F32),
        jax.ShapeDtypeStruct((n, 2048), F32),
    )
    return pl.pallas_call(
        _proj_kernel,
        grid=(n // tm,),
        in_specs=[
            pl.BlockSpec((tm, D_MODEL), row),
            pl.BlockSpec(w.shape, const),
            pl.BlockSpec(b.shape, const),
            pl.BlockSpec(wkv.shape, const),
            pl.BlockSpec(bkv.shape, const),
            pl.BlockSpec((tm, LANES), pos_map),
            pl.BlockSpec((tm, LANES), pos_map),
        ],
        out_specs=(
            pl.BlockSpec((tm, 512), row),
            kvt_spec, kvt_spec, kvt_spec,
            pl.BlockSpec((tm, NSA_KV_GROUPS * LANES), row),
            pl.BlockSpec((tm, 512), row),
            pl.BlockSpec((tm, 512), row),
            pl.BlockSpec((tm, 512), row),
            pl.BlockSpec((tm, 512), row),
            pl.BlockSpec((tm, 2048), row),
        ),
        out_shape=out_shape,
        compiler_params=_cparams(("parallel",)),
        name="proj",
    )(x2, w, b, wkv, bkv, cos_t, sin_t)


XPOSE_CHUNK = 512


def _compress_rows(xs_ref, wc_ref, w2_ref, b2_ref, pos_ref, wpe_ref, kc_ref, kvct_ref, perm_ref, nj):
    n_blk = 4 * nj
    pe = jnp.dot(pos_ref[...], wpe_ref[...], preferred_element_type=F32)[0:1]
    acc = jnp.zeros((n_blk, 2 * LANES), F32)
    for c2 in range(CMP_STRIDE // 2):
        xa = xs_ref[pl.ds(2 * c2, n_blk, stride=CMP_STRIDE), :]
        xb = xs_ref[pl.ds(2 * c2 + 1, n_blk, stride=CMP_STRIDE), :]
        x2 = jnp.concatenate([xa, xb], axis=1).astype(BF16)
        acc = acc + jnp.dot(x2, wc_ref[c2], preferred_element_type=F32)
    row = lax.broadcasted_iota(jnp.int32, (n_blk, LANES), 0)
    nxt = pltpu.roll(acc[:, LANES:], n_blk - 1, 0)
    second = jnp.where(row == n_blk - 1, 0.0, nxt)
    h = jax.nn.gelu(acc[:, :LANES] + second + pe)
    perm_ref[...] = jnp.dot(h.astype(BF16), w2_ref[...], preferred_element_type=F32) + b2_ref[...]
    lane = lax.broadcasted_iota(jnp.int32, (nj, LANES), 1)
    for rr in range(4):
        kv = perm_ref[pl.ds(rr, nj, stride=4), :]
        kc_ref[rr * nj:(rr + 1) * nj, :] = jnp.where(lane < HEAD_DIM, kv, 0.0).astype(BF16)
        kvct_ref[:, rr * nj:(rr + 1) * nj] = kv.T.astype(BF16)


def _compress_kernel(xt_ref, wc_ref, w2_ref, b2_ref, pos_ref, wpe_ref, kc_ref, kvct_ref, xs_ref, perm_ref, *, length):
    def xpose(i, _):
        t0 = pl.multiple_of(i * XPOSE_CHUNK, XPOSE_CHUNK)
        xs_ref[pl.ds(t0, XPOSE_CHUNK), :] = xt_ref[0, 0, :, pl.ds(t0, XPOSE_CHUNK)].T
        return 0

    lax.fori_loop(0, length // XPOSE_CHUNK, xpose, 0)
    _compress_rows(xs_ref, wc_ref, w2_ref, b2_ref, pos_ref, wpe_ref, kc_ref.at[0, 0], kvct_ref.at[0, 0], perm_ref,
                   length // SEL_BLOCK)


PAGES_PER_STEP = 64


def _compress_paged_kernel(pt_ref, *refs, n_pages, per_step):
    page_refs = refs[:per_step]
    (q_ref, wc_ref, w2_ref, b2_ref, pos_ref, wpe_ref, o_ref, imp_ref,
     xs_ref, perm_ref, kc_ref, kvct_ref) = refs[per_step:]
    s = pl.program_id(1)
    for k in range(per_step):
        r0 = pl.multiple_of((s * per_step + k) * PAGE_SIZE, PAGE_SIZE)
        for g in range(NSA_KV_GROUPS):
            xs_ref[g, pl.ds(r0, PAGE_SIZE), :] = page_refs[k][0, g].T

    @pl.when(s == n_pages // per_step - 1)
    def _():
        nj = n_pages * PAGE_SIZE // SEL_BLOCK
        for g in range(NSA_KV_GROUPS):
            _compress_rows(xs_ref.at[g], wc_ref, w2_ref, b2_ref, pos_ref, wpe_ref, kc_ref, kvct_ref, perm_ref, nj)
            o_ref[0, g], imp_ref[0, g] = _dec_compressed(q_ref[0, g], kvct_ref, nj)


def _compress_paged_attend(cache_t, page_table, cw, q8):
    wc, w2, b2, pos, wpe = cw
    nb, n_pages = page_table.shape
    g = cache_t.shape[1]
    length = n_pages * PAGE_SIZE
    n_rows = length // CMP_STRIDE
    nj = length // SEL_BLOCK
    per_step = min(PAGES_PER_STEP, n_pages)
    steps = n_pages // per_step
    page_spec = lambda k: pl.BlockSpec(
        (1, g, LANES, PAGE_SIZE), lambda b, s, pt: (pt[b * n_pages + s * per_step + k], 0, 0, 0))
    const = lambda a: pl.BlockSpec(a.shape, lambda b, s, pt: (0,) * a.ndim)
    per_seq = lambda *shape: pl.BlockSpec((1, g) + shape, lambda b, s, pt: (b, 0, 0, 0))
    return pl.pallas_call(
        functools.partial(_compress_paged_kernel, n_pages=n_pages, per_step=per_step),
        grid_spec=pltpu.PrefetchScalarGridSpec(
            num_scalar_prefetch=1,
            grid=(nb, steps),
            in_specs=[page_spec(k) for k in range(per_step)] + [per_seq(HEAD_ROWS, HEAD_DIM)]
            + [const(a) for a in (wc, w2, b2, pos, wpe)],
            out_specs=(per_seq(HEAD_ROWS, HEAD_DIM), per_seq(HEAD_ROWS, nj)),
            scratch_shapes=[pltpu.VMEM((g, length, LANES), F32), pltpu.VMEM((n_rows, LANES), F32),
                            pltpu.VMEM((n_rows, LANES), BF16), pltpu.VMEM((LANES, n_rows), BF16)],
        ),
        out_shape=(
            jax.ShapeDtypeStruct((nb, g, HEAD_ROWS, HEAD_DIM), F32),
            jax.ShapeDtypeStruct((nb, g, HEAD_ROWS, nj), F32),
        ),
        compiler_params=_cparams(("parallel", "arbitrary")),
        name="compress_paged",
    )(page_table.reshape(-1), *([cache_t] * per_step), q8, wc, w2, b2, pos, wpe)


def _compress_weights(cmp_pos, w_cmp1, w_cmp2, b_cmp2):
    eye = jnp.eye(2, dtype=F32)
    w5 = w_cmp1.reshape(2, 2, CMP_STRIDE, HEAD_DIM, CMP_HIDDEN)
    wc = jnp.einsum("vhcde,vw->cvdhwe", w5, eye).reshape(CMP_STRIDE // 2, 2 * LANES, 2 * LANES).astype(BF16)
    w2 = jnp.einsum("ved,vw->vewd", w_cmp2, eye).reshape(LANES, LANES).astype(BF16)
    b2 = b_cmp2.reshape(1, LANES)
    wpe = jnp.einsum("vcde,vw->vcdwe", w_cmp1, eye).reshape(2 * CMP_BLOCK * HEAD_DIM, LANES).astype(BF16)
    pos = jnp.broadcast_to(cmp_pos.reshape(1, -1), (8, 2 * CMP_BLOCK * HEAD_DIM)).astype(BF16)
    return wc, w2, b2, pos, wpe


def _compress(kvt, cw):
    wc, w2, b2, pos, wpe = cw
    nb, g, _, length = kvt.shape
    n_rows = length // CMP_STRIDE
    const2 = lambda b, gg: (0, 0)
    return pl.pallas_call(
        functools.partial(_compress_kernel, length=length),
        grid=(nb, g),
        in_specs=[
            pl.BlockSpec((1, 1, LANES, length), lambda b, gg: (b, gg, 0, 0)),
            pl.BlockSpec(wc.shape, lambda b, gg: (0, 0, 0)),
            pl.BlockSpec(w2.shape, const2),
            pl.BlockSpec(b2.shape, const2),
            pl.BlockSpec(pos.shape, const2),
            pl.BlockSpec(wpe.shape, const2),
        ],
        out_specs=(
            pl.BlockSpec((1, 1, n_rows, LANES), lambda b, gg: (b, gg, 0, 0)),
            pl.BlockSpec((1, 1, LANES, n_rows), lambda b, gg: (b, gg, 0, 0)),
        ),
        out_shape=(
            jax.ShapeDtypeStruct((nb, g, n_rows, LANES), BF16),
            jax.ShapeDtypeStruct((nb, g, LANES, n_rows), BF16),
        ),
        scratch_shapes=[pltpu.VMEM((length, LANES), F32), pltpu.VMEM((n_rows, LANES), F32)],
        compiler_params=_cparams(("parallel", "parallel")),
        name="compress",
    )(kvt, wc, w2, b2, pos, wpe)


BIAS_BLOCKS = 16
KEY_TILE = 512
PREP_TILE = 1024
N_COLS = NSA_GROUP_HEADS * Q_TILE
SLC_UNROLL = 4
V_ROWS = HEAD_DIM + 16
WIN_KEYS = WINDOW + Q_TILE
CMP_ROW_CHOICES = (64, 128, 192)


def _kv_prep_kernel(slc_ref, win_ref, kslc_ref, kwin_ref, vslc_ref, vwin_ref):
    shape = (PREP_TILE, LANES)
    lane = lax.broadcasted_iota(jnp.int32, shape, 1)
    blk = (lax.broadcasted_iota(jnp.int32, shape, 0) // SEL_BLOCK) % BIAS_BLOCKS
    onehot = jnp.where(lane - HEAD_DIM == blk, 1.0, 0.0)
    ks = slc_ref[0, 0].T
    kslc_ref[0, 0] = jnp.where(lane < HEAD_DIM, ks, onehot).astype(BF16)
    kw = win_ref[0, 0].T
    kwin_ref[0, 0] = jnp.where(lane < HEAD_DIM, kw, 0.0).astype(BF16)
    ones = jnp.ones((V_ROWS - HEAD_DIM, PREP_TILE), BF16)
    vslc_ref[0, 0] = jnp.concatenate([slc_ref[0, 0, HEAD_DIM:, :].astype(BF16), ones], axis=0)
    vwin_ref[0, 0] = jnp.concatenate([win_ref[0, 0, HEAD_DIM:, :].astype(BF16), ones], axis=0)


def _kv_prep(kvt_slc, kvt_win):
    nb, g, _, t_len = kvt_slc.shape
    in_spec = pl.BlockSpec((1, 1, LANES, PREP_TILE), lambda b, gg, i: (b, gg, 0, i))
    k_spec = pl.BlockSpec((1, 1, PREP_TILE, LANES), lambda b, gg, i: (b, gg, i, 0))
    v_spec = pl.BlockSpec((1, 1, V_ROWS, PREP_TILE), lambda b, gg, i: (b, gg, 0, i))
    k_shape = jax.ShapeDtypeStruct((nb, g, t_len, LANES), BF16)
    v_shape = jax.ShapeDtypeStruct((nb, g, V_ROWS, t_len), BF16)
    return pl.pallas_call(
        _kv_prep_kernel,
        grid=(nb, g, t_len // PREP_TILE),
        in_specs=[in_spec, in_spec],
        out_specs=(k_spec, k_spec, v_spec, v_spec),
        out_shape=(k_shape, k_shape, v_shape, v_shape),
        compiler_params=_cparams(("parallel", "parallel", "parallel")),
        name="kv_prep",
    )(kvt_slc, kvt_win)


def _online_update(carry, s, vt):
    m, acc = carry
    m_new = jnp.maximum(m, jnp.max(s, axis=0, keepdims=True))
    p = jnp.exp(s - m_new)
    acc = jnp.exp(m - m_new) * acc + jnp.dot(vt, p.astype(BF16), preferred_element_type=F32)
    return m_new, acc


def _softmax_out(acc):
    return acc[:HEAD_DIM] / acc[HEAD_DIM:HEAD_DIM + 1]


def _select_blocks(imp, q0):
    jrow = lax.broadcasted_iota(jnp.int32, imp.shape, 0)
    cur = (q0 + lax.broadcasted_iota(jnp.int32, imp.shape, 1)) // SEL_BLOCK
    work = jnp.where(jrow <= cur, imp, -1.0)
    bias = jnp.full(imp.shape, NEG_INF, F32)
    for forced_blk in (0, cur, cur - 1):
        hit = jrow == forced_blk
        bias = jnp.where(hit, 0.0, bias)
        work = jnp.where(hit, -3.0e38, work)
    jf = jrow.astype(F32)
    for _ in range(N_SEL - 3):
        mx = jnp.max(work, axis=0, keepdims=True)
        first = jnp.min(jnp.where(work == mx, jf, 1.0e9), axis=0, keepdims=True)
        hit = jf == first
        bias = jnp.where(hit, 0.0, bias)
        work = jnp.where(hit, -3.0e38, work)
    return bias


def _nsa_prompt_kernel(qa_ref, ga_ref, kc_ref, kvct_ref, kslc_ref, vslc_ref, kwin_ref, vwin_ref,
                       o_ref, qm_ref, bias_ref, ocmp_ref, mtail_ref, acctail_ref, *, nj):
    qi = pl.program_id(2)

    @pl.when(qi == 0)
    def _():
        qm_ref[...] = jnp.zeros_like(qm_ref)

    q_t = qa_ref[...].T
    for r in range(NSA_GROUP_HEADS):
        qm_ref[0:HEAD_DIM, r * Q_TILE:(r + 1) * Q_TILE] = q_t[r * HEAD_DIM:(r + 1) * HEAD_DIM].astype(BF16)
    ga_t = ga_ref[...].T

    def gate(branch):
        return jnp.concatenate([ga_t[r * 3 + branch:r * 3 + branch + 1] for r in range(NSA_GROUP_HEADS)], axis=1)

    q0 = qi * Q_TILE
    qpos_c = q0 + (lax.broadcasted_iota(jnp.int32, (1, N_COLS), 1) & (Q_TILE - 1))

    def compressed_and_select(n):
        kc = jnp.concatenate([kc_ref[0, 0, rr * nj:rr * nj + n, :] for rr in range(4)], axis=0)
        vct = jnp.concatenate([kvct_ref[0, 0, HEAD_DIM:, rr * nj:rr * nj + n] for rr in range(4)], axis=1)
        s = jnp.dot(kc, qm_ref[...], preferred_element_type=F32)
        j_end = lax.broadcasted_iota(jnp.int32, (n, N_COLS), 0) * SEL_BLOCK + (CMP_BLOCK - 1)
        cmp_end = jnp.concatenate([j_end + rr * CMP_STRIDE for rr in range(4)], axis=0)
        vis = cmp_end <= qpos_c
        s = jnp.where(vis, s, NEG_INF)
        m = jnp.max(s, axis=0, keepdims=True)
        p = jnp.where(vis, jnp.exp(s - m), 0.0)
        l = jnp.sum(p, axis=0, keepdims=True)
        pn = p / jnp.where(l > 0.0, l, 1.0)
        ocmp_ref[...] = jnp.dot(vct, pn.astype(BF16), preferred_element_type=F32)
        p3 = pn[3 * n:4 * n]
        jrow4 = lax.broadcasted_iota(jnp.int32, (n, N_COLS), 0)
        prev = jnp.where(jrow4 == 0, 0.0, pltpu.roll(p3, 1, 0))
        a = pn[0:n] + pn[n:2 * n] + pn[2 * n:3 * n] + p3 + prev
        imp = a[:, 0:Q_TILE]
        for r in range(1, NSA_GROUP_HEADS):
            imp = imp + a[:, r * Q_TILE:(r + 1) * Q_TILE]
        bias16 = _select_blocks(imp, q0).astype(BF16)
        for r in range(NSA_GROUP_HEADS):
            bias_ref[0:n, r * Q_TILE:(r + 1) * Q_TILE] = bias16

    sizes = [n for n in CMP_ROW_CHOICES if n < nj] + [nj]
    need = (q0 + Q_TILE) // SEL_BLOCK
    for v, n in enumerate(sizes):
        lo = sizes[v - 1] if v else 0
        pl.when((need > lo) & ((need <= n) | (v == len(sizes) - 1)))(functools.partial(compressed_and_select, n))
    o_cmp = ocmp_ref[...]


    init =(jnp.full((1, N_COLS), -jnp.inf, F32), jnp.zeros((V_ROWS, N_COLS), F32))
    q_rows = qm_ref[0:HEAD_DIM, :]
    pad_rows = jnp.zeros((LANES - HEAD_DIM - BIAS_BLOCKS, N_COLS), BF16)

    def slc_scores(st):
        b0 = pl.multiple_of((st * KEY_TILE // (SEL_BLOCK * BIAS_BLOCKS)) * BIAS_BLOCKS, BIAS_BLOCKS)
        qm = jnp.concatenate([q_rows, bias_ref[pl.ds(b0, BIAS_BLOCKS), :], pad_rows], axis=0)
        k0 = pl.multiple_of(st * KEY_TILE, KEY_TILE)
        return jnp.dot(kslc_ref[0, 0, pl.ds(k0, KEY_TILE), :], qm, preferred_element_type=F32)

    def slc_absorb(st, carry, sc):
        k0 = pl.multiple_of(st * KEY_TILE, KEY_TILE)
        return _online_update(carry, sc, vslc_ref[0, 0, :, pl.ds(k0, KEY_TILE)])

    def slc_group(i, carry):
        st0 = i * SLC_UNROLL
        sc = slc_scores(st0)
        for u in range(SLC_UNROLL):
            sc_next = slc_scores(st0 + u + 1) if u + 1 < SLC_UNROLL else None
            carry = slc_absorb(st0 + u, carry, sc)
            sc = sc_next
        return carry

    n_tiles = q0 // KEY_TILE + 1
    tail = (n_tiles - 1) % SLC_UNROLL + 1
    m_s, acc_s = lax.fori_loop(0, (n_tiles - tail) // SLC_UNROLL, slc_group, init)
    mtail_ref[...] = m_s
    acctail_ref[...] = acc_s

    def slc_tail(size):
        st0 = n_tiles - size
        carry = (mtail_ref[...], acctail_ref[...])
        sc = slc_scores(st0)
        for u in range(size):
            sc_next = slc_scores(st0 + u + 1) if u + 1 < size else None
            if u == size - 1:
                kpos = (st0 + u) * KEY_TILE + lax.broadcasted_iota(jnp.int32, (KEY_TILE, N_COLS), 0)
                sc = jnp.where(kpos <= qpos_c, sc, NEG_INF)
            carry = slc_absorb(st0 + u, carry, sc)
            sc = sc_next
        acctail_ref[...] = carry[1]

    for size in range(1, SLC_UNROLL + 1):
        pl.when(tail == size)(functools.partial(slc_tail, size))
    o_slc = _softmax_out(acctail_ref[...])

    kw0 = pl.multiple_of(jnp.maximum(q0 - WINDOW, 0), Q_TILE)
    sc = jnp.dot(kwin_ref[0, 0, pl.ds(kw0, WIN_KEYS), :], qm_ref[...], preferred_element_type=F32)
    dpos = qpos_c - (kw0 + lax.broadcasted_iota(jnp.int32, (WIN_KEYS, N_COLS), 0))
    sc = jnp.where(dpos >= 0, jnp.where(dpos < WINDOW, sc, NEG_INF), NEG_INF)
    p = jnp.exp(sc - jnp.max(sc, axis=0, keepdims=True))
    o_win = _softmax_out(jnp.dot(vwin_ref[0, 0, :, pl.ds(kw0, WIN_KEYS)], p.astype(BF16), preferred_element_type=F32))

    o = gate(0) * o_cmp + gate(1) * o_slc + gate(2) * o_win
    o_rows = jnp.concatenate([o[:, r * Q_TILE:(r + 1) * Q_TILE] for r in range(NSA_GROUP_HEADS)], axis=0)
    o_ref[...] = o_rows.T.astype(BF16)


def _nsa_prompt(qa, ga, kc, kvct, kslc, vslc, kwin, vwin):
    nb, g, t_len, _ = kslc.shape
    nq = t_len // Q_TILE
    nj = t_len // SEL_BLOCK
    n_cmp_rows = kc.shape[2]
    assert nj % BIAS_BLOCKS == 0 and n_cmp_rows == 4 * nj and t_len % KEY_TILE == 0 and t_len >= WIN_KEYS
    tile = lambda b, gg, i: (b * nq + i, gg)
    whole = lambda b, gg, i: (b, gg, 0, 0)
    return pl.pallas_call(
        functools.partial(_nsa_prompt_kernel, nj=nj),
        grid=(nb, g, nq),
        in_specs=[
            pl.BlockSpec((Q_TILE, NSA_GROUP_HEADS * HEAD_DIM), tile),
            pl.BlockSpec((Q_TILE, LANES), tile),
            pl.BlockSpec((1, 1, n_cmp_rows, LANES), whole),
            pl.BlockSpec((1, 1, LANES, n_cmp_rows), whole),
            pl.BlockSpec((1, 1, t_len, LANES), whole),
            pl.BlockSpec((1, 1, V_ROWS, t_len), whole),
            pl.BlockSpec((1, 1, t_len, LANES), whole),
            pl.BlockSpec((1, 1, V_ROWS, t_len), whole),
        ],
        out_specs=pl.BlockSpec((Q_TILE, NSA_GROUP_HEADS * HEAD_DIM), tile),
        out_shape=jax.ShapeDtypeStruct((nb * t_len, NSA_HEADS * HEAD_DIM), BF16),
        scratch_shapes=[pltpu.VMEM((LANES, N_COLS), BF16), pltpu.VMEM((nj, N_COLS), BF16),
                        pltpu.VMEM((HEAD_DIM, N_COLS), F32), pltpu.VMEM((1, N_COLS), F32),
                        pltpu.VMEM((V_ROWS, N_COLS), F32)],
        compiler_params=_cparams(("parallel", "parallel", "arbitrary")),
        name="nsa_prompt",
    )(qa, ga, kc, kvct, kslc, vslc, kwin, vwin)


HEAD_ROWS = 8
N_PAST_SEL = N_SEL - 1


def _softmax_lanes(s_list, mask_list):
    s_list = [jnp.where(mk, s, NEG_INF) for s, mk in zip(s_list, mask_list)]
    m = s_list[0].max(axis=-1, keepdims=True)
    for s in s_list[1:]:
        m = jnp.maximum(m, s.max(axis=-1, keepdims=True))
    p_list = [jnp.where(mk, jnp.exp(s - m), 0.0) for s, mk in zip(s_list, mask_list)]
    l = p_list[0].sum(axis=-1, keepdims=True)
    for p in p_list[1:]:
        l = l + p.sum(axis=-1, keepdims=True)
    inv = 1.0 / jnp.where(l > 0.0, l, 1.0)
    return [p * inv for p in p_list]


def _dot_nt(a, b):
    return lax.dot_general(a, b, (((1,), (1,)), ((), ())), preferred_element_type=F32)


def _dec_compressed(q, kvct_ref, nj):
    n = 4 * nj
    s = jnp.dot(q, kvct_ref[:HEAD_DIM, :], preferred_element_type=F32)
    lane = lax.broadcasted_iota(jnp.int32, (HEAD_ROWS, n), 1)
    blk = 4 * (lane % nj) + lane // nj
    (pn,) = _softmax_lanes([s], [blk <= n - 2])
    out = _dot_nt(pn.astype(BF16), kvct_ref[HEAD_DIM:, :])
    row = lax.broadcasted_iota(jnp.int32, (HEAD_ROWS, n), 0)
    ph = jnp.sum(jnp.where(row < NSA_GROUP_HEADS, pn, 0.0), axis=0, keepdims=True)
    p3 = ph[:, 3 * nj:]
    prev = jnp.where(lane[0:1, :nj] == 0, 0.0, pltpu.roll(p3, 1, 1))
    imp = ph[:, :nj] + ph[:, nj:2 * nj] + ph[:, 2 * nj:3 * nj] + p3 + prev
    return out, jnp.broadcast_to(imp, (HEAD_ROWS, nj))


def _dec_topk_kernel(imp_ref, idx_ref):
    work = imp_ref[...].T
    nj, n = work.shape
    jrow = lax.broadcasted_iota(jnp.int32, (nj, n), 0)
    jf = jrow.astype(F32)
    for forced_blk in (0, nj - 1):
        work = jnp.where(jrow == forced_blk, FORCE_SCORE, work)
    picks = []
    for _ in range(N_PAST_SEL):
        mx = jnp.max(work, axis=0, keepdims=True)
        first = jnp.min(jnp.where(work == mx, jf, 1.0e9), axis=0, keepdims=True)
        picks.append(first)
        work = jnp.where(jf == first, -3.0e38, work)
    picks.append(jnp.full((1, n), float(nj), F32))
    idx_ref[...] = jnp.concatenate(picks, axis=0)


def _dec_topk(imp2):
    n, nj = imp2.shape
    return pl.pallas_call(
        _dec_topk_kernel,
        out_shape=jax.ShapeDtypeStruct((N_SEL, n), F32),
        compiler_params=pltpu.CompilerParams(vmem_limit_bytes=VMEM_LIMIT),
        name="dec_topk",
    )(imp2)


def _dec_attn_kernel(phys_ref, half_ref, q_ref, gate_ref, ocmp_ref, *refs):
    page_refs = refs[:N_PAST_SEL]
    slcnew_ref, winnew_ref, win_ref, o_ref, winout_ref = refs[N_PAST_SEL:]
    b = pl.program_id(0)
    base = (b * pl.num_programs(1) + pl.program_id(1)) * N_PAST_SEL
    q = q_ref[0, 0]
    lane = lax.broadcasted_iota(jnp.int32, (HEAD_ROWS, LANES), 1)
    own = lane == b

    def attend(kv_blocks, masks):
        scores = [jnp.dot(q, kv[:HEAD_DIM].astype(BF16), preferred_element_type=F32) for kv in kv_blocks]
        probs = _softmax_lanes(scores, masks)
        out = jnp.zeros((HEAD_ROWS, HEAD_DIM), F32)
        for p, kv in zip(probs, kv_blocks):
            out = out + _dot_nt(p.astype(BF16), kv[HEAD_DIM:].astype(BF16))
        return out

    pages = [r[0, 0] for r in page_refs] + [slcnew_ref[0, 0]]
    masks = [(lane // SEL_BLOCK) == half_ref[base + k] for k in range(N_PAST_SEL)] + [own]
    o_slc = attend(pages, masks)
    st = win_ref[0, 0]
    win_c = st.shape[1]
    wlane = lax.broadcasted_iota(jnp.int32, (HEAD_ROWS, win_c), 1)
    o_win = attend([st, winnew_ref[0, 0]], [wlane > win_c - WINDOW, own])
    gates = gate_ref[0, 0]
    o_ref[0, 0] = gates[:, 0:1] * ocmp_ref[0, 0] + gates[:, 1:2] * o_slc + gates[:, 2:3] * o_win
    new = winnew_ref[0, 0]
    lane_full = lax.broadcasted_iota(jnp.int32, new.shape, 1)
    new_col = jnp.sum(jnp.where(lane_full == b, new, 0.0), axis=1, keepdims=True)
    col = lax.broadcasted_iota(jnp.int32, st.shape, 1)
    winout_ref[0, 0] = jnp.where(col == win_c - 1, new_col, pltpu.roll(st, win_c - 1, 1))


def _dec_attn(phys, half, q8, gates, o_cmp, slc_t, slc_new, win_new, win_t):
    nb, g, _, win_c = win_t.shape
    assert slc_new.shape[-1] == nb == LANES
    blk = lambda *shape: pl.BlockSpec((1, 1) + shape, lambda b, gg, ph, hf: (b, gg, 0, 0))
    page = lambda k: pl.BlockSpec((1, 1, LANES, PAGE_SIZE),
                                  lambda b, gg, ph, hf: (ph[(b * g + gg) * N_PAST_SEL + k], gg, 0, 0))
    new = pl.BlockSpec((1, 1, LANES, nb), lambda b, gg, ph, hf: (0, gg, 0, 0))
    return pl.pallas_call(
        _dec_attn_kernel,
        grid_spec=pltpu.PrefetchScalarGridSpec(
            num_scalar_prefetch=2,
            grid=(nb, g),
            in_specs=[blk(HEAD_ROWS, HEAD_DIM), blk(HEAD_ROWS, LANES), blk(HEAD_ROWS, HEAD_DIM)]
            + [page(k) for k in range(N_PAST_SEL)] + [new, new, blk(LANES, win_c)],
            out_specs=(blk(HEAD_ROWS, HEAD_DIM), blk(LANES, win_c)),
        ),
        out_shape=(jax.ShapeDtypeStruct((nb, g, HEAD_ROWS, HEAD_DIM), F32),
                   jax.ShapeDtypeStruct((nb, g, LANES, win_c), F32)),
        compiler_params=_cparams(("parallel", "parallel")),
        name="dec_attn",
    )(phys, half, q8, gates, o_cmp, *([slc_t] * N_PAST_SEL), slc_new, win_new, win_t)


RET_STEP_CHUNKS = 8


def _log_gamma():
    return jnp.log1p(-jnp.exp2(-5.0 - jnp.arange(RET_HEADS, dtype=F32)))


def _ret_tables(chunk):
    lg = _log_gamma()
    n = jnp.arange(chunk, dtype=F32)
    diff = n[:, None] - n[None, :]
    decay = jnp.exp(jnp.maximum(diff, 0.0)[None] * lg[:, None, None]) * (diff >= 0)[None]
    rep = lambda col: jnp.broadcast_to(col[:, :, None], (RET_HEADS, chunk, RET_DV))
    qdec = rep(jnp.exp((n + 1.0)[None, :] * lg[:, None]))
    kdec = rep(jnp.exp((chunk - 1.0 - n)[None, :] * lg[:, None]))
    sdec = jnp.broadcast_to(jnp.exp(chunk * lg)[:, None, None], (RET_HEADS, RET_DK, RET_DV))
    return decay.astype(F32), qdec, kdec, sdec


def _group_norm_gate(o, g, gn_g, gn_b):
    mu = jnp.mean(o, axis=-1, keepdims=True)
    var = jnp.mean(jnp.square(o - mu), axis=-1, keepdims=True)
    on = (o - mu) * lax.rsqrt(var + LN_EPS) * gn_g + gn_b
    return (jax.nn.silu(g) * on).astype(BF16)


def _ret_prompt_kernel(qb_ref, kb_ref, vb_ref, gb_ref, decay_ref, qdec_ref, kdec_ref, sdec_ref, gng_ref, gnb_ref,
                       bm_ref, state_ref, st_ref):
    @pl.when(pl.program_id(1) == 0)
    def _():
        st_ref[...] = jnp.zeros_like(st_ref)

    def chunk_step(c, _):
        rows = pl.ds(pl.multiple_of(c * RET_CHUNK, RET_CHUNK), RET_CHUNK)
        for h in range(RET_HEADS):
            cols = slice(h * RET_DK, (h + 1) * RET_DK)
            q = qb_ref[rows, cols]
            k = kb_ref[rows, cols]
            v = vb_ref[rows, cols]
            st = st_ref[h]
            inner = lax.dot_general(q, k.astype(BF16), (((1,), (1,)), ((), ())), preferred_element_type=F32)
            inner = inner * decay_ref[h]
            o = jnp.dot(inner.astype(BF16), v, preferred_element_type=F32)
            o = o + jnp.dot(q, st.astype(BF16), preferred_element_type=F32) * qdec_ref[h]
            kd = (k * kdec_ref[h]).astype(BF16)
            st_ref[h] = sdec_ref[h] * st + lax.dot_general(kd, v, (((0,), (0,)), ((), ())), preferred_element_type=F32)
            bm_ref[rows, cols] = _group_norm_gate(o, gb_ref[rows, cols], gng_ref[:, cols], gnb_ref[:, cols])
        return 0

    lax.fori_loop(0, RET_STEP_CHUNKS, chunk_step, 0)
    state_ref[0] = st_ref[...]


def _ret_prompt(qb, kb, vb, gb, gn_g, gn_b, nb):
    n = qb.shape[0]
    tt = RET_STEP_CHUNKS * RET_CHUNK
    nt = n // nb // tt
    decay, qdec, kdec, sdec = _ret_tables(RET_CHUNK)
    row = pl.BlockSpec((tt, RET_HEADS * RET_DK), lambda b, i: (b * nt + i, 0))
    tab = pl.BlockSpec((RET_HEADS, RET_CHUNK, RET_DV), lambda b, i: (0, 0, 0))
    vec = pl.BlockSpec((1, RET_HEADS * RET_DV), lambda b, i: (0, 0))
    return pl.pallas_call(
        _ret_prompt_kernel,
        grid=(nb, nt),
        in_specs=[row, row, row, row, tab, tab, tab, tab, vec, vec],
        out_specs=(row, pl.BlockSpec((1, RET_HEADS, RET_DK, RET_DV), lambda b, i: (b, 0, 0, 0))),
        out_shape=(jax.ShapeDtypeStruct((n, RET_HEADS * RET_DV), BF16),
                   jax.ShapeDtypeStruct((nb, RET_HEADS, RET_DK, RET_DV), F32)),
        scratch_shapes=[pltpu.VMEM((RET_HEADS, RET_DK, RET_DV), F32)],
        compiler_params=_cparams(("parallel", "arbitrary")),
        name="ret_prompt",
    )(qb, kb, vb, gb, decay, qdec, kdec, sdec, gn_g.reshape(1, -1), gn_b.reshape(1, -1))


def _ret_step_kernel(q_ref, k_ref, v_ref, g_ref, st_ref, qdec_ref, sdec_ref, gng_ref, gnb_ref, bm_ref, stout_ref):
    for s in range(q_ref.shape[0]):
        for h in range(RET_HEADS):
            cols = slice(h * RET_DK, (h + 1) * RET_DK)
            q = q_ref[s, :, cols]
            k = k_ref[s, :, cols].astype(BF16)
            v = v_ref[s, :, cols]
            st = st_ref[s, h]
            qk = jnp.sum(q.astype(F32) * k.astype(F32), axis=-1, keepdims=True)
            o = qk.astype(BF16).astype(F32) * v.astype(F32)
            o = o + jnp.dot(q, st.astype(BF16), preferred_element_type=F32) * qdec_ref[h]
            stout_ref[s, h] = sdec_ref[h] * st + lax.dot_general(k, v, (((0,), (0,)), ((), ())),
                                                                 preferred_element_type=F32)
            bm_ref[s, :, cols] = _group_norm_gate(o, g_ref[s, :, cols], gng_ref[:, cols], gnb_ref[:, cols])


RET_STEP_SEQS = 4


def _ret_step(q8, k8, v8, g8, state, gn_g, gn_b):
    nb = state.shape[0]
    ns = RET_STEP_SEQS if nb % RET_STEP_SEQS == 0 else 1
    _, qdec, _, sdec = _ret_tables(1)
    tok = pl.BlockSpec((ns, HEAD_ROWS, RET_HEADS * RET_DK), lambda b: (b, 0, 0))
    st = pl.BlockSpec((ns, RET_HEADS, RET_DK, RET_DV), lambda b: (b, 0, 0, 0))
    vec = pl.BlockSpec((1, RET_HEADS * RET_DV), lambda b: (0, 0))
    return pl.pallas_call(
        _ret_step_kernel,
        grid=(nb // ns,),
        in_specs=[tok, tok, tok, tok, st,
                  pl.BlockSpec((RET_HEADS, 1, RET_DV), lambda b: (0, 0, 0)),
                  pl.BlockSpec((RET_HEADS, RET_DK, RET_DV), lambda b: (0, 0, 0)), vec, vec],
        out_specs=(tok, st),
        out_shape=(jax.ShapeDtypeStruct((nb, HEAD_ROWS, RET_HEADS * RET_DV), BF16),
                   jax.ShapeDtypeStruct(state.shape, F32)),
        compiler_params=_cparams(("parallel",)),
        name="ret_step",
    )(q8, k8, v8, g8, state, qdec, sdec, gn_g.reshape(1, -1), gn_b.reshape(1, -1))


ROUTE_E0 = N_GROUPS
MOE_ROWS = 256
SAMPLE_MOE_ROWS = 32


def _layer_norm(v, g, b):
    mu = jnp.mean(v, axis=-1, keepdims=True)
    var = jnp.mean(jnp.square(v - mu), axis=-1, keepdims=True)
    return (v - mu) * lax.rsqrt(var + LN_EPS) * g + b


def _first_lane(hit, lane):
    return jnp.min(jnp.where(hit, lane, 4 * LANES), axis=-1, keepdims=True)


def _merge_kernel(on_ref, bm_ref, gm_ref, x_ref, wn_ref, wr_ref, wo_ref, g1_ref, b1_ref, wrt_ref, brt_ref, tri_ref,
                  x1_ref, route_ref, counts_ref, carry_ref):
    @pl.when(pl.program_id(0) == 0)
    def _():
        carry_ref[...] = jnp.zeros_like(carry_ref)

    gate_a = jax.nn.sigmoid(gm_ref[:, :D_MODEL])
    gate_b = jax.nn.sigmoid(gm_ref[:, D_MODEL:])
    m = gate_a * jnp.dot(on_ref[...], wn_ref[...], preferred_element_type=F32)
    m = m + gate_b * jnp.dot(bm_ref[...], wr_ref[...], preferred_element_type=F32)
    mix = jnp.dot(m.astype(BF16), wo_ref[...], preferred_element_type=F32)
    x1 = _layer_norm(DEEPNORM_ALPHA * x_ref[...] + mix, g1_ref[...], b1_ref[...])
    x1_ref[...] = x1

    x_hi = x1.astype(BF16)
    x_lo = (x1 - x_hi.astype(F32)).astype(BF16)
    hi_part = jnp.dot(x_hi, wrt_ref[...], preferred_element_type=F32)
    lg = hi_part[:, :LANES] + hi_part[:, LANES:] + brt_ref[...]
    lg = lg + jnp.dot(x_lo, wrt_ref[:, :LANES], preferred_element_type=F32)
    lane = lax.broadcasted_iota(jnp.int32, lg.shape, 1)
    gl = jnp.where(lane < N_GROUPS, lg, -jnp.inf)
    ge = jnp.exp(gl - jnp.max(gl, axis=-1, keepdims=True))
    pgrp = ge / jnp.sum(ge, axis=-1, keepdims=True)
    pg = jnp.max(pgrp, axis=-1, keepdims=True)
    grp = _first_lane(pgrp == pg, lane)
    lo = ROUTE_E0 + grp * EXPERTS_PER_GROUP
    in_g = (lane >= lo) & (lane < lo + EXPERTS_PER_GROUP)
    el = jnp.where(in_g, lg, -jnp.inf)
    ee = jnp.exp(el - jnp.max(el, axis=-1, keepdims=True))
    pc = jnp.where(in_g, ee / jnp.sum(ee, axis=-1, keepdims=True), -1.0)
    w1 = jnp.max(pc, axis=-1, keepdims=True)
    i1 = _first_lane(pc == w1, lane)
    pc2 = jnp.where(lane == i1, -1.0, pc)
    w2 = jnp.max(pc2, axis=-1, keepdims=True)
    i2 = _first_lane(pc2 == w2, lane)
    wsum = w1 + w2
    wt1 = pg * w1 / wsum
    wt2 = pg * w2 / wsum
    hit1 = lane == i1
    hit2 = lane == i2
    cnt = jnp.where(hit1, 1.0, 0.0) + jnp.where(hit2, 1.0, 0.0)
    base = carry_ref[0:1, :] + jnp.dot(tri_ref[...], cnt.astype(BF16), preferred_element_type=F32)
    rank1 = jnp.sum(jnp.where(hit1, base, 0.0), axis=-1, keepdims=True)
    rank2 = jnp.sum(jnp.where(hit2, base, 0.0), axis=-1, keepdims=True)
    carry = carry_ref[0:1, :] + jnp.sum(cnt, axis=0, keepdims=True)
    carry_ref[...] = jnp.broadcast_to(carry, carry_ref.shape)
    counts_ref[...] = jnp.broadcast_to(carry, counts_ref.shape)
    fields = ((i1 - ROUTE_E0).astype(F32), (i2 - ROUTE_E0).astype(F32), wt1, wt2, rank1, rank2)
    out = jnp.zeros(lg.shape, F32)
    for idx, val in enumerate(fields):
        out = jnp.where(lane == idx, val, out)
    route_ref[...] = out


def _merge_weights(w_nsa_proj, w_ret_proj, w_o, w_route_group, b_route_group, w_route_exp, b_route_exp):
    pad = LANES - N_GROUPS - N_EXPERTS
    wrt = jnp.concatenate([w_route_group, w_route_exp, jnp.zeros((D_MODEL, pad), F32)], axis=1)
    w_hi = wrt.astype(BF16)
    w_lo = (wrt - w_hi.astype(F32)).astype(BF16)
    wrt = jnp.concatenate([w_hi, w_lo], axis=1)
    brt = jnp.concatenate([b_route_group, b_route_exp, jnp.zeros((pad,), F32)])[None, :]
    return w_nsa_proj.astype(BF16), w_ret_proj.astype(BF16), w_o.astype(BF16), wrt, brt


def _merge(o_nsa, bm, gm, x2, mw, ln_g, ln_b, tm):
    wn, wr, wo, wrt, brt = mw
    n = x2.shape[0]
    tri = (jnp.arange(tm)[:, None] > jnp.arange(tm)[None, :]).astype(BF16)
    row = lambda width: pl.BlockSpec((tm, width), lambda i: (i, 0))
    const = lambda a: pl.BlockSpec(a.shape, lambda i: (0, 0))
    ln_g = ln_g.reshape(1, -1)
    ln_b = ln_b.reshape(1, -1)
    return pl.pallas_call(
        _merge_kernel,
        grid=(n // tm,),
        in_specs=[row(512), row(512), row(2 * D_MODEL), row(D_MODEL), const(wn), const(wr), const(wo),
                  const(ln_g), const(ln_b), const(wrt), const(brt), const(tri)],
        out_specs=(row(D_MODEL), row(LANES), pl.BlockSpec((8, LANES), lambda i: (0, 0))),
        out_shape=(jax.ShapeDtypeStruct((n, D_MODEL), F32), jax.ShapeDtypeStruct((n, LANES), F32),
                   jax.ShapeDtypeStruct((8, LANES), F32)),
        scratch_shapes=[pltpu.VMEM((8, LANES), F32)],
        compiler_params=_cparams(("arbitrary",)),
        name="merge_route",
    )(o_nsa, bm, gm, x2, wn, wr, wo, ln_g, ln_b, wrt, brt, tri)


SC_CHUNK_ROWS = 32


def _gather_rows(table, idx):
    info = plsc.get_sparse_core_info()
    n_cores = info.num_cores
    n_workers = n_cores * info.num_subcores
    n_idx, width = idx.shape[0], table.shape[1]
    per_worker = n_idx // n_workers
    chunk = max(c for c in range(8, SC_CHUNK_ROWS + 1, 8) if per_worker % c == 0)
    assert n_idx % (8 * n_workers) == 0
    mesh = plsc.VectorSubcoreMesh(core_axis_name="c", subcore_axis_name="s")

    def body(table_hbm, idx_hbm, out_hbm, idx_v, rows_v, sem):
        worker = lax.axis_index("s") * n_cores + lax.axis_index("c")
        base = worker * per_worker

        @pl.loop(0, per_worker // chunk)
        def _(c):
            off = pl.multiple_of(base + c * chunk, 8)
            pltpu.sync_copy(idx_hbm.at[pl.ds(off, chunk)], idx_v)
            pltpu.async_copy(table_hbm.at[idx_v], rows_v, sem).wait()
            pltpu.sync_copy(rows_v, out_hbm.at[pl.ds(off, chunk)])

    return pl.kernel(
        body, mesh=mesh,
        out_type=jax.ShapeDtypeStruct((n_idx, width), table.dtype),
        scratch_types=[pltpu.VMEM((chunk,), jnp.int32), pltpu.VMEM((chunk, width), table.dtype),
                       pltpu.SemaphoreType.DMA],
    )(table, idx)


def _scatter_rows(rows, dest, n_slots):
    info = plsc.get_sparse_core_info()
    n_cores = info.num_cores
    n_workers = n_cores * info.num_subcores
    n, width = rows.shape
    n_idx = dest.shape[0]
    per_worker = n_idx // n_workers
    chunk = max(c for c in range(8, SC_CHUNK_ROWS + 1, 8) if per_worker % c == 0)
    assert n_idx % (8 * n_workers) == 0 and n % per_worker == 0
    dest2 = dest.reshape(n_idx // chunk, chunk)
    mesh = plsc.VectorSubcoreMesh(core_axis_name="c", subcore_axis_name="s")

    def body(rows_hbm, dest_hbm, out_hbm, idx_v, rows_v):
        worker = lax.axis_index("s") * n_cores + lax.axis_index("c")
        base = worker * per_worker

        @pl.loop(0, per_worker // chunk)
        def _(c):
            off = base + c * chunk
            pltpu.sync_copy(dest_hbm.at[pl.ds(off // chunk, 1)], idx_v)
            pltpu.sync_copy(rows_hbm.at[pl.ds(pl.multiple_of(lax.rem(off, n), 8), chunk)], rows_v)
            pltpu.sync_copy(rows_v, out_hbm.at[idx_v.at[0]])

    return pl.kernel(
        body, mesh=mesh,
        out_type=jax.ShapeDtypeStruct((n_slots, width), rows.dtype),
        scratch_types=[pltpu.VMEM((1, chunk), jnp.int32), pltpu.VMEM((chunk, width), rows.dtype)],
    )(rows, dest2)


def _expert_kernel(be_ref, nused_ref, valid_ref, x_ref, w1_ref, w3_ref, w2_ref, y_ref):
    i = pl.program_id(0)

    @pl.when(i >= nused_ref[0])
    def _():
        y_ref[...] = jnp.zeros_like(y_ref)

    @pl.when(i < nused_ref[0])
    def _():
        row = lax.broadcasted_iota(jnp.int32, x_ref.shape, 0)
        x = jnp.where(row < valid_ref[i], x_ref[...], 0.0).astype(BF16)
        h = jax.nn.silu(jnp.dot(x, w1_ref[0].astype(BF16), preferred_element_type=F32))
        h = h * jnp.dot(x, w3_ref[0].astype(BF16), preferred_element_type=F32)
        y_ref[...] = jnp.dot(h.astype(BF16), w2_ref[0].astype(BF16), preferred_element_type=F32)


def _experts(x_sorted, block_e, n_used, valid, w1, w3, w2, rows):
    n_blocks = block_e.shape[0]
    xspec = pl.BlockSpec((rows, D_MODEL), lambda i, be, nu, va: (i, 0))
    wspec = lambda a: pl.BlockSpec((1,) + a.shape[1:], lambda i, be, nu, va: (be[i], 0, 0))
    return pl.pallas_call(
        _expert_kernel,
        grid_spec=pltpu.PrefetchScalarGridSpec(
            num_scalar_prefetch=3,
            grid=(n_blocks,),
            in_specs=[xspec, wspec(w1), wspec(w3), wspec(w2)],
            out_specs=xspec,
        ),
        out_shape=jax.ShapeDtypeStruct(x_sorted.shape, F32),
        compiler_params=_cparams(("arbitrary",)),
        name="experts",
    )(block_e, n_used, valid, x_sorted, w1, w3, w2)


def _combine_kernel(x1_ref, y0_ref, y1_ref, route_ref, g_ref, b_ref, o_ref):
    route = route_ref[...]
    ff = y0_ref[...] * route[:, 2:3] + y1_ref[...] * route[:, 3:4]
    o_ref[...] = _layer_norm(DEEPNORM_ALPHA * x1_ref[...] + ff, g_ref[...], b_ref[...])


def _combine(x1, y2, route, ln_g, ln_b, tm):
    n = x1.shape[0]
    row = lambda width: pl.BlockSpec((tm, width), lambda i: (i, 0))
    second = pl.BlockSpec((tm, D_MODEL), lambda i: (i + n // tm, 0))
    vec = pl.BlockSpec((1, D_MODEL), lambda i: (0, 0))
    return pl.pallas_call(
        _combine_kernel,
        grid=(n // tm,),
        in_specs=[row(D_MODEL), row(D_MODEL), second, row(LANES), vec, vec],
        out_specs=row(D_MODEL),
        out_shape=jax.ShapeDtypeStruct((n, D_MODEL), F32),
        compiler_params=_cparams(("parallel",)),
        name="combine_ln2",
    )(x1, y2, y2, route, ln_g.reshape(1, -1), ln_b.reshape(1, -1))


def _finish(o_nsa, bm, gm, x2, mw, ew, ln1, ln2, tm, rows):
    n = x2.shape[0]
    x1, route, counts = _merge(o_nsa, bm, gm, x2, mw, ln1[0], ln1[1], tm)
    sizes = counts[0, ROUTE_E0:ROUTE_E0 + N_EXPERTS].astype(jnp.int32)
    padded = (sizes + rows - 1) // rows * rows
    pad_end = jnp.cumsum(padded)
    pad_start = pad_end - padded
    e = route[:, 0:TOP_K].astype(jnp.int32)
    rank = route[:, 4:4 + TOP_K].astype(jnp.int32)
    dest = (pad_start[e] + rank).T.reshape(-1)
    n_blocks = -(-(n * TOP_K) // rows) + N_EXPERTS
    blk0 = jnp.arange(n_blocks, dtype=jnp.int32) * rows
    block_e = jnp.sum(blk0[:, None] >= pad_end[None, :], axis=1).astype(jnp.int32)
    block_e = jnp.minimum(block_e, N_EXPERTS - 1)
    n_used = (pad_end[-1:] // rows).astype(jnp.int32)
    valid = jnp.clip(sizes[block_e] - (blk0 - pad_start[block_e]), 0, rows).astype(jnp.int32)
    y_sorted = _experts(_scatter_rows(x1, dest, n_blocks * rows), block_e, n_used, valid, *ew, rows)
    return _combine(x1, _gather_rows(y_sorted, dest), route, ln2[0], ln2[1], tm)


PROMPT_ROWS = 256


def _feature_major(a):
    lead = a.shape[:-3]
    return jnp.moveaxis(a, -3, -1).reshape(lead + (2 * HEAD_DIM, a.shape[-3]))


def _row_major(a):
    lead = a.shape[:-2]
    return jnp.moveaxis(a.reshape(lead + (2, HEAD_DIM, a.shape[-1])), -1, -3)


def _pad_rows(a):
    return jnp.pad(a[:, None, :], ((0, 0), (0, HEAD_ROWS - 1), (0, 0)))


def kernel(x_prompt, x_sample, cache_kv_cmp, cache_kv_slc, state_kv_win, state_ret, page_table, w_in, b_in, cmp_pos, w_cmp1, w_cmp2, b_cmp2, ret_gn_g, ret_gn_b, w_nsa_proj, w_ret_proj, w_o, ln1_g, ln1_b, w_route_group, b_route_group, w_route_exp, b_route_exp, w_e1, w_e3, w_e2, ln2_g, ln2_b):
    nb, t_len, _ = x_prompt.shape
    ns, s_len, _ = x_sample.shape
    assert s_len == 1 and ns == LANES and t_len % (RET_STEP_CHUNKS * RET_CHUNK) == 0 and t_len >= WINDOW
    g = NSA_KV_GROUPS
    pw = _proj_weights(w_in, b_in)
    cw = _compress_weights(cmp_pos, w_cmp1, w_cmp2, b_cmp2)
    mw = _merge_weights(w_nsa_proj, w_ret_proj, w_o, w_route_group, b_route_group, w_route_exp, b_route_exp)
    ew = (w_e1, w_e3, w_e2)
    ln1 = (ln1_g, ln1_b)
    ln2 = (ln2_g, ln2_b)

    cos_p, sin_p = _rope_tables(jnp.arange(t_len, dtype=jnp.int32))
    x2 = x_prompt.reshape(nb * t_len, D_MODEL)
    qa, kvt_cmp, kvt_slc, kvt_win, ga, qb, kb, vb, gb, gm = _project(x2, nb, pw, cos_p, sin_p, PROMPT_ROWS)
    kc, kvct = _compress(kvt_cmp, cw)
    kslc, kwin, vslc, vwin = _kv_prep(kvt_slc, kvt_win)
    o_nsa = _nsa_prompt(qa, ga, kc, kvct, kslc, vslc, kwin, vwin)
    bm, ret_p = _ret_prompt(qb, kb, vb, gb, ret_gn_g, ret_gn_b, nb)
    y_prompt = _finish(o_nsa, bm, gm, x2, mw, ew, ln1, ln2, PROMPT_ROWS, MOE_ROWS).reshape(x_prompt.shape)
    kv_cmp_p = _row_major(kvt_cmp)
    kv_slc_p = _row_major(kvt_slc)
    win_p = _row_major(kvt_win[..., t_len - WINDOW:])

    n_pages = page_table.shape[1]
    past_len = n_pages * PAGE_SIZE
    cos_s, sin_s = _rope_tables(jnp.full((ns,), past_len, jnp.int32))
    xs2 = x_sample.reshape(ns, D_MODEL)
    qa, new_cmp, new_slc, new_win, ga, qb, kb, vb, gb, gm = _project(xs2, 1, pw, cos_s, sin_s, ns)
    q8 = jnp.pad(qa.reshape(ns, g, NSA_GROUP_HEADS, HEAD_DIM),
                 ((0, 0), (0, 0), (0, HEAD_ROWS - NSA_GROUP_HEADS), (0, 0))).astype(BF16)
    o_cmp, imp = _compress_paged_attend(_feature_major(cache_kv_cmp), page_table, cw, q8)
    picks = _dec_topk(imp[:, :, 0, :].reshape(ns * g, -1))
    idx = picks[:N_PAST_SEL].T.astype(jnp.int32)
    blocks_per_page = PAGE_SIZE // SEL_BLOCK
    seq = jnp.arange(ns * g, dtype=jnp.int32)[:, None] // g
    phys = page_table[seq, idx // blocks_per_page]
    half = idx % blocks_per_page
    gates = ga.reshape(ns, g, LANES)[:, :, :3 * NSA_GROUP_HEADS].reshape(ns, g, NSA_GROUP_HEADS, 3)
    gates = jnp.pad(gates, ((0, 0), (0, 0), (0, HEAD_ROWS - NSA_GROUP_HEADS), (0, LANES - 3)))
    o_s, win_t = _dec_attn(phys.reshape(-1), half.reshape(-1), q8, gates, o_cmp, _feature_major(cache_kv_slc),
                           new_slc, new_win, _feature_major(state_kv_win))
    o_nsa_s = o_s[:, :, :NSA_GROUP_HEADS].reshape(ns, NSA_HEADS * HEAD_DIM).astype(BF16)
    bm8, ret_s = _ret_step(_pad_rows(qb), _pad_rows(kb), _pad_rows(vb), _pad_rows(gb), state_ret, ret_gn_g, ret_gn_b)
    y_sample = _finish(o_nsa_s, bm8[:, 0], gm, xs2, mw, ew, ln1, ln2, ns, SAMPLE_MOE_ROWS).reshape(x_sample.shape)
    kv_cmp_s = jnp.moveaxis(new_cmp[0], -1, 0).reshape(ns, g, 1, 2, HEAD_DIM)
    kv_slc_s = jnp.moveaxis(new_slc[0], -1, 0).reshape(ns, g, 1, 2, HEAD_DIM)
    win_s = _row_major(win_t)
    return (y_prompt, y_sample, kv_cmp_p, kv_slc_p, win_p, ret_p, kv_cmp_s, kv_slc_s, win_s, ret_s)
```

```python
import functools

import numpy as np
import jax
import jax.numpy as jnp
from jax import lax
from jax.experimental import pallas as pl
from jax.experimental.pallas import tpu as pltpu
from jax.experimental.pallas import tpu_sc as plsc

F32 = jnp.float32
BF16 = jnp.bfloat16

D_MODEL = 1024
DEPTH = 1
PAGE_SIZE = 128
NSA_HEADS = 8
NSA_KV_GROUPS = 2
NSA_GROUP_HEADS = NSA_HEADS // NSA_KV_GROUPS
HEAD_DIM = 64
CMP_BLOCK = 32
CMP_STRIDE = 16
CMP_HIDDEN = 64
SEL_BLOCK = 64
N_SEL = 16
WINDOW = 512
Q_TILE = 256
FORCE_SCORE = 1.0e3
NEG_INF = -1.0e30
RET_HEADS = 4
RET_DK = 128
RET_DV = 128
RET_CHUNK = 128
ROPE_BASE = 10000.0
N_GROUPS = 4
EXPERTS_PER_GROUP = 8
N_EXPERTS = N_GROUPS * EXPERTS_PER_GROUP
TOP_K = 2
EXPERT_FF = 512
LN_EPS = 1e-5
DEEPNORM_ALPHA = (2.0 * DEPTH) ** 0.25

NSA_Q_COLS = NSA_HEADS * HEAD_DIM
NSA_KV_COLS = 3 * 2 * NSA_KV_GROUPS * HEAD_DIM
NSA_GATE_COLS = 3 * NSA_HEADS
RET_QK_COLS = RET_HEADS * RET_DK
RET_V_COLS = RET_HEADS * RET_DV

LANES = 128

P_QA = 0
P_GA = P_QA + NSA_Q_COLS
P_QB = P_GA + NSA_KV_GROUPS * LANES
P_KB = P_QB + RET_QK_COLS
P_VB = P_KB + RET_QK_COLS
P_GB = P_VB + RET_V_COLS
P_GM = P_GB + RET_V_COLS
VMEM_LIMIT = 56 * 1024 * 1024


def _cparams(sem):
    return pltpu.CompilerParams(dimension_semantics=sem, vmem_limit_bytes=VMEM_LIMIT)


def _proj_kernel(x_ref, w_ref, b_ref, wkv_ref, bkv_ref, cos_ref, sin_ref,
                 qa_ref, kcmp_ref, kslc_ref, kwin_ref, ga_ref, qb_ref, kb_ref, vb_ref, gb_ref, gm_ref):
    x = x_ref[...].astype(BF16)

    def cols(c0, n):
        h = jnp.dot(x, w_ref[:, c0:c0 + n], preferred_element_type=F32)
        return h + b_ref[:, c0:c0 + n]

    qa_ref[...] = cols(P_QA, 512) * (HEAD_DIM ** -0.5)
    ga_ref[...] = jax.nn.sigmoid(cols(P_GA, NSA_KV_GROUPS * LANES))
    cos = cos_ref[...]
    sin = sin_ref[...]
    for h in range(RET_HEADS):
        q = cols(P_QB + h * RET_DK, RET_DK)
        qb_ref[:, h * RET_DK:(h + 1) * RET_DK] = (q * cos + pltpu.roll(q, RET_DK // 2, 1) * sin).astype(BF16)
        k = cols(P_KB + h * RET_DK, RET_DK)
        kb_ref[:, h * RET_DK:(h + 1) * RET_DK] = (k * cos + pltpu.roll(k, RET_DK // 2, 1) * sin) * (RET_DK ** -0.5)
    vb_ref[...] = cols(P_VB, 512).astype(BF16)
    gb_ref[...] = cols(P_GB, 512)
    gm_ref[...] = cols(P_GM, 2048)
    kvt = lax.dot_general(wkv_ref[...], x, (((1,), (1,)), ((), ())), preferred_element_type=F32)
    kvt = kvt + bkv_ref[...]
    for br, out_ref in enumerate((kcmp_ref, kslc_ref, kwin_ref)):
        for g in range(NSA_KV_GROUPS):
            i = br * NSA_KV_GROUPS + g
            out_ref[0, g] = kvt[i * LANES:(i + 1) * LANES]


def _proj_weights(w_in, b_in):
    o_kv = NSA_Q_COLS
    o_ga = o_kv + NSA_KV_COLS
    o_qb = o_ga + NSA_GATE_COLS
    per_g = NSA_GATE_COLS // NSA_KV_GROUPS
    ga_cols = []
    for g in range(NSA_KV_GROUPS):
        ga_cols += [np.arange(o_ga + g * per_g, o_ga + (g + 1) * per_g), np.full(LANES - per_g, -1)]
    perm = np.concatenate([
        np.arange(0, NSA_Q_COLS),
        *ga_cols,
        np.arange(o_qb, o_qb + 2 * RET_QK_COLS + 2 * RET_V_COLS + 2 * D_MODEL),
    ])
    keep = jnp.asarray(perm >= 0)
    idx = jnp.asarray(np.maximum(perm, 0))
    w = jnp.where(keep[None, :], w_in[:, idx], 0.0).astype(BF16)
    b = jnp.where(keep, b_in[idx], 0.0)[None, :]
    br, g, kv, d = np.meshgrid(np.arange(3), np.arange(NSA_KV_GROUPS), np.arange(2), np.arange(HEAD_DIM), indexing="ij")
    kv_idx = jnp.asarray((o_kv + br * 256 + kv * 128 + g * 64 + d).reshape(-1))
    wkv = w_in[:, kv_idx].T.astype(BF16)
    bkv = b_in[kv_idx][:, None]
    return w, b, wkv, bkv


def _rope_tables(pos):
    half = RET_DK // 2
    inv = ROPE_BASE ** (-jnp.arange(half, dtype=F32) / half)
    ang = pos.astype(F32)[:, None] * inv[None, :]
    cos, sin = jnp.cos(ang), jnp.sin(ang)
    return jnp.concatenate([cos, cos], -1), jnp.concatenate([-sin, sin], -1)


def _project(x2, n_batch, pw, cos_t, sin_t, tm):
    w, b, wkv, bkv = pw
    n = x2.shape[0]
    t_len = n // n_batch
    nt = t_len // tm
    n_pos_blocks = cos_t.shape[0] // tm
    row = lambda i: (i, 0)
    const = lambda i: (0, 0)
    pos_map = lambda i: (i % n_pos_blocks, 0)
    kvt_shape = jax.ShapeDtypeStruct((n_batch, NSA_KV_GROUPS, LANES, t_len), F32)
    kvt_spec = pl.BlockSpec((1, NSA_KV_GROUPS, LANES, tm), lambda i: (i // nt, 0, 0, i % nt))
    out_shape = (
        jax.ShapeDtypeStruct((n, 512), F32),
        kvt_shape, kvt_shape, kvt_shape,
        jax.ShapeDtypeStruct((n, NSA_KV_GROUPS * LANES), F32),
        jax.ShapeDtypeStruct((n, 512), BF16),
        jax.ShapeDtypeStruct((n, 512), F32),
        jax.ShapeDtypeStruct((n, 512), BF16),
        jax.ShapeDtypeStruct((n, 512), F32),
        jax.ShapeDtypeStruct((n, 2048), F32),
    )
    return pl.pallas_call(
        _proj_kernel,
        grid=(n // tm,),
        in_specs=[
            pl.BlockSpec((tm, D_MODEL), row),
            pl.BlockSpec(w.shape, const),
            pl.BlockSpec(b.shape, const),
            pl.BlockSpec(wkv.shape, const),
            pl.BlockSpec(bkv.shape, const),
            pl.BlockSpec((tm, LANES), pos_map),
            pl.BlockSpec((tm, LANES), pos_map),
        ],
        out_specs=(
            pl.BlockSpec((tm, 512), row),
            kvt_spec, kvt_spec, kvt_spec,
            pl.BlockSpec((tm, NSA_KV_GROUPS * LANES), row),
            pl.BlockSpec((tm, 512), row),
            pl.BlockSpec((tm, 512), row),
            pl.BlockSpec((tm, 512), row),
            pl.BlockSpec((tm, 512), row),
            pl.BlockSpec((tm, 2048), row),
        ),
        out_shape=out_shape,
        compiler_params=_cparams(("parallel",)),
        name="proj",
    )(x2, w, b, wkv, bkv, cos_t, sin_t)


XPOSE_CHUNK = 512


def _compress_rows(xs_ref, wc_ref, w2_ref, b2_ref, pos_ref, wpe_ref, kc_ref, kvct_ref, perm_ref, nj):
    n_blk = 4 * nj
    pe = jnp.dot(pos_ref[...], wpe_ref[...], preferred_element_type=F32)[0:1]
    acc = jnp.zeros((n_blk, 2 * LANES), F32)
    for c2 in range(CMP_STRIDE // 2):
        xa = xs_ref[pl.ds(2 * c2, n_blk, stride=CMP_STRIDE), :]
        xb = xs_ref[pl.ds(2 * c2 + 1, n_blk, stride=CMP_STRIDE), :]
        x2 = jnp.concatenate([xa, xb], axis=1).astype(BF16)
        acc = acc + jnp.dot(x2, wc_ref[c2], preferred_element_type=F32)
    row = lax.broadcasted_iota(jnp.int32, (n_blk, LANES), 0)
    nxt = pltpu.roll(acc[:, LANES:], n_blk - 1, 0)
    second = jnp.where(row == n_blk - 1, 0.0, nxt)
    h = jax.nn.gelu(acc[:, :LANES] + second + pe)
    perm_ref[...] = jnp.dot(h.astype(BF16), w2_ref[...], preferred_element_type=F32) + b2_ref[...]
    lane = lax.broadcasted_iota(jnp.int32, (nj, LANES), 1)
    for rr in range(4):
        kv = perm_ref[pl.ds(rr, nj, stride=4), :]
        kc_ref[rr * nj:(rr + 1) * nj, :] = jnp.where(lane < HEAD_DIM, kv, 0.0).astype(BF16)
        kvct_ref[:, rr * nj:(rr + 1) * nj] = kv.T.astype(BF16)


def _compress_kernel(xt_ref, wc_ref, w2_ref, b2_ref, pos_ref, wpe_ref, kc_ref, kvct_ref, xs_ref, perm_ref, *, length):
    def xpose(i, _):
        t0 = pl.multiple_of(i * XPOSE_CHUNK, XPOSE_CHUNK)
        xs_ref[pl.ds(t0, XPOSE_CHUNK), :] = xt_ref[0, 0, :, pl.ds(t0, XPOSE_CHUNK)].T
        return 0

    lax.fori_loop(0, length // XPOSE_CHUNK, xpose, 0)
    _compress_rows(xs_ref, wc_ref, w2_ref, b2_ref, pos_ref, wpe_ref, kc_ref.at[0, 0], kvct_ref.at[0, 0], perm_ref,
                   length // SEL_BLOCK)


PAGES_PER_STEP = 64


def _compress_paged_kernel(pt_ref, *refs, n_pages, per_step):
    page_refs = refs[:per_step]
    (q_ref, wc_ref, w2_ref, b2_ref, pos_ref, wpe_ref, o_ref, imp_ref,
     xs_ref, perm_ref, kc_ref, kvct_ref) = refs[per_step:]
    s = pl.program_id(1)
    for k in range(per_step):
        r0 = pl.multiple_of((s * per_step + k) * PAGE_SIZE, PAGE_SIZE)
        for g in range(NSA_KV_GROUPS):
            xs_ref[g, pl.ds(r0, PAGE_SIZE), :] = page_refs[k][0, g].T

    @pl.when(s == n_pages // per_step - 1)
    def _():
        nj = n_pages * PAGE_SIZE // SEL_BLOCK
        for g in range(NSA_KV_GROUPS):
            _compress_rows(xs_ref.at[g], wc_ref, w2_ref, b2_ref, pos_ref, wpe_ref, kc_ref, kvct_ref, perm_ref, nj)
            o_ref[0, g], imp_ref[0, g] = _dec_compressed(q_ref[0, g], kvct_ref, nj)


def _compress_paged_attend(cache_t, page_table, cw, q8):
    wc, w2, b2, pos, wpe = cw
    nb, n_pages = page_table.shape
    g = cache_t.shape[1]
    length = n_pages * PAGE_SIZE
    n_rows = length // CMP_STRIDE
    nj = length // SEL_BLOCK
    per_step = min(PAGES_PER_STEP, n_pages)
    steps = n_pages // per_step
    page_spec = lambda k: pl.BlockSpec(
        (1, g, LANES, PAGE_SIZE), lambda b, s, pt: (pt[b * n_pages + s * per_step + k], 0, 0, 0))
    const = lambda a: pl.BlockSpec(a.shape, lambda b, s, pt: (0,) * a.ndim)
    per_seq = lambda *shape: pl.BlockSpec((1, g) + shape, lambda b, s, pt: (b, 0, 0, 0))
    return pl.pallas_call(
        functools.partial(_compress_paged_kernel, n_pages=n_pages, per_step=per_step),
        grid_spec=pltpu.PrefetchScalarGridSpec(
            num_scalar_prefetch=1,
            grid=(nb, steps),
            in_specs=[page_spec(k) for k in range(per_step)] + [per_seq(HEAD_ROWS, HEAD_DIM)]
            + [const(a) for a in (wc, w2, b2, pos, wpe)],
            out_specs=(per_seq(HEAD_ROWS, HEAD_DIM), per_seq(HEAD_ROWS, nj)),
            scratch_shapes=[pltpu.VMEM((g, length, LANES), F32), pltpu.VMEM((n_rows, LANES), F32),
                            pltpu.VMEM((n_rows, LANES), BF16), pltpu.VMEM((LANES, n_rows), BF16)],
        ),
        out_shape=(
            jax.ShapeDtypeStruct((nb, g, HEAD_ROWS, HEAD_DIM), F32),
            jax.ShapeDtypeStruct((nb, g, HEAD_ROWS, nj), F32),
        ),
        compiler_params=_cparams(("parallel", "arbitrary")),
        name="compress_paged",
    )(page_table.reshape(-1), *([cache_t] * per_step), q8, wc, w2, b2, pos, wpe)


def _compress_weights(cmp_pos, w_cmp1, w_cmp2, b_cmp2):
    eye = jnp.eye(2, dtype=F32)
    w5 = w_cmp1.reshape(2, 2, CMP_STRIDE, HEAD_DIM, CMP_HIDDEN)
    wc = jnp.einsum("vhcde,vw->cvdhwe", w5, eye).reshape(CMP_STRIDE // 2, 2 * LANES, 2 * LANES).astype(BF16)
    w2 = jnp.einsum("ved,vw->vewd", w_cmp2, eye).reshape(LANES, LANES).astype(BF16)
    b2 = b_cmp2.reshape(1, LANES)
    wpe = jnp.einsum("vcde,vw->vcdwe", w_cmp1, eye).reshape(2 * CMP_BLOCK * HEAD_DIM, LANES).astype(BF16)
    pos = jnp.broadcast_to(cmp_pos.reshape(1, -1), (8, 2 * CMP_BLOCK * HEAD_DIM)).astype(BF16)
    return wc, w2, b2, pos, wpe


def _compress(kvt, cw):
    wc, w2, b2, pos, wpe = cw
    nb, g, _, length = kvt.shape
    n_rows = length // CMP_STRIDE
    const2 = lambda b, gg: (0, 0)
    return pl.pallas_call(
        functools.partial(_compress_kernel, length=length),
        grid=(nb, g),
        in_specs=[
            pl.BlockSpec((1, 1, LANES, length), lambda b, gg: (b, gg, 0, 0)),
            pl.BlockSpec(wc.shape, lambda b, gg: (0, 0, 0)),
            pl.BlockSpec(w2.shape, const2),
            pl.BlockSpec(b2.shape, const2),
            pl.BlockSpec(pos.shape, const2),
            pl.BlockSpec(wpe.shape, const2),
        ],
        out_specs=(
            pl.BlockSpec((1, 1, n_rows, LANES), lambda b, gg: (b, gg, 0, 0)),
            pl.BlockSpec((1, 1, LANES, n_rows), lambda b, gg: (b, gg, 0, 0)),
        ),
        out_shape=(
            jax.ShapeDtypeStruct((nb, g, n_rows, LANES), BF16),
            jax.ShapeDtypeStruct((nb, g, LANES, n_rows), BF16),
        ),
        scratch_shapes=[pltpu.VMEM((length, LANES), F32), pltpu.VMEM((n_rows, LANES), F32)],
        compiler_params=_cparams(("parallel", "parallel")),
        name="compress",
    )(kvt, wc, w2, b2, pos, wpe)


BIAS_BLOCKS = 16
KEY_TILE = 512
PREP_TILE = 1024
N_COLS = NSA_GROUP_HEADS * Q_TILE
SLC_UNROLL = 4
V_ROWS = HEAD_DIM + 16
WIN_KEYS = WINDOW + Q_TILE
CMP_ROW_CHOICES = (64, 128, 192)


def _kv_prep_kernel(slc_ref, win_ref, kslc_ref, kwin_ref, vslc_ref, vwin_ref):
    shape = (PREP_TILE, LANES)
    lane = lax.broadcasted_iota(jnp.int32, shape, 1)
    blk = (lax.broadcasted_iota(jnp.int32, shape, 0) // SEL_BLOCK) % BIAS_BLOCKS
    onehot = jnp.where(lane - HEAD_DIM == blk, 1.0, 0.0)
    ks = slc_ref[0, 0].T
    kslc_ref[0, 0] = jnp.where(lane < HEAD_DIM, ks, onehot).astype(BF16)
    kw = win_ref[0, 0].T
    kwin_ref[0, 0] = jnp.where(lane < HEAD_DIM, kw, 0.0).astype(BF16)
    ones = jnp.ones((V_ROWS - HEAD_DIM, PREP_TILE), BF16)
    vslc_ref[0, 0] = jnp.concatenate([slc_ref[0, 0, HEAD_DIM:, :].astype(BF16), ones], axis=0)
    vwin_ref[0, 0] = jnp.concatenate([win_ref[0, 0, HEAD_DIM:, :].astype(BF16), ones], axis=0)


def _kv_prep(kvt_slc, kvt_win):
    nb, g, _, t_len = kvt_slc.shape
    in_spec = pl.BlockSpec((1, 1, LANES, PREP_TILE), lambda b, gg, i: (b, gg, 0, i))
    k_spec = pl.BlockSpec((1, 1, PREP_TILE, LANES), lambda b, gg, i: (b, gg, i, 0))
    v_spec = pl.BlockSpec((1, 1, V_ROWS, PREP_TILE), lambda b, gg, i: (b, gg, 0, i))
    k_shape = jax.ShapeDtypeStruct((nb, g, t_len, LANES), BF16)
    v_shape = jax.ShapeDtypeStruct((nb, g, V_ROWS, t_len), BF16)
    return pl.pallas_call(
        _kv_prep_kernel,
        grid=(nb, g, t_len // PREP_TILE),
        in_specs=[in_spec, in_spec],
        out_specs=(k_spec, k_spec, v_spec, v_spec),
        out_shape=(k_shape, k_shape, v_shape, v_shape),
        compiler_params=_cparams(("parallel", "parallel", "parallel")),
        name="kv_prep",
    )(kvt_slc, kvt_win)


def _online_update(carry, s, vt):
    m, acc = carry
    m_new = jnp.maximum(m, jnp.max(s, axis=0, keepdims=True))
    p = jnp.exp(s - m_new)
    acc = jnp.exp(m - m_new) * acc + jnp.dot(vt, p.astype(BF16), preferred_element_type=F32)
    return m_new, acc


def _softmax_out(acc):
    return acc[:HEAD_DIM] / acc[HEAD_DIM:HEAD_DIM + 1]


def _select_blocks(imp, q0):
    jrow = lax.broadcasted_iota(jnp.int32, imp.shape, 0)
    cur = (q0 + lax.broadcasted_iota(jnp.int32, imp.shape, 1)) // SEL_BLOCK
    work = jnp.where(jrow <= cur, imp, -1.0)
    bias = jnp.full(imp.shape, NEG_INF, F32)
    for forced_blk in (0, cur, cur - 1):
        hit = jrow == forced_blk
        bias = jnp.where(hit, 0.0, bias)
        work = jnp.where(hit, -3.0e38, work)
    jf = jrow.astype(F32)
    for _ in range(N_SEL - 3):
        mx = jnp.max(work, axis=0, keepdims=True)
        first = jnp.min(jnp.where(work == mx, jf, 1.0e9), axis=0, keepdims=True)
        hit = jf == first
        bias = jnp.where(hit, 0.0, bias)
        work = jnp.where(hit, -3.0e38, work)
    return bias


def _nsa_prompt_kernel(qa_ref, ga_ref, kc_ref, kvct_ref, kslc_ref, vslc_ref, kwin_ref, vwin_ref,
                       o_ref, qm_ref, bias_ref, ocmp_ref, mtail_ref, acctail_ref, *, nj):
    qi = pl.program_id(2)

    @pl.when(qi == 0)
    def _():
        qm_ref[...] = jnp.zeros_like(qm_ref)

    q_t = qa_ref[...].T
    for r in range(NSA_GROUP_HEADS):
        qm_ref[0:HEAD_DIM, r * Q_TILE:(r + 1) * Q_TILE] = q_t[r * HEAD_DIM:(r + 1) * HEAD_DIM].astype(BF16)
    ga_t = ga_ref[...].T

    def gate(branch):
        return jnp.concatenate([ga_t[r * 3 + branch:r * 3 + branch + 1] for r in range(NSA_GROUP_HEADS)], axis=1)

    q0 = qi * Q_TILE
    qpos_c = q0 + (lax.broadcasted_iota(jnp.int32, (1, N_COLS), 1) & (Q_TILE - 1))

    def compressed_and_select(n):
        kc = jnp.concatenate([kc_ref[0, 0, rr * nj:rr * nj + n, :] for rr in range(4)], axis=0)
        vct = jnp.concatenate([kvct_ref[0, 0, HEAD_DIM:, rr * nj:rr * nj + n] for rr in range(4)], axis=1)
        s = jnp.dot(kc, qm_ref[...], preferred_element_type=F32)
        j_end = lax.broadcasted_iota(jnp.int32, (n, N_COLS), 0) * SEL_BLOCK + (CMP_BLOCK - 1)
        cmp_end = jnp.concatenate([j_end + rr * CMP_STRIDE for rr in range(4)], axis=0)
        vis = cmp_end <= qpos_c
        s = jnp.where(vis, s, NEG_INF)
        m = jnp.max(s, axis=0, keepdims=True)
        p = jnp.where(vis, jnp.exp(s - m), 0.0)
        l = jnp.sum(p, axis=0, keepdims=True)
        pn = p / jnp.where(l > 0.0, l, 1.0)
        ocmp_ref[...] = jnp.dot(vct, pn.astype(BF16), preferred_element_type=F32)
        p3 = pn[3 * n:4 * n]
        jrow4 = lax.broadcasted_iota(jnp.int32, (n, N_COLS), 0)
        prev = jnp.where(jrow4 == 0, 0.0, pltpu.roll(p3, 1, 0))
        a = pn[0:n] + pn[n:2 * n] + pn[2 * n:3 * n] + p3 + prev
        imp = a[:, 0:Q_TILE]
        for r in range(1, NSA_GROUP_HEADS):
            imp = imp + a[:, r * Q_TILE:(r + 1) * Q_TILE]
        bias16 = _select_blocks(imp, q0).astype(BF16)
        for r in range(NSA_GROUP_HEADS):
            bias_ref[0:n, r * Q_TILE:(r + 1) * Q_TILE] = bias16

    sizes = [n for n in CMP_ROW_CHOICES if n < nj] + [nj]
    need = (q0 + Q_TILE) // SEL_BLOCK
    for v, n in enumerate(sizes):
        lo = sizes[v - 1] if v else 0
        pl.when((need > lo) & ((need <= n) | (v == len(sizes) - 1)))(functools.partial(compressed_and_select, n))
    o_cmp = ocmp_ref[...]


    init =(jnp.full((1, N_COLS), -jnp.inf, F32), jnp.zeros((V_ROWS, N_COLS), F32))
    q_rows = qm_ref[0:HEAD_DIM, :]
    pad_rows = jnp.zeros((LANES - HEAD_DIM - BIAS_BLOCKS, N_COLS), BF16)

    def slc_scores(st):
        b0 = pl.multiple_of((st * KEY_TILE // (SEL_BLOCK * BIAS_BLOCKS)) * BIAS_BLOCKS, BIAS_BLOCKS)
        qm = jnp.concatenate([q_rows, bias_ref[pl.ds(b0, BIAS_BLOCKS), :], pad_rows], axis=0)
        k0 = pl.multiple_of(st * KEY_TILE, KEY_TILE)
        return jnp.dot(kslc_ref[0, 0, pl.ds(k0, KEY_TILE), :], qm, preferred_element_type=F32)

    def slc_absorb(st, carry, sc):
        k0 = pl.multiple_of(st * KEY_TILE, KEY_TILE)
        return _online_update(carry, sc, vslc_ref[0, 0, :, pl.ds(k0, KEY_TILE)])

    def slc_group(i, carry):
        st0 = i * SLC_UNROLL
        sc = slc_scores(st0)
        for u in range(SLC_UNROLL):
            sc_next = slc_scores(st0 + u + 1) if u + 1 < SLC_UNROLL else None
            carry = slc_absorb(st0 + u, carry, sc)
            sc = sc_next
        return carry

    n_tiles = q0 // KEY_TILE + 1
    tail = (n_tiles - 1) % SLC_UNROLL + 1
    m_s, acc_s = lax.fori_loop(0, (n_tiles - tail) // SLC_UNROLL, slc_group, init)
    mtail_ref[...] = m_s
    acctail_ref[...] = acc_s

    def slc_tail(size):
        st0 = n_tiles - size
        carry = (mtail_ref[...], acctail_ref[...])
        sc = slc_scores(st0)
        for u in range(size):
            sc_next = slc_scores(st0 + u + 1) if u + 1 < size else None
            if u == size - 1:
                kpos = (st0 + u) * KEY_TILE + lax.broadcasted_iota(jnp.int32, (KEY_TILE, N_COLS), 0)
                sc = jnp.where(kpos <= qpos_c, sc, NEG_INF)
            carry = slc_absorb(st0 + u, carry, sc)
            sc = sc_next
        acctail_ref[...] = carry[1]

    for size in range(1, SLC_UNROLL + 1):
        pl.when(tail == size)(functools.partial(slc_tail, size))
    o_slc = _softmax_out(acctail_ref[...])

    kw0 = pl.multiple_of(jnp.maximum(q0 - WINDOW, 0), Q_TILE)
    sc = jnp.dot(kwin_ref[0, 0, pl.ds(kw0, WIN_KEYS), :], qm_ref[...], preferred_element_type=F32)
    dpos = qpos_c - (kw0 + lax.broadcasted_iota(jnp.int32, (WIN_KEYS, N_COLS), 0))
    sc = jnp.where(dpos >= 0, jnp.where(dpos < WINDOW, sc, NEG_INF), NEG_INF)
    p = jnp.exp(sc - jnp.max(sc, axis=0, keepdims=True))
    o_win = _softmax_out(jnp.dot(vwin_ref[0, 0, :, pl.ds(kw0, WIN_KEYS)], p.astype(BF16), preferred_element_type=F32))

    o = gate(0) * o_cmp + gate(1) * o_slc + gate(2) * o_win
    o_rows = jnp.concatenate([o[:, r * Q_TILE:(r + 1) * Q_TILE] for r in range(NSA_GROUP_HEADS)], axis=0)
    o_ref[...] = o_rows.T.astype(BF16)


def _nsa_prompt(qa, ga, kc, kvct, kslc, vslc, kwin, vwin):
    nb, g, t_len, _ = kslc.shape
    nq = t_len // Q_TILE
    nj = t_len // SEL_BLOCK
    n_cmp_rows = kc.shape[2]
    assert nj % BIAS_BLOCKS == 0 and n_cmp_rows == 4 * nj and t_len % KEY_TILE == 0 and t_len >= WIN_KEYS
    tile = lambda b, gg, i: (b * nq + i, gg)
    whole = lambda b, gg, i: (b, gg, 0, 0)
    return pl.pallas_call(
        functools.partial(_nsa_prompt_kernel, nj=nj),
        grid=(nb, g, nq),
        in_specs=[
            pl.BlockSpec((Q_TILE, NSA_GROUP_HEADS * HEAD_DIM), tile),
            pl.BlockSpec((Q_TILE, LANES), tile),
            pl.BlockSpec((1, 1, n_cmp_rows, LANES), whole),
            pl.BlockSpec((1, 1, LANES, n_cmp_rows), whole),
            pl.BlockSpec((1, 1, t_len, LANES), whole),
            pl.BlockSpec((1, 1, V_ROWS, t_len), whole),
            pl.BlockSpec((1, 1, t_len, LANES), whole),
            pl.BlockSpec((1, 1, V_ROWS, t_len), whole),
        ],
        out_specs=pl.BlockSpec((Q_TILE, NSA_GROUP_HEADS * HEAD_DIM), tile),
        out_shape=jax.ShapeDtypeStruct((nb * t_len, NSA_HEADS * HEAD_DIM), BF16),
        scratch_shapes=[pltpu.VMEM((LANES, N_COLS), BF16), pltpu.VMEM((nj, N_COLS), BF16),
                        pltpu.VMEM((HEAD_DIM, N_COLS), F32), pltpu.VMEM((1, N_COLS), F32),
                        pltpu.VMEM((V_ROWS, N_COLS), F32)],
        compiler_params=_cparams(("parallel", "parallel", "arbitrary")),
        name="nsa_prompt",
    )(qa, ga, kc, kvct, kslc, vslc, kwin, vwin)


HEAD_ROWS = 8
N_PAST_SEL = N_SEL - 1


def _softmax_lanes(s_list, mask_list):
    s_list = [jnp.where(mk, s, NEG_INF) for s, mk in zip(s_list, mask_list)]
    m = s_list[0].max(axis=-1, keepdims=True)
    for s in s_list[1:]:
        m = jnp.maximum(m, s.max(axis=-1, keepdims=True))
    p_list = [jnp.where(mk, jnp.exp(s - m), 0.0) for s, mk in zip(s_list, mask_list)]
    l = p_list[0].sum(axis=-1, keepdims=True)
    for p in p_list[1:]:
        l = l + p.sum(axis=-1, keepdims=True)
    inv = 1.0 / jnp.where(l > 0.0, l, 1.0)
    return [p * inv for p in p_list]


def _dot_nt(a, b):
    return lax.dot_general(a, b, (((1,), (1,)), ((), ())), preferred_element_type=F32)


def _dec_compressed(q, kvct_ref, nj):
    n = 4 * nj
    s = jnp.dot(q, kvct_ref[:HEAD_DIM, :], preferred_element_type=F32)
    lane = lax.broadcasted_iota(jnp.int32, (HEAD_ROWS, n), 1)
    blk = 4 * (lane % nj) + lane // nj
    (pn,) = _softmax_lanes([s], [blk <= n - 2])
    out = _dot_nt(pn.astype(BF16), kvct_ref[HEAD_DIM:, :])
    row = lax.broadcasted_iota(jnp.int32, (HEAD_ROWS, n), 0)
    ph = jnp.sum(jnp.where(row < NSA_GROUP_HEADS, pn, 0.0), axis=0, keepdims=True)
    p3 = ph[:, 3 * nj:]
    prev = jnp.where(lane[0:1, :nj] == 0, 0.0, pltpu.roll(p3, 1, 1))
    imp = ph[:, :nj] + ph[:, nj:2 * nj] + ph[:, 2 * nj:3 * nj] + p3 + prev
    return out, jnp.broadcast_to(imp, (HEAD_ROWS, nj))


def _dec_topk_kernel(imp_ref, idx_ref):
    work = imp_ref[...].T
    nj, n = work.shape
    jrow = lax.broadcasted_iota(jnp.int32, (nj, n), 0)
    jf = jrow.astype(F32)
    for forced_blk in (0, nj - 1):
        work = jnp.where(jrow == forced_blk, FORCE_SCORE, work)
    picks = []
    for _ in range(N_PAST_SEL):
        mx = jnp.max(work, axis=0, keepdims=True)
        first = jnp.min(jnp.where(work == mx, jf, 1.0e9), axis=0, keepdims=True)
        picks.append(first)
        work = jnp.where(jf == first, -3.0e38, work)
    picks.append(jnp.full((1, n), float(nj), F32))
    idx_ref[...] = jnp.concatenate(picks, axis=0)


def _dec_topk(imp2):
    n, nj = imp2.shape
    return pl.pallas_call(
        _dec_topk_kernel,
        out_shape=jax.ShapeDtypeStruct((N_SEL, n), F32),
        compiler_params=pltpu.CompilerParams(vmem_limit_bytes=VMEM_LIMIT),
        name="dec_topk",
    )(imp2)


def _dec_attn_kernel(phys_ref, half_ref, q_ref, gate_ref, ocmp_ref, *refs):
    page_refs = refs[:N_PAST_SEL]
    slcnew_ref, winnew_ref, win_ref, o_ref, winout_ref = refs[N_PAST_SEL:]
    b = pl.program_id(0)
    base = (b * pl.num_programs(1) + pl.program_id(1)) * N_PAST_SEL
    q = q_ref[0, 0]
    lane = lax.broadcasted_iota(jnp.int32, (HEAD_ROWS, LANES), 1)
    own = lane == b

    def attend(kv_blocks, masks):
        scores = [jnp.dot(q, kv[:HEAD_DIM].astype(BF16), preferred_element_type=F32) for kv in kv_blocks]
        probs = _softmax_lanes(scores, masks)
        out = jnp.zeros((HEAD_ROWS, HEAD_DIM), F32)
        for p, kv in zip(probs, kv_blocks):
            out = out + _dot_nt(p.astype(BF16), kv[HEAD_DIM:].astype(BF16))
        return out

    pages = [r[0, 0] for r in page_refs] + [slcnew_ref[0, 0]]
    masks = [(lane // SEL_BLOCK) == half_ref[base + k] for k in range(N_PAST_SEL)] + [own]
    o_slc = attend(pages, masks)
    st = win_ref[0, 0]
    win_c = st.shape[1]
    wlane = lax.broadcasted_iota(jnp.int32, (HEAD_ROWS, win_c), 1)
    o_win = attend([st, winnew_ref[0, 0]], [wlane > win_c - WINDOW, own])
    gates = gate_ref[0, 0]
    o_ref[0, 0] = gates[:, 0:1] * ocmp_ref[0, 0] + gates[:, 1:2] * o_slc + gates[:, 2:3] * o_win
    new = winnew_ref[0, 0]
    lane_full = lax.broadcasted_iota(jnp.int32, new.shape, 1)
    new_col = jnp.sum(jnp.where(lane_full == b, new, 0.0), axis=1, keepdims=True)
    col = lax.broadcasted_iota(jnp.int32, st.shape, 1)
    winout_ref[0, 0] = jnp.where(col == win_c - 1, new_col, pltpu.roll(st, win_c - 1, 1))


def _dec_attn(phys, half, q8, gates, o_cmp, slc_t, slc_new, win_new, win_t):
    nb, g, _, win_c = win_t.shape
    assert slc_new.shape[-1] == nb == LANES
    blk = lambda *shape: pl.BlockSpec((1, 1) + shape, lambda b, gg, ph, hf: (b, gg, 0, 0))
    page = lambda k: pl.BlockSpec((1, 1, LANES, PAGE_SIZE),
                                  lambda b, gg, ph, hf: (ph[(b * g + gg) * N_PAST_SEL + k], gg, 0, 0))
    new = pl.BlockSpec((1, 1, LANES, nb), lambda b, gg, ph, hf: (0, gg, 0, 0))
    return pl.pallas_call(
        _dec_attn_kernel,
        grid_spec=pltpu.PrefetchScalarGridSpec(
            num_scalar_prefetch=2,
            grid=(nb, g),
            in_specs=[blk(HEAD_ROWS, HEAD_DIM), blk(HEAD_ROWS, LANES), blk(HEAD_ROWS, HEAD_DIM)]
            + [page(k) for k in range(N_PAST_SEL)] + [new, new, blk(LANES, win_c)],
            out_specs=(blk(HEAD_ROWS, HEAD_DIM), blk(LANES, win_c)),
        ),
        out_shape=(jax.ShapeDtypeStruct((nb, g, HEAD_ROWS, HEAD_DIM), F32),
                   jax.ShapeDtypeStruct((nb, g, LANES, win_c), F32)),
        compiler_params=_cparams(("parallel", "parallel")),
        name="dec_attn",
    )(phys, half, q8, gates, o_cmp, *([slc_t] * N_PAST_SEL), slc_new, win_new, win_t)


RET_STEP_CHUNKS = 8


def _log_gamma():
    return jnp.log1p(-jnp.exp2(-5.0 - jnp.arange(RET_HEADS, dtype=F32)))


def _ret_tables(chunk):
    lg = _log_gamma()
    n = jnp.arange(chunk, dtype=F32)
    diff = n[:, None] - n[None, :]
    decay = jnp.exp(jnp.maximum(diff, 0.0)[None] * lg[:, None, None]) * (diff >= 0)[None]
    rep = lambda col: jnp.broadcast_to(col[:, :, None], (RET_HEADS, chunk, RET_DV))
    qdec = rep(jnp.exp((n + 1.0)[None, :] * lg[:, None]))
    kdec = rep(jnp.exp((chunk - 1.0 - n)[None, :] * lg[:, None]))
    sdec = jnp.broadcast_to(jnp.exp(chunk * lg)[:, None, None], (RET_HEADS, RET_DK, RET_DV))
    return decay.astype(F32), qdec, kdec, sdec


def _group_norm_gate(o, g, gn_g, gn_b):
    mu = jnp.mean(o, axis=-1, keepdims=True)
    var = jnp.mean(jnp.square(o - mu), axis=-1, keepdims=True)
    on = (o - mu) * lax.rsqrt(var + LN_EPS) * gn_g + gn_b
    return (jax.nn.silu(g) * on).astype(BF16)


def _ret_prompt_kernel(qb_ref, kb_ref, vb_ref, gb_ref, decay_ref, qdec_ref, kdec_ref, sdec_ref, gng_ref, gnb_ref,
                       bm_ref, state_ref, st_ref):
    @pl.when(pl.program_id(0) == 0)
    def _():
        st_ref[...] = jnp.zeros_like(st_ref)

    def chunk_step(c, _):
        rows = pl.ds(pl.multiple_of(c * RET_CHUNK, RET_CHUNK), RET_CHUNK)
        for b in range(qb_ref.shape[0]):
            for h in range(RET_HEADS):
                cols = slice(h * RET_DK, (h + 1) * RET_DK)
                q = qb_ref[b, rows, cols]
                k = kb_ref[b, rows, cols]
                v = vb_ref[b, rows, cols]
                st = st_ref[b, h]
                inner = lax.dot_general(q, k.astype(BF16), (((1,), (1,)), ((), ())), preferred_element_type=F32)
                inner = inner * decay_ref[h]
                o = jnp.dot(inner.astype(BF16), v, preferred_element_type=F32)
                o = o + jnp.dot(q, st.astype(BF16), preferred_element_type=F32) * qdec_ref[h]
                kd = (k * kdec_ref[h]).astype(BF16)
                st_ref[b, h] = sdec_ref[h] * st + lax.dot_general(kd, v, (((0,), (0,)), ((), ())),
                                                                  preferred_element_type=F32)
                bm_ref[b, rows, cols] = _group_norm_gate(o, gb_ref[b, rows, cols], gng_ref[:, cols], gnb_ref[:, cols])
        return 0

    lax.fori_loop(0, RET_STEP_CHUNKS, chunk_step, 0)
    state_ref[...] = st_ref[...]


def _ret_prompt(qb, kb, vb, gb, gn_g, gn_b, nb):
    n, width = qb.shape
    t_len = n // nb
    tt = RET_STEP_CHUNKS * RET_CHUNK
    decay, qdec, kdec, sdec = _ret_tables(RET_CHUNK)
    seqs = lambda a: a.reshape(nb, t_len, width)
    row = pl.BlockSpec((nb, tt, width), lambda i: (0, i, 0))
    tab = pl.BlockSpec((RET_HEADS, RET_CHUNK, RET_DV), lambda i: (0, 0, 0))
    vec = pl.BlockSpec((1, RET_HEADS * RET_DV), lambda i: (0, 0))
    state_shape = (nb, RET_HEADS, RET_DK, RET_DV)
    bm, state = pl.pallas_call(
        _ret_prompt_kernel,
        grid=(t_len // tt,),
        in_specs=[row, row, row, row, tab, tab, tab, tab, vec, vec],
        out_specs=(row, pl.BlockSpec(state_shape, lambda i: (0, 0, 0, 0))),
        out_shape=(jax.ShapeDtypeStruct((nb, t_len, width), BF16), jax.ShapeDtypeStruct(state_shape, F32)),
        scratch_shapes=[pltpu.VMEM(state_shape, F32)],
        compiler_params=_cparams(("arbitrary",)),
        name="ret_prompt",
    )(seqs(qb), seqs(kb), seqs(vb), seqs(gb), decay, qdec, kdec, sdec, gn_g.reshape(1, -1), gn_b.reshape(1, -1))
    return bm.reshape(n, width), state


def _ret_step_kernel(q_ref, k_ref, v_ref, g_ref, st_ref, qdec_ref, sdec_ref, gng_ref, gnb_ref, bm_ref, stout_ref):
    for s in range(q_ref.shape[0]):
        for h in range(RET_HEADS):
            cols = slice(h * RET_DK, (h + 1) * RET_DK)
            q = q_ref[s, :, cols]
            k = k_ref[s, :, cols].astype(BF16)
            v = v_ref[s, :, cols]
            st = st_ref[s, h]
            qk = jnp.sum(q.astype(F32) * k.astype(F32), axis=-1, keepdims=True)
            o = qk.astype(BF16).astype(F32) * v.astype(F32)
            o = o + jnp.dot(q, st.astype(BF16), preferred_element_type=F32) * qdec_ref[h]
            stout_ref[s, h] = sdec_ref[h] * st + lax.dot_general(k, v, (((0,), (0,)), ((), ())),
                                                                 preferred_element_type=F32)
            bm_ref[s, :, cols] = _group_norm_gate(o, g_ref[s, :, cols], gng_ref[:, cols], gnb_ref[:, cols])


RET_STEP_SEQS = 4


def _ret_step(q8, k8, v8, g8, state, gn_g, gn_b):
    nb = state.shape[0]
    ns = RET_STEP_SEQS if nb % RET_STEP_SEQS == 0 else 1
    _, qdec, _, sdec = _ret_tables(1)
    tok = pl.BlockSpec((ns, HEAD_ROWS, RET_HEADS * RET_DK), lambda b: (b, 0, 0))
    st = pl.BlockSpec((ns, RET_HEADS, RET_DK, RET_DV), lambda b: (b, 0, 0, 0))
    vec = pl.BlockSpec((1, RET_HEADS * RET_DV), lambda b: (0, 0))
    return pl.pallas_call(
        _ret_step_kernel,
        grid=(nb // ns,),
        in_specs=[tok, tok, tok, tok, st,
                  pl.BlockSpec((RET_HEADS, 1, RET_DV), lambda b: (0, 0, 0)),
                  pl.BlockSpec((RET_HEADS, RET_DK, RET_DV), lambda b: (0, 0, 0)), vec, vec],
        out_specs=(tok, st),
        out_shape=(jax.ShapeDtypeStruct((nb, HEAD_ROWS, RET_HEADS * RET_DV), BF16),
                   jax.ShapeDtypeStruct(state.shape, F32)),
        compiler_params=_cparams(("parallel",)),
        name="ret_step",
    )(q8, k8, v8, g8, state, qdec, sdec, gn_g.reshape(1, -1), gn_b.reshape(1, -1))


ROUTE_E0 = N_GROUPS
MOE_ROWS = 256
SAMPLE_MOE_ROWS = 32


def _layer_norm(v, g, b):
    mu = jnp.mean(v, axis=-1, keepdims=True)
    var = jnp.mean(jnp.square(v - mu), axis=-1, keepdims=True)
    return (v - mu) * lax.rsqrt(var + LN_EPS) * g + b


def _first_lane(hit, lane):
    return jnp.min(jnp.where(hit, lane, 4 * LANES), axis=-1, keepdims=True)


def _merge_kernel(on_ref, bm_ref, gm_ref, x_ref, wn_ref, wr_ref, wo_ref, g1_ref, b1_ref, wrt_ref, brt_ref, tri_ref,
                  x1_ref, route_ref, counts_ref, carry_ref):
    @pl.when(pl.program_id(0) == 0)
    def _():
        carry_ref[...] = jnp.zeros_like(carry_ref)

    gate_a = jax.nn.sigmoid(gm_ref[:, :D_MODEL])
    gate_b = jax.nn.sigmoid(gm_ref[:, D_MODEL:])
    m = gate_a * jnp.dot(on_ref[...], wn_ref[...], preferred_element_type=F32)
    m = m + gate_b * jnp.dot(bm_ref[...], wr_ref[...], preferred_element_type=F32)
    mix = jnp.dot(m.astype(BF16), wo_ref[...], preferred_element_type=F32)
    x1 = _layer_norm(DEEPNORM_ALPHA * x_ref[...] + mix, g1_ref[...], b1_ref[...])
    x1_ref[...] = x1

    x_hi = x1.astype(BF16)
    x_lo = (x1 - x_hi.astype(F32)).astype(BF16)
    hi_part = jnp.dot(x_hi, wrt_ref[...], preferred_element_type=F32)
    lg = hi_part[:, :LANES] + hi_part[:, LANES:] + brt_ref[...]
    lg = lg + jnp.dot(x_lo, wrt_ref[:, :LANES], preferred_element_type=F32)
    lane = lax.broadcasted_iota(jnp.int32, lg.shape, 1)
    gl = jnp.where(lane < N_GROUPS, lg, -jnp.inf)
    ge = jnp.exp(gl - jnp.max(gl, axis=-1, keepdims=True))
    pgrp = ge / jnp.sum(ge, axis=-1, keepdims=True)
    pg = jnp.max(pgrp, axis=-1, keepdims=True)
    grp = _first_lane(pgrp == pg, lane)
    lo = ROUTE_E0 + grp * EXPERTS_PER_GROUP
    in_g = (lane >= lo) & (lane < lo + EXPERTS_PER_GROUP)
    el = jnp.where(in_g, lg, -jnp.inf)
    ee = jnp.exp(el - jnp.max(el, axis=-1, keepdims=True))
    pc = jnp.where(in_g, ee / jnp.sum(ee, axis=-1, keepdims=True), -1.0)
    w1 = jnp.max(pc, axis=-1, keepdims=True)
    i1 = _first_lane(pc == w1, lane)
    pc2 = jnp.where(lane == i1, -1.0, pc)
    w2 = jnp.max(pc2, axis=-1, keepdims=True)
    i2 = _first_lane(pc2 == w2, lane)
    wsum = w1 + w2
    wt1 = pg * w1 / wsum
    wt2 = pg * w2 / wsum
    hit1 = lane == i1
    hit2 = lane == i2
    cnt = jnp.where(hit1, 1.0, 0.0) + jnp.where(hit2, 1.0, 0.0)
    base = carry_ref[0:1, :] + jnp.dot(tri_ref[...], cnt.astype(BF16), preferred_element_type=F32)
    rank1 = jnp.sum(jnp.where(hit1, base, 0.0), axis=-1, keepdims=True)
    rank2 = jnp.sum(jnp.where(hit2, base, 0.0), axis=-1, keepdims=True)
    carry = carry_ref[0:1, :] + jnp.sum(cnt, axis=0, keepdims=True)
    carry_ref[...] = jnp.broadcast_to(carry, carry_ref.shape)
    counts_ref[...] = jnp.broadcast_to(carry, counts_ref.shape)
    fields = ((i1 - ROUTE_E0).astype(F32), (i2 - ROUTE_E0).astype(F32), wt1, wt2, rank1, rank2)
    out = jnp.zeros(lg.shape, F32)
    for idx, val in enumerate(fields):
        out = jnp.where(lane == idx, val, out)
    route_ref[...] = out


def _merge_weights(w_nsa_proj, w_ret_proj, w_o, w_route_group, b_route_group, w_route_exp, b_route_exp):
    pad = LANES - N_GROUPS - N_EXPERTS
    wrt = jnp.concatenate([w_route_group, w_route_exp, jnp.zeros((D_MODEL, pad), F32)], axis=1)
    w_hi = wrt.astype(BF16)
    w_lo = (wrt - w_hi.astype(F32)).astype(BF16)
    wrt = jnp.concatenate([w_hi, w_lo], axis=1)
    brt = jnp.concatenate([b_route_group, b_route_exp, jnp.zeros((pad,), F32)])[None, :]
    return w_nsa_proj.astype(BF16), w_ret_proj.astype(BF16), w_o.astype(BF16), wrt, brt


def _merge(o_nsa, bm, gm, x2, mw, ln_g, ln_b, tm):
    wn, wr, wo, wrt, brt = mw
    n = x2.shape[0]
    tri = (jnp.arange(tm)[:, None] > jnp.arange(tm)[None, :]).astype(BF16)
    row = lambda width: pl.BlockSpec((tm, width), lambda i: (i, 0))
    const = lambda a: pl.BlockSpec(a.shape, lambda i: (0, 0))
    ln_g = ln_g.reshape(1, -1)
    ln_b = ln_b.reshape(1, -1)
    return pl.pallas_call(
        _merge_kernel,
        grid=(n // tm,),
        in_specs=[row(512), row(512), row(2 * D_MODEL), row(D_MODEL), const(wn), const(wr), const(wo),
                  const(ln_g), const(ln_b), const(wrt), const(brt), const(tri)],
        out_specs=(row(D_MODEL), row(LANES), pl.BlockSpec((8, LANES), lambda i: (0, 0))),
        out_shape=(jax.ShapeDtypeStruct((n, D_MODEL), F32), jax.ShapeDtypeStruct((n, LANES), F32),
                   jax.ShapeDtypeStruct((8, LANES), F32)),
        scratch_shapes=[pltpu.VMEM((8, LANES), F32)],
        compiler_params=_cparams(("arbitrary",)),
        name="merge_route",
    )(o_nsa, bm, gm, x2, wn, wr, wo, ln_g, ln_b, wrt, brt, tri)


SC_CHUNK_ROWS = 32


def _gather_rows(table, idx):
    info = plsc.get_sparse_core_info()
    n_cores = info.num_cores
    n_workers = n_cores * info.num_subcores
    n_idx, width = idx.shape[0], table.shape[1]
    per_worker = n_idx // n_workers
    chunk = max(c for c in range(8, SC_CHUNK_ROWS + 1, 8) if per_worker % c == 0)
    assert n_idx % (8 * n_workers) == 0
    mesh = plsc.VectorSubcoreMesh(core_axis_name="c", subcore_axis_name="s")

    def body(table_hbm, idx_hbm, out_hbm, idx_v, rows_v, sem):
        worker = lax.axis_index("s") * n_cores + lax.axis_index("c")
        base = worker * per_worker

        @pl.loop(0, per_worker // chunk)
        def _(c):
            off = pl.multiple_of(base + c * chunk, 8)
            pltpu.sync_copy(idx_hbm.at[pl.ds(off, chunk)], idx_v)
            pltpu.async_copy(table_hbm.at[idx_v], rows_v, sem).wait()
            pltpu.sync_copy(rows_v, out_hbm.at[pl.ds(off, chunk)])

    return pl.kernel(
        body, mesh=mesh,
        out_type=jax.ShapeDtypeStruct((n_idx, width), table.dtype),
        scratch_types=[pltpu.VMEM((chunk,), jnp.int32), pltpu.VMEM((chunk, width), table.dtype),
                       pltpu.SemaphoreType.DMA],
    )(table, idx)


def _scatter_rows(rows, dest, n_slots):
    info = plsc.get_sparse_core_info()
    n_cores = info.num_cores
    n_workers = n_cores * info.num_subcores
    n, width = rows.shape
    n_idx = dest.shape[0]
    per_worker = n_idx // n_workers
    chunk = max(c for c in range(8, SC_CHUNK_ROWS + 1, 8) if per_worker % c == 0)
    assert n_idx % (8 * n_workers) == 0 and n % per_worker == 0
    dest2 = dest.reshape(n_idx // chunk, chunk)
    mesh = plsc.VectorSubcoreMesh(core_axis_name="c", subcore_axis_name="s")

    def body(rows_hbm, dest_hbm, out_hbm, idx_v, rows_v):
        worker = lax.axis_index("s") * n_cores + lax.axis_index("c")
        base = worker * per_worker

        @pl.loop(0, per_worker // chunk)
        def _(c):
            off = base + c * chunk
            pltpu.sync_copy(dest_hbm.at[pl.ds(off // chunk, 1)], idx_v)
            pltpu.sync_copy(rows_hbm.at[pl.ds(pl.multiple_of(lax.rem(off, n), 8), chunk)], rows_v)
            pltpu.sync_copy(rows_v, out_hbm.at[idx_v.at[0]])

    return pl.kernel(
        body, mesh=mesh,
        out_type=jax.ShapeDtypeStruct((n_slots, width), rows.dtype),
        scratch_types=[pltpu.VMEM((1, chunk), jnp.int32), pltpu.VMEM((chunk, width), rows.dtype)],
    )(rows, dest2)


def _expert_kernel(be_ref, nused_ref, valid_ref, x_ref, w1_ref, w3_ref, w2_ref, y_ref):
    i = pl.program_id(0)

    @pl.when(i >= nused_ref[0])
    def _():
        y_ref[...] = jnp.zeros_like(y_ref)

    @pl.when(i < nused_ref[0])
    def _():
        row = lax.broadcasted_iota(jnp.int32, x_ref.shape, 0)
        x = jnp.where(row < valid_ref[i], x_ref[...], 0.0).astype(BF16)
        h = jax.nn.silu(jnp.dot(x, w1_ref[0].astype(BF16), preferred_element_type=F32))
        h = h * jnp.dot(x, w3_ref[0].astype(BF16), preferred_element_type=F32)
        y_ref[...] = jnp.dot(h.astype(BF16), w2_ref[0].astype(BF16), preferred_element_type=F32)


def _experts(x_sorted, block_e, n_used, valid, w1, w3, w2, rows):
    n_blocks = block_e.shape[0]
    xspec = pl.BlockSpec((rows, D_MODEL), lambda i, be, nu, va: (i, 0))
    wspec = lambda a: pl.BlockSpec((1,) + a.shape[1:], lambda i, be, nu, va: (be[i], 0, 0))
    return pl.pallas_call(
        _expert_kernel,
        grid_spec=pltpu.PrefetchScalarGridSpec(
            num_scalar_prefetch=3,
            grid=(n_blocks,),
            in_specs=[xspec, wspec(w1), wspec(w3), wspec(w2)],
            out_specs=xspec,
        ),
        out_shape=jax.ShapeDtypeStruct(x_sorted.shape, F32),
        compiler_params=_cparams(("arbitrary",)),
        name="experts",
    )(block_e, n_used, valid, x_sorted, w1, w3, w2)


def _combine_kernel(x1_ref, y0_ref, y1_ref, route_ref, g_ref, b_ref, o_ref):
    route = route_ref[...]
    ff = y0_ref[...] * route[:, 2:3] + y1_ref[...] * route[:, 3:4]
    o_ref[...] = _layer_norm(DEEPNORM_ALPHA * x1_ref[...] + ff, g_ref[...], b_ref[...])


def _combine(x1, y2, route, ln_g, ln_b, tm):
    n = x1.shape[0]
    row = lambda width: pl.BlockSpec((tm, width), lambda i: (i, 0))
    second = pl.BlockSpec((tm, D_MODEL), lambda i: (i + n // tm, 0))
    vec = pl.BlockSpec((1, D_MODEL), lambda i: (0, 0))
    return pl.pallas_call(
        _combine_kernel,
        grid=(n // tm,),
        in_specs=[row(D_MODEL), row(D_MODEL), second, row(LANES), vec, vec],
        out_specs=row(D_MODEL),
        out_shape=jax.ShapeDtypeStruct((n, D_MODEL), F32),
        compiler_params=_cparams(("parallel",)),
        name="combine_ln2",
    )(x1, y2, y2, route, ln_g.reshape(1, -1), ln_b.reshape(1, -1))


def _finish(o_nsa, bm, gm, x2, mw, ew, ln1, ln2, tm, rows):
    n = x2.shape[0]
    x1, route, counts = _merge(o_nsa, bm, gm, x2, mw, ln1[0], ln1[1], tm)
    sizes = counts[0, ROUTE_E0:ROUTE_E0 + N_EXPERTS].astype(jnp.int32)
    padded = (sizes + rows - 1) // rows * rows
    pad_end = jnp.cumsum(padded)
    pad_start = pad_end - padded
    e = route[:, 0:TOP_K].astype(jnp.int32)
    rank = route[:, 4:4 + TOP_K].astype(jnp.int32)
    dest = (pad_start[e] + rank).T.reshape(-1)
    n_blocks = -(-(n * TOP_K) // rows) + N_EXPERTS
    blk0 = jnp.arange(n_blocks, dtype=jnp.int32) * rows
    block_e = jnp.sum(blk0[:, None] >= pad_end[None, :], axis=1).astype(jnp.int32)
    block_e = jnp.minimum(block_e, N_EXPERTS - 1)
    n_used = (pad_end[-1:] // rows).astype(jnp.int32)
    valid = jnp.clip(sizes[block_e] - (blk0 - pad_start[block_e]), 0, rows).astype(jnp.int32)
    y_sorted = _experts(_scatter_rows(x1, dest, n_blocks * rows), block_e, n_used, valid, *ew, rows)
    return _combine(x1, _gather_rows(y_sorted, dest), route, ln2[0], ln2[1], tm)


PROMPT_ROWS = 512


def _feature_major(a):
    lead = a.shape[:-3]
    return jnp.moveaxis(a, -3, -1).reshape(lead + (2 * HEAD_DIM, a.shape[-3]))


def _row_major(a):
    lead = a.shape[:-2]
    return jnp.moveaxis(a.reshape(lead + (2, HEAD_DIM, a.shape[-1])), -1, -3)


def _pad_rows(a):
    return jnp.pad(a[:, None, :], ((0, 0), (0, HEAD_ROWS - 1), (0, 0)))


def kernel(x_prompt, x_sample, cache_kv_cmp, cache_kv_slc, state_kv_win, state_ret, page_table, w_in, b_in, cmp_pos, w_cmp1, w_cmp2, b_cmp2, ret_gn_g, ret_gn_b, w_nsa_proj, w_ret_proj, w_o, ln1_g, ln1_b, w_route_group, b_route_group, w_route_exp, b_route_exp, w_e1, w_e3, w_e2, ln2_g, ln2_b):
    nb, t_len, _ = x_prompt.shape
    ns, s_len, _ = x_sample.shape
    assert s_len == 1 and ns == LANES and t_len % (RET_STEP_CHUNKS * RET_CHUNK) == 0 and t_len >= WINDOW
    g = NSA_KV_GROUPS
    pw = _proj_weights(w_in, b_in)
    cw = _compress_weights(cmp_pos, w_cmp1, w_cmp2, b_cmp2)
    mw = _merge_weights(w_nsa_proj, w_ret_proj, w_o, w_route_group, b_route_group, w_route_exp, b_route_exp)
    ew = (w_e1, w_e3, w_e2)
    ln1 = (ln1_g, ln1_b)
    ln2 = (ln2_g, ln2_b)

    cos_p, sin_p = _rope_tables(jnp.arange(t_len, dtype=jnp.int32))
    x2 = x_prompt.reshape(nb * t_len, D_MODEL)
    qa, kvt_cmp, kvt_slc, kvt_win, ga, qb, kb, vb, gb, gm = _project(x2, nb, pw, cos_p, sin_p, PROMPT_ROWS)
    kc, kvct = _compress(kvt_cmp, cw)
    kslc, kwin, vslc, vwin = _kv_prep(kvt_slc, kvt_win)
    o_nsa = _nsa_prompt(qa, ga, kc, kvct, kslc, vslc, kwin, vwin)
    bm, ret_p = _ret_prompt(qb, kb, vb, gb, ret_gn_g, ret_gn_b, nb)
    y_prompt = _finish(o_nsa, bm, gm, x2, mw, ew, ln1, ln2, PROMPT_ROWS, MOE_ROWS).reshape(x_prompt.shape)
    kv_cmp_p = _row_major(kvt_cmp)
    kv_slc_p = _row_major(kvt_slc)
    win_p = _row_major(kvt_win[..., t_len - WINDOW:])

    n_pages = page_table.shape[1]
    past_len = n_pages * PAGE_SIZE
    cos_s, sin_s = _rope_tables(jnp.full((ns,), past_len, jnp.int32))
    xs2 = x_sample.reshape(ns, D_MODEL)
    qa, new_cmp, new_slc, new_win, ga, qb, kb, vb, gb, gm = _project(xs2, 1, pw, cos_s, sin_s, ns)
    q8 = jnp.pad(qa.reshape(ns, g, NSA_GROUP_HEADS, HEAD_DIM),
                 ((0, 0), (0, 0), (0, HEAD_ROWS - NSA_GROUP_HEADS), (0, 0))).astype(BF16)
    o_cmp, imp = _compress_paged_attend(_feature_major(cache_kv_cmp), page_table, cw, q8)
    picks = _dec_topk(imp[:, :, 0, :].reshape(ns * g, -1))
    idx = picks[:N_PAST_SEL].T.astype(jnp.int32)
    blocks_per_page = PAGE_SIZE // SEL_BLOCK
    seq = jnp.arange(ns * g, dtype=jnp.int32)[:, None] // g
    phys = page_table[seq, idx // blocks_per_page]
    half = idx % blocks_per_page
    gates = ga.reshape(ns, g, LANES)[:, :, :3 * NSA_GROUP_HEADS].reshape(ns, g, NSA_GROUP_HEADS, 3)
    gates = jnp.pad(gates, ((0, 0), (0, 0), (0, HEAD_ROWS - NSA_GROUP_HEADS), (0, LANES - 3)))
    o_s, win_t = _dec_attn(phys.reshape(-1), half.reshape(-1), q8, gates, o_cmp, _feature_major(cache_kv_slc),
                           new_slc, new_win, _feature_major(state_kv_win))
    o_nsa_s = o_s[:, :, :NSA_GROUP_HEADS].reshape(ns, NSA_HEADS * HEAD_DIM).astype(BF16)
    bm8, ret_s = _ret_step(_pad_rows(qb), _pad_rows(kb), _pad_rows(vb), _pad_rows(gb), state_ret, ret_gn_g, ret_gn_b)
    y_sample = _finish(o_nsa_s, bm8[:, 0], gm, xs2, mw, ew, ln1, ln2, ns, SAMPLE_MOE_ROWS).reshape(x_sample.shape)
    kv_cmp_s = jnp.moveaxis(new_cmp[0], -1, 0).reshape(ns, g, 1, 2, HEAD_DIM)
    kv_slc_s = jnp.moveaxis(new_slc[0], -1, 0).reshape(ns, g, 1, 2, HEAD_DIM)
    win_s = _row_major(win_t)
    return (y_prompt, y_sample, kv_cmp_p, kv_slc_p, win_p, ret_p, kv_cmp_s, kv_slc_s, win_s, ret_s)
```

```python
import functools

import numpy as np
import jax
import jax.numpy as jnp
from jax import lax
from jax.experimental import pallas as pl
from jax.experimental.pallas import tpu as pltpu
from jax.experimental.pallas import tpu_sc as plsc

F32 = jnp.float32
BF16 = jnp.bfloat16

D_MODEL = 1024
DEPTH = 1
PAGE_SIZE = 128
NSA_HEADS = 8
NSA_KV_GROUPS = 2
NSA_GROUP_HEADS = NSA_HEADS // NSA_KV_GROUPS
HEAD_DIM = 64
CMP_BLOCK = 32
CMP_STRIDE = 16
CMP_HIDDEN = 64
SEL_BLOCK = 64
N_SEL = 16
WINDOW = 512
Q_TILE = 256
FORCE_SCORE = 1.0e3
NEG_INF = -1.0e30
RET_HEADS = 4
RET_DK = 128
RET_DV = 128
RET_CHUNK = 128
ROPE_BASE = 10000.0
N_GROUPS = 4
EXPERTS_PER_GROUP = 8
N_EXPERTS = N_GROUPS * EXPERTS_PER_GROUP
TOP_K = 2
EXPERT_FF = 512
LN_EPS = 1e-5
DEEPNORM_ALPHA = (2.0 * DEPTH) ** 0.25

NSA_Q_COLS = NSA_HEADS * HEAD_DIM
NSA_KV_COLS = 3 * 2 * NSA_KV_GROUPS * HEAD_DIM
NSA_GATE_COLS = 3 * NSA_HEADS
RET_QK_COLS = RET_HEADS * RET_DK
RET_V_COLS = RET_HEADS * RET_DV

LANES = 128

P_QA = 0
P_GA = P_QA + NSA_Q_COLS
P_QB = P_GA + NSA_KV_GROUPS * LANES
P_KB = P_QB + RET_QK_COLS
P_VB = P_KB + RET_QK_COLS
P_GB = P_VB + RET_V_COLS
P_GM = P_GB + RET_V_COLS
VMEM_LIMIT = 56 * 1024 * 1024


def _cparams(sem):
    return pltpu.CompilerParams(dimension_semantics=sem, vmem_limit_bytes=VMEM_LIMIT)


def _proj_kernel(x_ref, w_ref, b_ref, wkv_ref, bkv_ref, cos_ref, sin_ref,
                 qa_ref, kcmp_ref, kslc_ref, kwin_ref, ga_ref, qb_ref, kb_ref, vb_ref, gb_ref, gm_ref):
    x = x_ref[...].astype(BF16)

    def cols(c0, n):
        h = jnp.dot(x, w_ref[:, c0:c0 + n], preferred_element_type=F32)
        return h + b_ref[:, c0:c0 + n]

    qa_ref[...] = cols(P_QA, 512) * (HEAD_DIM ** -0.5)
    ga_ref[...] = jax.nn.sigmoid(cols(P_GA, NSA_KV_GROUPS * LANES))
    cos = cos_ref[...]
    sin = sin_ref[...]
    for h in range(RET_HEADS):
        q = cols(P_QB + h * RET_DK, RET_DK)
        qb_ref[:, h * RET_DK:(h + 1) * RET_DK] = (q * cos + pltpu.roll(q, RET_DK // 2, 1) * sin).astype(BF16)
        k = cols(P_KB + h * RET_DK, RET_DK)
        kb_ref[:, h * RET_DK:(h + 1) * RET_DK] = (k * cos + pltpu.roll(k, RET_DK // 2, 1) * sin) * (RET_DK ** -0.5)
    vb_ref[...] = cols(P_VB, 512).astype(BF16)
    gb_ref[...] = cols(P_GB, 512)
    gm_ref[...] = cols(P_GM, 2048)
    kvt = lax.dot_general(wkv_ref[...], x, (((1,), (1,)), ((), ())), preferred_element_type=F32)
    kvt = kvt + bkv_ref[...]
    for br, out_ref in enumerate((kcmp_ref, kslc_ref, kwin_ref)):
        for g in range(NSA_KV_GROUPS):
            i = br * NSA_KV_GROUPS + g
            out_ref[0, g] = kvt[i * LANES:(i + 1) * LANES]


def _proj_weights(w_in, b_in):
    o_kv = NSA_Q_COLS
    o_ga = o_kv + NSA_KV_COLS
    o_qb = o_ga + NSA_GATE_COLS
    per_g = NSA_GATE_COLS // NSA_KV_GROUPS
    ga_cols = []
    for g in range(NSA_KV_GROUPS):
        ga_cols += [np.arange(o_ga + g * per_g, o_ga + (g + 1) * per_g), np.full(LANES - per_g, -1)]
    perm = np.concatenate([
        np.arange(0, NSA_Q_COLS),
        *ga_cols,
        np.arange(o_qb, o_qb + 2 * RET_QK_COLS + 2 * RET_V_COLS + 2 * D_MODEL),
    ])
    keep = jnp.asarray(perm >= 0)
    idx = jnp.asarray(np.maximum(perm, 0))
    w = jnp.where(keep[None, :], w_in[:, idx], 0.0).astype(BF16)
    b = jnp.where(keep, b_in[idx], 0.0)[None, :]
    br, g, kv, d = np.meshgrid(np.arange(3), np.arange(NSA_KV_GROUPS), np.arange(2), np.arange(HEAD_DIM), indexing="ij")
    kv_idx = jnp.asarray((o_kv + br * 256 + kv * 128 + g * 64 + d).reshape(-1))
    wkv = w_in[:, kv_idx].T.astype(BF16)
    bkv = b_in[kv_idx][:, None]
    return w, b, wkv, bkv


def _rope_tables(pos):
    half = RET_DK // 2
    inv = ROPE_BASE ** (-jnp.arange(half, dtype=F32) / half)
    ang = pos.astype(F32)[:, None] * inv[None, :]
    cos, sin = jnp.cos(ang), jnp.sin(ang)
    return jnp.concatenate([cos, cos], -1), jnp.concatenate([-sin, sin], -1)


def _project(x2, n_batch, pw, cos_t, sin_t, tm):
    w, b, wkv, bkv = pw
    n = x2.shape[0]
    t_len = n // n_batch
    nt = t_len // tm
    n_pos_blocks = cos_t.shape[0] // tm
    row = lambda i: (i, 0)
    const = lambda i: (0, 0)
    pos_map = lambda i: (i % n_pos_blocks, 0)
    kvt_shape = jax.ShapeDtypeStruct((n_batch, NSA_KV_GROUPS, LANES, t_len), F32)
    kvt_spec = pl.BlockSpec((1, NSA_KV_GROUPS, LANES, tm), lambda i: (i // nt, 0, 0, i % nt))
    out_shape = (
        jax.ShapeDtypeStruct((n, 512), F32),
        kvt_shape, kvt_shape, kvt_shape,
        jax.ShapeDtypeStruct((n, NSA_KV_GROUPS * LANES), F32),
        jax.ShapeDtypeStruct((n, 512), BF16),
        jax.ShapeDtypeStruct((n, 512), F32),
        jax.ShapeDtypeStruct((n, 512), BF16),
        jax.ShapeDtypeStruct((n, 512), F32),
        jax.ShapeDtypeStruct((n, 2048), F32),
    )
    return pl.pallas_call(
        _proj_kernel,
        grid=(n // tm,),
        in_specs=[
            pl.BlockSpec((tm, D_MODEL), row),
            pl.BlockSpec(w.shape, const),
            pl.BlockSpec(b.shape, const),
            pl.BlockSpec(wkv.shape, const),
            pl.BlockSpec(bkv.shape, const),
            pl.BlockSpec((tm, LANES), pos_map),
            pl.BlockSpec((tm, LANES), pos_map),
        ],
        out_specs=(
            pl.BlockSpec((tm, 512), row),
            kvt_spec, kvt_spec, kvt_spec,
            pl.BlockSpec((tm, NSA_KV_GROUPS * LANES), row),
            pl.BlockSpec((tm, 512), row),
            pl.BlockSpec((tm, 512), row),
            pl.BlockSpec((tm, 512), row),
            pl.BlockSpec((tm, 512), row),
            pl.BlockSpec((tm, 2048), row),
        ),
        out_shape=out_shape,
        compiler_params=_cparams(("parallel",)),
        name="proj",
    )(x2, w, b, wkv, bkv, cos_t, sin_t)


XPOSE_CHUNK = 512


def _compress_rows(xs_ref, wc_ref, w2_ref, b2_ref, pos_ref, wpe_ref, kc_ref, kvct_ref, perm_ref, nj):
    n_blk = 4 * nj
    pe = jnp.dot(pos_ref[...], wpe_ref[...], preferred_element_type=F32)[0:1]
    acc = jnp.zeros((n_blk, 2 * LANES), F32)
    for c2 in range(CMP_STRIDE // 2):
        xa = xs_ref[pl.ds(2 * c2, n_blk, stride=CMP_STRIDE), :]
        xb = xs_ref[pl.ds(2 * c2 + 1, n_blk, stride=CMP_STRIDE), :]
        x2 = jnp.concatenate([xa, xb], axis=1).astype(BF16)
        acc = acc + jnp.dot(x2, wc_ref[c2], preferred_element_type=F32)
    row = lax.broadcasted_iota(jnp.int32, (n_blk, LANES), 0)
    nxt = pltpu.roll(acc[:, LANES:], n_blk - 1, 0)
    second = jnp.where(row == n_blk - 1, 0.0, nxt)
    h = jax.nn.gelu(acc[:, :LANES] + second + pe)
    perm_ref[...] = jnp.dot(h.astype(BF16), w2_ref[...], preferred_element_type=F32) + b2_ref[...]
    lane = lax.broadcasted_iota(jnp.int32, (nj, LANES), 1)
    for rr in range(4):
        kv = perm_ref[pl.ds(rr, nj, stride=4), :]
        kc_ref[rr * nj:(rr + 1) * nj, :] = jnp.where(lane < HEAD_DIM, kv, 0.0).astype(BF16)
        kvct_ref[:, rr * nj:(rr + 1) * nj] = kv.T.astype(BF16)


def _compress_kernel(xt_ref, wc_ref, w2_ref, b2_ref, pos_ref, wpe_ref, kc_ref, kvct_ref, xs_ref, perm_ref, *, length):
    def xpose(i, _):
        t0 = pl.multiple_of(i * XPOSE_CHUNK, XPOSE_CHUNK)
        xs_ref[pl.ds(t0, XPOSE_CHUNK), :] = xt_ref[0, 0, :, pl.ds(t0, XPOSE_CHUNK)].T
        return 0

    lax.fori_loop(0, length // XPOSE_CHUNK, xpose, 0)
    _compress_rows(xs_ref, wc_ref, w2_ref, b2_ref, pos_ref, wpe_ref, kc_ref.at[0, 0], kvct_ref.at[0, 0], perm_ref,
                   length // SEL_BLOCK)


PAGES_PER_STEP = 64


def _compress_paged_kernel(pt_ref, *refs, n_pages, per_step):
    page_refs = refs[:per_step]
    (q_ref, wc_ref, w2_ref, b2_ref, pos_ref, wpe_ref, o_ref, imp_ref,
     xs_ref, perm_ref, kc_ref, kvct_ref) = refs[per_step:]
    s = pl.program_id(1)
    for k in range(per_step):
        r0 = pl.multiple_of((s * per_step + k) * PAGE_SIZE, PAGE_SIZE)
        for g in range(NSA_KV_GROUPS):
            xs_ref[g, pl.ds(r0, PAGE_SIZE), :] = page_refs[k][0, g].T

    @pl.when(s == n_pages // per_step - 1)
    def _():
        nj = n_pages * PAGE_SIZE // SEL_BLOCK
        for g in range(NSA_KV_GROUPS):
            _compress_rows(xs_ref.at[g], wc_ref, w2_ref, b2_ref, pos_ref, wpe_ref, kc_ref, kvct_ref, perm_ref, nj)
            o_ref[0, g], imp_ref[0, g] = _dec_compressed(q_ref[0, g], kvct_ref, nj)


def _compress_paged_attend(cache_t, page_table, cw, q8):
    wc, w2, b2, pos, wpe = cw
    nb, n_pages = page_table.shape
    g = cache_t.shape[1]
    length = n_pages * PAGE_SIZE
    n_rows = length // CMP_STRIDE
    nj = length // SEL_BLOCK
    per_step = min(PAGES_PER_STEP, n_pages)
    steps = n_pages // per_step
    page_spec = lambda k: pl.BlockSpec(
        (1, g, LANES, PAGE_SIZE), lambda b, s, pt: (pt[b * n_pages + s * per_step + k], 0, 0, 0))
    const = lambda a: pl.BlockSpec(a.shape, lambda b, s, pt: (0,) * a.ndim)
    per_seq = lambda *shape: pl.BlockSpec((1, g) + shape, lambda b, s, pt: (b, 0, 0, 0))
    return pl.pallas_call(
        functools.partial(_compress_paged_kernel, n_pages=n_pages, per_step=per_step),
        grid_spec=pltpu.PrefetchScalarGridSpec(
            num_scalar_prefetch=1,
            grid=(nb, steps),
            in_specs=[page_spec(k) for k in range(per_step)] + [per_seq(HEAD_ROWS, HEAD_DIM)]
            + [const(a) for a in (wc, w2, b2, pos, wpe)],
            out_specs=(per_seq(HEAD_ROWS, HEAD_DIM), per_seq(HEAD_ROWS, nj)),
            scratch_shapes=[pltpu.VMEM((g, length, LANES), F32), pltpu.VMEM((n_rows, LANES), F32),
                            pltpu.VMEM((n_rows, LANES), BF16), pltpu.VMEM((LANES, n_rows), BF16)],
        ),
        out_shape=(
            jax.ShapeDtypeStruct((nb, g, HEAD_ROWS, HEAD_DIM), F32),
            jax.ShapeDtypeStruct((nb, g, HEAD_ROWS, nj), F32),
        ),
        compiler_params=_cparams(("parallel", "arbitrary")),
        name="compress_paged",
    )(page_table.reshape(-1), *([cache_t] * per_step), q8, wc, w2, b2, pos, wpe)


def _compress_weights(cmp_pos, w_cmp1, w_cmp2, b_cmp2):
    eye = jnp.eye(2, dtype=F32)
    w5 = w_cmp1.reshape(2, 2, CMP_STRIDE, HEAD_DIM, CMP_HIDDEN)
    wc = jnp.einsum("vhcde,vw->cvdhwe", w5, eye).reshape(CMP_STRIDE // 2, 2 * LANES, 2 * LANES).astype(BF16)
    w2 = jnp.einsum("ved,vw->vewd", w_cmp2, eye).reshape(LANES, LANES).astype(BF16)
    b2 = b_cmp2.reshape(1, LANES)
    wpe = jnp.einsum("vcde,vw->vcdwe", w_cmp1, eye).reshape(2 * CMP_BLOCK * HEAD_DIM, LANES).astype(BF16)
    pos = jnp.broadcast_to(cmp_pos.reshape(1, -1), (8, 2 * CMP_BLOCK * HEAD_DIM)).astype(BF16)
    return wc, w2, b2, pos, wpe


def _compress(kvt, cw):
    wc, w2, b2, pos, wpe = cw
    nb, g, _, length = kvt.shape
    n_rows = length // CMP_STRIDE
    const2 = lambda b, gg: (0, 0)
    return pl.pallas_call(
        functools.partial(_compress_kernel, length=length),
        grid=(nb, g),
        in_specs=[
            pl.BlockSpec((1, 1, LANES, length), lambda b, gg: (b, gg, 0, 0)),
            pl.BlockSpec(wc.shape, lambda b, gg: (0, 0, 0)),
            pl.BlockSpec(w2.shape, const2),
            pl.BlockSpec(b2.shape, const2),
            pl.BlockSpec(pos.shape, const2),
            pl.BlockSpec(wpe.shape, const2),
        ],
        out_specs=(
            pl.BlockSpec((1, 1, n_rows, LANES), lambda b, gg: (b, gg, 0, 0)),
            pl.BlockSpec((1, 1, LANES, n_rows), lambda b, gg: (b, gg, 0, 0)),
        ),
        out_shape=(
            jax.ShapeDtypeStruct((nb, g, n_rows, LANES), BF16),
            jax.ShapeDtypeStruct((nb, g, LANES, n_rows), BF16),
        ),
        scratch_shapes=[pltpu.VMEM((length, LANES), F32), pltpu.VMEM((n_rows, LANES), F32)],
        compiler_params=_cparams(("parallel", "parallel")),
        name="compress",
    )(kvt, wc, w2, b2, pos, wpe)


BIAS_BLOCKS = 16
KEY_TILE = 512
PREP_TILE = 1024
N_COLS = NSA_GROUP_HEADS * Q_TILE
SLC_UNROLL = 4
V_ROWS = HEAD_DIM + 16
WIN_KEYS = WINDOW + Q_TILE
CMP_ROW_CHOICES = (64, 128, 192)


def _kv_prep_kernel(slc_ref, win_ref, kslc_ref, kwin_ref, vslc_ref, vwin_ref):
    shape = (PREP_TILE, LANES)
    lane = lax.broadcasted_iota(jnp.int32, shape, 1)
    blk = (lax.broadcasted_iota(jnp.int32, shape, 0) // SEL_BLOCK) % BIAS_BLOCKS
    onehot = jnp.where(lane - HEAD_DIM == blk, 1.0, 0.0)
    ks = slc_ref[0, 0].T
    kslc_ref[0, 0] = jnp.where(lane < HEAD_DIM, ks, onehot).astype(BF16)
    kw = win_ref[0, 0].T
    kwin_ref[0, 0] = jnp.where(lane < HEAD_DIM, kw, 0.0).astype(BF16)
    ones = jnp.ones((V_ROWS - HEAD_DIM, PREP_TILE), BF16)
    vslc_ref[0, 0] = jnp.concatenate([slc_ref[0, 0, HEAD_DIM:, :].astype(BF16), ones], axis=0)
    vwin_ref[0, 0] = jnp.concatenate([win_ref[0, 0, HEAD_DIM:, :].astype(BF16), ones], axis=0)


def _kv_prep(kvt_slc, kvt_win):
    nb, g, _, t_len = kvt_slc.shape
    in_spec = pl.BlockSpec((1, 1, LANES, PREP_TILE), lambda b, gg, i: (b, gg, 0, i))
    k_spec = pl.BlockSpec((1, 1, PREP_TILE, LANES), lambda b, gg, i: (b, gg, i, 0))
    v_spec = pl.BlockSpec((1, 1, V_ROWS, PREP_TILE), lambda b, gg, i: (b, gg, 0, i))
    k_shape = jax.ShapeDtypeStruct((nb, g, t_len, LANES), BF16)
    v_shape = jax.ShapeDtypeStruct((nb, g, V_ROWS, t_len), BF16)
    return pl.pallas_call(
        _kv_prep_kernel,
        grid=(nb, g, t_len // PREP_TILE),
        in_specs=[in_spec, in_spec],
        out_specs=(k_spec, k_spec, v_spec, v_spec),
        out_shape=(k_shape, k_shape, v_shape, v_shape),
        compiler_params=_cparams(("parallel", "parallel", "parallel")),
        name="kv_prep",
    )(kvt_slc, kvt_win)


def _online_update(carry, s, vt):
    m, acc = carry
    m_new = jnp.maximum(m, jnp.max(s, axis=0, keepdims=True))
    p = jnp.exp(s - m_new)
    acc = jnp.exp(m - m_new) * acc + jnp.dot(vt, p.astype(BF16), preferred_element_type=F32)
    return m_new, acc


def _softmax_out(acc):
    return acc[:HEAD_DIM] / acc[HEAD_DIM:HEAD_DIM + 1]


def _select_blocks(imp, q0):
    jrow = lax.broadcasted_iota(jnp.int32, imp.shape, 0)
    cur = (q0 + lax.broadcasted_iota(jnp.int32, imp.shape, 1)) // SEL_BLOCK
    work = jnp.where(jrow <= cur, imp, -1.0)
    bias = jnp.full(imp.shape, NEG_INF, F32)
    for forced_blk in (0, cur, cur - 1):
        hit = jrow == forced_blk
        bias = jnp.where(hit, 0.0, bias)
        work = jnp.where(hit, -3.0e38, work)
    jf = jrow.astype(F32)
    for _ in range(N_SEL - 3):
        mx = jnp.max(work, axis=0, keepdims=True)
        first = jnp.min(jnp.where(work == mx, jf, 1.0e9), axis=0, keepdims=True)
        hit = jf == first
        bias = jnp.where(hit, 0.0, bias)
        work = jnp.where(hit, -3.0e38, work)
    return bias


def _nsa_prompt_kernel(qa_ref, ga_ref, kc_ref, kvct_ref, kslc_ref, vslc_ref, kwin_ref, vwin_ref,
                       o_ref, qm_ref, bias_ref, ocmp_ref, mtail_ref, acctail_ref, *, nj):
    qi = pl.program_id(2)

    @pl.when(qi == 0)
    def _():
        qm_ref[...] = jnp.zeros_like(qm_ref)

    q_t = qa_ref[...].T
    for r in range(NSA_GROUP_HEADS):
        qm_ref[0:HEAD_DIM, r * Q_TILE:(r + 1) * Q_TILE] = q_t[r * HEAD_DIM:(r + 1) * HEAD_DIM].astype(BF16)
    ga_t = ga_ref[...].T

    def gate(branch):
        return jnp.concatenate([ga_t[r * 3 + branch:r * 3 + branch + 1] for r in range(NSA_GROUP_HEADS)], axis=1)

    q0 = qi * Q_TILE
    qpos_c = q0 + (lax.broadcasted_iota(jnp.int32, (1, N_COLS), 1) & (Q_TILE - 1))

    def compressed_and_select(n):
        kc = jnp.concatenate([kc_ref[0, 0, rr * nj:rr * nj + n, :] for rr in range(4)], axis=0)
        vct = jnp.concatenate([kvct_ref[0, 0, HEAD_DIM:, rr * nj:rr * nj + n] for rr in range(4)], axis=1)
        s = jnp.dot(kc, qm_ref[...], preferred_element_type=F32)
        j_end = lax.broadcasted_iota(jnp.int32, (n, N_COLS), 0) * SEL_BLOCK + (CMP_BLOCK - 1)
        cmp_end = jnp.concatenate([j_end + rr * CMP_STRIDE for rr in range(4)], axis=0)
        vis = cmp_end <= qpos_c
        s = jnp.where(vis, s, NEG_INF)
        m = jnp.max(s, axis=0, keepdims=True)
        p = jnp.where(vis, jnp.exp(s - m), 0.0)
        l = jnp.sum(p, axis=0, keepdims=True)
        pn = p / jnp.where(l > 0.0, l, 1.0)
        ocmp_ref[...] = jnp.dot(vct, pn.astype(BF16), preferred_element_type=F32)
        p3 = pn[3 * n:4 * n]
        jrow4 = lax.broadcasted_iota(jnp.int32, (n, N_COLS), 0)
        prev = jnp.where(jrow4 == 0, 0.0, pltpu.roll(p3, 1, 0))
        a = pn[0:n] + pn[n:2 * n] + pn[2 * n:3 * n] + p3 + prev
        imp = a[:, 0:Q_TILE]
        for r in range(1, NSA_GROUP_HEADS):
            imp = imp + a[:, r * Q_TILE:(r + 1) * Q_TILE]
        bias16 = _select_blocks(imp, q0).astype(BF16)
        for r in range(NSA_GROUP_HEADS):
            bias_ref[0:n, r * Q_TILE:(r + 1) * Q_TILE] = bias16

    sizes = [n for n in CMP_ROW_CHOICES if n < nj] + [nj]
    need = (q0 + Q_TILE) // SEL_BLOCK
    for v, n in enumerate(sizes):
        lo = sizes[v - 1] if v else 0
        pl.when((need > lo) & ((need <= n) | (v == len(sizes) - 1)))(functools.partial(compressed_and_select, n))
    o_cmp = ocmp_ref[...]


    init =(jnp.full((1, N_COLS), -jnp.inf, F32), jnp.zeros((V_ROWS, N_COLS), F32))
    q_rows = qm_ref[0:HEAD_DIM, :]
    pad_rows = jnp.zeros((LANES - HEAD_DIM - BIAS_BLOCKS, N_COLS), BF16)

    def slc_scores(st):
        b0 = pl.multiple_of((st * KEY_TILE // (SEL_BLOCK * BIAS_BLOCKS)) * BIAS_BLOCKS, BIAS_BLOCKS)
        qm = jnp.concatenate([q_rows, bias_ref[pl.ds(b0, BIAS_BLOCKS), :], pad_rows], axis=0)
        k0 = pl.multiple_of(st * KEY_TILE, KEY_TILE)
        return jnp.dot(kslc_ref[0, 0, pl.ds(k0, KEY_TILE), :], qm, preferred_element_type=F32)

    def slc_absorb(st, carry, sc):
        k0 = pl.multiple_of(st * KEY_TILE, KEY_TILE)
        return _online_update(carry, sc, vslc_ref[0, 0, :, pl.ds(k0, KEY_TILE)])

    def slc_group(i, carry):
        st0 = i * SLC_UNROLL
        sc = slc_scores(st0)
        for u in range(SLC_UNROLL):
            sc_next = slc_scores(st0 + u + 1) if u + 1 < SLC_UNROLL else None
            carry = slc_absorb(st0 + u, carry, sc)
            sc = sc_next
        return carry

    n_tiles = q0 // KEY_TILE + 1
    tail = (n_tiles - 1) % SLC_UNROLL + 1
    m_s, acc_s = lax.fori_loop(0, (n_tiles - tail) // SLC_UNROLL, slc_group, init)
    mtail_ref[...] = m_s
    acctail_ref[...] = acc_s

    def slc_tail(size):
        st0 = n_tiles - size
        carry = (mtail_ref[...], acctail_ref[...])
        sc = slc_scores(st0)
        for u in range(size):
            sc_next = slc_scores(st0 + u + 1) if u + 1 < size else None
            if u == size - 1:
                kpos = (st0 + u) * KEY_TILE + lax.broadcasted_iota(jnp.int32, (KEY_TILE, N_COLS), 0)
                sc = jnp.where(kpos <= qpos_c, sc, NEG_INF)
            carry = slc_absorb(st0 + u, carry, sc)
            sc = sc_next
        acctail_ref[...] = carry[1]

    for size in range(1, SLC_UNROLL + 1):
        pl.when(tail == size)(functools.partial(slc_tail, size))
    o_slc = _softmax_out(acctail_ref[...])

    kw0 = pl.multiple_of(jnp.maximum(q0 - WINDOW, 0), Q_TILE)
    sc = jnp.dot(kwin_ref[0, 0, pl.ds(kw0, WIN_KEYS), :], qm_ref[...], preferred_element_type=F32)
    dpos = qpos_c - (kw0 + lax.broadcasted_iota(jnp.int32, (WIN_KEYS, N_COLS), 0))
    sc = jnp.where(dpos >= 0, jnp.where(dpos < WINDOW, sc, NEG_INF), NEG_INF)
    p = jnp.exp(sc - jnp.max(sc, axis=0, keepdims=True))
    o_win = _softmax_out(jnp.dot(vwin_ref[0, 0, :, pl.ds(kw0, WIN_KEYS)], p.astype(BF16), preferred_element_type=F32))

    o = gate(0) * o_cmp + gate(1) * o_slc + gate(2) * o_win
    o_rows = jnp.concatenate([o[:, r * Q_TILE:(r + 1) * Q_TILE] for r in range(NSA_GROUP_HEADS)], axis=0)
    o_ref[...] = o_rows.T.astype(BF16)


def _nsa_prompt(qa, ga, kc, kvct, kslc, vslc, kwin, vwin):
    nb, g, t_len, _ = kslc.shape
    nq = t_len // Q_TILE
    nj = t_len // SEL_BLOCK
    n_cmp_rows = kc.shape[2]
    assert nj % BIAS_BLOCKS == 0 and n_cmp_rows == 4 * nj and t_len % KEY_TILE == 0 and t_len >= WIN_KEYS
    tile = lambda b, gg, i: (b * nq + i, gg)
    whole = lambda b, gg, i: (b, gg, 0, 0)
    return pl.pallas_call(
        functools.partial(_nsa_prompt_kernel, nj=nj),
        grid=(nb, g, nq),
        in_specs=[
            pl.BlockSpec((Q_TILE, NSA_GROUP_HEADS * HEAD_DIM), tile),
            pl.BlockSpec((Q_TILE, LANES), tile),
            pl.BlockSpec((1, 1, n_cmp_rows, LANES), whole),
            pl.BlockSpec((1, 1, LANES, n_cmp_rows), whole),
            pl.BlockSpec((1, 1, t_len, LANES), whole),
            pl.BlockSpec((1, 1, V_ROWS, t_len), whole),
            pl.BlockSpec((1, 1, t_len, LANES), whole),
            pl.BlockSpec((1, 1, V_ROWS, t_len), whole),
        ],
        out_specs=pl.BlockSpec((Q_TILE, NSA_GROUP_HEADS * HEAD_DIM), tile),
        out_shape=jax.ShapeDtypeStruct((nb * t_len, NSA_HEADS * HEAD_DIM), BF16),
        scratch_shapes=[pltpu.VMEM((LANES, N_COLS), BF16), pltpu.VMEM((nj, N_COLS), BF16),
                        pltpu.VMEM((HEAD_DIM, N_COLS), F32), pltpu.VMEM((1, N_COLS), F32),
                        pltpu.VMEM((V_ROWS, N_COLS), F32)],
        compiler_params=_cparams(("parallel", "parallel", "arbitrary")),
        name="nsa_prompt",
    )(qa, ga, kc, kvct, kslc, vslc, kwin, vwin)


HEAD_ROWS = 8
N_PAST_SEL = N_SEL - 1


def _softmax_lanes(s_list, mask_list):
    s_list = [jnp.where(mk, s, NEG_INF) for s, mk in zip(s_list, mask_list)]
    m = s_list[0].max(axis=-1, keepdims=True)
    for s in s_list[1:]:
        m = jnp.maximum(m, s.max(axis=-1, keepdims=True))
    p_list = [jnp.where(mk, jnp.exp(s - m), 0.0) for s, mk in zip(s_list, mask_list)]
    l = p_list[0].sum(axis=-1, keepdims=True)
    for p in p_list[1:]:
        l = l + p.sum(axis=-1, keepdims=True)
    inv = 1.0 / jnp.where(l > 0.0, l, 1.0)
    return [p * inv for p in p_list]


def _dot_nt(a, b):
    return lax.dot_general(a, b, (((1,), (1,)), ((), ())), preferred_element_type=F32)


def _dec_compressed(q, kvct_ref, nj):
    n = 4 * nj
    s = jnp.dot(q, kvct_ref[:HEAD_DIM, :], preferred_element_type=F32)
    lane = lax.broadcasted_iota(jnp.int32, (HEAD_ROWS, n), 1)
    blk = 4 * (lane % nj) + lane // nj
    (pn,) = _softmax_lanes([s], [blk <= n - 2])
    out = _dot_nt(pn.astype(BF16), kvct_ref[HEAD_DIM:, :])
    row = lax.broadcasted_iota(jnp.int32, (HEAD_ROWS, n), 0)
    ph = jnp.sum(jnp.where(row < NSA_GROUP_HEADS, pn, 0.0), axis=0, keepdims=True)
    p3 = ph[:, 3 * nj:]
    prev = jnp.where(lane[0:1, :nj] == 0, 0.0, pltpu.roll(p3, 1, 1))
    imp = ph[:, :nj] + ph[:, nj:2 * nj] + ph[:, 2 * nj:3 * nj] + p3 + prev
    return out, jnp.broadcast_to(imp, (HEAD_ROWS, nj))


def _dec_topk_kernel(imp_ref, idx_ref):
    work = imp_ref[...].T
    nj, n = work.shape
    jrow = lax.broadcasted_iota(jnp.int32, (nj, n), 0)
    jf = jrow.astype(F32)
    for forced_blk in (0, nj - 1):
        work = jnp.where(jrow == forced_blk, FORCE_SCORE, work)
    picks = []
    for _ in range(N_PAST_SEL):
        mx = jnp.max(work, axis=0, keepdims=True)
        first = jnp.min(jnp.where(work == mx, jf, 1.0e9), axis=0, keepdims=True)
        picks.append(first)
        work = jnp.where(jf == first, -3.0e38, work)
    picks.append(jnp.full((1, n), float(nj), F32))
    idx_ref[...] = jnp.concatenate(picks, axis=0)


def _dec_topk(imp2):
    n, nj = imp2.shape
    return pl.pallas_call(
        _dec_topk_kernel,
        out_shape=jax.ShapeDtypeStruct((N_SEL, n), F32),
        compiler_params=pltpu.CompilerParams(vmem_limit_bytes=VMEM_LIMIT),
        name="dec_topk",
    )(imp2)


def _dec_attn_kernel(phys_ref, half_ref, q_ref, gate_ref, ocmp_ref, *refs):
    page_refs = refs[:N_PAST_SEL]
    slcnew_ref, winnew_ref, win_ref, o_ref, winout_ref = refs[N_PAST_SEL:]
    b = pl.program_id(0)
    base = (b * pl.num_programs(1) + pl.program_id(1)) * N_PAST_SEL
    q = q_ref[0, 0]
    lane = lax.broadcasted_iota(jnp.int32, (HEAD_ROWS, LANES), 1)
    own = lane == b

    def attend(kv_blocks, masks):
        scores = [jnp.dot(q, kv[:HEAD_DIM].astype(BF16), preferred_element_type=F32) for kv in kv_blocks]
        probs = _softmax_lanes(scores, masks)
        out = jnp.zeros((HEAD_ROWS, HEAD_DIM), F32)
        for p, kv in zip(probs, kv_blocks):
            out = out + _dot_nt(p.astype(BF16), kv[HEAD_DIM:].astype(BF16))
        return out

    pages = [r[0, 0] for r in page_refs] + [slcnew_ref[0, 0]]
    masks = [(lane // SEL_BLOCK) == half_ref[base + k] for k in range(N_PAST_SEL)] + [own]
    o_slc = attend(pages, masks)
    st = win_ref[0, 0]
    win_c = st.shape[1]
    wlane = lax.broadcasted_iota(jnp.int32, (HEAD_ROWS, win_c), 1)
    o_win = attend([st, winnew_ref[0, 0]], [wlane > win_c - WINDOW, own])
    gates = gate_ref[0, 0]
    o_ref[0, 0] = gates[:, 0:1] * ocmp_ref[0, 0] + gates[:, 1:2] * o_slc + gates[:, 2:3] * o_win
    new = winnew_ref[0, 0]
    lane_full = lax.broadcasted_iota(jnp.int32, new.shape, 1)
    new_col = jnp.sum(jnp.where(lane_full == b, new, 0.0), axis=1, keepdims=True)
    col = lax.broadcasted_iota(jnp.int32, st.shape, 1)
    winout_ref[0, 0] = jnp.where(col == win_c - 1, new_col, pltpu.roll(st, win_c - 1, 1))


def _dec_attn(phys, half, q8, gates, o_cmp, slc_t, slc_new, win_new, win_t):
    nb, g, _, win_c = win_t.shape
    assert slc_new.shape[-1] == nb == LANES
    blk = lambda *shape: pl.BlockSpec((1, 1) + shape, lambda b, gg, ph, hf: (b, gg, 0, 0))
    page = lambda k: pl.BlockSpec((1, 1, LANES, PAGE_SIZE),
                                  lambda b, gg, ph, hf: (ph[(b * g + gg) * N_PAST_SEL + k], gg, 0, 0))
    new = pl.BlockSpec((1, 1, LANES, nb), lambda b, gg, ph, hf: (0, gg, 0, 0))
    return pl.pallas_call(
        _dec_attn_kernel,
        grid_spec=pltpu.PrefetchScalarGridSpec(
            num_scalar_prefetch=2,
            grid=(nb, g),
            in_specs=[blk(HEAD_ROWS, HEAD_DIM), blk(HEAD_ROWS, LANES), blk(HEAD_ROWS, HEAD_DIM)]
            + [page(k) for k in range(N_PAST_SEL)] + [new, new, blk(LANES, win_c)],
            out_specs=(blk(HEAD_ROWS, HEAD_DIM), blk(LANES, win_c)),
        ),
        out_shape=(jax.ShapeDtypeStruct((nb, g, HEAD_ROWS, HEAD_DIM), F32),
                   jax.ShapeDtypeStruct((nb, g, LANES, win_c), F32)),
        compiler_params=_cparams(("parallel", "parallel")),
        name="dec_attn",
    )(phys, half, q8, gates, o_cmp, *([slc_t] * N_PAST_SEL), slc_new, win_new, win_t)


RET_STEP_CHUNKS = 8


def _log_gamma():
    return jnp.log1p(-jnp.exp2(-5.0 - jnp.arange(RET_HEADS, dtype=F32)))


def _ret_tables(chunk):
    lg = _log_gamma()
    n = jnp.arange(chunk, dtype=F32)
    diff = n[:, None] - n[None, :]
    decay = jnp.exp(jnp.maximum(diff, 0.0)[None] * lg[:, None, None]) * (diff >= 0)[None]
    rep = lambda col: jnp.broadcast_to(col[:, :, None], (RET_HEADS, chunk, RET_DV))
    qdec = rep(jnp.exp((n + 1.0)[None, :] * lg[:, None]))
    kdec = rep(jnp.exp((chunk - 1.0 - n)[None, :] * lg[:, None]))
    sdec = jnp.broadcast_to(jnp.exp(chunk * lg)[:, None, None], (RET_HEADS, RET_DK, RET_DV))
    return decay.astype(F32), qdec, kdec, sdec


def _group_norm_gate(o, g, gn_g, gn_b):
    mu = jnp.mean(o, axis=-1, keepdims=True)
    var = jnp.mean(jnp.square(o - mu), axis=-1, keepdims=True)
    on = (o - mu) * lax.rsqrt(var + LN_EPS) * gn_g + gn_b
    return (jax.nn.silu(g) * on).astype(BF16)


def _ret_prompt_kernel(qb_ref, kb_ref, vb_ref, gb_ref, decay_ref, qdec_ref, kdec_ref, sdec_ref, gng_ref, gnb_ref,
                       bm_ref, state_ref, st_ref):
    @pl.when(pl.program_id(0) == 0)
    def _():
        st_ref[...] = jnp.zeros_like(st_ref)

    def chunk_step(c, _):
        rows = pl.ds(pl.multiple_of(c * RET_CHUNK, RET_CHUNK), RET_CHUNK)
        for b in range(qb_ref.shape[0]):
            for h in range(RET_HEADS):
                cols = slice(h * RET_DK, (h + 1) * RET_DK)
                q = qb_ref[b, rows, cols]
                k = kb_ref[b, rows, cols]
                v = vb_ref[b, rows, cols]
                st = st_ref[b, h]
                inner = lax.dot_general(q, k.astype(BF16), (((1,), (1,)), ((), ())), preferred_element_type=F32)
                inner = inner * decay_ref[h]
                o = jnp.dot(inner.astype(BF16), v, preferred_element_type=F32)
                o = o + jnp.dot(q, st.astype(BF16), preferred_element_type=F32) * qdec_ref[h]
                kd = (k * kdec_ref[h]).astype(BF16)
                st_ref[b, h] = sdec_ref[h] * st + lax.dot_general(kd, v, (((0,), (0,)), ((), ())),
                                                                  preferred_element_type=F32)
                bm_ref[b, rows, cols] = _group_norm_gate(o, gb_ref[b, rows, cols], gng_ref[:, cols], gnb_ref[:, cols])
        return 0

    lax.fori_loop(0, RET_STEP_CHUNKS, chunk_step, 0)
    state_ref[...] = st_ref[...]


def _ret_prompt(qb, kb, vb, gb, gn_g, gn_b, nb):
    n, width = qb.shape
    t_len = n // nb
    tt = RET_STEP_CHUNKS * RET_CHUNK
    decay, qdec, kdec, sdec = _ret_tables(RET_CHUNK)
    seqs = lambda a: a.reshape(nb, t_len, width)
    row = pl.BlockSpec((nb, tt, width), lambda i: (0, i, 0))
    tab = pl.BlockSpec((RET_HEADS, RET_CHUNK, RET_DV), lambda i: (0, 0, 0))
    vec = pl.BlockSpec((1, RET_HEADS * RET_DV), lambda i: (0, 0))
    state_shape = (nb, RET_HEADS, RET_DK, RET_DV)
    bm, state = pl.pallas_call(
        _ret_prompt_kernel,
        grid=(t_len // tt,),
        in_specs=[row, row, row, row, tab, tab, tab, tab, vec, vec],
        out_specs=(row, pl.BlockSpec(state_shape, lambda i: (0, 0, 0, 0))),
        out_shape=(jax.ShapeDtypeStruct((nb, t_len, width), BF16), jax.ShapeDtypeStruct(state_shape, F32)),
        scratch_shapes=[pltpu.VMEM(state_shape, F32)],
        compiler_params=_cparams(("arbitrary",)),
        name="ret_prompt",
    )(seqs(qb), seqs(kb), seqs(vb), seqs(gb), decay, qdec, kdec, sdec, gn_g.reshape(1, -1), gn_b.reshape(1, -1))
    return bm.reshape(n, width), state


def _ret_step_kernel(q_ref, k_ref, v_ref, g_ref, st_ref, qdec_ref, sdec_ref, gng_ref, gnb_ref, bm_ref, stout_ref):
    for s in range(q_ref.shape[0]):
        for h in range(RET_HEADS):
            cols = slice(h * RET_DK, (h + 1) * RET_DK)
            q = q_ref[s, :, cols]
            k = k_ref[s, :, cols].astype(BF16)
            v = v_ref[s, :, cols]
            st = st_ref[s, h]
            qk = jnp.sum(q.astype(F32) * k.astype(F32), axis=-1, keepdims=True)
            o = qk.astype(BF16).astype(F32) * v.astype(F32)
            o = o + jnp.dot(q, st.astype(BF16), preferred_element_type=F32) * qdec_ref[h]
            stout_ref[s, h] = sdec_ref[h] * st + lax.dot_general(k, v, (((0,), (0,)), ((), ())),
                                                                 preferred_element_type=F32)
            bm_ref[s, :, cols] = _group_norm_gate(o, g_ref[s, :, cols], gng_ref[:, cols], gnb_ref[:, cols])


RET_STEP_SEQS = 4


def _ret_step(q8, k8, v8, g8, state, gn_g, gn_b):
    nb = state.shape[0]
    ns = RET_STEP_SEQS if nb % RET_STEP_SEQS == 0 else 1
    _, qdec, _, sdec = _ret_tables(1)
    tok = pl.BlockSpec((ns, HEAD_ROWS, RET_HEADS * RET_DK), lambda b: (b, 0, 0))
    st = pl.BlockSpec((ns, RET_HEADS, RET_DK, RET_DV), lambda b: (b, 0, 0, 0))
    vec = pl.BlockSpec((1, RET_HEADS * RET_DV), lambda b: (0, 0))
    return pl.pallas_call(
        _ret_step_kernel,
        grid=(nb // ns,),
        in_specs=[tok, tok, tok, tok, st,
                  pl.BlockSpec((RET_HEADS, 1, RET_DV), lambda b: (0, 0, 0)),
                  pl.BlockSpec((RET_HEADS, RET_DK, RET_DV), lambda b: (0, 0, 0)), vec, vec],
        out_specs=(tok, st),
        out_shape=(jax.ShapeDtypeStruct((nb, HEAD_ROWS, RET_HEADS * RET_DV), BF16),
                   jax.ShapeDtypeStruct(state.shape, F32)),
        compiler_params=_cparams(("parallel",)),
        name="ret_step",
    )(q8, k8, v8, g8, state, qdec, sdec, gn_g.reshape(1, -1), gn_b.reshape(1, -1))


ROUTE_E0 = N_GROUPS
MOE_ROWS = 256
SAMPLE_MOE_ROWS = 32


def _layer_norm(v, g, b):
    mu = jnp.mean(v, axis=-1, keepdims=True)
    var = jnp.mean(jnp.square(v - mu), axis=-1, keepdims=True)
    return (v - mu) * lax.rsqrt(var + LN_EPS) * g + b


def _first_lane(hit, lane):
    return jnp.min(jnp.where(hit, lane, 4 * LANES), axis=-1, keepdims=True)


def _merge_kernel(on_ref, bm_ref, gm_ref, x_ref, wn_ref, wr_ref, wo_ref, g1_ref, b1_ref, wrt_ref, brt_ref, tri_ref,
                  x1_ref, route_ref, counts_ref, carry_ref):
    @pl.when(pl.program_id(0) == 0)
    def _():
        carry_ref[...] = jnp.zeros_like(carry_ref)

    gate_a = jax.nn.sigmoid(gm_ref[:, :D_MODEL])
    gate_b = jax.nn.sigmoid(gm_ref[:, D_MODEL:])
    m = gate_a * jnp.dot(on_ref[...], wn_ref[...], preferred_element_type=F32)
    m = m + gate_b * jnp.dot(bm_ref[...], wr_ref[...], preferred_element_type=F32)
    mix = jnp.dot(m.astype(BF16), wo_ref[...], preferred_element_type=F32)
    x1 = _layer_norm(DEEPNORM_ALPHA * x_ref[...] + mix, g1_ref[...], b1_ref[...])
    x1_ref[...] = x1

    x_hi = x1.astype(BF16)
    x_lo = (x1 - x_hi.astype(F32)).astype(BF16)
    hi_part = jnp.dot(x_hi, wrt_ref[...], preferred_element_type=F32)
    lg = hi_part[:, :LANES] + hi_part[:, LANES:] + brt_ref[...]
    lg = lg + jnp.dot(x_lo, wrt_ref[:, :LANES], preferred_element_type=F32)
    lane = lax.broadcasted_iota(jnp.int32, lg.shape, 1)
    gl = jnp.where(lane < N_GROUPS, lg, -jnp.inf)
    ge = jnp.exp(gl - jnp.max(gl, axis=-1, keepdims=True))
    pgrp = ge / jnp.sum(ge, axis=-1, keepdims=True)
    pg = jnp.max(pgrp, axis=-1, keepdims=True)
    grp = _first_lane(pgrp == pg, lane)
    lo = ROUTE_E0 + grp * EXPERTS_PER_GROUP
    in_g = (lane >= lo) & (lane < lo + EXPERTS_PER_GROUP)
    el = jnp.where(in_g, lg, -jnp.inf)
    ee = jnp.exp(el - jnp.max(el, axis=-1, keepdims=True))
    pc = jnp.where(in_g, ee / jnp.sum(ee, axis=-1, keepdims=True), -1.0)
    w1 = jnp.max(pc, axis=-1, keepdims=True)
    i1 = _first_lane(pc == w1, lane)
    pc2 = jnp.where(lane == i1, -1.0, pc)
    w2 = jnp.max(pc2, axis=-1, keepdims=True)
    i2 = _first_lane(pc2 == w2, lane)
    wsum = w1 + w2
    wt1 = pg * w1 / wsum
    wt2 = pg * w2 / wsum
    hit1 = lane == i1
    hit2 = lane == i2
    cnt = jnp.where(hit1, 1.0, 0.0) + jnp.where(hit2, 1.0, 0.0)
    base = carry_ref[0:1, :] + jnp.dot(tri_ref[...], cnt.astype(BF16), preferred_element_type=F32)
    rank1 = jnp.sum(jnp.where(hit1, base, 0.0), axis=-1, keepdims=True)
    rank2 = jnp.sum(jnp.where(hit2, base, 0.0), axis=-1, keepdims=True)
    carry = carry_ref[0:1, :] + jnp.sum(cnt, axis=0, keepdims=True)
    carry_ref[...] = jnp.broadcast_to(carry, carry_ref.shape)
    counts_ref[...] = jnp.broadcast_to(carry, counts_ref.shape)
    fields = ((i1 - ROUTE_E0).astype(F32), (i2 - ROUTE_E0).astype(F32), wt1, wt2, rank1, rank2)
    out = jnp.zeros(lg.shape, F32)
    for idx, val in enumerate(fields):
        out = jnp.where(lane == idx, val, out)
    route_ref[...] = out


def _merge_weights(w_nsa_proj, w_ret_proj, w_o, w_route_group, b_route_group, w_route_exp, b_route_exp):
    pad = LANES - N_GROUPS - N_EXPERTS
    wrt = jnp.concatenate([w_route_group, w_route_exp, jnp.zeros((D_MODEL, pad), F32)], axis=1)
    w_hi = wrt.astype(BF16)
    w_lo = (wrt - w_hi.astype(F32)).astype(BF16)
    wrt = jnp.concatenate([w_hi, w_lo], axis=1)
    brt = jnp.concatenate([b_route_group, b_route_exp, jnp.zeros((pad,), F32)])[None, :]
    return w_nsa_proj.astype(BF16), w_ret_proj.astype(BF16), w_o.astype(BF16), wrt, brt


def _merge(o_nsa, bm, gm, x2, mw, ln_g, ln_b, tm):
    wn, wr, wo, wrt, brt = mw
    n = x2.shape[0]
    tri = (jnp.arange(tm)[:, None] > jnp.arange(tm)[None, :]).astype(BF16)
    row = lambda width: pl.BlockSpec((tm, width), lambda i: (i, 0))
    const = lambda a: pl.BlockSpec(a.shape, lambda i: (0, 0))
    ln_g = ln_g.reshape(1, -1)
    ln_b = ln_b.reshape(1, -1)
    return pl.pallas_call(
        _merge_kernel,
        grid=(n // tm,),
        in_specs=[row(512), row(512), row(2 * D_MODEL), row(D_MODEL), const(wn), const(wr), const(wo),
                  const(ln_g), const(ln_b), const(wrt), const(brt), const(tri)],
        out_specs=(row(D_MODEL), row(LANES), pl.BlockSpec((8, LANES), lambda i: (0, 0))),
        out_shape=(jax.ShapeDtypeStruct((n, D_MODEL), F32), jax.ShapeDtypeStruct((n, LANES), F32),
                   jax.ShapeDtypeStruct((8, LANES), F32)),
        scratch_shapes=[pltpu.VMEM((8, LANES), F32)],
        compiler_params=_cparams(("arbitrary",)),
        name="merge_route",
    )(o_nsa, bm, gm, x2, wn, wr, wo, ln_g, ln_b, wrt, brt, tri)


SC_CHUNK_ROWS = 64


def _gather_rows(table, idx):
    info = plsc.get_sparse_core_info()
    n_cores = info.num_cores
    n_workers = n_cores * info.num_subcores
    n_idx, width = idx.shape[0], table.shape[1]
    per_worker = n_idx // n_workers
    chunk = max(c for c in range(8, SC_CHUNK_ROWS + 1, 8) if per_worker % c == 0)
    assert n_idx % (8 * n_workers) == 0
    mesh = plsc.VectorSubcoreMesh(core_axis_name="c", subcore_axis_name="s")

    def body(table_hbm, idx_hbm, out_hbm, idx_v, rows_v, sem):
        worker = lax.axis_index("s") * n_cores + lax.axis_index("c")
        base = worker * per_worker

        @pl.loop(0, per_worker // chunk)
        def _(c):
            off = pl.multiple_of(base + c * chunk, 8)
            pltpu.sync_copy(idx_hbm.at[pl.ds(off, chunk)], idx_v)
            pltpu.async_copy(table_hbm.at[idx_v], rows_v, sem).wait()
            pltpu.sync_copy(rows_v, out_hbm.at[pl.ds(off, chunk)])

    return pl.kernel(
        body, mesh=mesh,
        out_type=jax.ShapeDtypeStruct((n_idx, width), table.dtype),
        scratch_types=[pltpu.VMEM((chunk,), jnp.int32), pltpu.VMEM((chunk, width), table.dtype),
                       pltpu.SemaphoreType.DMA],
    )(table, idx)


def _scatter_rows(rows, dest, n_slots):
    info = plsc.get_sparse_core_info()
    n_cores = info.num_cores
    n_workers = n_cores * info.num_subcores
    n, width = rows.shape
    n_idx = dest.shape[0]
    per_worker = n_idx // n_workers
    chunk = max(c for c in range(8, SC_CHUNK_ROWS + 1, 8) if per_worker % c == 0)
    assert n_idx % (8 * n_workers) == 0 and n % per_worker == 0
    dest2 = dest.reshape(n_idx // chunk, chunk)
    mesh = plsc.VectorSubcoreMesh(core_axis_name="c", subcore_axis_name="s")

    def body(rows_hbm, dest_hbm, out_hbm, idx_v, rows_v):
        worker = lax.axis_index("s") * n_cores + lax.axis_index("c")
        base = worker * per_worker

        @pl.loop(0, per_worker // chunk)
        def _(c):
            off = base + c * chunk
            pltpu.sync_copy(dest_hbm.at[pl.ds(off // chunk, 1)], idx_v)
            pltpu.sync_copy(rows_hbm.at[pl.ds(pl.multiple_of(lax.rem(off, n), 8), chunk)], rows_v)
            pltpu.sync_copy(rows_v, out_hbm.at[idx_v.at[0]])

    return pl.kernel(
        body, mesh=mesh,
        out_type=jax.ShapeDtypeStruct((n_slots, width), rows.dtype),
        scratch_types=[pltpu.VMEM((1, chunk), jnp.int32), pltpu.VMEM((chunk, width), rows.dtype)],
    )(rows, dest2)


def _expert_kernel(be_ref, nused_ref, valid_ref, x_ref, w1_ref, w3_ref, w2_ref, y_ref):
    i = pl.program_id(0)

    @pl.when(i >= nused_ref[0])
    def _():
        y_ref[...] = jnp.zeros_like(y_ref)

    @pl.when(i < nused_ref[0])
    def _():
        row = lax.broadcasted_iota(jnp.int32, x_ref.shape, 0)
        x = jnp.where(row < valid_ref[i], x_ref[...], 0.0).astype(BF16)
        h = jax.nn.silu(jnp.dot(x, w1_ref[0].astype(BF16), preferred_element_type=F32))
        h = h * jnp.dot(x, w3_ref[0].astype(BF16), preferred_element_type=F32)
        y_ref[...] = jnp.dot(h.astype(BF16), w2_ref[0].astype(BF16), preferred_element_type=F32)


def _experts(x_sorted, block_e, n_used, valid, w1, w3, w2, rows):
    n_blocks = block_e.shape[0]
    xspec = pl.BlockSpec((rows, D_MODEL), lambda i, be, nu, va: (i, 0))
    wspec = lambda a: pl.BlockSpec((1,) + a.shape[1:], lambda i, be, nu, va: (be[i], 0, 0))
    return pl.pallas_call(
        _expert_kernel,
        grid_spec=pltpu.PrefetchScalarGridSpec(
            num_scalar_prefetch=3,
            grid=(n_blocks,),
            in_specs=[xspec, wspec(w1), wspec(w3), wspec(w2)],
            out_specs=xspec,
        ),
        out_shape=jax.ShapeDtypeStruct(x_sorted.shape, F32),
        compiler_params=_cparams(("arbitrary",)),
        name="experts",
    )(block_e, n_used, valid, x_sorted, w1, w3, w2)


def _combine_kernel(x1_ref, y0_ref, y1_ref, route_ref, g_ref, b_ref, o_ref):
    route = route_ref[...]
    ff = y0_ref[...] * route[:, 2:3] + y1_ref[...] * route[:, 3:4]
    o_ref[...] = _layer_norm(DEEPNORM_ALPHA * x1_ref[...] + ff, g_ref[...], b_ref[...])


def _combine(x1, y2, route, ln_g, ln_b, tm):
    n = x1.shape[0]
    row = lambda width: pl.BlockSpec((tm, width), lambda i: (i, 0))
    second = pl.BlockSpec((tm, D_MODEL), lambda i: (i + n // tm, 0))
    vec = pl.BlockSpec((1, D_MODEL), lambda i: (0, 0))
    return pl.pallas_call(
        _combine_kernel,
        grid=(n // tm,),
        in_specs=[row(D_MODEL), row(D_MODEL), second, row(LANES), vec, vec],
        out_specs=row(D_MODEL),
        out_shape=jax.ShapeDtypeStruct((n, D_MODEL), F32),
        compiler_params=_cparams(("parallel",)),
        name="combine_ln2",
    )(x1, y2, y2, route, ln_g.reshape(1, -1), ln_b.reshape(1, -1))


def _finish(o_nsa, bm, gm, x2, mw, ew, ln1, ln2, tm, rows):
    n = x2.shape[0]
    x1, route, counts = _merge(o_nsa, bm, gm, x2, mw, ln1[0], ln1[1], tm)
    sizes = counts[0, ROUTE_E0:ROUTE_E0 + N_EXPERTS].astype(jnp.int32)
    padded = (sizes + rows - 1) // rows * rows
    pad_end = jnp.cumsum(padded)
    pad_start = pad_end - padded
    e = route[:, 0:TOP_K].astype(jnp.int32)
    rank = route[:, 4:4 + TOP_K].astype(jnp.int32)
    dest = (pad_start[e] + rank).T.reshape(-1)
    n_blocks = -(-(n * TOP_K) // rows) + N_EXPERTS
    blk0 = jnp.arange(n_blocks, dtype=jnp.int32) * rows
    block_e = jnp.sum(blk0[:, None] >= pad_end[None, :], axis=1).astype(jnp.int32)
    block_e = jnp.minimum(block_e, N_EXPERTS - 1)
    n_used = (pad_end[-1:] // rows).astype(jnp.int32)
    valid = jnp.clip(sizes[block_e] - (blk0 - pad_start[block_e]), 0, rows).astype(jnp.int32)
    y_sorted = _experts(_scatter_rows(x1, dest, n_blocks * rows), block_e, n_used, valid, *ew, rows)
    return _combine(x1, _gather_rows(y_sorted, dest), route, ln2[0], ln2[1], tm)


PROMPT_ROWS = 512


def _feature_major(a):
    lead = a.shape[:-3]
    return jnp.moveaxis(a, -3, -1).reshape(lead + (2 * HEAD_DIM, a.shape[-3]))


def _row_major(a):
    lead = a.shape[:-2]
    return jnp.moveaxis(a.reshape(lead + (2, HEAD_DIM, a.shape[-1])), -1, -3)


def _pad_rows(a):
    return jnp.pad(a[:, None, :], ((0, 0), (0, HEAD_ROWS - 1), (0, 0)))


def kernel(x_prompt, x_sample, cache_kv_cmp, cache_kv_slc, state_kv_win, state_ret, page_table, w_in, b_in, cmp_pos, w_cmp1, w_cmp2, b_cmp2, ret_gn_g, ret_gn_b, w_nsa_proj, w_ret_proj, w_o, ln1_g, ln1_b, w_route_group, b_route_group, w_route_exp, b_route_exp, w_e1, w_e3, w_e2, ln2_g, ln2_b):
    nb, t_len, _ = x_prompt.shape
    ns, s_len, _ = x_sample.shape
    assert s_len == 1 and ns == LANES and t_len % (RET_STEP_CHUNKS * RET_CHUNK) == 0 and t_len >= WINDOW
    g = NSA_KV_GROUPS
    pw = _proj_weights(w_in, b_in)
    cw = _compress_weights(cmp_pos, w_cmp1, w_cmp2, b_cmp2)
    mw = _merge_weights(w_nsa_proj, w_ret_proj, w_o, w_route_group, b_route_group, w_route_exp, b_route_exp)
    ew = (w_e1, w_e3, w_e2)
    ln1 = (ln1_g, ln1_b)
    ln2 = (ln2_g, ln2_b)

    cos_p, sin_p = _rope_tables(jnp.arange(t_len, dtype=jnp.int32))
    x2 = x_prompt.reshape(nb * t_len, D_MODEL)
    qa, kvt_cmp, kvt_slc, kvt_win, ga, qb, kb, vb, gb, gm = _project(x2, nb, pw, cos_p, sin_p, PROMPT_ROWS)
    kc, kvct = _compress(kvt_cmp, cw)
    kslc, kwin, vslc, vwin = _kv_prep(kvt_slc, kvt_win)
    o_nsa = _nsa_prompt(qa, ga, kc, kvct, kslc, vslc, kwin, vwin)
    bm, ret_p = _ret_prompt(qb, kb, vb, gb, ret_gn_g, ret_gn_b, nb)
    y_prompt = _finish(o_nsa, bm, gm, x2, mw, ew, ln1, ln2, PROMPT_ROWS, MOE_ROWS).reshape(x_prompt.shape)
    kv_cmp_p = _row_major(kvt_cmp)
    kv_slc_p = _row_major(kvt_slc)
    win_p = _row_major(kvt_win[..., t_len - WINDOW:])

    n_pages = page_table.shape[1]
    past_len = n_pages * PAGE_SIZE
    cos_s, sin_s = _rope_tables(jnp.full((ns,), past_len, jnp.int32))
    xs2 = x_sample.reshape(ns, D_MODEL)
    qa, new_cmp, new_slc, new_win, ga, qb, kb, vb, gb, gm = _project(xs2, 1, pw, cos_s, sin_s, ns)
    q8 = jnp.pad(qa.reshape(ns, g, NSA_GROUP_HEADS, HEAD_DIM),
                 ((0, 0), (0, 0), (0, HEAD_ROWS - NSA_GROUP_HEADS), (0, 0))).astype(BF16)
    o_cmp, imp = _compress_paged_attend(_feature_major(cache_kv_cmp), page_table, cw, q8)
    picks = _dec_topk(imp[:, :, 0, :].reshape(ns * g, -1))
    idx = picks[:N_PAST_SEL].T.astype(jnp.int32)
    blocks_per_page = PAGE_SIZE // SEL_BLOCK
    seq = jnp.arange(ns * g, dtype=jnp.int32)[:, None] // g
    phys = page_table[seq, idx // blocks_per_page]
    half = idx % blocks_per_page
    gates = ga.reshape(ns, g, LANES)[:, :, :3 * NSA_GROUP_HEADS].reshape(ns, g, NSA_GROUP_HEADS, 3)
    gates = jnp.pad(gates, ((0, 0), (0, 0), (0, HEAD_ROWS - NSA_GROUP_HEADS), (0, LANES - 3)))
    o_s, win_t = _dec_attn(phys.reshape(-1), half.reshape(-1), q8, gates, o_cmp, _feature_major(cache_kv_slc),
                           new_slc, new_win, _feature_major(state_kv_win))
    o_nsa_s = o_s[:, :, :NSA_GROUP_HEADS].reshape(ns, NSA_HEADS * HEAD_DIM).astype(BF16)
    bm8, ret_s = _ret_step(_pad_rows(qb), _pad_rows(kb), _pad_rows(vb), _pad_rows(gb), state_ret, ret_gn_g, ret_gn_b)
    y_sample = _finish(o_nsa_s, bm8[:, 0], gm, xs2, mw, ew, ln1, ln2, ns, SAMPLE_MOE_ROWS).reshape(x_sample.shape)
    kv_cmp_s = jnp.moveaxis(new_cmp[0], -1, 0).reshape(ns, g, 1, 2, HEAD_DIM)
    kv_slc_s = jnp.moveaxis(new_slc[0], -1, 0).reshape(ns, g, 1, 2, HEAD_DIM)
    win_s = _row_major(win_t)
    return (y_prompt, y_sample, kv_cmp_p, kv_slc_p, win_p, ret_p, kv_cmp_s, kv_slc_s, win_s, ret_s)
```
